```python
import jax, jax.numpy as jnp
from jax import lax
import numpy as np

D_MODEL = 1024
BATCH = 8
SEQ = 8192
DEPTH = 1

HEAD_DIM = 64
CONV_WIDTH_CH = 512
N_HEADS = 8
ATTN_WIDTH = N_HEADS * HEAD_DIM
MIX_WIDTH = CONV_WIDTH_CH + ATTN_WIDTH
CONV_K = 31
DILATED_PATTERNS = ((128, 1), (512, 4), (2048, 16))
Q_BLOCK = 128
D_FF = 2816
FFN_CONV_K = 3
EPS = 1e-6

kernel_name = "hymba_conformer_dilated_alibi_convffn"


def rms_norm(x, g):
    xf = x.astype(jnp.float32)
    y = xf * lax.rsqrt(jnp.mean(xf * xf, axis=-1, keepdims=True) + EPS)
    return (y * g.astype(jnp.float32)).astype(x.dtype)


def layer_norm(x, g, b):
    xf = x.astype(jnp.float32)
    mu = jnp.mean(xf, axis=-1, keepdims=True)
    var = jnp.mean(jnp.square(xf - mu), axis=-1, keepdims=True)
    y = (xf - mu) * lax.rsqrt(var + EPS)
    return (y * g.astype(jnp.float32) + b.astype(jnp.float32)).astype(x.dtype)


def causal_depthwise_conv(x, w, b):
    K, C = w.shape
    y = lax.conv_general_dilated(
        x, w[:, None, :].astype(x.dtype), window_strides=(1,), padding=[(K - 1, 0)],
        dimension_numbers=("NWC", "WIO", "NWC"), feature_group_count=C)
    return y + b.astype(x.dtype)


def alibi_slopes(n_heads):
    return 2.0 ** (-8.0 * jnp.arange(1, n_heads + 1, dtype=jnp.float32) / n_heads)


def dilated_window_attention(q, k, v, slopes, window, dilation):
    B, S, H, E = q.shape
    d = dilation
    n_back = window // d
    L = S // d
    nb = -(-L // Q_BLOCK)
    Lp = nb * Q_BLOCK

    def to_streams(a):
        a = a.reshape(B, L, d, H, E)
        a = jnp.pad(a, ((0, 0), (0, Lp - L), (0, 0), (0, 0), (0, 0)))
        return a.reshape(B, nb, Q_BLOCK, d, H, E)

    def with_prev(a):
        prev = jnp.concatenate([jnp.zeros_like(a[:, :1]), a[:, :-1]], axis=1)
        return jnp.concatenate([prev, a], axis=2)

    qs = to_streams(q)
    kk = with_prev(to_streams(k))
    vv = with_prev(to_streams(v))

    s = jnp.einsum("bnqrhe,bnkrhe->bnrhqk", qs, kk, preferred_element_type=jnp.float32)
    qi = jnp.arange(Q_BLOCK)[:, None]
    ki = jnp.arange(2 * Q_BLOCK)[None, :]
    delta = qi + Q_BLOCK - ki
    band = (delta >= 0) & (delta <= n_back)
    key_pos = jnp.arange(nb)[:, None] * Q_BLOCK - Q_BLOCK + jnp.arange(2 * Q_BLOCK)[None, :]
    key_ok = key_pos >= 0
    mask = band[None, :, :] & key_ok[:, None, :]
    dist = (delta * d).astype(jnp.float32)
    bias = -slopes[:, None, None] * dist[None]
    s = s + bias[None, None, None]
    s = jnp.where(mask[None, :, None, None], s, -jnp.inf)
    lse = jax.nn.logsumexp(s, axis=-1, keepdims=True)
    p = jnp.exp(s - lse)
    o = jnp.einsum("bnrhqk,bnkrhe->bnqrhe", p.astype(vv.dtype), vv,
                   preferred_element_type=jnp.float32)
    o = o.reshape(B, Lp, d, H, E)[:, :L].reshape(B, S, H, E)
    lse = jnp.transpose(lse[..., 0], (0, 1, 4, 2, 3))
    lse = lse.reshape(B, Lp, d, H)[:, :L].reshape(B, S, H)
    return o, lse


def _fwd_setup_inputs(seed: int = 0) -> dict:
    key = jax.random.key(seed)
    ks = jax.random.split(key, 16)
    f32 = jnp.float32
    n_in = 2 * CONV_WIDTH_CH + 3 * ATTN_WIDTH
    nrm = lambda k, shape, fan: jax.random.normal(k, shape, f32) * (fan ** -0.5)
    gain = lambda k, n: 1.0 + 0.02 * jax.random.normal(k, (n,), f32)
    small = lambda k, n: 0.02 * jax.random.normal(k, (n,), f32)
    return {
        "x": jax.random.normal(ks[0], (BATCH, SEQ, D_MODEL), f32),
        "norm1_g": gain(ks[1], D_MODEL),
        "w_in": nrm(ks[2], (D_MODEL, n_in), D_MODEL),
        "conv_w": nrm(ks[3], (CONV_K, CONV_WIDTH_CH), CONV_K),
        "conv_b": small(ks[4], CONV_WIDTH_CH),
        "cn_g": gain(ks[5], CONV_WIDTH_CH),
        "cn_b": small(ks[6], CONV_WIDTH_CH),
        "q_norm_g": gain(ks[7], HEAD_DIM),
        "k_norm_g": gain(ks[8], HEAD_DIM),
        "w_out": nrm(ks[9], (MIX_WIDTH, D_MODEL), MIX_WIDTH),
        "norm2_g": gain(ks[10], D_MODEL),
        "w_up": nrm(ks[11], (D_MODEL, 2 * D_FF), D_MODEL),
        "ffconv_w": nrm(ks[12], (FFN_CONV_K, 2 * D_FF), FFN_CONV_K),
        "ffconv_b": small(ks[13], 2 * D_FF),
        "w_down": nrm(ks[14], (D_FF, D_MODEL), D_FF),
    }


def _fwd_reference(x, norm1_g, w_in, conv_w, conv_b, cn_g, cn_b, q_norm_g, k_norm_g, w_out,
              norm2_g, w_up, ffconv_w, ffconv_b, w_down):
    B, S, _ = x.shape
    slopes = alibi_slopes(N_HEADS)
    for _layer in range(DEPTH):
        h = rms_norm(x, norm1_g)
        proj = h @ w_in
        c = CONV_WIDTH_CH
        a_val, a_gate, q, k, v = jnp.split(
            proj, [c, 2 * c, 2 * c + ATTN_WIDTH, 2 * c + 2 * ATTN_WIDTH], axis=-1)

        u = a_val * jax.nn.sigmoid(a_gate)
        u = causal_depthwise_conv(u, conv_w, conv_b)
        u = jax.nn.silu(layer_norm(u, cn_g, cn_b))

        q = rms_norm(q.reshape(B, S, N_HEADS, HEAD_DIM), q_norm_g) * (HEAD_DIM ** -0.5)
        k = rms_norm(k.reshape(B, S, N_HEADS, HEAD_DIM), k_norm_g)
        v = v.reshape(B, S, N_HEADS, HEAD_DIM)
        outs, lses = [], []
        for window, dilation in DILATED_PATTERNS:
            o_i, lse_i = dilated_window_attention(q, k, v, slopes, window, dilation)
            outs.append(o_i)
            lses.append(lse_i)
        wts = jax.nn.softmax(jnp.stack(lses, axis=0), axis=0)
        o = jnp.sum(wts[..., None] * jnp.stack(outs, axis=0), axis=0)
        o = o.astype(x.dtype).reshape(B, S, ATTN_WIDTH)

        x = x + jnp.concatenate([u, o], axis=-1) @ w_out

        h2 = rms_norm(x, norm2_g)
        up = causal_depthwise_conv(h2 @ w_up, ffconv_w, ffconv_b)
        gate, val = jnp.split(up, 2, axis=-1)
        x = x + (jax.nn.silu(gate) * val) @ w_down
    return x


import jax as _jax
import jax.numpy as _jnp

TWIN_FORMAT = 'train_step'
FWD_PARAMS = ['x', 'norm1_g', 'w_in', 'conv_w', 'conv_b', 'cn_g', 'cn_b', 'q_norm_g', 'k_norm_g', 'w_out', 'norm2_g', 'w_up', 'ffconv_w', 'ffconv_b', 'w_down']
TWIN_WEIGHTS = ['norm1_g', 'w_in', 'conv_w', 'conv_b', 'cn_g', 'cn_b', 'q_norm_g', 'k_norm_g', 'w_out', 'norm2_g', 'w_up', 'ffconv_w', 'ffconv_b', 'w_down']
TWIN_DIFF_INPUT = 'x'
TWIN_INPUTS = ['x', 'norm1_g', 'w_in', 'conv_w', 'conv_b', 'cn_g', 'cn_b', 'q_norm_g', 'k_norm_g', 'w_out', 'norm2_g', 'w_up', 'ffconv_w', 'ffconv_b', 'w_down', 'loss_target', 'm_norm1_g', 'm_w_in', 'm_conv_w', 'm_conv_b', 'm_cn_g', 'm_cn_b', 'm_q_norm_g', 'm_k_norm_g', 'm_w_out', 'm_norm2_g', 'm_w_up', 'm_ffconv_w', 'm_ffconv_b', 'm_w_down', 'v_norm1_g', 'v_w_in', 'v_conv_w', 'v_conv_b', 'v_cn_g', 'v_cn_b', 'v_q_norm_g', 'v_k_norm_g', 'v_w_out', 'v_norm2_g', 'v_w_up', 'v_ffconv_w', 'v_ffconv_b', 'v_w_down']
TWIN_OUTPUTS = ['loss', 'grad_x', 'grad_norm1_g', 'grad_w_in', 'grad_conv_w', 'grad_conv_b', 'grad_cn_g', 'grad_cn_b', 'grad_q_norm_g', 'grad_k_norm_g', 'grad_w_out', 'grad_norm2_g', 'grad_w_up', 'grad_ffconv_w', 'grad_ffconv_b', 'grad_w_down', 'delta_norm1_g', 'delta_w_in', 'delta_conv_w', 'delta_conv_b', 'delta_cn_g', 'delta_cn_b', 'delta_q_norm_g', 'delta_k_norm_g', 'delta_w_out', 'delta_norm2_g', 'delta_w_up', 'delta_ffconv_w', 'delta_ffconv_b', 'delta_w_down', 'new_m_norm1_g', 'new_m_w_in', 'new_m_conv_w', 'new_m_conv_b', 'new_m_cn_g', 'new_m_cn_b', 'new_m_q_norm_g', 'new_m_k_norm_g', 'new_m_w_out', 'new_m_norm2_g', 'new_m_w_up', 'new_m_ffconv_w', 'new_m_ffconv_b', 'new_m_w_down', 'new_v_norm1_g', 'new_v_w_in', 'new_v_conv_w', 'new_v_conv_b', 'new_v_cn_g', 'new_v_cn_b', 'new_v_q_norm_g', 'new_v_k_norm_g', 'new_v_w_out', 'new_v_norm2_g', 'new_v_w_up', 'new_v_ffconv_w', 'new_v_ffconv_b', 'new_v_w_down']
TWIN_LEAF_KINDS = {'loss': 'loss', 'grad_x': 'grad_x', 'grad_norm1_g': 'grad_w', 'grad_w_in': 'grad_w', 'grad_conv_w': 'grad_w', 'grad_conv_b': 'grad_w', 'grad_cn_g': 'grad_w', 'grad_cn_b': 'grad_w', 'grad_q_norm_g': 'grad_w', 'grad_k_norm_g': 'grad_w', 'grad_w_out': 'grad_w', 'grad_norm2_g': 'grad_w', 'grad_w_up': 'grad_w', 'grad_ffconv_w': 'grad_w', 'grad_ffconv_b': 'grad_w', 'grad_w_down': 'grad_w', 'delta_norm1_g': 'delta_w', 'delta_w_in': 'delta_w', 'delta_conv_w': 'delta_w', 'delta_conv_b': 'delta_w', 'delta_cn_g': 'delta_w', 'delta_cn_b': 'delta_w', 'delta_q_norm_g': 'delta_w', 'delta_k_norm_g': 'delta_w', 'delta_w_out': 'delta_w', 'delta_norm2_g': 'delta_w', 'delta_w_up': 'delta_w', 'delta_ffconv_w': 'delta_w', 'delta_ffconv_b': 'delta_w', 'delta_w_down': 'delta_w', 'new_m_norm1_g': 'new_m', 'new_m_w_in': 'new_m', 'new_m_conv_w': 'new_m', 'new_m_conv_b': 'new_m', 'new_m_cn_g': 'new_m', 'new_m_cn_b': 'new_m', 'new_m_q_norm_g': 'new_m', 'new_m_k_norm_g': 'new_m', 'new_m_w_out': 'new_m', 'new_m_norm2_g': 'new_m', 'new_m_w_up': 'new_m', 'new_m_ffconv_w': 'new_m', 'new_m_ffconv_b': 'new_m', 'new_m_w_down': 'new_m', 'new_v_norm1_g': 'new_v', 'new_v_w_in': 'new_v', 'new_v_conv_w': 'new_v', 'new_v_conv_b': 'new_v', 'new_v_cn_g': 'new_v', 'new_v_cn_b': 'new_v', 'new_v_q_norm_g': 'new_v', 'new_v_k_norm_g': 'new_v', 'new_v_w_out': 'new_v', 'new_v_norm2_g': 'new_v', 'new_v_w_up': 'new_v', 'new_v_ffconv_w': 'new_v', 'new_v_ffconv_b': 'new_v', 'new_v_w_down': 'new_v'}


def _forward(args):
    return _fwd_reference(*[args[k] for k in FWD_PARAMS])


def _output_shape():
    out = _jax.eval_shape(lambda: _forward(_fwd_setup_inputs(0)))
    return out.shape, out.dtype

N_MICROBATCH = 1
ADAM_LR = 0.001
ADAM_B1 = 0.9
ADAM_B2 = 0.999
ADAM_EPS = 1e-08
ADAM_WD = 0.01
ADAM_STEP = 10
PER_EXAMPLE_BATCH_AXIS = {'x': 0, 'loss_target': 0}
SHARED_INPUTS = []
_WEIGHT_DTYPES = {'norm1_g': _jnp.float32, 'w_in': _jnp.float32, 'conv_w': _jnp.float32, 'conv_b': _jnp.float32, 'cn_g': _jnp.float32, 'cn_b': _jnp.float32, 'q_norm_g': _jnp.float32, 'k_norm_g': _jnp.float32, 'w_out': _jnp.float32, 'norm2_g': _jnp.float32, 'w_up': _jnp.float32, 'ffconv_w': _jnp.float32, 'ffconv_b': _jnp.float32, 'w_down': _jnp.float32}
MOMENT_SCALE = {'norm1_g': 3.089696e+00, 'w_in': 3.198511e-01, 'conv_w': 7.715784e-01, 'conv_b': 1.812585e+01, 'cn_g': 2.885011e+01, 'cn_b': 2.103534e+01, 'q_norm_g': 1.166689e+01, 'k_norm_g': 1.159118e+01, 'w_out': 2.572193e+00, 'norm2_g': 5.062024e+01, 'w_up': 8.551358e-01, 'ffconv_w': 7.223842e+00, 'ffconv_b': 6.489198e+00, 'w_down': 5.277681e-01}


def _to_microbatches(a, axis):
    t = _jnp.moveaxis(a, axis, 0)
    t = t.reshape((N_MICROBATCH, t.shape[0] // N_MICROBATCH) + t.shape[1:])
    return _jnp.moveaxis(t, 1, axis + 1)


def setup_inputs(seed: int = 0) -> dict:
    inp = _fwd_setup_inputs(seed)
    key = _jax.random.fold_in(_jax.random.key(seed), 7919)
    shape, _ = _output_shape()
    out = dict(inp)
    out["loss_target"] = _jax.random.normal(_jax.random.fold_in(key, 0), shape, _jnp.float32)
    for i, name in enumerate(TWIN_WEIGHTS):
        w = inp[name].astype(_jnp.float32)
        if MOMENT_SCALE is None:
            s = _jnp.sqrt(_jnp.mean(_jnp.square(w)) + 1e-30)
        else:
            s = MOMENT_SCALE[name]
        km, kv = _jax.random.split(_jax.random.fold_in(key, i + 1))
        out[name] = w
        out["m_" + name] = s * _jax.random.normal(km, w.shape, _jnp.float32)
        out["v_" + name] = (s * s) * _jax.random.uniform(kv, w.shape, _jnp.float32, 0.5, 1.5)
    if N_MICROBATCH > 1:
        for name, axis in PER_EXAMPLE_BATCH_AXIS.items():
            out[name] = _to_microbatches(out[name], axis)
    return {'x': out['x'], 'norm1_g': out['norm1_g'], 'w_in': out['w_in'], 'conv_w': out['conv_w'], 'conv_b': out['conv_b'], 'cn_g': out['cn_g'], 'cn_b': out['cn_b'], 'q_norm_g': out['q_norm_g'], 'k_norm_g': out['k_norm_g'], 'w_out': out['w_out'], 'norm2_g': out['norm2_g'], 'w_up': out['w_up'], 'ffconv_w': out['ffconv_w'], 'ffconv_b': out['ffconv_b'], 'w_down': out['w_down'], 'loss_target': out['loss_target'], 'm_norm1_g': out['m_norm1_g'], 'm_w_in': out['m_w_in'], 'm_conv_w': out['m_conv_w'], 'm_conv_b': out['m_conv_b'], 'm_cn_g': out['m_cn_g'], 'm_cn_b': out['m_cn_b'], 'm_q_norm_g': out['m_q_norm_g'], 'm_k_norm_g': out['m_k_norm_g'], 'm_w_out': out['m_w_out'], 'm_norm2_g': out['m_norm2_g'], 'm_w_up': out['m_w_up'], 'm_ffconv_w': out['m_ffconv_w'], 'm_ffconv_b': out['m_ffconv_b'], 'm_w_down': out['m_w_down'], 'v_norm1_g': out['v_norm1_g'], 'v_w_in': out['v_w_in'], 'v_conv_w': out['v_conv_w'], 'v_conv_b': out['v_conv_b'], 'v_cn_g': out['v_cn_g'], 'v_cn_b': out['v_cn_b'], 'v_q_norm_g': out['v_q_norm_g'], 'v_k_norm_g': out['v_k_norm_g'], 'v_w_out': out['v_w_out'], 'v_norm2_g': out['v_norm2_g'], 'v_w_up': out['v_w_up'], 'v_ffconv_w': out['v_ffconv_w'], 'v_ffconv_b': out['v_ffconv_b'], 'v_w_down': out['v_w_down']}


def _loss(weights, diff, rest, loss_target):
    with _jax.named_scope("forward"):
        args = {**rest, TWIN_DIFF_INPUT: diff, **{k: w.astype(_WEIGHT_DTYPES[k]) for k, w in weights.items()}}
        y = _forward(args)
    with _jax.named_scope("loss_head"):
        err = _jnp.square(y.astype(_jnp.float32) - loss_target)
        return 0.5 * _jnp.sum(_jnp.mean(err, axis=-1)) if err.ndim else 0.5 * err


def _adamw(w, g, m, v):
    m = ADAM_B1 * m + (1.0 - ADAM_B1) * g
    v = ADAM_B2 * v + (1.0 - ADAM_B2) * _jnp.square(g)
    m_hat = m / (1.0 - ADAM_B1 ** ADAM_STEP)
    v_hat = v / (1.0 - ADAM_B2 ** ADAM_STEP)
    delta = -ADAM_LR * (m_hat / (_jnp.sqrt(v_hat) + ADAM_EPS) + ADAM_WD * w)
    return delta, m, v


def reference(x, norm1_g, w_in, conv_w, conv_b, cn_g, cn_b, q_norm_g, k_norm_g, w_out, norm2_g, w_up, ffconv_w, ffconv_b, w_down, loss_target, m_norm1_g, m_w_in, m_conv_w, m_conv_b, m_cn_g, m_cn_b, m_q_norm_g, m_k_norm_g, m_w_out, m_norm2_g, m_w_up, m_ffconv_w, m_ffconv_b, m_w_down, v_norm1_g, v_w_in, v_conv_w, v_conv_b, v_cn_g, v_cn_b, v_q_norm_g, v_k_norm_g, v_w_out, v_norm2_g, v_w_up, v_ffconv_w, v_ffconv_b, v_w_down):
    given = dict(x=x, norm1_g=norm1_g, w_in=w_in, conv_w=conv_w, conv_b=conv_b, cn_g=cn_g, cn_b=cn_b, q_norm_g=q_norm_g, k_norm_g=k_norm_g, w_out=w_out, norm2_g=norm2_g, w_up=w_up, ffconv_w=ffconv_w, ffconv_b=ffconv_b, w_down=w_down, loss_target=loss_target, m_norm1_g=m_norm1_g, m_w_in=m_w_in, m_conv_w=m_conv_w, m_conv_b=m_conv_b, m_cn_g=m_cn_g, m_cn_b=m_cn_b, m_q_norm_g=m_q_norm_g, m_k_norm_g=m_k_norm_g, m_w_out=m_w_out, m_norm2_g=m_norm2_g, m_w_up=m_w_up, m_ffconv_w=m_ffconv_w, m_ffconv_b=m_ffconv_b, m_w_down=m_w_down, v_norm1_g=v_norm1_g, v_w_in=v_w_in, v_conv_w=v_conv_w, v_conv_b=v_conv_b, v_cn_g=v_cn_g, v_cn_b=v_cn_b, v_q_norm_g=v_q_norm_g, v_k_norm_g=v_k_norm_g, v_w_out=v_w_out, v_norm2_g=v_norm2_g, v_w_up=v_w_up, v_ffconv_w=v_ffconv_w, v_ffconv_b=v_ffconv_b, v_w_down=v_w_down)
    weights = {n: given[n] for n in TWIN_WEIGHTS}
    shared = {n: given[n] for n in SHARED_INPUTS}
    per_example = {n: given[n] for n in ['x']}
    grad_fn = _jax.value_and_grad(_loss, argnums=(0, 1))

    def one_microbatch(ex, loss_target):
        ex = dict(ex)
        diff = ex.pop(TWIN_DIFF_INPUT)
        return grad_fn(weights, diff, {**shared, **ex}, loss_target)

    if N_MICROBATCH == 1:
        loss, (grad_w, grad_x) = one_microbatch(per_example, given["loss_target"])
    else:
        def body(carry, xs):
            loss_sum, grad_sum = carry
            l_k, (gw_k, gx_k) = one_microbatch(xs[0], xs[1])
            with _jax.named_scope("update"):
                return (loss_sum + l_k, _jax.tree.map(_jnp.add, grad_sum, gw_k)), gx_k

        init = (_jnp.zeros((), _jnp.float32), _jax.tree.map(_jnp.zeros_like, weights))
        (loss, grad_w), grad_x = _jax.lax.scan(body, init, (per_example, given["loss_target"]))
    with _jax.named_scope("update"):
        delta_w, new_m, new_v = {}, {}, {}
        for n in TWIN_WEIGHTS:
            delta_w[n], new_m[n], new_v[n] = _adamw(weights[n], grad_w[n], given["m_" + n], given["v_" + n])
    return (loss, grad_x, *[grad_w[n] for n in TWIN_WEIGHTS], *[delta_w[n] for n in TWIN_WEIGHTS],
            *[new_m[n] for n in TWIN_WEIGHTS], *[new_v[n] for n in TWIN_WEIGHTS])
```

```python
import functools

import jax
import jax.numpy as jnp
from jax import lax
from jax.experimental import pallas as pl
from jax.experimental.pallas import tpu as pltpu

F32, BF16 = jnp.float32, jnp.bfloat16
N_DEV = 8
N_HEADS, HEAD_DIM = 8, 64
CONV_CH = 512
ATTN_W = N_HEADS * HEAD_DIM
CONV_K, FF_K = 31, 3
CONV_HALO = 32
FF_HALO = 8
PATTERN_DILATIONS = (1, 4, 16)
Q_BLOCK = 128
EPS = 1e-6
NEG = -1e30
ADAM_LR, ADAM_B1, ADAM_B2, ADAM_EPS, ADAM_WD, ADAM_STEP = 0.001, 0.9, 0.999, 1e-08, 0.01, 10
MESH_AXES = ("x", "y", "c")
VMEM_BIG = 56 * 1024 * 1024
ANY = pl.BlockSpec(memory_space=pl.ANY)


def _params(sem=None, vmem=None):
    return pltpu.CompilerParams(dimension_semantics=sem, vmem_limit_bytes=vmem)


def _dot(a, b, trans_a=False, trans_b=False):
    dims = (((0 if trans_a else 1,), (1 if trans_b else 0,)), ((), ()))
    return lax.dot_general(a, b, dims, preferred_element_type=F32)


def _sigmoid(z):
    return 1.0 / (1.0 + jnp.exp(-z))


def _tile(n, pref, mult):
    if n <= pref:
        return n
    t = (pref // mult) * mult
    while n % t:
        t -= mult
    return t


def _head_sum(v, ones_ref):
    hi = v.astype(BF16)
    lo = (v - hi.astype(F32)).astype(BF16)
    ones = ones_ref[...]
    return _dot(hi, ones) + _dot(lo, ones)


def _linear(p):
    return 4 * p[0] + 2 * p[1] + p[2]


def _all_gather(name, arrs):
    na = len(arrs)

    def body(*refs):
        ins, outs = refs[:na], refs[na:2 * na]
        send_sems, recv_sems, local_sems = refs[2 * na:]
        x, y, c = lax.axis_index("x"), lax.axis_index("y"), lax.axis_index("c")
        me, sib = (x, y, c), (x, y, 1 - c)
        chips = [(1 - x, y), (x, 1 - y), (1 - x, 1 - y)]

        def copy(a, k, block, to, src=None):
            dst = outs[a].at[_linear(block)]
            return pltpu.make_async_remote_copy(
                src_ref=dst if src is None else src, dst_ref=dst,
                send_sem=send_sems.at[7 * a + k], recv_sem=recv_sems.at[7 * a + k],
                device_id=to, device_id_type=pl.DeviceIdType.MESH)

        mine = [pltpu.make_async_copy(ins[a], outs[a].at[_linear(me)], local_sems.at[a]) for a in range(na)]
        for cp in mine:
            cp.start()
        first = []
        for a in range(na):
            first.append(copy(a, 0, me, sib, src=ins[a]))
            first += [copy(a, 1 + j, me, (*chip, c), src=ins[a]) for j, chip in enumerate(chips)]
        for cp in first:
            cp.start()
        passed = []
        for j, chip in enumerate(chips):
            for a in range(na):
                copy(a, 1 + j, (*chip, c), me).wait_recv()
                fwd = copy(a, 4 + j, (*chip, c), sib)
                fwd.start()
                passed.append(fwd)
        for a in range(na):
            copy(a, 0, sib, me).wait_recv()
            for j, chip in enumerate(chips):
                copy(a, 4 + j, (*chip, 1 - c), me).wait_recv()
        for cp in first + passed:
            cp.wait_send()
        for cp in mine:
            cp.wait()

    return pl.pallas_call(
        body, name=name,
        out_shape=[jax.ShapeDtypeStruct((N_DEV,) + a.shape, a.dtype) for a in arrs],
        in_specs=[ANY] * na, out_specs=[ANY] * na,
        scratch_shapes=[pltpu.SemaphoreType.DMA((7 * na,)), pltpu.SemaphoreType.DMA((7 * na,)),
                        pltpu.SemaphoreType.DMA((na,))],
    )(*arrs)


def _exchange(name, arrs):
    na = len(arrs)

    def body(*refs):
        ins, outs = refs[:na], refs[na:2 * na]
        send_sems, recv_sems, local_sems = refs[2 * na:]
        x, y, c = lax.axis_index("x"), lax.axis_index("y"), lax.axis_index("c")
        me = _linear((x, y, c))
        mine = [pltpu.make_async_copy(ins[a].at[me], outs[a].at[me], local_sems.at[a]) for a in range(na)]
        for cp in mine:
            cp.start()
        sends, recvs = [], []
        for p in range(1, N_DEV):
            peer = (1 - x if p & 4 else x, 1 - y if p & 2 else y, 1 - c if p & 1 else c)
            for a in range(na):
                sems = dict(send_sem=send_sems.at[7 * a + p - 1], recv_sem=recv_sems.at[7 * a + p - 1],
                            device_id=peer, device_id_type=pl.DeviceIdType.MESH)
                sends.append(pltpu.make_async_remote_copy(
                    src_ref=ins[a].at[_linear(peer)], dst_ref=outs[a].at[me], **sems))
                recvs.append(pltpu.make_async_remote_copy(
                    src_ref=ins[a].at[_linear(peer)], dst_ref=outs[a].at[_linear(peer)], **sems))
        for cp in sends:
            cp.start()
        for cp in recvs:
            cp.wait_recv()
        for cp in sends:
            cp.wait_send()
        for cp in mine:
            cp.wait()

    return pl.pallas_call(
        body, name=name,
        out_shape=[jax.ShapeDtypeStruct(a.shape, a.dtype) for a in arrs],
        in_specs=[ANY] * na, out_specs=[ANY] * na,
        scratch_shapes=[pltpu.SemaphoreType.DMA((7 * na,)), pltpu.SemaphoreType.DMA((7 * na,)),
                        pltpu.SemaphoreType.DMA((na,))],
    )(*arrs)


def _mm(name, a, w, trans_b, out_dtype, tm, tn):
    m, k = a.shape
    n = w.shape[0] if trans_b else w.shape[1]

    def body(a_ref, w_ref, o_ref):
        o_ref[...] = _dot(a_ref[...], w_ref[...], trans_b=trans_b).astype(o_ref.dtype)

    w_spec = pl.BlockSpec((tn, k), lambda j, i: (j, 0)) if trans_b else pl.BlockSpec((k, tn), lambda j, i: (0, j))
    return pl.pallas_call(
        body, name=name, grid=(n // tn, m // tm),
        in_specs=[pl.BlockSpec((tm, k), lambda j, i: (i, 0)), w_spec],
        out_specs=pl.BlockSpec((tm, tn), lambda j, i: (i, j)),
        out_shape=jax.ShapeDtypeStruct((m, n), out_dtype),
        compiler_params=_params(("parallel", "parallel"), VMEM_BIG),
    )(a, w)


def _mm_tn(name, parts, b, tn, tk):
    s, d = b.shape
    blocks = [p.shape[1] // tn for p in parts]
    starts = [sum(blocks[:i]) for i in range(len(parts))]
    np_ = len(parts)

    def body(*refs):
        a_refs, b_ref, o_ref = refs[:np_], refs[np_], refs[np_ + 1]
        j, kk = pl.program_id(0), pl.program_id(1)

        @pl.when(kk == 0)
        def _():
            o_ref[...] = jnp.zeros_like(o_ref)

        for p in range(np_):
            @pl.when((j >= starts[p]) & (j < starts[p] + blocks[p]))
            def _(p=p):
                o_ref[...] += _dot(a_refs[p][...], b_ref[...], trans_a=True)

    def a_spec(p):
        return pl.BlockSpec((tk, tn), lambda j, kk: (kk, jnp.clip(j - starts[p], 0, blocks[p] - 1)))

    return pl.pallas_call(
        body, name=name, grid=(sum(blocks), s // tk),
        in_specs=[a_spec(p) for p in range(np_)] + [pl.BlockSpec((tk, d), lambda j, kk: (kk, 0))],
        out_specs=pl.BlockSpec((tn, d), lambda j, kk: (j, 0)),
        out_shape=jax.ShapeDtypeStruct((sum(blocks) * tn, d), F32),
        compiler_params=_params(("parallel", "arbitrary"), VMEM_BIG),
    )(*parts, b)


def _proj_fwd(x, g1, w_in_t, qg, kg, ones):
    s, d = x.shape
    tm = _tile(s, 256, 8)
    c2, aw = 2 * CONV_CH, ATTN_W

    def body(x_ref, g_ref, w_ref, qg_ref, kg_ref, ones_ref, h_ref, a_ref, qk_ref, qn_ref, kn_ref, v_ref):
        xv = x_ref[...]
        r = lax.rsqrt(jnp.mean(xv * xv, axis=-1, keepdims=True) + EPS)
        h = (xv * r * g_ref[...]).astype(BF16)
        h_ref[...] = h
        proj = _dot(h, w_ref[...], trans_b=True)
        a_ref[...] = proj[:, :c2]
        qk_ref[...] = proj[:, c2:c2 + 2 * aw]
        v_ref[...] = proj[:, c2 + 2 * aw:].astype(BF16)
        q, k = proj[:, c2:c2 + aw], proj[:, c2 + aw:c2 + 2 * aw]
        rq = lax.rsqrt(_head_sum(q * q, ones_ref) * (1.0 / HEAD_DIM) + EPS)
        rk = lax.rsqrt(_head_sum(k * k, ones_ref) * (1.0 / HEAD_DIM) + EPS)
        qn_ref[...] = (q * rq * qg_ref[...] * (HEAD_DIM ** -0.5)).astype(BF16)
        kn_ref[...] = (k * rk * kg_ref[...]).astype(BF16)

    row = lambda w: pl.BlockSpec((tm, w), lambda i: (i, 0))
    full = lambda a: pl.BlockSpec(a.shape, lambda i: (0, 0))
    return pl.pallas_call(
        body, name="proj_fwd", grid=(s // tm,),
        in_specs=[row(d), full(g1), full(w_in_t), full(qg), full(kg), full(ones)],
        out_specs=[row(d), row(c2), row(2 * aw), row(aw), row(aw), row(aw)],
        out_shape=[jax.ShapeDtypeStruct((s, d), BF16), jax.ShapeDtypeStruct((s, c2), F32),
                   jax.ShapeDtypeStruct((s, 2 * aw), F32), jax.ShapeDtypeStruct((s, aw), BF16),
                   jax.ShapeDtypeStruct((s, aw), BF16), jax.ShapeDtypeStruct((s, aw), BF16)],
        compiler_params=_params(("parallel",), VMEM_BIG),
    )(x, g1, w_in_t, qg, kg, ones)


def _mix_out(u, o, w_out, x, g2):
    s, d = x.shape
    tm = _tile(s, 512, 8)

    def body(u_ref, o_ref, w_ref, x_ref, g_ref, x1_ref, h2_ref):
        x1 = x_ref[...] + _dot(u_ref[...], w_ref[:CONV_CH, :]) + _dot(o_ref[...], w_ref[CONV_CH:, :])
        x1_ref[...] = x1
        r = lax.rsqrt(jnp.mean(x1 * x1, axis=-1, keepdims=True) + EPS)
        h2_ref[...] = (x1 * r * g_ref[...]).astype(BF16)

    row = lambda w: pl.BlockSpec((tm, w), lambda i: (i, 0))
    full = lambda a: pl.BlockSpec(a.shape, lambda i: (0, 0))
    return pl.pallas_call(
        body, name="mix_out", grid=(s // tm,),
        in_specs=[row(CONV_CH), row(ATTN_W), full(w_out), row(d), full(g2)],
        out_specs=[row(d), row(d)],
        out_shape=[jax.ShapeDtypeStruct((s, d), F32), jax.ShapeDtypeStruct((s, d), BF16)],
        compiler_params=_params(("parallel",), VMEM_BIG),
    )(u, o, w_out, x, g2)


def _up_fwd(h2, w_up_t, f):
    s, d = h2.shape
    tm, tn = _tile(s, 512, 8), _tile(f, 1408, 128)
    nb = f // tn

    def body(h_ref, wg_ref, wv_ref, g_ref, v_ref):
        hv = h_ref[...]
        g_ref[...] = _dot(hv, wg_ref[...], trans_b=True)
        v_ref[...] = _dot(hv, wv_ref[...], trans_b=True)

    return pl.pallas_call(
        body, name="up_fwd", grid=(nb, s // tm),
        in_specs=[pl.BlockSpec((tm, d), lambda j, i: (i, 0)), pl.BlockSpec((tn, d), lambda j, i: (j, 0)),
                  pl.BlockSpec((tn, d), lambda j, i: (j + nb, 0))],
        out_specs=[pl.BlockSpec((tm, tn), lambda j, i: (i, j))] * 2,
        out_shape=[jax.ShapeDtypeStruct((s, f), F32)] * 2,
        compiler_params=_params(("parallel", "parallel"), VMEM_BIG),
    )(h2, w_up_t, w_up_t)


def _down_loss(act, w_down, x1, target):
    s, d = x1.shape
    f = act.shape[1]
    tm = _tile(s, 256, 8)

    def body(a_ref, w_ref, x1_ref, t_ref, loss_ref, dyf_ref, dyb_ref):
        @pl.when(pl.program_id(0) == 0)
        def _():
            loss_ref[...] = jnp.zeros_like(loss_ref)

        diff = x1_ref[...] + _dot(a_ref[...], w_ref[...]) - t_ref[...]
        sq = jnp.sum(jnp.sum(diff * diff, axis=1, keepdims=True), axis=0, keepdims=True)
        loss_ref[...] += jnp.broadcast_to(sq * (0.5 / d), loss_ref.shape)
        dy = diff * (1.0 / d)
        dyf_ref[...] = dy
        dyb_ref[...] = dy.astype(BF16)

    row = lambda w: pl.BlockSpec((tm, w), lambda i: (i, 0))
    return pl.pallas_call(
        body, name="down_loss", grid=(s // tm,),
        in_specs=[row(f), pl.BlockSpec((f, d), lambda i: (0, 0)), row(d), row(d)],
        out_specs=[pl.BlockSpec((8, 128), lambda i: (0, 0)), row(d), row(d)],
        out_shape=[jax.ShapeDtypeStruct((8, 128), F32), jax.ShapeDtypeStruct((s, d), F32),
                   jax.ShapeDtypeStruct((s, d), BF16)],
        compiler_params=_params(("arbitrary",), VMEM_BIG),
    )(act, w_down, x1, target)


def _norm_bwd_tail(dh, xv, g, resid, d):
    r = lax.rsqrt(jnp.mean(xv * xv, axis=-1, keepdims=True) + EPS)
    xh = xv * r
    gd = dh * g
    dx = r * (gd - xh * (jnp.sum(gd * xh, axis=-1, keepdims=True) * (1.0 / d)))
    return resid + dx, jnp.sum(dh * xh, axis=0, keepdims=True)


def _up_bwd(dg, dv, w_up_t, x1, dyf, g2):
    s, d = x1.shape
    f = dg.shape[1]
    tm = _tile(s, 256, 8)

    def body(dg_ref, dv_ref, w_ref, x1_ref, dy_ref, g_ref, dxf_ref, dxb_ref, gg_ref):
        @pl.when(pl.program_id(0) == 0)
        def _():
            gg_ref[...] = jnp.zeros_like(gg_ref)

        dh = _dot(dg_ref[...], w_ref[:f, :]) + _dot(dv_ref[...], w_ref[f:, :])
        dx, gg = _norm_bwd_tail(dh, x1_ref[...], g_ref[...], dy_ref[...], d)
        dxf_ref[...] = dx
        dxb_ref[...] = dx.astype(BF16)
        gg_ref[...] += gg

    row = lambda w: pl.BlockSpec((tm, w), lambda i: (i, 0))
    full = lambda a: pl.BlockSpec(a.shape, lambda i: (0, 0))
    return pl.pallas_call(
        body, name="up_bwd", grid=(s // tm,),
        in_specs=[row(f), row(f), full(w_up_t), row(d), row(d), full(g2)],
        out_specs=[row(d), row(d), pl.BlockSpec((1, d), lambda i: (0, 0))],
        out_shape=[jax.ShapeDtypeStruct((s, d), F32), jax.ShapeDtypeStruct((s, d), BF16),
                   jax.ShapeDtypeStruct((1, d), F32)],
        compiler_params=_params(("arbitrary",), VMEM_BIG),
    )(dg, dv, w_up_t, x1, dyf, g2)


def _proj_bwd(d_a, d_qkv, w_in_t, x, dx1, g1):
    s, d = x.shape
    na, nq = d_a.shape[1], d_qkv.shape[1]
    tm = _tile(s, 256, 8)

    def body(da_ref, dq_ref, w_ref, x_ref, r_ref, g_ref, gx_ref, gg_ref):
        @pl.when(pl.program_id(0) == 0)
        def _():
            gg_ref[...] = jnp.zeros_like(gg_ref)

        dh = _dot(da_ref[...], w_ref[:na, :]) + _dot(dq_ref[...], w_ref[na:, :])
        dx, gg = _norm_bwd_tail(dh, x_ref[...], g_ref[...], r_ref[...], d)
        gx_ref[...] = dx
        gg_ref[...] += gg

    row = lambda w: pl.BlockSpec((tm, w), lambda i: (i, 0))
    full = lambda a: pl.BlockSpec(a.shape, lambda i: (0, 0))
    return pl.pallas_call(
        body, name="proj_bwd", grid=(s // tm,),
        in_specs=[row(na), row(nq), full(w_in_t), row(d), row(d), full(g1)],
        out_specs=[row(d), pl.BlockSpec((1, d), lambda i: (0, 0))],
        out_shape=[jax.ShapeDtypeStruct((s, d), F32), jax.ShapeDtypeStruct((1, d), F32)],
        compiler_params=_params(("arbitrary",), VMEM_BIG),
    )(d_a, d_qkv, w_in_t, x, dx1, g1)


CONV_CHUNK = 64


def _glu(av):
    return av[:, :CONV_CH] * _sigmoid(av[:, CONV_CH:])


def _layer_norm_stats(u1):
    mu = jnp.mean(u1, axis=-1, keepdims=True)
    cen = u1 - mu
    rstd = lax.rsqrt(jnp.mean(cen * cen, axis=-1, keepdims=True) + EPS)
    return cen * rstd, rstd


def _conv_fwd(a, cw, cb, cg, cbeta):
    s = a.shape[0]
    tm = _tile(s, 256, CONV_CHUNK)
    hb = tm // CONV_HALO

    def body(a_ref, ap_ref, cw_ref, cb_ref, cg_ref, cbeta_ref, u1_ref, u_ref, ubuf):
        i = pl.program_id(0)
        ubuf[0:CONV_HALO, :] = jnp.where(i > 0, _glu(ap_ref[...]), 0.0)
        ubuf[CONV_HALO:, :] = _glu(a_ref[...])
        for c0 in range(0, tm, CONV_CHUNK):
            acc = jnp.broadcast_to(cb_ref[...], (CONV_CHUNK, CONV_CH))
            for k in range(CONV_K):
                acc = acc + cw_ref[k:k + 1, :] * ubuf[c0 + 2 + k:c0 + 2 + k + CONV_CHUNK, :]
            u1_ref[c0:c0 + CONV_CHUNK, :] = acc
        xh, _ = _layer_norm_stats(u1_ref[...])
        z = xh * cg_ref[...] + cbeta_ref[...]
        u_ref[...] = (z * _sigmoid(z)).astype(BF16)

    full = lambda t: pl.BlockSpec(t.shape, lambda i: (0, 0))
    return pl.pallas_call(
        body, name="conv_fwd", grid=(s // tm,),
        in_specs=[pl.BlockSpec((tm, 2 * CONV_CH), lambda i: (i, 0)),
                  pl.BlockSpec((CONV_HALO, 2 * CONV_CH), lambda i: (jnp.maximum(i * hb - 1, 0), 0)),
                  full(cw), full(cb), full(cg), full(cbeta)],
        out_specs=[pl.BlockSpec((tm, CONV_CH), lambda i: (i, 0))] * 2,
        out_shape=[jax.ShapeDtypeStruct((s, CONV_CH), F32), jax.ShapeDtypeStruct((s, CONV_CH), BF16)],
        scratch_shapes=[pltpu.VMEM((CONV_HALO + tm, CONV_CH), F32)],
        compiler_params=_params(("parallel",), VMEM_BIG),
    )(a, a, cw, cb, cg, cbeta)


def _conv_bwd(a, u1, d_cat, cw, cg, cbeta):
    s = a.shape[0]
    tm = _tile(s, 256, CONV_CHUNK)
    hb = tm // CONV_HALO
    last_halo = s // CONV_HALO - 1
    nt = s // tm
    te = tm + CONV_HALO

    def body(a_ref, ap_ref, u1_ref, u1n_ref, du_ref, dun_ref, cw_ref, cg_ref, cbeta_ref,
             da_ref, gw_ref, gb_ref, gg_ref, gbeta_ref, ubuf, dbuf):
        i = pl.program_id(0)

        @pl.when(i == 0)
        def _():
            gw_ref[...] = jnp.zeros_like(gw_ref)
            gb_ref[...] = jnp.zeros_like(gb_ref)
            gg_ref[...] = jnp.zeros_like(gg_ref)
            gbeta_ref[...] = jnp.zeros_like(gbeta_ref)

        def du1_of(u1, du):
            xh, rstd = _layer_norm_stats(u1)
            z = xh * cg_ref[...] + cbeta_ref[...]
            sz = _sigmoid(z)
            dz = du * (sz * (1.0 + z * (1.0 - sz)))
            dxh = dz * cg_ref[...]
            du1 = rstd * (dxh - jnp.mean(dxh, axis=-1, keepdims=True)
                          - xh * jnp.mean(dxh * xh, axis=-1, keepdims=True))
            return du1, dz, xh

        du1, dz, xh = du1_of(u1_ref[...], du_ref[...])
        gg_ref[...] += jnp.sum(dz * xh, axis=0, keepdims=True)
        gbeta_ref[...] += jnp.sum(dz, axis=0, keepdims=True)
        gb_ref[...] += jnp.sum(du1, axis=0, keepdims=True)
        dbuf[0:tm, :] = du1
        du1n, _, _ = du1_of(u1n_ref[...], jnp.where(i < nt - 1, dun_ref[...], 0.0))
        dbuf[tm:, :] = du1n

        av = a_ref[...]
        ubuf[0:CONV_HALO, :] = jnp.where(i > 0, _glu(ap_ref[...]), 0.0)
        sg = _sigmoid(av[:, CONV_CH:])
        ubuf[CONV_HALO:, :] = av[:, :CONV_CH] * sg

        for k in range(CONV_K):
            part = jnp.zeros((1, CONV_CH), F32)
            for c0 in range(0, tm, CONV_CHUNK):
                prod = dbuf[c0:c0 + CONV_CHUNK, :] * ubuf[c0 + 2 + k:c0 + 2 + k + CONV_CHUNK, :]
                part = part + jnp.sum(prod, axis=0, keepdims=True)
            gw_ref[k:k + 1, :] += part

        for c0 in range(0, tm, CONV_CHUNK):
            acc = jnp.zeros((CONV_CHUNK, CONV_CH), F32)
            for k in range(CONV_K):
                off = c0 + CONV_K - 1 - k
                acc = acc + cw_ref[k:k + 1, :] * dbuf[off:off + CONV_CHUNK, :]
            sgc = sg[c0:c0 + CONV_CHUNK, :]
            avc = av[c0:c0 + CONV_CHUNK, :CONV_CH]
            da_ref[c0:c0 + CONV_CHUNK, :CONV_CH] = (acc * sgc).astype(BF16)
            da_ref[c0:c0 + CONV_CHUNK, CONV_CH:] = (acc * avc * sgc * (1.0 - sgc)).astype(BF16)

    full = lambda t: pl.BlockSpec(t.shape, lambda i: (0, 0))
    vec = pl.BlockSpec((1, CONV_CH), lambda i: (0, 0))
    nxt = lambda i: (jnp.minimum((i + 1) * hb, last_halo), 0)
    return pl.pallas_call(
        body, name="conv_bwd", grid=(nt,),
        in_specs=[pl.BlockSpec((tm, 2 * CONV_CH), lambda i: (i, 0)),
                  pl.BlockSpec((CONV_HALO, 2 * CONV_CH), lambda i: (jnp.maximum(i * hb - 1, 0), 0)),
                  pl.BlockSpec((tm, CONV_CH), lambda i: (i, 0)), pl.BlockSpec((CONV_HALO, CONV_CH), nxt),
                  pl.BlockSpec((tm, CONV_CH), lambda i: (i, 0)), pl.BlockSpec((CONV_HALO, CONV_CH), nxt),
                  full(cw), full(cg), full(cbeta)],
        out_specs=[pl.BlockSpec((tm, 2 * CONV_CH), lambda i: (i, 0)),
                   pl.BlockSpec((CONV_HALO, CONV_CH), lambda i: (0, 0)), vec, vec, vec],
        out_shape=[jax.ShapeDtypeStruct((s, 2 * CONV_CH), BF16), jax.ShapeDtypeStruct((CONV_HALO, CONV_CH), F32),
                   jax.ShapeDtypeStruct((1, CONV_CH), F32), jax.ShapeDtypeStruct((1, CONV_CH), F32),
                   jax.ShapeDtypeStruct((1, CONV_CH), F32)],
        scratch_shapes=[pltpu.VMEM((CONV_HALO + tm, CONV_CH), F32), pltpu.VMEM((te, CONV_CH), F32)],
        compiler_params=_params(("arbitrary",), VMEM_BIG),
    )(a, a, u1, u1, d_cat, d_cat, cw, cg, cbeta)


def _ff_tiles(s, f):
    return _tile(s, 256, 8), _tile(f, 1408, 128)


def _ff_conv(pre_buf, w_ref, b_ref, half, rows):
    acc = b_ref[half:half + 1, :]
    for k in range(FF_K):
        off = FF_HALO - (FF_K - 1) + k
        acc = acc + w_ref[k, half:half + 1, :] * pre_buf[half, off:off + rows, :]
    return acc


def _ff_act(up_g, up_v, fw, fb):
    s, f = up_g.shape
    tm, tc = _ff_tiles(s, f)
    hb = tm // FF_HALO

    def body(g_ref, gp_ref, v_ref, vp_ref, w_ref, b_ref, act_ref, buf):
        i = pl.program_id(1)
        for half, (m_ref, p_ref) in enumerate(((g_ref, gp_ref), (v_ref, vp_ref))):
            buf[half, 0:FF_HALO, :] = jnp.where(i > 0, p_ref[...], 0.0)
            buf[half, FF_HALO:, :] = m_ref[...]
        gate = _ff_conv(buf, w_ref, b_ref, 0, tm)
        val = _ff_conv(buf, w_ref, b_ref, 1, tm)
        act_ref[...] = (gate * _sigmoid(gate) * val).astype(BF16)

    main = pl.BlockSpec((tm, tc), lambda j, i: (i, j))
    prev = pl.BlockSpec((FF_HALO, tc), lambda j, i: (jnp.maximum(i * hb - 1, 0), j))
    return pl.pallas_call(
        body, name="ff_act", grid=(f // tc, s // tm),
        in_specs=[main, prev, main, prev, pl.BlockSpec((FF_K, 2, tc), lambda j, i: (0, 0, j)),
                  pl.BlockSpec((2, tc), lambda j, i: (0, j))],
        out_specs=main,
        out_shape=jax.ShapeDtypeStruct((s, f), BF16),
        scratch_shapes=[pltpu.VMEM((2, FF_HALO + tm, tc), F32)],
        compiler_params=_params(("parallel", "parallel"), VMEM_BIG),
    )(up_g, up_g, up_v, up_v, fw, fb)


def _ff_bwd(up_g, up_v, d_act, fw, fb):
    s, f = up_g.shape
    tm, tc = _ff_tiles(s, f)
    hb = tm // FF_HALO
    nt = s // tm
    last_halo = s // FF_HALO - 1
    te = tm + FF_HALO

    def body(g_ref, gp_ref, gn_ref, v_ref, vp_ref, vn_ref, da_ref, dan_ref, w_ref, b_ref,
             dg_ref, dv_ref, gw_ref, gb_ref, buf, dbuf):
        i = pl.program_id(1)

        @pl.when(i == 0)
        def _():
            gw_ref[...] = jnp.zeros_like(gw_ref)
            gb_ref[...] = jnp.zeros_like(gb_ref)

        for half, (m_ref, p_ref, n_ref) in enumerate(((g_ref, gp_ref, gn_ref), (v_ref, vp_ref, vn_ref))):
            buf[half, 0:FF_HALO, :] = jnp.where(i > 0, p_ref[...], 0.0)
            buf[half, FF_HALO:FF_HALO + tm, :] = m_ref[...]
            buf[half, FF_HALO + tm:, :] = n_ref[...]
        gate = _ff_conv(buf, w_ref, b_ref, 0, te)
        val = _ff_conv(buf, w_ref, b_ref, 1, te)
        dan = jnp.where(i < nt - 1, dan_ref[...], 0.0)
        sg_m, sg_n = _sigmoid(gate[:tm]), _sigmoid(gate[tm:])
        for lo, hi, da, sg in ((0, tm, da_ref[...], sg_m), (tm, te, dan, sg_n)):
            gt, vl = gate[lo:hi], val[lo:hi]
            dbuf[0, lo:hi, :] = da * vl * (sg * (1.0 + gt * (1.0 - sg)))
            dbuf[1, lo:hi, :] = da * (gt * sg)
        for half, out_ref in enumerate((dg_ref, dv_ref)):
            dup = dbuf[half, 0:tm, :]
            gb_ref[half:half + 1, :] += jnp.sum(dup, axis=0, keepdims=True)
            acc = jnp.zeros((tm, tc), F32)
            for k in range(FF_K):
                off = FF_HALO - (FF_K - 1) + k
                gw_ref[k, half:half + 1, :] += jnp.sum(dup * buf[half, off:off + tm, :], axis=0, keepdims=True)
                acc = acc + w_ref[k, half:half + 1, :] * dbuf[half, FF_K - 1 - k:FF_K - 1 - k + tm, :]
            out_ref[...] = acc.astype(BF16)

    main = pl.BlockSpec((tm, tc), lambda j, i: (i, j))
    prev = pl.BlockSpec((FF_HALO, tc), lambda j, i: (jnp.maximum(i * hb - 1, 0), j))
    nxt = pl.BlockSpec((FF_HALO, tc), lambda j, i: (jnp.minimum((i + 1) * hb, last_halo), j))
    return pl.pallas_call(
        body, name="ff_bwd", grid=(f // tc, nt),
        in_specs=[main, prev, nxt, main, prev, nxt, main, nxt,
                  pl.BlockSpec((FF_K, 2, tc), lambda j, i: (0, 0, j)), pl.BlockSpec((2, tc), lambda j, i: (0, j))],
        out_specs=[main, main, pl.BlockSpec((FF_K, 2, tc), lambda j, i: (0, 0, j)),
                   pl.BlockSpec((2, tc), lambda j, i: (0, j))],
        out_shape=[jax.ShapeDtypeStruct((s, f), BF16), jax.ShapeDtypeStruct((s, f), BF16),
                   jax.ShapeDtypeStruct((FF_K, 2, f), F32), jax.ShapeDtypeStruct((2, f), F32)],
        scratch_shapes=[pltpu.VMEM((2, FF_HALO + te, tc), F32), pltpu.VMEM((2, te, tc), F32)],
        compiler_params=_params(("parallel", "arbitrary"), VMEM_BIG),
    )(up_g, up_g, up_g, up_v, up_v, up_v, d_act, d_act, fw, fb)


def _attn_geometry(s, d):
    length = s // d
    tq = _tile(length, 1024, Q_BLOCK)
    return length, d * ATTN_W, tq, length // tq, tq // Q_BLOCK


def _band(slope_ref, head_pair, d):
    qi = lax.broadcasted_iota(jnp.int32, (Q_BLOCK, 2 * Q_BLOCK), 0)
    ki = lax.broadcasted_iota(jnp.int32, (Q_BLOCK, 2 * Q_BLOCK), 1)
    delta = qi + Q_BLOCK - ki
    valid = (delta >= 0) & (delta <= Q_BLOCK)
    dist = delta.astype(F32) * float(d)
    return [jnp.where(valid, dist * (-slope_ref[2 * head_pair + hh]), NEG) for hh in range(2)], ki


def _attn_fwd(qn, kn, v, slopes, d, prev):
    s = qn.shape[0]
    length, width, tq, nt, nqb = _attn_geometry(s, d)
    view = lambda t: t.reshape(length, width)
    merge = prev is not None

    def body(*refs):
        sl_ref, q_ref, k_ref, kh_ref, v_ref, vh_ref = refs[:6]
        if merge:
            oa_ref, la_ref = refs[6:8]
        o_ref, l_ref, kbuf, vbuf = refs[-4:]
        cb, i = pl.program_id(0), pl.program_id(1)
        biases, ki = _band(sl_ref, cb % 4, d)
        head0 = lax.broadcasted_iota(jnp.int32, (Q_BLOCK, Q_BLOCK), 1) < HEAD_DIM
        kbuf[0:Q_BLOCK, :] = kh_ref[...]
        kbuf[Q_BLOCK:, :] = k_ref[...]
        vbuf[0:Q_BLOCK, :] = vh_ref[...]
        vbuf[Q_BLOCK:, :] = v_ref[...]

        def q_block(qb, carry):
            r0 = pl.multiple_of(qb * Q_BLOCK, Q_BLOCK)
            rows = pl.ds(r0, Q_BLOCK)
            qv = q_ref[rows, :]
            k2, v2 = kbuf[pl.ds(r0, 2 * Q_BLOCK), :], vbuf[pl.ds(r0, 2 * Q_BLOCK), :]
            first_key = jnp.where((i == 0) & (qb == 0), Q_BLOCK, 0)
            o_h, l_h = [], []
            for hh in range(2):
                mine = head0 if hh == 0 else jnp.logical_not(head0)
                sc = _dot(jnp.where(mine, qv, jnp.zeros_like(qv)), k2, trans_b=True) + biases[hh]
                sc = jnp.where(ki < first_key, NEG, sc)
                mx = jnp.max(sc, axis=1, keepdims=True)
                p = jnp.exp(sc - mx)
                den = jnp.sum(p, axis=1, keepdims=True)
                o_h.append(_dot(p.astype(BF16), v2) * (1.0 / den))
                l_h.append(jnp.broadcast_to(mx + jnp.log(den), (Q_BLOCK, Q_BLOCK)))
            o_new = jnp.where(head0, o_h[0], o_h[1])
            l_new = jnp.where(head0, l_h[0], l_h[1])
            if merge:
                oa, la = oa_ref[rows, :], la_ref[rows, :]
                mm = jnp.maximum(la, l_new)
                wa, wn = jnp.exp(la - mm), jnp.exp(l_new - mm)
                o_new = (wa * oa + wn * o_new) / (wa + wn)
                l_new = mm + jnp.log(wa + wn)
            o_ref[rows, :] = o_new
            l_ref[rows, :] = l_new
            return carry

        lax.fori_loop(0, nqb, q_block, 0)

    main = pl.BlockSpec((tq, Q_BLOCK), lambda cb, i: (i, cb))
    halo = pl.BlockSpec((Q_BLOCK, Q_BLOCK), lambda cb, i: (jnp.maximum(i * nqb - 1, 0), cb))
    ins = [slopes, view(qn), view(kn), view(kn), view(v), view(v)] + ([view(prev[0]), view(prev[1])] if merge else [])
    o, lse = pl.pallas_call(
        body, name=f"attn_fwd_d{d}", grid=(width // Q_BLOCK, nt),
        in_specs=[pl.BlockSpec(memory_space=pltpu.SMEM), main, main, halo, main, halo] + ([main, main] if merge else []),
        out_specs=[main, main],
        out_shape=[jax.ShapeDtypeStruct((length, width), F32)] * 2,
        scratch_shapes=[pltpu.VMEM((Q_BLOCK + tq, Q_BLOCK), BF16)] * 2,
        compiler_params=_params(("parallel", "parallel"), VMEM_BIG),
    )(*ins)
    return o.reshape(s, ATTN_W), lse.reshape(s, ATTN_W)


def _attn_bwd_prep(d_cat, o, ones):
    s = o.shape[0]
    tm = _tile(s, 512, 8)

    def body(do_ref, o_ref, ones_ref, dob_ref, dd_ref):
        do = do_ref[...]
        dob_ref[...] = do.astype(BF16)
        dd_ref[...] = _head_sum(do * o_ref[...], ones_ref)

    row = pl.BlockSpec((tm, ATTN_W), lambda i: (i, 0))
    return pl.pallas_call(
        body, name="attn_bwd_prep", grid=(s // tm,),
        in_specs=[pl.BlockSpec((tm, ATTN_W), lambda i: (i, 1)), row, pl.BlockSpec(ones.shape, lambda i: (0, 0))],
        out_specs=[row, row],
        out_shape=[jax.ShapeDtypeStruct((s, ATTN_W), BF16), jax.ShapeDtypeStruct((s, ATTN_W), F32)],
        compiler_params=_params(("parallel",), VMEM_BIG),
    )(d_cat, o, ones)


def _attn_bwd(qn, kn, v, dob, lse, dd, slopes, d, acc):
    s = qn.shape[0]
    length, width, tq, nt, nqb = _attn_geometry(s, d)
    view = lambda t: t.reshape(length, width)
    add = acc is not None

    def body(*refs):
        sl_ref, q_ref, k_ref, kh_ref, v_ref, vh_ref, do_ref, l_ref, dd_ref = refs[:9]
        if add:
            dqa_ref, dka_ref, dva_ref = refs[9:12]
        dq_ref, dk_ref, dv_ref, kbuf, vbuf, dkacc, dvacc, kcarry, vcarry = refs[-9:]
        cb, step = pl.program_id(0), pl.program_id(1)
        tile = nt - 1 - step
        biases, ki = _band(sl_ref, cb % 4, d)
        head0 = lax.broadcasted_iota(jnp.int32, (Q_BLOCK, Q_BLOCK), 1) < HEAD_DIM
        kbuf[0:Q_BLOCK, :] = kh_ref[...]
        kbuf[Q_BLOCK:, :] = k_ref[...]
        vbuf[0:Q_BLOCK, :] = vh_ref[...]
        vbuf[Q_BLOCK:, :] = v_ref[...]
        dkacc[...] = jnp.zeros_like(dkacc)
        dvacc[...] = jnp.zeros_like(dvacc)

        @pl.when(step > 0)
        def _():
            dkacc[tq:, :] = kcarry[...]
            dvacc[tq:, :] = vcarry[...]

        def q_block(qb, carry):
            r0 = pl.multiple_of(qb * Q_BLOCK, Q_BLOCK)
            rows, rows2 = pl.ds(r0, Q_BLOCK), pl.ds(r0, 2 * Q_BLOCK)
            qv, dov = q_ref[rows, :], do_ref[rows, :]
            lv, ddv = l_ref[rows, :], dd_ref[rows, :]
            k2, v2 = kbuf[rows2, :], vbuf[rows2, :]
            first_key = jnp.where((tile == 0) & (qb == 0), Q_BLOCK, 0)
            dq_h = []
            dk2 = jnp.zeros((2 * Q_BLOCK, Q_BLOCK), F32)
            dv2 = jnp.zeros((2 * Q_BLOCK, Q_BLOCK), F32)
            for hh in range(2):
                mine = head0 if hh == 0 else jnp.logical_not(head0)
                qm = jnp.where(mine, qv, jnp.zeros_like(qv))
                dom = jnp.where(mine, dov, jnp.zeros_like(dov))
                lcol = jnp.max(jnp.where(mine, lv, -jnp.inf), axis=1, keepdims=True)
                dcol = jnp.max(jnp.where(mine, ddv, -jnp.inf), axis=1, keepdims=True)
                sc = _dot(qm, k2, trans_b=True) + biases[hh]
                sc = jnp.where(ki < first_key, NEG, sc)
                p = jnp.exp(sc - lcol)
                ds = p * (_dot(dom, v2, trans_b=True) - dcol)
                dq_h.append(_dot(ds.astype(BF16), k2))
                dk2 = dk2 + _dot(ds.T.astype(BF16), qm)
                dv2 = dv2 + _dot(p.T.astype(BF16), dom)
            dq = jnp.where(head0, dq_h[0], dq_h[1])
            if add:
                dq = dq + dqa_ref[rows, :]
            dq_ref[rows, :] = dq
            dkacc[rows2, :] += dk2
            dvacc[rows2, :] += dv2
            return carry

        lax.fori_loop(0, nqb, q_block, 0)
        dk, dv = dkacc[Q_BLOCK:, :], dvacc[Q_BLOCK:, :]
        if add:
            dk, dv = dk + dka_ref[...], dv + dva_ref[...]
        dk_ref[...] = dk
        dv_ref[...] = dv
        kcarry[...] = dkacc[0:Q_BLOCK, :]
        vcarry[...] = dvacc[0:Q_BLOCK, :]

    main = pl.BlockSpec((tq, Q_BLOCK), lambda cb, st: (nt - 1 - st, cb))
    halo = pl.BlockSpec((Q_BLOCK, Q_BLOCK), lambda cb, st: (jnp.maximum((nt - 1 - st) * nqb - 1, 0), cb))
    ins = [slopes, view(qn), view(kn), view(kn), view(v), view(v), view(dob), view(lse), view(dd)]
    ins += [view(t) for t in acc] if add else []
    outs = pl.pallas_call(
        body, name=f"attn_bwd_d{d}", grid=(width // Q_BLOCK, nt),
        in_specs=[pl.BlockSpec(memory_space=pltpu.SMEM), main, main, halo, main, halo, main, main, main]
        + ([main] * 3 if add else []),
        out_specs=[main] * 3,
        out_shape=[jax.ShapeDtypeStruct((length, width), F32)] * 3,
        scratch_shapes=[pltpu.VMEM((Q_BLOCK + tq, Q_BLOCK), BF16)] * 2 + [pltpu.VMEM((Q_BLOCK + tq, Q_BLOCK), F32)] * 2
        + [pltpu.VMEM((Q_BLOCK, Q_BLOCK), F32)] * 2,
        compiler_params=_params(("parallel", "arbitrary"), VMEM_BIG),
    )(*ins)
    return tuple(t.reshape(s, ATTN_W) for t in outs)


def _attn_bwd_post(dq, dk, dv, qk, qg, kg, ones):
    s = dq.shape[0]
    tm = _tile(s, 512, 8)
    aw = ATTN_W

    def body(dq_ref, dk_ref, dv_ref, qk_ref, qg_ref, kg_ref, ones_ref, out_ref, gq_ref, gk_ref):
        @pl.when(pl.program_id(0) == 0)
        def _():
            gq_ref[...] = jnp.zeros_like(gq_ref)
            gk_ref[...] = jnp.zeros_like(gk_ref)

        def norm_bwd(dy, raw, g, scale):
            r = lax.rsqrt(_head_sum(raw * raw, ones_ref) * (1.0 / HEAD_DIM) + EPS)
            xh = raw * r
            gd = dy * (g * scale)
            dx = r * (gd - xh * (_head_sum(gd * xh, ones_ref) * (1.0 / HEAD_DIM)))
            gsum = jnp.sum(dy * xh, axis=0, keepdims=True) * scale
            for shift in (256, 128, 64):
                gsum = gsum + pltpu.roll(gsum, shift, 1)
            return dx, gsum

        dxq, gq = norm_bwd(dq_ref[...], qk_ref[:, :aw], qg_ref[...], HEAD_DIM ** -0.5)
        dxk, gk = norm_bwd(dk_ref[...], qk_ref[:, aw:], kg_ref[...], 1.0)
        out_ref[:, :aw] = dxq.astype(BF16)
        out_ref[:, aw:2 * aw] = dxk.astype(BF16)
        out_ref[:, 2 * aw:] = dv_ref[...].astype(BF16)
        gq_ref[...] += gq
        gk_ref[...] += gk

    row = lambda w: pl.BlockSpec((tm, w), lambda i: (i, 0))
    full = lambda t: pl.BlockSpec(t.shape, lambda i: (0, 0))
    vec = pl.BlockSpec((1, aw), lambda i: (0, 0))
    return pl.pallas_call(
        body, name="attn_bwd_post", grid=(s // tm,),
        in_specs=[row(aw), row(aw), row(aw), row(2 * aw), full(qg), full(kg), full(ones)],
        out_specs=[row(3 * aw), vec, vec],
        out_shape=[jax.ShapeDtypeStruct((s, 3 * aw), BF16), jax.ShapeDtypeStruct((1, aw), F32),
                   jax.ShapeDtypeStruct((1, aw), F32)],
        compiler_params=_params(("arbitrary",), VMEM_BIG),
    )(dq, dk, dv, qk, qg, kg, ones)


def _sum_parts(name, parts):
    _, n, w = parts.shape
    tn = _tile(n, 256, 8)

    def body(p_ref, o_ref):
        acc = p_ref[0]
        for j in range(1, N_DEV):
            acc = acc + p_ref[j]
        o_ref[...] = acc

    return pl.pallas_call(
        body, name=name, grid=(n // tn,),
        in_specs=[pl.BlockSpec((N_DEV, tn, w), lambda i: (0, i, 0))],
        out_specs=pl.BlockSpec((tn, w), lambda i: (i, 0)),
        out_shape=jax.ShapeDtypeStruct((n, w), F32),
        compiler_params=_params(("parallel",), VMEM_BIG),
    )(parts)


def _adamw(name, w, g, m, v):
    n, cols = w.shape
    tn = _tile(n, 256, 8)
    c1 = 1.0 - ADAM_B1 ** ADAM_STEP
    c2 = 1.0 - ADAM_B2 ** ADAM_STEP

    def body(w_ref, g_ref, m_ref, v_ref, d_ref, nm_ref, nv_ref):
        gv = g_ref[...]
        nm = ADAM_B1 * m_ref[...] + (1.0 - ADAM_B1) * gv
        nv = ADAM_B2 * v_ref[...] + (1.0 - ADAM_B2) * (gv * gv)
        nm_ref[...] = nm
        nv_ref[...] = nv
        d_ref[...] = -ADAM_LR * ((nm / c1) / (jnp.sqrt(nv / c2) + ADAM_EPS) + ADAM_WD * w_ref[...])

    blk = pl.BlockSpec((tn, cols), lambda i: (i, 0))
    return pl.pallas_call(
        body, name=name, grid=(n // tn,),
        in_specs=[blk] * 4, out_specs=[blk] * 3,
        out_shape=[jax.ShapeDtypeStruct((n, cols), F32)] * 3,
        compiler_params=_params(("parallel",), VMEM_BIG),
    )(w, g, m, v)


def _pack(vectors, width):
    flat = jnp.concatenate([t.reshape(-1) for t in vectors])
    rows = -(-flat.shape[0] // (8 * width)) * 8
    return jnp.pad(flat, (0, rows * width - flat.shape[0])).reshape(rows, width)


def _unpack(packed, shapes):
    flat = packed.reshape(-1)
    out, at = [], 0
    for shp in shapes:
        size = 1
        for dim in shp:
            size *= dim
        out.append(flat[at:at + size].reshape(shp))
        at += size
    return out


def kernel(x, norm1_g, w_in, conv_w, conv_b, cn_g, cn_b, q_norm_g, k_norm_g, w_out, norm2_g, w_up, ffconv_w, ffconv_b, w_down, loss_target, m_norm1_g, m_w_in, m_conv_w, m_conv_b, m_cn_g, m_cn_b, m_q_norm_g, m_k_norm_g, m_w_out, m_norm2_g, m_w_up, m_ffconv_w, m_ffconv_b, m_w_down, v_norm1_g, v_w_in, v_conv_w, v_conv_b, v_cn_g, v_cn_b, v_q_norm_g, v_k_norm_g, v_w_out, v_norm2_g, v_w_up, v_ffconv_w, v_ffconv_b, v_w_down):
    s, d = x.shape[1], x.shape[2]
    f = w_down.shape[0] * N_DEV
    n_in = w_in.shape[1] * N_DEV
    xs, target = x.reshape(s, d), loss_target.reshape(s, d)
    me = _linear((lax.axis_index("x"), lax.axis_index("y"), lax.axis_index("c")))

    shards = [w_in.T.astype(BF16), w_out.astype(BF16), w_up.T.astype(BF16), w_down.astype(BF16),
              _pack([conv_w, ffconv_w], 128)]
    g_in, g_out, g_up, g_down, g_filt = _all_gather("gather_weights", shards)
    w_in_t = g_in.reshape(n_in, d)
    w_out_f = g_out.reshape(2 * CONV_CH, d)
    w_up_t = g_up.reshape(2 * f, d)
    w_down_f = g_down.reshape(f, d)
    filt = g_filt.reshape(N_DEV, -1)
    n_cw = conv_w.size
    cw = filt[:, :n_cw].reshape(N_DEV, CONV_K, -1).transpose(1, 0, 2).reshape(CONV_K, CONV_CH)
    cw = jnp.pad(cw, ((0, CONV_HALO - CONV_K), (0, 0)))
    fw = filt[:, n_cw:n_cw + ffconv_w.size].reshape(N_DEV, FF_K, -1).transpose(1, 0, 2).reshape(FF_K, 2, f)
    fb = ffconv_b.reshape(2, f)
    row = lambda t: t.reshape(1, -1)
    g1, g2 = row(norm1_g), row(norm2_g)
    qg, kg = row(jnp.tile(q_norm_g, N_HEADS)), row(jnp.tile(k_norm_g, N_HEADS))
    lanes = jnp.arange(ATTN_W) // HEAD_DIM
    ones = (lanes[:, None] == lanes[None, :]).astype(BF16)
    slopes = 2.0 ** (-8.0 * jnp.arange(1, N_HEADS + 1, dtype=F32) / N_HEADS)

    h, a, qk, qn, kn, vv = _proj_fwd(xs, g1, w_in_t, qg, kg, ones)
    u1, u = _conv_fwd(a, cw, row(conv_b), row(cn_g), row(cn_b))
    run = None
    for dil in PATTERN_DILATIONS:
        run = _attn_fwd(qn, kn, vv, slopes, dil, run)
    o, lse = run
    ob = o.astype(BF16)
    x1, h2 = _mix_out(u, ob, w_out_f, xs, g2)
    up_g, up_v = _up_fwd(h2, w_up_t, f)
    act = _ff_act(up_g, up_v, fw, fb)
    loss_acc, dyf, dyb = _down_loss(act, w_down_f, x1, target)
    loss = lax.psum(loss_acc[0, 0], MESH_AXES)

    tk = _tile(s, 512, 8)
    gw_down = _mm_tn("grad_w_down", [act], dyb, _tile(f, 1408, 128), tk)
    d_act = _mm("d_act", dyb, w_down_f, True, F32, _tile(s, 512, 8), _tile(f, 1408, 128))
    dpre_g, dpre_v, gfw, gfb = _ff_bwd(up_g, up_v, d_act, fw, fb)
    gw_up = _mm_tn("grad_w_up", [dpre_g, dpre_v], h2, _tile(f, 1408, 128), tk)
    dx1f, dx1b, gg2 = _up_bwd(dpre_g, dpre_v, w_up_t, x1, dyf, g2)
    gw_out = _mm_tn("grad_w_out", [u, ob], dx1b, CONV_CH, tk)
    d_cat = _mm("d_cat", dx1b, w_out_f, True, F32, _tile(s, 512, 8), 2 * CONV_CH)
    d_a, gcw, gcb, gcg, gcbeta = _conv_bwd(a, u1, d_cat, cw, row(cn_g), row(cn_b))
    dob, dd = _attn_bwd_prep(d_cat, o, ones)
    acc = None
    for dil in PATTERN_DILATIONS:
        acc = _attn_bwd(qn, kn, vv, dob, lse, dd, slopes, dil, acc)
    d_qkv, gqg, gkg = _attn_bwd_post(*acc, qk, qg, kg, ones)
    gw_in = _mm_tn("grad_w_in", [d_a, d_qkv], h, 512, tk)
    grad_x, gg1 = _proj_bwd(d_a, d_qkv, w_in_t, xs, dx1f, g1)

    big = [gw_in, gw_out, gw_up, gw_down]
    recv = _exchange("exchange_weight_grads", [t.reshape(N_DEV, t.shape[0] // N_DEV, d) for t in big])
    g_w_in_t, g_w_out, g_w_up_t, g_w_down = [_sum_parts(f"sum_grad_{i}", r) for i, r in enumerate(recv)]
    g_w_in, g_w_up = g_w_in_t.T, g_w_up_t.T

    small_shapes = [(d,), (d,), (CONV_CH,), (CONV_CH,), (CONV_CH,), (ATTN_W,), (ATTN_W,), (2 * f,),
                    (CONV_K, CONV_CH), (FF_K, 2 * f)]
    small = _pack([gg1, gg2, gcb, gcg, gcbeta, gqg, gkg, gfb, gcw[:CONV_K], gfw], 1024)
    (small_all,) = _all_gather("gather_small_grads", [small])
    sg1, sg2, scb, scg, scbeta, sqg, skg, sfb, scw, sfw = _unpack(_sum_parts("sum_small_grads", small_all), small_shapes)
    cwl, fwl = conv_w.shape[1], ffconv_w.shape[1]
    g_small = [sg1, lax.dynamic_slice_in_dim(scw, me * cwl, cwl, 1), scb, scg, scbeta,
               sqg[:HEAD_DIM], skg[:HEAD_DIM], sg2, lax.dynamic_slice_in_dim(sfw, me * fwl, fwl, 1), sfb]

    w_small = [norm1_g, conv_w, conv_b, cn_g, cn_b, q_norm_g, k_norm_g, norm2_g, ffconv_w, ffconv_b]
    m_small = [m_norm1_g, m_conv_w, m_conv_b, m_cn_g, m_cn_b, m_q_norm_g, m_k_norm_g, m_norm2_g, m_ffconv_w, m_ffconv_b]
    v_small = [v_norm1_g, v_conv_w, v_conv_b, v_cn_g, v_cn_b, v_q_norm_g, v_k_norm_g, v_norm2_g, v_ffconv_w, v_ffconv_b]
    shapes = [t.shape for t in w_small]
    packed = _adamw("adamw_small", *[_pack(ts, 128) for ts in (w_small, g_small, m_small, v_small)])
    d_small, nm_small, nv_small = [_unpack(p, shapes) for p in packed]
    upd = {}
    for name, wt, gt, mt, vt in (("w_in", w_in, g_w_in, m_w_in, v_w_in), ("w_out", w_out, g_w_out, m_w_out, v_w_out),
                                 ("w_up", w_up, g_w_up, m_w_up, v_w_up), ("w_down", w_down, g_w_down, m_w_down, v_w_down)):
        upd[name] = (gt,) + tuple(_adamw("adamw_" + name, wt, gt, mt, vt))
    order = ["norm1_g", "w_in", "conv_w", "conv_b", "cn_g", "cn_b", "q_norm_g", "k_norm_g", "w_out", "norm2_g",
             "w_up", "ffconv_w", "ffconv_b", "w_down"]
    small_names = ["norm1_g", "conv_w", "conv_b", "cn_g", "cn_b", "q_norm_g", "k_norm_g", "norm2_g", "ffconv_w", "ffconv_b"]
    for i, name in enumerate(small_names):
        upd[name] = (g_small[i], d_small[i], nm_small[i], nv_small[i])
    outs = [loss, grad_x.reshape(x.shape)]
    for field in range(4):
        outs += [upd[name][field] for name in order]
    return tuple(outs)
```

```python
import functools

import jax
import jax.numpy as jnp
from jax import lax
from jax.experimental import pallas as pl
from jax.experimental.pallas import tpu as pltpu

F32, BF16 = jnp.float32, jnp.bfloat16
N_DEV = 8
N_HEADS, HEAD_DIM = 8, 64
CONV_CH = 512
ATTN_W = N_HEADS * HEAD_DIM
CONV_K, FF_K = 31, 3
CONV_HALO = 32
FF_HALO = 8
PATTERN_DILATIONS = (1, 4, 16)
Q_BLOCK = 128
EPS = 1e-6
NEG = -1e30
ADAM_LR, ADAM_B1, ADAM_B2, ADAM_EPS, ADAM_WD, ADAM_STEP = 0.001, 0.9, 0.999, 1e-08, 0.01, 10
MESH_AXES = ("x", "y", "c")
VMEM_BIG = 56 * 1024 * 1024
ANY = pl.BlockSpec(memory_space=pl.ANY)


def _params(sem=None, vmem=None):
    return pltpu.CompilerParams(dimension_semantics=sem, vmem_limit_bytes=vmem)


def _dot(a, b, trans_a=False, trans_b=False):
    dims = (((0 if trans_a else 1,), (1 if trans_b else 0,)), ((), ()))
    return lax.dot_general(a, b, dims, preferred_element_type=F32)


def _sigmoid(z):
    return 1.0 / (1.0 + jnp.exp(-z))


def _tile(n, pref, mult):
    if n <= pref:
        return n
    t = (pref // mult) * mult
    while n % t:
        t -= mult
    return t


def _head_sum(v, ones_ref):
    hi = v.astype(BF16)
    lo = (v - hi.astype(F32)).astype(BF16)
    ones = ones_ref[...]
    return _dot(hi, ones) + _dot(lo, ones)


def _linear(p):
    return 4 * p[0] + 2 * p[1] + p[2]


def _all_gather(name, arrs):
    na = len(arrs)

    def body(*refs):
        ins, outs = refs[:na], refs[na:2 * na]
        send_sems, recv_sems, local_sems = refs[2 * na:]
        x, y, c = lax.axis_index("x"), lax.axis_index("y"), lax.axis_index("c")
        me, sib = (x, y, c), (x, y, 1 - c)
        chips = [(1 - x, y), (x, 1 - y), (1 - x, 1 - y)]

        def copy(a, k, block, to, src=None):
            dst = outs[a].at[_linear(block)]
            return pltpu.make_async_remote_copy(
                src_ref=dst if src is None else src, dst_ref=dst,
                send_sem=send_sems.at[7 * a + k], recv_sem=recv_sems.at[7 * a + k],
                device_id=to, device_id_type=pl.DeviceIdType.MESH)

        mine = [pltpu.make_async_copy(ins[a], outs[a].at[_linear(me)], local_sems.at[a]) for a in range(na)]
        for cp in mine:
            cp.start()
        first = []
        for a in range(na):
            first.append(copy(a, 0, me, sib, src=ins[a]))
            first += [copy(a, 1 + j, me, (*chip, c), src=ins[a]) for j, chip in enumerate(chips)]
        for cp in first:
            cp.start()
        passed = []
        for j, chip in enumerate(chips):
            for a in range(na):
                copy(a, 1 + j, (*chip, c), me).wait_recv()
                fwd = copy(a, 4 + j, (*chip, c), sib)
                fwd.start()
                passed.append(fwd)
        for a in range(na):
            copy(a, 0, sib, me).wait_recv()
            for j, chip in enumerate(chips):
                copy(a, 4 + j, (*chip, 1 - c), me).wait_recv()
        for cp in first + passed:
            cp.wait_send()
        for cp in mine:
            cp.wait()

    return pl.pallas_call(
        body, name=name,
        out_shape=[jax.ShapeDtypeStruct((N_DEV,) + a.shape, a.dtype) for a in arrs],
        in_specs=[ANY] * na, out_specs=[ANY] * na,
        scratch_shapes=[pltpu.SemaphoreType.DMA((7 * na,)), pltpu.SemaphoreType.DMA((7 * na,)),
                        pltpu.SemaphoreType.DMA((na,))],
    )(*arrs)


def _exchange(name, arrs):
    na = len(arrs)

    def body(*refs):
        ins, outs = refs[:na], refs[na:2 * na]
        send_sems, recv_sems, local_sems = refs[2 * na:]
        x, y, c = lax.axis_index("x"), lax.axis_index("y"), lax.axis_index("c")
        me = _linear((x, y, c))
        mine = [pltpu.make_async_copy(ins[a].at[me], outs[a].at[me], local_sems.at[a]) for a in range(na)]
        for cp in mine:
            cp.start()
        sends, recvs = [], []
        for p in range(1, N_DEV):
            peer = (1 - x if p & 4 else x, 1 - y if p & 2 else y, 1 - c if p & 1 else c)
            for a in range(na):
                sems = dict(send_sem=send_sems.at[7 * a + p - 1], recv_sem=recv_sems.at[7 * a + p - 1],
                            device_id=peer, device_id_type=pl.DeviceIdType.MESH)
                sends.append(pltpu.make_async_remote_copy(
                    src_ref=ins[a].at[_linear(peer)], dst_ref=outs[a].at[me], **sems))
                recvs.append(pltpu.make_async_remote_copy(
                    src_ref=ins[a].at[_linear(peer)], dst_ref=outs[a].at[_linear(peer)], **sems))
        for cp in sends:
            cp.start()
        for cp in recvs:
            cp.wait_recv()
        for cp in sends:
            cp.wait_send()
        for cp in mine:
            cp.wait()

    return pl.pallas_call(
        body, name=name,
        out_shape=[jax.ShapeDtypeStruct(a.shape, a.dtype) for a in arrs],
        in_specs=[ANY] * na, out_specs=[ANY] * na,
        scratch_shapes=[pltpu.SemaphoreType.DMA((7 * na,)), pltpu.SemaphoreType.DMA((7 * na,)),
                        pltpu.SemaphoreType.DMA((na,))],
    )(*arrs)


def _mm(name, a, w, trans_b, out_dtype, tm, tn):
    m, k = a.shape
    n = w.shape[0] if trans_b else w.shape[1]

    def body(a_ref, w_ref, o_ref):
        o_ref[...] = _dot(a_ref[...], w_ref[...], trans_b=trans_b).astype(o_ref.dtype)

    w_spec = pl.BlockSpec((tn, k), lambda j, i: (j, 0)) if trans_b else pl.BlockSpec((k, tn), lambda j, i: (0, j))
    return pl.pallas_call(
        body, name=name, grid=(n // tn, m // tm),
        in_specs=[pl.BlockSpec((tm, k), lambda j, i: (i, 0)), w_spec],
        out_specs=pl.BlockSpec((tm, tn), lambda j, i: (i, j)),
        out_shape=jax.ShapeDtypeStruct((m, n), out_dtype),
        compiler_params=_params(("parallel", "parallel"), VMEM_BIG),
    )(a, w)


def _mm_tn(name, parts, b, tn, tk):
    s, d = b.shape
    blocks = [p.shape[1] // tn for p in parts]
    starts = [sum(blocks[:i]) for i in range(len(parts))]
    np_ = len(parts)

    nk = s // tk

    def body(*refs):
        a_refs, b_ref, o_ref, acc_ref = refs[:np_], refs[np_], refs[np_ + 1], refs[np_ + 2]
        j, kk = pl.program_id(0), pl.program_id(1)

        @pl.when(kk == 0)
        def _():
            acc_ref[...] = jnp.zeros_like(acc_ref)

        for p in range(np_):
            @pl.when((j >= starts[p]) & (j < starts[p] + blocks[p]))
            def _(p=p):
                acc_ref[...] += _dot(a_refs[p][...].astype(BF16), b_ref[...], trans_a=True)

        @pl.when(kk == nk - 1)
        def _():
            o_ref[...] = acc_ref[...].astype(o_ref.dtype)

    def a_spec(p):
        return pl.BlockSpec((tk, tn), lambda j, kk: (kk, jnp.clip(j - starts[p], 0, blocks[p] - 1)))

    return pl.pallas_call(
        body, name=name, grid=(sum(blocks), nk),
        in_specs=[a_spec(p) for p in range(np_)] + [pl.BlockSpec((tk, d), lambda j, kk: (kk, 0))],
        out_specs=pl.BlockSpec((tn, d), lambda j, kk: (j, 0)),
        out_shape=jax.ShapeDtypeStruct((sum(blocks) * tn, d), BF16),
        scratch_shapes=[pltpu.VMEM((tn, d), F32)],
        compiler_params=_params(("parallel", "arbitrary"), VMEM_BIG),
    )(*parts, b)


def _proj_fwd(x, g1, w_in_t, qg, kg, ones):
    s, d = x.shape
    tm = _tile(s, 256, 8)
    c2, aw = 2 * CONV_CH, ATTN_W

    def body(x_ref, g_ref, w_ref, qg_ref, kg_ref, ones_ref, h_ref, a_ref, qk_ref, qkv_ref):
        xv = x_ref[...]
        r = lax.rsqrt(jnp.mean(xv * xv, axis=-1, keepdims=True) + EPS)
        h = (xv * r * g_ref[...]).astype(BF16)
        h_ref[...] = h
        proj = _dot(h, w_ref[...], trans_b=True)
        a_ref[...] = proj[:, :c2]
        qk_ref[...] = proj[:, c2:c2 + 2 * aw]
        q, k = proj[:, c2:c2 + aw], proj[:, c2 + aw:c2 + 2 * aw]
        rq = lax.rsqrt(_head_sum(q * q, ones_ref) * (1.0 / HEAD_DIM) + EPS)
        rk = lax.rsqrt(_head_sum(k * k, ones_ref) * (1.0 / HEAD_DIM) + EPS)
        qkv_ref[:, :aw] = q * rq * qg_ref[...] * (HEAD_DIM ** -0.5)
        qkv_ref[:, aw:2 * aw] = k * rk * kg_ref[...]
        qkv_ref[:, 2 * aw:] = proj[:, c2 + 2 * aw:]

    row = lambda w: pl.BlockSpec((tm, w), lambda i: (i, 0))
    full = lambda a: pl.BlockSpec(a.shape, lambda i: (0, 0))
    return pl.pallas_call(
        body, name="proj_fwd", grid=(s // tm,),
        in_specs=[row(d), full(g1), full(w_in_t), full(qg), full(kg), full(ones)],
        out_specs=[row(d), row(c2), row(2 * aw), row(3 * aw)],
        out_shape=[jax.ShapeDtypeStruct((s, d), BF16), jax.ShapeDtypeStruct((s, c2), F32),
                   jax.ShapeDtypeStruct((s, 2 * aw), F32), jax.ShapeDtypeStruct((s, 3 * aw), F32)],
        compiler_params=_params(("parallel",), VMEM_BIG),
    )(x, g1, w_in_t, qg, kg, ones)


def _mix_out(u, o, w_out, x, g2):
    s, d = x.shape
    tm = _tile(s, 512, 8)

    def body(u_ref, o_ref, w_ref, x_ref, g_ref, x1_ref, h2_ref):
        x1 = x_ref[...] + _dot(u_ref[...], w_ref[:CONV_CH, :]) + _dot(o_ref[...].astype(BF16), w_ref[CONV_CH:, :])
        x1_ref[...] = x1
        r = lax.rsqrt(jnp.mean(x1 * x1, axis=-1, keepdims=True) + EPS)
        h2_ref[...] = (x1 * r * g_ref[...]).astype(BF16)

    row = lambda w: pl.BlockSpec((tm, w), lambda i: (i, 0))
    full = lambda a: pl.BlockSpec(a.shape, lambda i: (0, 0))
    return pl.pallas_call(
        body, name="mix_out", grid=(s // tm,),
        in_specs=[row(CONV_CH), row(ATTN_W), full(w_out), row(d), full(g2)],
        out_specs=[row(d), row(d)],
        out_shape=[jax.ShapeDtypeStruct((s, d), F32), jax.ShapeDtypeStruct((s, d), BF16)],
        compiler_params=_params(("parallel",), VMEM_BIG),
    )(u, o, w_out, x, g2)


def _up_fwd(h2, w_up_t, f):
    s, d = h2.shape
    tm, tn = _tile(s, 512, 8), _tile(f, 1408, 128)
    nb = f // tn

    def body(h_ref, wg_ref, wv_ref, g_ref, v_ref):
        hv = h_ref[...]
        g_ref[...] = _dot(hv, wg_ref[...], trans_b=True)
        v_ref[...] = _dot(hv, wv_ref[...], trans_b=True)

    return pl.pallas_call(
        body, name="up_fwd", grid=(nb, s // tm),
        in_specs=[pl.BlockSpec((tm, d), lambda j, i: (i, 0)), pl.BlockSpec((tn, d), lambda j, i: (j, 0)),
                  pl.BlockSpec((tn, d), lambda j, i: (j + nb, 0))],
        out_specs=[pl.BlockSpec((tm, tn), lambda j, i: (i, j))] * 2,
        out_shape=[jax.ShapeDtypeStruct((s, f), F32)] * 2,
        compiler_params=_params(("parallel", "parallel"), VMEM_BIG),
    )(h2, w_up_t, w_up_t)


def _down_loss(act, w_down, x1, target):
    s, d = x1.shape
    f = act.shape[1]
    tm = _tile(s, 256, 8)

    def body(a_ref, w_ref, x1_ref, t_ref, loss_ref, dyf_ref, dyb_ref):
        @pl.when(pl.program_id(0) == 0)
        def _():
            loss_ref[...] = jnp.zeros_like(loss_ref)

        diff = x1_ref[...] + _dot(a_ref[...], w_ref[...]) - t_ref[...]
        sq = jnp.sum(jnp.sum(diff * diff, axis=1, keepdims=True), axis=0, keepdims=True)
        loss_ref[...] += jnp.broadcast_to(sq * (0.5 / d), loss_ref.shape)
        dy = diff * (1.0 / d)
        dyf_ref[...] = dy
        dyb_ref[...] = dy.astype(BF16)

    row = lambda w: pl.BlockSpec((tm, w), lambda i: (i, 0))
    return pl.pallas_call(
        body, name="down_loss", grid=(s // tm,),
        in_specs=[row(f), pl.BlockSpec((f, d), lambda i: (0, 0)), row(d), row(d)],
        out_specs=[pl.BlockSpec((8, 128), lambda i: (0, 0)), row(d), row(d)],
        out_shape=[jax.ShapeDtypeStruct((8, 128), F32), jax.ShapeDtypeStruct((s, d), F32),
                   jax.ShapeDtypeStruct((s, d), BF16)],
        compiler_params=_params(("arbitrary",), VMEM_BIG),
    )(act, w_down, x1, target)


def _norm_bwd_tail(dh, xv, g, resid, d):
    r = lax.rsqrt(jnp.mean(xv * xv, axis=-1, keepdims=True) + EPS)
    xh = xv * r
    gd = dh * g
    dx = r * (gd - xh * (jnp.sum(gd * xh, axis=-1, keepdims=True) * (1.0 / d)))
    return resid + dx, jnp.sum(dh * xh, axis=0, keepdims=True)


def _up_bwd(dg, dv, w_up_t, x1, dyf, g2):
    s, d = x1.shape
    f = dg.shape[1]
    tm = _tile(s, 256, 8)

    def body(dg_ref, dv_ref, w_ref, x1_ref, dy_ref, g_ref, dxf_ref, dxb_ref, gg_ref):
        @pl.when(pl.program_id(0) == 0)
        def _():
            gg_ref[...] = jnp.zeros_like(gg_ref)

        dh = _dot(dg_ref[...], w_ref[:f, :]) + _dot(dv_ref[...], w_ref[f:, :])
        dx, gg = _norm_bwd_tail(dh, x1_ref[...], g_ref[...], dy_ref[...], d)
        dxf_ref[...] = dx
        dxb_ref[...] = dx.astype(BF16)
        gg_ref[...] += gg

    row = lambda w: pl.BlockSpec((tm, w), lambda i: (i, 0))
    full = lambda a: pl.BlockSpec(a.shape, lambda i: (0, 0))
    return pl.pallas_call(
        body, name="up_bwd", grid=(s // tm,),
        in_specs=[row(f), row(f), full(w_up_t), row(d), row(d), full(g2)],
        out_specs=[row(d), row(d), pl.BlockSpec((1, d), lambda i: (0, 0))],
        out_shape=[jax.ShapeDtypeStruct((s, d), F32), jax.ShapeDtypeStruct((s, d), BF16),
                   jax.ShapeDtypeStruct((1, d), F32)],
        compiler_params=_params(("arbitrary",), VMEM_BIG),
    )(dg, dv, w_up_t, x1, dyf, g2)


def _proj_bwd(d_a, d_qkv, w_in_t, x, dx1, g1):
    s, d = x.shape
    na, nq = d_a.shape[1], d_qkv.shape[1]
    tm = _tile(s, 256, 8)

    def body(da_ref, dq_ref, w_ref, x_ref, r_ref, g_ref, gx_ref, gg_ref):
        @pl.when(pl.program_id(0) == 0)
        def _():
            gg_ref[...] = jnp.zeros_like(gg_ref)

        dh = _dot(da_ref[...], w_ref[:na, :]) + _dot(dq_ref[...], w_ref[na:, :])
        dx, gg = _norm_bwd_tail(dh, x_ref[...], g_ref[...], r_ref[...], d)
        gx_ref[...] = dx
        gg_ref[...] += gg

    row = lambda w: pl.BlockSpec((tm, w), lambda i: (i, 0))
    full = lambda a: pl.BlockSpec(a.shape, lambda i: (0, 0))
    return pl.pallas_call(
        body, name="proj_bwd", grid=(s // tm,),
        in_specs=[row(na), row(nq), full(w_in_t), row(d), row(d), full(g1)],
        out_specs=[row(d), pl.BlockSpec((1, d), lambda i: (0, 0))],
        out_shape=[jax.ShapeDtypeStruct((s, d), F32), jax.ShapeDtypeStruct((1, d), F32)],
        compiler_params=_params(("arbitrary",), VMEM_BIG),
    )(d_a, d_qkv, w_in_t, x, dx1, g1)


CONV_CHUNK = 64


def _glu(av):
    return av[:, :CONV_CH] * _sigmoid(av[:, CONV_CH:])


def _layer_norm_stats(u1):
    mu = jnp.mean(u1, axis=-1, keepdims=True)
    cen = u1 - mu
    rstd = lax.rsqrt(jnp.mean(cen * cen, axis=-1, keepdims=True) + EPS)
    return cen * rstd, rstd


def _conv_fwd(a, cw, cb, cg, cbeta):
    s = a.shape[0]
    tm = _tile(s, 256, CONV_CHUNK)
    hb = tm // CONV_HALO

    def body(a_ref, ap_ref, cw_ref, cb_ref, cg_ref, cbeta_ref, u1_ref, u_ref, ubuf):
        i = pl.program_id(0)
        ubuf[0:CONV_HALO, :] = jnp.where(i > 0, _glu(ap_ref[...]), 0.0)
        ubuf[CONV_HALO:, :] = _glu(a_ref[...])
        for c0 in range(0, tm, CONV_CHUNK):
            acc = jnp.broadcast_to(cb_ref[...], (CONV_CHUNK, CONV_CH))
            for k in range(CONV_K):
                acc = acc + cw_ref[k:k + 1, :] * ubuf[c0 + 2 + k:c0 + 2 + k + CONV_CHUNK, :]
            u1_ref[c0:c0 + CONV_CHUNK, :] = acc
        xh, _ = _layer_norm_stats(u1_ref[...])
        z = xh * cg_ref[...] + cbeta_ref[...]
        u_ref[...] = (z * _sigmoid(z)).astype(BF16)

    full = lambda t: pl.BlockSpec(t.shape, lambda i: (0, 0))
    return pl.pallas_call(
        body, name="conv_fwd", grid=(s // tm,),
        in_specs=[pl.BlockSpec((tm, 2 * CONV_CH), lambda i: (i, 0)),
                  pl.BlockSpec((CONV_HALO, 2 * CONV_CH), lambda i: (jnp.maximum(i * hb - 1, 0), 0)),
                  full(cw), full(cb), full(cg), full(cbeta)],
        out_specs=[pl.BlockSpec((tm, CONV_CH), lambda i: (i, 0))] * 2,
        out_shape=[jax.ShapeDtypeStruct((s, CONV_CH), F32), jax.ShapeDtypeStruct((s, CONV_CH), BF16)],
        scratch_shapes=[pltpu.VMEM((CONV_HALO + tm, CONV_CH), F32)],
        compiler_params=_params(("parallel",), VMEM_BIG),
    )(a, a, cw, cb, cg, cbeta)


def _conv_bwd(a, u1, d_cat, cw, cg, cbeta):
    s = a.shape[0]
    tm = _tile(s, 256, CONV_CHUNK)
    hb = tm // CONV_HALO
    last_halo = s // CONV_HALO - 1
    nt = s // tm
    te = tm + CONV_HALO

    def body(a_ref, ap_ref, u1_ref, u1n_ref, du_ref, dun_ref, cw_ref, cg_ref, cbeta_ref,
             da_ref, gw_ref, gb_ref, gg_ref, gbeta_ref, ubuf, dbuf):
        i = pl.program_id(0)

        @pl.when(i == 0)
        def _():
            gw_ref[...] = jnp.zeros_like(gw_ref)
            gb_ref[...] = jnp.zeros_like(gb_ref)
            gg_ref[...] = jnp.zeros_like(gg_ref)
            gbeta_ref[...] = jnp.zeros_like(gbeta_ref)

        def du1_of(u1, du):
            xh, rstd = _layer_norm_stats(u1)
            z = xh * cg_ref[...] + cbeta_ref[...]
            sz = _sigmoid(z)
            dz = du * (sz * (1.0 + z * (1.0 - sz)))
            dxh = dz * cg_ref[...]
            du1 = rstd * (dxh - jnp.mean(dxh, axis=-1, keepdims=True)
                          - xh * jnp.mean(dxh * xh, axis=-1, keepdims=True))
            return du1, dz, xh

        du1, dz, xh = du1_of(u1_ref[...], du_ref[...])
        gg_ref[...] += jnp.sum(dz * xh, axis=0, keepdims=True)
        gbeta_ref[...] += jnp.sum(dz, axis=0, keepdims=True)
        gb_ref[...] += jnp.sum(du1, axis=0, keepdims=True)
        dbuf[0:tm, :] = du1
        du1n, _, _ = du1_of(u1n_ref[...], jnp.where(i < nt - 1, dun_ref[...], 0.0))
        dbuf[tm:, :] = du1n

        av = a_ref[...]
        ubuf[0:CONV_HALO, :] = jnp.where(i > 0, _glu(ap_ref[...]), 0.0)
        sg = _sigmoid(av[:, CONV_CH:])
        ubuf[CONV_HALO:, :] = av[:, :CONV_CH] * sg

        for k in range(CONV_K):
            part = jnp.zeros((1, CONV_CH), F32)
            for c0 in range(0, tm, CONV_CHUNK):
                prod = dbuf[c0:c0 + CONV_CHUNK, :] * ubuf[c0 + 2 + k:c0 + 2 + k + CONV_CHUNK, :]
                part = part + jnp.sum(prod, axis=0, keepdims=True)
            gw_ref[k:k + 1, :] += part

        for c0 in range(0, tm, CONV_CHUNK):
            acc = jnp.zeros((CONV_CHUNK, CONV_CH), F32)
            for k in range(CONV_K):
                off = c0 + CONV_K - 1 - k
                acc = acc + cw_ref[k:k + 1, :] * dbuf[off:off + CONV_CHUNK, :]
            sgc = sg[c0:c0 + CONV_CHUNK, :]
            avc = av[c0:c0 + CONV_CHUNK, :CONV_CH]
            da_ref[c0:c0 + CONV_CHUNK, :CONV_CH] = (acc * sgc).astype(BF16)
            da_ref[c0:c0 + CONV_CHUNK, CONV_CH:] = (acc * avc * sgc * (1.0 - sgc)).astype(BF16)

    full = lambda t: pl.BlockSpec(t.shape, lambda i: (0, 0))
    vec = pl.BlockSpec((1, CONV_CH), lambda i: (0, 0))
    nxt = lambda i: (jnp.minimum((i + 1) * hb, last_halo), 0)
    return pl.pallas_call(
        body, name="conv_bwd", grid=(nt,),
        in_specs=[pl.BlockSpec((tm, 2 * CONV_CH), lambda i: (i, 0)),
                  pl.BlockSpec((CONV_HALO, 2 * CONV_CH), lambda i: (jnp.maximum(i * hb - 1, 0), 0)),
                  pl.BlockSpec((tm, CONV_CH), lambda i: (i, 0)), pl.BlockSpec((CONV_HALO, CONV_CH), nxt),
                  pl.BlockSpec((tm, CONV_CH), lambda i: (i, 0)), pl.BlockSpec((CONV_HALO, CONV_CH), nxt),
                  full(cw), full(cg), full(cbeta)],
        out_specs=[pl.BlockSpec((tm, 2 * CONV_CH), lambda i: (i, 0)),
                   pl.BlockSpec((CONV_HALO, CONV_CH), lambda i: (0, 0)), vec, vec, vec],
        out_shape=[jax.ShapeDtypeStruct((s, 2 * CONV_CH), BF16), jax.ShapeDtypeStruct((CONV_HALO, CONV_CH), F32),
                   jax.ShapeDtypeStruct((1, CONV_CH), F32), jax.ShapeDtypeStruct((1, CONV_CH), F32),
                   jax.ShapeDtypeStruct((1, CONV_CH), F32)],
        scratch_shapes=[pltpu.VMEM((CONV_HALO + tm, CONV_CH), F32), pltpu.VMEM((te, CONV_CH), F32)],
        compiler_params=_params(("arbitrary",), VMEM_BIG),
    )(a, a, u1, u1, d_cat, d_cat, cw, cg, cbeta)


def _ff_tiles(s, f):
    return _tile(s, 256, 8), _tile(f, 1408, 128)


def _ff_conv(pre_buf, w_ref, b_ref, half, rows):
    acc = b_ref[half:half + 1, :]
    for k in range(FF_K):
        off = FF_HALO - (FF_K - 1) + k
        acc = acc + w_ref[k, half:half + 1, :] * pre_buf[half, off:off + rows, :]
    return acc


def _ff_act(up_g, up_v, fw, fb):
    s, f = up_g.shape
    tm, tc = _ff_tiles(s, f)
    hb = tm // FF_HALO

    def body(g_ref, gp_ref, v_ref, vp_ref, w_ref, b_ref, act_ref, buf):
        i = pl.program_id(1)
        for half, (m_ref, p_ref) in enumerate(((g_ref, gp_ref), (v_ref, vp_ref))):
            buf[half, 0:FF_HALO, :] = jnp.where(i > 0, p_ref[...], 0.0)
            buf[half, FF_HALO:, :] = m_ref[...]
        gate = _ff_conv(buf, w_ref, b_ref, 0, tm)
        val = _ff_conv(buf, w_ref, b_ref, 1, tm)
        act_ref[...] = (gate * _sigmoid(gate) * val).astype(BF16)

    main = pl.BlockSpec((tm, tc), lambda j, i: (i, j))
    prev = pl.BlockSpec((FF_HALO, tc), lambda j, i: (jnp.maximum(i * hb - 1, 0), j))
    return pl.pallas_call(
        body, name="ff_act", grid=(f // tc, s // tm),
        in_specs=[main, prev, main, prev, pl.BlockSpec((FF_K, 2, tc), lambda j, i: (0, 0, j)),
                  pl.BlockSpec((2, tc), lambda j, i: (0, j))],
        out_specs=main,
        out_shape=jax.ShapeDtypeStruct((s, f), BF16),
        scratch_shapes=[pltpu.VMEM((2, FF_HALO + tm, tc), F32)],
        compiler_params=_params(("parallel", "parallel"), VMEM_BIG),
    )(up_g, up_g, up_v, up_v, fw, fb)


def _ff_bwd(up_g, up_v, d_act, fw, fb):
    s, f = up_g.shape
    tm, tc = _ff_tiles(s, f)
    hb = tm // FF_HALO
    nt = s // tm
    last_halo = s // FF_HALO - 1
    te = tm + FF_HALO

    def body(g_ref, gp_ref, gn_ref, v_ref, vp_ref, vn_ref, da_ref, dan_ref, w_ref, b_ref,
             dg_ref, dv_ref, gw_ref, gb_ref, buf, dbuf):
        i = pl.program_id(1)

        @pl.when(i == 0)
        def _():
            gw_ref[...] = jnp.zeros_like(gw_ref)
            gb_ref[...] = jnp.zeros_like(gb_ref)

        for half, (m_ref, p_ref, n_ref) in enumerate(((g_ref, gp_ref, gn_ref), (v_ref, vp_ref, vn_ref))):
            buf[half, 0:FF_HALO, :] = jnp.where(i > 0, p_ref[...], 0.0)
            buf[half, FF_HALO:FF_HALO + tm, :] = m_ref[...]
            buf[half, FF_HALO + tm:, :] = n_ref[...]
        gate = _ff_conv(buf, w_ref, b_ref, 0, te)
        val = _ff_conv(buf, w_ref, b_ref, 1, te)
        dan = jnp.where(i < nt - 1, dan_ref[...], 0.0)
        sg_m, sg_n = _sigmoid(gate[:tm]), _sigmoid(gate[tm:])
        for lo, hi, da, sg in ((0, tm, da_ref[...], sg_m), (tm, te, dan, sg_n)):
            gt, vl = gate[lo:hi], val[lo:hi]
            dbuf[0, lo:hi, :] = da * vl * (sg * (1.0 + gt * (1.0 - sg)))
            dbuf[1, lo:hi, :] = da * (gt * sg)
        for half, out_ref in enumerate((dg_ref, dv_ref)):
            dup = dbuf[half, 0:tm, :]
            gb_ref[half:half + 1, :] += jnp.sum(dup, axis=0, keepdims=True)
            acc = jnp.zeros((tm, tc), F32)
            for k in range(FF_K):
                off = FF_HALO - (FF_K - 1) + k
                gw_ref[k, half:half + 1, :] += jnp.sum(dup * buf[half, off:off + tm, :], axis=0, keepdims=True)
                acc = acc + w_ref[k, half:half + 1, :] * dbuf[half, FF_K - 1 - k:FF_K - 1 - k + tm, :]
            out_ref[...] = acc.astype(BF16)

    main = pl.BlockSpec((tm, tc), lambda j, i: (i, j))
    prev = pl.BlockSpec((FF_HALO, tc), lambda j, i: (jnp.maximum(i * hb - 1, 0), j))
    nxt = pl.BlockSpec((FF_HALO, tc), lambda j, i: (jnp.minimum((i + 1) * hb, last_halo), j))
    return pl.pallas_call(
        body, name="ff_bwd", grid=(f // tc, nt),
        in_specs=[main, prev, nxt, main, prev, nxt, main, nxt,
                  pl.BlockSpec((FF_K, 2, tc), lambda j, i: (0, 0, j)), pl.BlockSpec((2, tc), lambda j, i: (0, j))],
        out_specs=[main, main, pl.BlockSpec((FF_K, 2, tc), lambda j, i: (0, 0, j)),
                   pl.BlockSpec((2, tc), lambda j, i: (0, j))],
        out_shape=[jax.ShapeDtypeStruct((s, f), BF16), jax.ShapeDtypeStruct((s, f), BF16),
                   jax.ShapeDtypeStruct((FF_K, 2, f), F32), jax.ShapeDtypeStruct((2, f), F32)],
        scratch_shapes=[pltpu.VMEM((2, FF_HALO + te, tc), F32), pltpu.VMEM((2, te, tc), F32)],
        compiler_params=_params(("parallel", "arbitrary"), VMEM_BIG),
    )(up_g, up_g, up_g, up_v, up_v, up_v, d_act, d_act, fw, fb)


ATT_TILE = Q_BLOCK * max(PATTERN_DILATIONS)


def _stream_rows(start, d, n=Q_BLOCK):
    return pl.ds(start, n) if d == 1 else pl.ds(start, n, stride=d)


def _band_geometry():
    qi = lax.broadcasted_iota(jnp.int32, (Q_BLOCK, 2 * Q_BLOCK), 0)
    ki = lax.broadcasted_iota(jnp.int32, (Q_BLOCK, 2 * Q_BLOCK), 1)
    delta = qi + Q_BLOCK - ki
    return (delta >= 0) & (delta <= Q_BLOCK), delta.astype(F32), ki


def _stream_blocks(d):
    out = []
    for r in range(d):
        for b in range(ATT_TILE // (Q_BLOCK * d)):
            start = b * Q_BLOCK * d + r
            out.append((start, start - Q_BLOCK * d if b > 0 else None))
    return out


def _attn_forward(qkv, slopes):
    s = qkv.shape[0]
    nt = s // ATT_TILE
    nhp = ATTN_W // Q_BLOCK

    def body(sl_ref, q_ref, k_ref, kp_ref, v_ref, vp_ref, o_ref, l_ref):
        hp, i = pl.program_id(0), pl.program_id(1)
        head0 = lax.broadcasted_iota(jnp.int32, (Q_BLOCK, Q_BLOCK), 1) < HEAD_DIM
        head0_k = lax.broadcasted_iota(jnp.int32, (2 * Q_BLOCK, Q_BLOCK), 1) < HEAD_DIM
        valid, dist, ki = _band_geometry()
        first_key = jnp.where(i == 0, Q_BLOCK, 0)
        for d in PATTERN_DILATIONS:
            biases = [jnp.where(valid, dist * (-sl_ref[2 * hp + hh] * d), NEG) for hh in range(2)]
            for start, prev in _stream_blocks(d):
                rows = _stream_rows(start, d)
                if prev is None:
                    prow = _stream_rows(ATT_TILE - Q_BLOCK * d + start, d)
                    kp, vp = kp_ref[prow, :], vp_ref[prow, :]
                else:
                    prow = _stream_rows(prev, d)
                    kp, vp = k_ref[prow, :], v_ref[prow, :]
                qv = q_ref[rows, :].astype(BF16)
                k2 = jnp.concatenate([kp, k_ref[rows, :]], axis=0).astype(BF16)
                v2 = jnp.concatenate([vp, v_ref[rows, :]], axis=0).astype(BF16)
                res, mxs = [], []
                for hh in range(2):
                    mine = head0 if hh == 0 else jnp.logical_not(head0)
                    mine_k = head0_k if hh == 0 else jnp.logical_not(head0_k)
                    sc = _dot(jnp.where(mine, qv, jnp.zeros_like(qv)), k2, trans_b=True) + biases[hh]
                    if prev is None:
                        sc = jnp.where(ki < first_key, NEG, sc)
                    mx = jnp.max(sc, axis=1, keepdims=True)
                    p = jnp.exp(sc - mx).astype(BF16)
                    res.append(_dot(p, jnp.where(mine_k, v2, jnp.ones_like(v2))))
                    mxs.append(mx)
                num = jnp.where(head0, res[0], res[1])
                den = pltpu.roll(jnp.where(head0, res[1], res[0]), HEAD_DIM, 1)
                o_new = num / den
                l_new = jnp.where(head0, mxs[0], mxs[1]) + jnp.log(den)
                if d != PATTERN_DILATIONS[0]:
                    oa, la = o_ref[rows, :], l_ref[rows, :]
                    mm = jnp.maximum(la, l_new)
                    wa, wn = jnp.exp(la - mm), jnp.exp(l_new - mm)
                    o_new = (wa * oa + wn * o_new) / (wa + wn)
                    l_new = mm + jnp.log(wa + wn)
                o_ref[rows, :] = o_new
                l_ref[rows, :] = l_new

    def col(off):
        return pl.BlockSpec((ATT_TILE, Q_BLOCK), lambda hp, i: (i, off + hp))

    def col_prev(off):
        return pl.BlockSpec((ATT_TILE, Q_BLOCK), lambda hp, i: (jnp.maximum(i - 1, 0), off + hp))

    return pl.pallas_call(
        body, name="attn_fwd", grid=(nhp, nt),
        in_specs=[pl.BlockSpec(memory_space=pltpu.SMEM), col(0), col(nhp), col_prev(nhp), col(2 * nhp), col_prev(2 * nhp)],
        out_specs=[col(0), col(0)],
        out_shape=[jax.ShapeDtypeStruct((s, ATTN_W), F32)] * 2,
        compiler_params=_params(("parallel", "parallel"), VMEM_BIG),
    )(slopes, qkv, qkv, qkv, qkv, qkv)


def _attn_backward(qkv, d_cat, o, lse, slopes, ones2):
    s = qkv.shape[0]
    nt = s // ATT_TILE
    nhp = ATTN_W // Q_BLOCK
    tt = ATT_TILE

    def body(sl_ref, q_ref, k_ref, kp_ref, v_ref, vp_ref, do_ref, o_ref, l_ref, ones_ref,
             dq_ref, dk_ref, dv_ref, dkacc, dvacc, dd):
        hp, step = pl.program_id(0), pl.program_id(1)
        tile = nt - 1 - step
        head0 = lax.broadcasted_iota(jnp.int32, (Q_BLOCK, Q_BLOCK), 1) < HEAD_DIM
        valid, dist, ki = _band_geometry()
        first_key = jnp.where(tile == 0, Q_BLOCK, 0)

        @pl.when(step == 0)
        def _():
            dkacc[tt:, :] = jnp.zeros((tt, Q_BLOCK), F32)
            dvacc[tt:, :] = jnp.zeros((tt, Q_BLOCK), F32)

        @pl.when(step > 0)
        def _():
            dkacc[tt:, :] = dkacc[:tt, :]
            dvacc[tt:, :] = dvacc[:tt, :]

        dkacc[:tt, :] = jnp.zeros((tt, Q_BLOCK), F32)
        dvacc[:tt, :] = jnp.zeros((tt, Q_BLOCK), F32)
        dd[...] = _head_sum(do_ref[...] * o_ref[...], ones_ref)

        for d in PATTERN_DILATIONS:
            biases = [jnp.where(valid, dist * (-sl_ref[2 * hp + hh] * d), NEG) for hh in range(2)]
            for start, prev in _stream_blocks(d):
                rows = _stream_rows(start, d)
                if prev is None:
                    prow = _stream_rows(ATT_TILE - Q_BLOCK * d + start, d)
                    kp, vp = kp_ref[prow, :], vp_ref[prow, :]
                else:
                    prow = _stream_rows(prev, d)
                    kp, vp = k_ref[prow, :], v_ref[prow, :]
                acc_rows = _stream_rows(tt + start - Q_BLOCK * d, d, 2 * Q_BLOCK)
                qv = q_ref[rows, :].astype(BF16)
                dov = do_ref[rows, :].astype(BF16)
                lv, ddv = l_ref[rows, :], dd[rows, :]
                lsw, dsw = pltpu.roll(lv, HEAD_DIM, 1), pltpu.roll(ddv, HEAD_DIM, 1)
                k2 = jnp.concatenate([kp, k_ref[rows, :]], axis=0).astype(BF16)
                v2 = jnp.concatenate([vp, v_ref[rows, :]], axis=0).astype(BF16)
                dq_h = []
                dk2 = jnp.zeros((2 * Q_BLOCK, Q_BLOCK), F32)
                dv2 = jnp.zeros((2 * Q_BLOCK, Q_BLOCK), F32)
                for hh in range(2):
                    mine = head0 if hh == 0 else jnp.logical_not(head0)
                    qm = jnp.where(mine, qv, jnp.zeros_like(qv))
                    dom = jnp.where(mine, dov, jnp.zeros_like(dov))
                    lfull, dfull = jnp.where(mine, lv, lsw), jnp.where(mine, ddv, dsw)
                    sc = _dot(qm, k2, trans_b=True) + biases[hh]
                    if prev is None:
                        sc = jnp.where(ki < first_key, NEG, sc)
                    p = jnp.exp(sc - jnp.concatenate([lfull, lfull], axis=1))
                    ds = p * (_dot(dom, v2, trans_b=True) - jnp.concatenate([dfull, dfull], axis=1))
                    dq_h.append(_dot(ds.astype(BF16), k2))
                    dk2 = dk2 + _dot(ds.T.astype(BF16), qm)
                    dv2 = dv2 + _dot(p.T.astype(BF16), dom)
                dq = jnp.where(head0, dq_h[0], dq_h[1])
                if d == PATTERN_DILATIONS[0]:
                    dq_ref[rows, :] = dq
                else:
                    dq_ref[rows, :] += dq
                dkacc[acc_rows, :] += dk2
                dvacc[acc_rows, :] += dv2
        dk_ref[...] = dkacc[tt:, :]
        dv_ref[...] = dvacc[tt:, :]

    def col(off):
        return pl.BlockSpec((tt, Q_BLOCK), lambda hp, st: (nt - 1 - st, off + hp))

    def col_prev(off):
        return pl.BlockSpec((tt, Q_BLOCK), lambda hp, st: (jnp.maximum(nt - 2 - st, 0), off + hp))

    return pl.pallas_call(
        body, name="attn_bwd", grid=(nhp, nt),
        in_specs=[pl.BlockSpec(memory_space=pltpu.SMEM), col(0), col(nhp), col_prev(nhp), col(2 * nhp), col_prev(2 * nhp),
                  col(nhp), col(0), col(0), pl.BlockSpec((Q_BLOCK, Q_BLOCK), lambda hp, st: (0, 0))],
        out_specs=[col(0)] * 3,
        out_shape=[jax.ShapeDtypeStruct((s, ATTN_W), F32)] * 3,
        scratch_shapes=[pltpu.VMEM((2 * tt, Q_BLOCK), F32)] * 2 + [pltpu.VMEM((tt, Q_BLOCK), F32)],
        compiler_params=_params(("parallel", "arbitrary"), VMEM_BIG),
    )(slopes, qkv, qkv, qkv, qkv, qkv, d_cat, o, lse, ones2)


def _attn_geometry(s, d):
    length = s // d
    tq = _tile(length, 1024, Q_BLOCK)
    return length, d * ATTN_W, tq, length // tq, tq // Q_BLOCK


def _band(slope_ref, head_pair, d):
    qi = lax.broadcasted_iota(jnp.int32, (Q_BLOCK, 2 * Q_BLOCK), 0)
    ki = lax.broadcasted_iota(jnp.int32, (Q_BLOCK, 2 * Q_BLOCK), 1)
    delta = qi + Q_BLOCK - ki
    valid = (delta >= 0) & (delta <= Q_BLOCK)
    dist = delta.astype(F32) * float(d)
    return [jnp.where(valid, dist * (-slope_ref[2 * head_pair + hh]), NEG) for hh in range(2)], ki


def _attn_fwd(qn, kn, v, slopes, d, prev):
    s = qn.shape[0]
    length, width, tq, nt, nqb = _attn_geometry(s, d)
    view = lambda t: t.reshape(length, width)
    merge = prev is not None

    def body(*refs):
        sl_ref, q_ref, k_ref, kh_ref, v_ref, vh_ref = refs[:6]
        if merge:
            oa_ref, la_ref = refs[6:8]
        o_ref, l_ref, kbuf, vbuf = refs[-4:]
        cb, i = pl.program_id(0), pl.program_id(1)
        biases, ki = _band(sl_ref, cb % 4, d)
        head0 = lax.broadcasted_iota(jnp.int32, (Q_BLOCK, Q_BLOCK), 1) < HEAD_DIM
        kbuf[0:Q_BLOCK, :] = kh_ref[...]
        kbuf[Q_BLOCK:, :] = k_ref[...]
        vbuf[0:Q_BLOCK, :] = vh_ref[...]
        vbuf[Q_BLOCK:, :] = v_ref[...]

        def q_block(qb, carry):
            r0 = pl.multiple_of(qb * Q_BLOCK, Q_BLOCK)
            rows = pl.ds(r0, Q_BLOCK)
            qv = q_ref[rows, :]
            k2, v2 = kbuf[pl.ds(r0, 2 * Q_BLOCK), :], vbuf[pl.ds(r0, 2 * Q_BLOCK), :]
            first_key = jnp.where((i == 0) & (qb == 0), Q_BLOCK, 0)
            o_h, l_h = [], []
            for hh in range(2):
                mine = head0 if hh == 0 else jnp.logical_not(head0)
                sc = _dot(jnp.where(mine, qv, jnp.zeros_like(qv)), k2, trans_b=True) + biases[hh]
                sc = jnp.where(ki < first_key, NEG, sc)
                mx = jnp.max(sc, axis=1, keepdims=True)
                p = jnp.exp(sc - mx)
                den = jnp.sum(p, axis=1, keepdims=True)
                o_h.append(_dot(p.astype(BF16), v2) * (1.0 / den))
                l_h.append(jnp.broadcast_to(mx + jnp.log(den), (Q_BLOCK, Q_BLOCK)))
            o_new = jnp.where(head0, o_h[0], o_h[1])
            l_new = jnp.where(head0, l_h[0], l_h[1])
            if merge:
                oa, la = oa_ref[rows, :], la_ref[rows, :]
                mm = jnp.maximum(la, l_new)
                wa, wn = jnp.exp(la - mm), jnp.exp(l_new - mm)
                o_new = (wa * oa + wn * o_new) / (wa + wn)
                l_new = mm + jnp.log(wa + wn)
            o_ref[rows, :] = o_new
            l_ref[rows, :] = l_new
            return carry

        lax.fori_loop(0, nqb, q_block, 0)

    main = pl.BlockSpec((tq, Q_BLOCK), lambda cb, i: (i, cb))
    halo = pl.BlockSpec((Q_BLOCK, Q_BLOCK), lambda cb, i: (jnp.maximum(i * nqb - 1, 0), cb))
    ins = [slopes, view(qn), view(kn), view(kn), view(v), view(v)] + ([view(prev[0]), view(prev[1])] if merge else [])
    o, lse = pl.pallas_call(
        body, name=f"attn_fwd_d{d}", grid=(width // Q_BLOCK, nt),
        in_specs=[pl.BlockSpec(memory_space=pltpu.SMEM), main, main, halo, main, halo] + ([main, main] if merge else []),
        out_specs=[main, main],
        out_shape=[jax.ShapeDtypeStruct((length, width), F32)] * 2,
        scratch_shapes=[pltpu.VMEM((Q_BLOCK + tq, Q_BLOCK), BF16)] * 2,
        compiler_params=_params(("parallel", "parallel"), VMEM_BIG),
    )(*ins)
    return o.reshape(s, ATTN_W), lse.reshape(s, ATTN_W)


def _attn_bwd_prep(d_cat, o, ones):
    s = o.shape[0]
    tm = _tile(s, 512, 8)

    def body(do_ref, o_ref, ones_ref, dob_ref, dd_ref):
        do = do_ref[...]
        dob_ref[...] = do.astype(BF16)
        dd_ref[...] = _head_sum(do * o_ref[...], ones_ref)

    row = pl.BlockSpec((tm, ATTN_W), lambda i: (i, 0))
    return pl.pallas_call(
        body, name="attn_bwd_prep", grid=(s // tm,),
        in_specs=[pl.BlockSpec((tm, ATTN_W), lambda i: (i, 1)), row, pl.BlockSpec(ones.shape, lambda i: (0, 0))],
        out_specs=[row, row],
        out_shape=[jax.ShapeDtypeStruct((s, ATTN_W), BF16), jax.ShapeDtypeStruct((s, ATTN_W), F32)],
        compiler_params=_params(("parallel",), VMEM_BIG),
    )(d_cat, o, ones)


def _attn_bwd(qn, kn, v, dob, lse, dd, slopes, d, acc):
    s = qn.shape[0]
    length, width, tq, nt, nqb = _attn_geometry(s, d)
    view = lambda t: t.reshape(length, width)
    add = acc is not None

    def body(*refs):
        sl_ref, q_ref, k_ref, kh_ref, v_ref, vh_ref, do_ref, l_ref, dd_ref = refs[:9]
        if add:
            dqa_ref, dka_ref, dva_ref = refs[9:12]
        dq_ref, dk_ref, dv_ref, kbuf, vbuf, dkacc, dvacc, kcarry, vcarry = refs[-9:]
        cb, step = pl.program_id(0), pl.program_id(1)
        tile = nt - 1 - step
        biases, ki = _band(sl_ref, cb % 4, d)
        head0 = lax.broadcasted_iota(jnp.int32, (Q_BLOCK, Q_BLOCK), 1) < HEAD_DIM
        kbuf[0:Q_BLOCK, :] = kh_ref[...]
        kbuf[Q_BLOCK:, :] = k_ref[...]
        vbuf[0:Q_BLOCK, :] = vh_ref[...]
        vbuf[Q_BLOCK:, :] = v_ref[...]
        dkacc[...] = jnp.zeros_like(dkacc)
        dvacc[...] = jnp.zeros_like(dvacc)

        @pl.when(step > 0)
        def _():
            dkacc[tq:, :] = kcarry[...]
            dvacc[tq:, :] = vcarry[...]

        def q_block(qb, carry):
            r0 = pl.multiple_of(qb * Q_BLOCK, Q_BLOCK)
            rows, rows2 = pl.ds(r0, Q_BLOCK), pl.ds(r0, 2 * Q_BLOCK)
            qv, dov = q_ref[rows, :], do_ref[rows, :]
            lv, ddv = l_ref[rows, :], dd_ref[rows, :]
            k2, v2 = kbuf[rows2, :], vbuf[rows2, :]
            first_key = jnp.where((tile == 0) & (qb == 0), Q_BLOCK, 0)
            dq_h = []
            dk2 = jnp.zeros((2 * Q_BLOCK, Q_BLOCK), F32)
            dv2 = jnp.zeros((2 * Q_BLOCK, Q_BLOCK), F32)
            for hh in range(2):
                mine = head0 if hh == 0 else jnp.logical_not(head0)
                qm = jnp.where(mine, qv, jnp.zeros_like(qv))
                dom = jnp.where(mine, dov, jnp.zeros_like(dov))
                lcol = jnp.max(jnp.where(mine, lv, -jnp.inf), axis=1, keepdims=True)
                dcol = jnp.max(jnp.where(mine, ddv, -jnp.inf), axis=1, keepdims=True)
                sc = _dot(qm, k2, trans_b=True) + biases[hh]
                sc = jnp.where(ki < first_key, NEG, sc)
                p = jnp.exp(sc - lcol)
                ds = p * (_dot(dom, v2, trans_b=True) - dcol)
                dq_h.append(_dot(ds.astype(BF16), k2))
                dk2 = dk2 + _dot(ds.T.astype(BF16), qm)
                dv2 = dv2 + _dot(p.T.astype(BF16), dom)
            dq = jnp.where(head0, dq_h[0], dq_h[1])
            if add:
                dq = dq + dqa_ref[rows, :]
            dq_ref[rows, :] = dq
            dkacc[rows2, :] += dk2
            dvacc[rows2, :] += dv2
            return carry

        lax.fori_loop(0, nqb, q_block, 0)
        dk, dv = dkacc[Q_BLOCK:, :], dvacc[Q_BLOCK:, :]
        if add:
            dk, dv = dk + dka_ref[...], dv + dva_ref[...]
        dk_ref[...] = dk
        dv_ref[...] = dv
        kcarry[...] = dkacc[0:Q_BLOCK, :]
        vcarry[...] = dvacc[0:Q_BLOCK, :]

    main = pl.BlockSpec((tq, Q_BLOCK), lambda cb, st: (nt - 1 - st, cb))
    halo = pl.BlockSpec((Q_BLOCK, Q_BLOCK), lambda cb, st: (jnp.maximum((nt - 1 - st) * nqb - 1, 0), cb))
    ins = [slopes, view(qn), view(kn), view(kn), view(v), view(v), view(dob), view(lse), view(dd)]
    ins += [view(t) for t in acc] if add else []
    outs = pl.pallas_call(
        body, name=f"attn_bwd_d{d}", grid=(width // Q_BLOCK, nt),
        in_specs=[pl.BlockSpec(memory_space=pltpu.SMEM), main, main, halo, main, halo, main, main, main]
        + ([main] * 3 if add else []),
        out_specs=[main] * 3,
        out_shape=[jax.ShapeDtypeStruct((length, width), F32)] * 3,
        scratch_shapes=[pltpu.VMEM((Q_BLOCK + tq, Q_BLOCK), BF16)] * 2 + [pltpu.VMEM((Q_BLOCK + tq, Q_BLOCK), F32)] * 2
        + [pltpu.VMEM((Q_BLOCK, Q_BLOCK), F32)] * 2,
        compiler_params=_params(("parallel", "arbitrary"), VMEM_BIG),
    )(*ins)
    return tuple(t.reshape(s, ATTN_W) for t in outs)


def _attn_bwd_post(dq, dk, dv, qk, qg, kg, ones):
    s = dq.shape[0]
    tm = _tile(s, 512, 8)
    aw = ATTN_W

    def body(dq_ref, dk_ref, dv_ref, qk_ref, qg_ref, kg_ref, ones_ref, out_ref, gq_ref, gk_ref):
        @pl.when(pl.program_id(0) == 0)
        def _():
            gq_ref[...] = jnp.zeros_like(gq_ref)
            gk_ref[...] = jnp.zeros_like(gk_ref)

        def norm_bwd(dy, raw, g, scale):
            r = lax.rsqrt(_head_sum(raw * raw, ones_ref) * (1.0 / HEAD_DIM) + EPS)
            xh = raw * r
            gd = dy * (g * scale)
            dx = r * (gd - xh * (_head_sum(gd * xh, ones_ref) * (1.0 / HEAD_DIM)))
            gsum = jnp.sum(dy * xh, axis=0, keepdims=True) * scale
            for shift in (256, 128, 64):
                gsum = gsum + pltpu.roll(gsum, shift, 1)
            return dx, gsum

        dxq, gq = norm_bwd(dq_ref[...], qk_ref[:, :aw], qg_ref[...], HEAD_DIM ** -0.5)
        dxk, gk = norm_bwd(dk_ref[...], qk_ref[:, aw:], kg_ref[...], 1.0)
        out_ref[:, :aw] = dxq.astype(BF16)
        out_ref[:, aw:2 * aw] = dxk.astype(BF16)
        out_ref[:, 2 * aw:] = dv_ref[...].astype(BF16)
        gq_ref[...] += gq
        gk_ref[...] += gk

    row = lambda w: pl.BlockSpec((tm, w), lambda i: (i, 0))
    full = lambda t: pl.BlockSpec(t.shape, lambda i: (0, 0))
    vec = pl.BlockSpec((1, aw), lambda i: (0, 0))
    return pl.pallas_call(
        body, name="attn_bwd_post", grid=(s // tm,),
        in_specs=[row(aw), row(aw), row(aw), row(2 * aw), full(qg), full(kg), full(ones)],
        out_specs=[row(3 * aw), vec, vec],
        out_shape=[jax.ShapeDtypeStruct((s, 3 * aw), BF16), jax.ShapeDtypeStruct((1, aw), F32),
                   jax.ShapeDtypeStruct((1, aw), F32)],
        compiler_params=_params(("arbitrary",), VMEM_BIG),
    )(dq, dk, dv, qk, qg, kg, ones)


def _sum_parts(name, parts):
    _, n, w = parts.shape
    tn = _tile(n, 256, 8)

    def body(p_ref, o_ref):
        acc = p_ref[0].astype(F32)
        for j in range(1, N_DEV):
            acc = acc + p_ref[j].astype(F32)
        o_ref[...] = acc

    return pl.pallas_call(
        body, name=name, grid=(n // tn,),
        in_specs=[pl.BlockSpec((N_DEV, tn, w), lambda i: (0, i, 0))],
        out_specs=pl.BlockSpec((tn, w), lambda i: (i, 0)),
        out_shape=jax.ShapeDtypeStruct((n, w), F32),
        compiler_params=_params(("parallel",), VMEM_BIG),
    )(parts)


def _adamw(name, w, g, m, v):
    n, cols = w.shape
    tn = _tile(n, 256, 8)
    c1 = 1.0 - ADAM_B1 ** ADAM_STEP
    c2 = 1.0 - ADAM_B2 ** ADAM_STEP

    def body(w_ref, g_ref, m_ref, v_ref, d_ref, nm_ref, nv_ref):
        gv = g_ref[...]
        nm = ADAM_B1 * m_ref[...] + (1.0 - ADAM_B1) * gv
        nv = ADAM_B2 * v_ref[...] + (1.0 - ADAM_B2) * (gv * gv)
        nm_ref[...] = nm
        nv_ref[...] = nv
        d_ref[...] = -ADAM_LR * ((nm / c1) / (jnp.sqrt(nv / c2) + ADAM_EPS) + ADAM_WD * w_ref[...])

    blk = pl.BlockSpec((tn, cols), lambda i: (i, 0))
    return pl.pallas_call(
        body, name=name, grid=(n // tn,),
        in_specs=[blk] * 4, out_specs=[blk] * 3,
        out_shape=[jax.ShapeDtypeStruct((n, cols), F32)] * 3,
        compiler_params=_params(("parallel",), VMEM_BIG),
    )(w, g, m, v)


def _pack(vectors, width):
    flat = jnp.concatenate([t.reshape(-1) for t in vectors])
    rows = -(-flat.shape[0] // (8 * width)) * 8
    return jnp.pad(flat, (0, rows * width - flat.shape[0])).reshape(rows, width)


def _unpack(packed, shapes):
    flat = packed.reshape(-1)
    out, at = [], 0
    for shp in shapes:
        size = 1
        for dim in shp:
            size *= dim
        out.append(flat[at:at + size].reshape(shp))
        at += size
    return out


def kernel(x, norm1_g, w_in, conv_w, conv_b, cn_g, cn_b, q_norm_g, k_norm_g, w_out, norm2_g, w_up, ffconv_w, ffconv_b, w_down, loss_target, m_norm1_g, m_w_in, m_conv_w, m_conv_b, m_cn_g, m_cn_b, m_q_norm_g, m_k_norm_g, m_w_out, m_norm2_g, m_w_up, m_ffconv_w, m_ffconv_b, m_w_down, v_norm1_g, v_w_in, v_conv_w, v_conv_b, v_cn_g, v_cn_b, v_q_norm_g, v_k_norm_g, v_w_out, v_norm2_g, v_w_up, v_ffconv_w, v_ffconv_b, v_w_down):
    s, d = x.shape[1], x.shape[2]
    f = w_down.shape[0] * N_DEV
    n_in = w_in.shape[1] * N_DEV
    xs, target = x.reshape(s, d), loss_target.reshape(s, d)
    me = _linear((lax.axis_index("x"), lax.axis_index("y"), lax.axis_index("c")))

    shards = [w_in.T.astype(BF16), w_out.astype(BF16), w_up.T.astype(BF16), w_down.astype(BF16),
              _pack([conv_w, ffconv_w], 128)]
    g_in, g_out, g_up, g_down, g_filt = _all_gather("gather_weights", shards)
    w_in_t = g_in.reshape(n_in, d)
    w_out_f = g_out.reshape(2 * CONV_CH, d)
    w_up_t = g_up.reshape(2 * f, d)
    w_down_f = g_down.reshape(f, d)
    filt = g_filt.reshape(N_DEV, -1)
    n_cw = conv_w.size
    cw = filt[:, :n_cw].reshape(N_DEV, CONV_K, -1).transpose(1, 0, 2).reshape(CONV_K, CONV_CH)
    cw = jnp.pad(cw, ((0, CONV_HALO - CONV_K), (0, 0)))
    fw = filt[:, n_cw:n_cw + ffconv_w.size].reshape(N_DEV, FF_K, -1).transpose(1, 0, 2).reshape(FF_K, 2, f)
    fb = ffconv_b.reshape(2, f)
    row = lambda t: t.reshape(1, -1)
    g1, g2 = row(norm1_g), row(norm2_g)
    qg, kg = row(jnp.tile(q_norm_g, N_HEADS)), row(jnp.tile(k_norm_g, N_HEADS))
    lanes = jnp.arange(ATTN_W) // HEAD_DIM
    ones = (lanes[:, None] == lanes[None, :]).astype(BF16)
    slopes = 2.0 ** (-8.0 * jnp.arange(1, N_HEADS + 1, dtype=F32) / N_HEADS)

    h, a, qk, qkv = _proj_fwd(xs, g1, w_in_t, qg, kg, ones)
    u1, u = _conv_fwd(a, cw, row(conv_b), row(cn_g), row(cn_b))
    o, lse = _attn_forward(qkv, slopes)
    x1, h2 = _mix_out(u, o, w_out_f, xs, g2)
    up_g, up_v = _up_fwd(h2, w_up_t, f)
    act = _ff_act(up_g, up_v, fw, fb)
    loss_acc, dyf, dyb = _down_loss(act, w_down_f, x1, target)
    loss = lax.psum(loss_acc[0, 0], MESH_AXES)

    tk = _tile(s, 512, 8)
    gw_down = _mm_tn("grad_w_down", [act], dyb, _tile(f, 1408, 128), tk)
    d_act = _mm("d_act", dyb, w_down_f, True, F32, _tile(s, 512, 8), _tile(f, 1408, 128))
    dpre_g, dpre_v, gfw, gfb = _ff_bwd(up_g, up_v, d_act, fw, fb)
    gw_up = _mm_tn("grad_w_up", [dpre_g, dpre_v], h2, _tile(f, 1408, 128), tk)
    dx1f, dx1b, gg2 = _up_bwd(dpre_g, dpre_v, w_up_t, x1, dyf, g2)
    gw_out = _mm_tn("grad_w_out", [u, o], dx1b, CONV_CH, tk)
    d_cat = _mm("d_cat", dx1b, w_out_f, True, F32, _tile(s, 512, 8), 2 * CONV_CH)
    d_a, gcw, gcb, gcg, gcbeta = _conv_bwd(a, u1, d_cat, cw, row(cn_g), row(cn_b))
    dq, dk, dv = _attn_backward(qkv, d_cat, o, lse, slopes, ones[:Q_BLOCK, :Q_BLOCK])
    d_qkv, gqg, gkg = _attn_bwd_post(dq, dk, dv, qk, qg, kg, ones)
    gw_in = _mm_tn("grad_w_in", [d_a, d_qkv], h, 512, tk)
    grad_x, gg1 = _proj_bwd(d_a, d_qkv, w_in_t, xs, dx1f, g1)

    big = [gw_in, gw_out, gw_up, gw_down]
    recv = _exchange("exchange_weight_grads", [t.reshape(N_DEV, t.shape[0] // N_DEV, d) for t in big])
    g_w_in_t, g_w_out, g_w_up_t, g_w_down = [_sum_parts(f"sum_grad_{i}", r) for i, r in enumerate(recv)]
    g_w_in, g_w_up = g_w_in_t.T, g_w_up_t.T

    small_shapes = [(d,), (d,), (CONV_CH,), (CONV_CH,), (CONV_CH,), (ATTN_W,), (ATTN_W,), (2 * f,),
                    (CONV_K, CONV_CH), (FF_K, 2 * f)]
    small = _pack([gg1, gg2, gcb, gcg, gcbeta, gqg, gkg, gfb, gcw[:CONV_K], gfw], 1024)
    (small_all,) = _all_gather("gather_small_grads", [small])
    sg1, sg2, scb, scg, scbeta, sqg, skg, sfb, scw, sfw = _unpack(_sum_parts("sum_small_grads", small_all), small_shapes)
    cwl, fwl = conv_w.shape[1], ffconv_w.shape[1]
    g_small = [sg1, lax.dynamic_slice_in_dim(scw, me * cwl, cwl, 1), scb, scg, scbeta,
               sqg[:HEAD_DIM], skg[:HEAD_DIM], sg2, lax.dynamic_slice_in_dim(sfw, me * fwl, fwl, 1), sfb]

    w_small = [norm1_g, conv_w, conv_b, cn_g, cn_b, q_norm_g, k_norm_g, norm2_g, ffconv_w, ffconv_b]
    m_small = [m_norm1_g, m_conv_w, m_conv_b, m_cn_g, m_cn_b, m_q_norm_g, m_k_norm_g, m_norm2_g, m_ffconv_w, m_ffconv_b]
    v_small = [v_norm1_g, v_conv_w, v_conv_b, v_cn_g, v_cn_b, v_q_norm_g, v_k_norm_g, v_norm2_g, v_ffconv_w, v_ffconv_b]
    shapes = [t.shape for t in w_small]
    packed = _adamw("adamw_small", *[_pack(ts, 128) for ts in (w_small, g_small, m_small, v_small)])
    d_small, nm_small, nv_small = [_unpack(p, shapes) for p in packed]
    upd = {}
    for name, wt, gt, mt, vt in (("w_in", w_in, g_w_in, m_w_in, v_w_in), ("w_out", w_out, g_w_out, m_w_out, v_w_out),
                                 ("w_up", w_up, g_w_up, m_w_up, v_w_up), ("w_down", w_down, g_w_down, m_w_down, v_w_down)):
        upd[name] = (gt,) + tuple(_adamw("adamw_" + name, wt, gt, mt, vt))
    order = ["norm1_g", "w_in", "conv_w", "conv_b", "cn_g", "cn_b", "q_norm_g", "k_norm_g", "w_out", "norm2_g",
             "w_up", "ffconv_w", "ffconv_b", "w_down"]
    small_names = ["norm1_g", "conv_w", "conv_b", "cn_g", "cn_b", "q_norm_g", "k_norm_g", "norm2_g", "ffconv_w", "ffconv_b"]
    for i, name in enumerate(small_names):
        upd[name] = (g_small[i], d_small[i], nm_small[i], nv_small[i])
    outs = [loss, grad_x.reshape(x.shape)]
    for field in range(4):
        outs += [upd[name][field] for name in order]
    return tuple(outs)
```

```python
import functools

import jax
import jax.numpy as jnp
from jax import lax
from jax.experimental import pallas as pl
from jax.experimental.pallas import tpu as pltpu

F32, BF16 = jnp.float32, jnp.bfloat16
N_DEV = 8
N_HEADS, HEAD_DIM = 8, 64
CONV_CH = 512
ATTN_W = N_HEADS * HEAD_DIM
CONV_K, FF_K = 31, 3
CONV_HALO = 32
FF_HALO = 8
PATTERN_DILATIONS = (1, 4, 16)
Q_BLOCK = 128
EPS = 1e-6
NEG = -1e30
ADAM_LR, ADAM_B1, ADAM_B2, ADAM_EPS, ADAM_WD, ADAM_STEP = 0.001, 0.9, 0.999, 1e-08, 0.01, 10
MESH_AXES = ("x", "y", "c")
VMEM_BIG = 56 * 1024 * 1024
ANY = pl.BlockSpec(memory_space=pl.ANY)


def _params(sem=None, vmem=None):
    return pltpu.CompilerParams(dimension_semantics=sem, vmem_limit_bytes=vmem)


def _dot(a, b, trans_a=False, trans_b=False):
    dims = (((0 if trans_a else 1,), (1 if trans_b else 0,)), ((), ()))
    return lax.dot_general(a, b, dims, preferred_element_type=F32)


def _sigmoid(z):
    return 1.0 / (1.0 + jnp.exp(-z))


def _tile(n, pref, mult):
    if n <= pref:
        return n
    t = (pref // mult) * mult
    while n % t:
        t -= mult
    return t


def _head_sum(v, ones_ref):
    hi = v.astype(BF16)
    lo = (v - hi.astype(F32)).astype(BF16)
    ones = ones_ref[...]
    return _dot(hi, ones) + _dot(lo, ones)


def _linear(p):
    return 4 * p[0] + 2 * p[1] + p[2]


class _Gather:
    def __init__(self, arrs):
        self.arrs = list(arrs)
        self.n = len(self.arrs)
        self.out_shape = [jax.ShapeDtypeStruct((N_DEV,) + a.shape, a.dtype) for a in self.arrs]

    def _setup(self, ins, outs, sems):
        send_sems, recv_sems, local_sems = sems
        x, y, c = lax.axis_index("x"), lax.axis_index("y"), lax.axis_index("c")
        me, sib = (x, y, c), (x, y, 1 - c)
        chips = [(1 - x, y), (x, 1 - y), (1 - x, 1 - y)]

        def copy(a, k, block, to, src=None):
            dst = outs[a].at[_linear(block)]
            return pltpu.make_async_remote_copy(
                src_ref=dst if src is None else src, dst_ref=dst,
                send_sem=send_sems.at[7 * a + k], recv_sem=recv_sems.at[7 * a + k],
                device_id=to, device_id_type=pl.DeviceIdType.MESH)

        mine = [pltpu.make_async_copy(ins[a], outs[a].at[_linear(me)], local_sems.at[a]) for a in range(self.n)]
        first = []
        for a in range(self.n):
            first.append(copy(a, 0, me, sib, src=ins[a]))
            first += [copy(a, 1 + j, me, (*chip, c), src=ins[a]) for j, chip in enumerate(chips)]
        return copy, mine, first, me, sib, chips, c

    def start(self, ins, outs, sems):
        _, mine, first, *_ = self._setup(ins, outs, sems)
        for cp in mine + first:
            cp.start()

    def finish(self, ins, outs, sems):
        copy, mine, first, me, sib, chips, c = self._setup(ins, outs, sems)
        passed = []
        for j, chip in enumerate(chips):
            for a in range(self.n):
                copy(a, 1 + j, (*chip, c), me).wait_recv()
                fwd = copy(a, 4 + j, (*chip, c), sib)
                fwd.start()
                passed.append(fwd)
        for a in range(self.n):
            copy(a, 0, sib, me).wait_recv()
            for j, chip in enumerate(chips):
                copy(a, 4 + j, (*chip, 1 - c), me).wait_recv()
        for cp in first + passed:
            cp.wait_send()
        for cp in mine:
            cp.wait()


class _Exchange:
    def __init__(self, arrs):
        self.arrs = list(arrs)
        self.n = len(self.arrs)
        self.out_shape = [jax.ShapeDtypeStruct(a.shape, a.dtype) for a in self.arrs]

    def _setup(self, ins, outs, sems):
        send_sems, recv_sems, local_sems = sems
        x, y, c = lax.axis_index("x"), lax.axis_index("y"), lax.axis_index("c")
        me = _linear((x, y, c))
        mine = [pltpu.make_async_copy(ins[a].at[me], outs[a].at[me], local_sems.at[a]) for a in range(self.n)]
        sends, recvs = [], []
        for p in range(1, N_DEV):
            peer = (1 - x if p & 4 else x, 1 - y if p & 2 else y, 1 - c if p & 1 else c)
            for a in range(self.n):
                sem = dict(send_sem=send_sems.at[7 * a + p - 1], recv_sem=recv_sems.at[7 * a + p - 1],
                           device_id=peer, device_id_type=pl.DeviceIdType.MESH)
                sends.append(pltpu.make_async_remote_copy(
                    src_ref=ins[a].at[_linear(peer)], dst_ref=outs[a].at[me], **sem))
                recvs.append(pltpu.make_async_remote_copy(
                    src_ref=ins[a].at[_linear(peer)], dst_ref=outs[a].at[_linear(peer)], **sem))
        return mine, sends, recvs

    def start(self, ins, outs, sems):
        mine, sends, _ = self._setup(ins, outs, sems)
        for cp in mine + sends:
            cp.start()

    def finish(self, ins, outs, sems):
        mine, sends, recvs = self._setup(ins, outs, sems)
        for cp in recvs:
            cp.wait_recv()
        for cp in sends:
            cp.wait_send()
        for cp in mine:
            cp.wait()


def _rider_scratch(rider):
    return [pltpu.SemaphoreType.DMA((7 * rider.n,)), pltpu.SemaphoreType.DMA((7 * rider.n,)),
            pltpu.SemaphoreType.DMA((rider.n,))]


def _communicate(name, rider):
    na = rider.n

    def body(*refs):
        ins, outs, sems = refs[:na], refs[na:2 * na], refs[2 * na:]
        rider.start(ins, outs, sems)
        rider.finish(ins, outs, sems)

    return pl.pallas_call(
        body, name=name, out_shape=rider.out_shape, in_specs=[ANY] * na, out_specs=[ANY] * na,
        scratch_shapes=_rider_scratch(rider),
    )(*rider.arrs)


def _call(body, rider=None, *, name, grid, in_specs, out_specs, out_shape, scratch_shapes=(), compiler_params, args):
    if rider is None:
        out = pl.pallas_call(body, name=name, grid=grid, in_specs=in_specs, out_specs=out_specs, out_shape=out_shape,
                             scratch_shapes=list(scratch_shapes), compiler_params=compiler_params)(*args)
        return out, None
    n_in, n_out, n_scr, na = len(in_specs), len(out_specs), len(scratch_shapes), rider.n

    def carried(*refs):
        ins, refs = refs[:n_in], refs[n_in:]
        r_ins, refs = refs[:na], refs[na:]
        outs, refs = refs[:n_out], refs[n_out:]
        r_outs, refs = refs[:na], refs[na:]
        scratch, sems = refs[:n_scr], refs[n_scr:]
        ids = [pl.program_id(ax) for ax in range(len(grid))]
        first, last = ids[0] == 0, ids[0] == grid[0] - 1
        for ax in range(1, len(grid)):
            first, last = first & (ids[ax] == 0), last & (ids[ax] == grid[ax] - 1)

        @pl.when(first)
        def _():
            rider.start(r_ins, r_outs, sems)

        body(*ins, *outs, *scratch)

        @pl.when(last)
        def _():
            rider.finish(r_ins, r_outs, sems)

    out = pl.pallas_call(
        carried, name=name, grid=grid, in_specs=list(in_specs) + [ANY] * na, out_specs=list(out_specs) + [ANY] * na,
        out_shape=list(out_shape) + rider.out_shape, scratch_shapes=list(scratch_shapes) + _rider_scratch(rider),
        compiler_params=compiler_params)(*args, *rider.arrs)
    return out[:n_out], out[n_out:]


def _mm(name, a, w, trans_b, out_dtype, tm, tn):
    m, k = a.shape
    n = w.shape[0] if trans_b else w.shape[1]

    def body(a_ref, w_ref, o_ref):
        o_ref[...] = _dot(a_ref[...], w_ref[...], trans_b=trans_b).astype(o_ref.dtype)

    w_spec = pl.BlockSpec((tn, k), lambda j, i: (j, 0)) if trans_b else pl.BlockSpec((k, tn), lambda j, i: (0, j))
    return pl.pallas_call(
        body, name=name, grid=(n // tn, m // tm),
        in_specs=[pl.BlockSpec((tm, k), lambda j, i: (i, 0)), w_spec],
        out_specs=pl.BlockSpec((tm, tn), lambda j, i: (i, j)),
        out_shape=jax.ShapeDtypeStruct((m, n), out_dtype),
        compiler_params=_params(("parallel", "parallel"), VMEM_BIG),
    )(a, w)


def _mm_tn(name, parts, b, tn, tk):
    s, d = b.shape
    blocks = [p.shape[1] // tn for p in parts]
    starts = [sum(blocks[:i]) for i in range(len(parts))]
    np_ = len(parts)

    nk = s // tk

    def body(*refs):
        a_refs, b_ref, o_ref, acc_ref = refs[:np_], refs[np_], refs[np_ + 1], refs[np_ + 2]
        j, kk = pl.program_id(0), pl.program_id(1)

        @pl.when(kk == 0)
        def _():
            acc_ref[...] = jnp.zeros_like(acc_ref)

        for p in range(np_):
            @pl.when((j >= starts[p]) & (j < starts[p] + blocks[p]))
            def _(p=p):
                acc_ref[...] += _dot(a_refs[p][...].astype(BF16), b_ref[...], trans_a=True)

        @pl.when(kk == nk - 1)
        def _():
            o_ref[...] = acc_ref[...].astype(o_ref.dtype)

    def a_spec(p):
        return pl.BlockSpec((tk, tn), lambda j, kk: (kk, jnp.clip(j - starts[p], 0, blocks[p] - 1)))

    return pl.pallas_call(
        body, name=name, grid=(sum(blocks), nk),
        in_specs=[a_spec(p) for p in range(np_)] + [pl.BlockSpec((tk, d), lambda j, kk: (kk, 0))],
        out_specs=pl.BlockSpec((tn, d), lambda j, kk: (j, 0)),
        out_shape=jax.ShapeDtypeStruct((sum(blocks) * tn, d), BF16),
        scratch_shapes=[pltpu.VMEM((tn, d), F32)],
        compiler_params=_params(("parallel", "arbitrary"), VMEM_BIG),
    )(*parts, b)


def _proj_fwd(x, g1, w_in_t, qg, kg, ones, rider=None):
    s, d = x.shape
    tm = _tile(s, 256, 8)
    c2, aw = 2 * CONV_CH, ATTN_W

    def body(x_ref, g_ref, w_ref, qg_ref, kg_ref, ones_ref, h_ref, a_ref, qk_ref, qkv_ref):
        xv = x_ref[...]
        r = lax.rsqrt(jnp.mean(xv * xv, axis=-1, keepdims=True) + EPS)
        h = (xv * r * g_ref[...]).astype(BF16)
        h_ref[...] = h
        proj = _dot(h, w_ref[...], trans_b=True)
        a_ref[...] = proj[:, :c2]
        qk_ref[...] = proj[:, c2:c2 + 2 * aw]
        q, k = proj[:, c2:c2 + aw], proj[:, c2 + aw:c2 + 2 * aw]
        rq = lax.rsqrt(_head_sum(q * q, ones_ref) * (1.0 / HEAD_DIM) + EPS)
        rk = lax.rsqrt(_head_sum(k * k, ones_ref) * (1.0 / HEAD_DIM) + EPS)
        qkv_ref[:, :aw] = q * rq * qg_ref[...] * (HEAD_DIM ** -0.5)
        qkv_ref[:, aw:2 * aw] = k * rk * kg_ref[...]
        qkv_ref[:, 2 * aw:] = proj[:, c2 + 2 * aw:]

    row = lambda w: pl.BlockSpec((tm, w), lambda i: (i, 0))
    full = lambda a: pl.BlockSpec(a.shape, lambda i: (0, 0))
    outs, carried = _call(
        body, rider, name="proj_fwd", grid=(s // tm,),
        in_specs=[row(d), full(g1), full(w_in_t), full(qg), full(kg), full(ones)],
        out_specs=[row(d), row(c2), row(2 * aw), row(3 * aw)],
        out_shape=[jax.ShapeDtypeStruct((s, d), BF16), jax.ShapeDtypeStruct((s, c2), F32),
                   jax.ShapeDtypeStruct((s, 2 * aw), F32), jax.ShapeDtypeStruct((s, 3 * aw), F32)],
        compiler_params=_params(("arbitrary",), VMEM_BIG),
        args=(x, g1, w_in_t, qg, kg, ones))
    return (*outs, carried)


def _mix_out(u, o, w_out, x, g2):
    s, d = x.shape
    tm = _tile(s, 512, 8)

    def body(u_ref, o_ref, w_ref, x_ref, g_ref, x1_ref, h2_ref):
        x1 = x_ref[...] + _dot(u_ref[...], w_ref[:CONV_CH, :]) + _dot(o_ref[...].astype(BF16), w_ref[CONV_CH:, :])
        x1_ref[...] = x1
        r = lax.rsqrt(jnp.mean(x1 * x1, axis=-1, keepdims=True) + EPS)
        h2_ref[...] = (x1 * r * g_ref[...]).astype(BF16)

    row = lambda w: pl.BlockSpec((tm, w), lambda i: (i, 0))
    full = lambda a: pl.BlockSpec(a.shape, lambda i: (0, 0))
    return pl.pallas_call(
        body, name="mix_out", grid=(s // tm,),
        in_specs=[row(CONV_CH), row(ATTN_W), full(w_out), row(d), full(g2)],
        out_specs=[row(d), row(d)],
        out_shape=[jax.ShapeDtypeStruct((s, d), F32), jax.ShapeDtypeStruct((s, d), BF16)],
        compiler_params=_params(("parallel",), VMEM_BIG),
    )(u, o, w_out, x, g2)


def _up_fwd(h2, w_up_t, f):
    s, d = h2.shape
    tm, tn = _tile(s, 512, 8), _tile(f, 1408, 128)
    nb = f // tn

    def body(h_ref, wg_ref, wv_ref, g_ref, v_ref):
        hv = h_ref[...]
        g_ref[...] = _dot(hv, wg_ref[...], trans_b=True)
        v_ref[...] = _dot(hv, wv_ref[...], trans_b=True)

    return pl.pallas_call(
        body, name="up_fwd", grid=(nb, s // tm),
        in_specs=[pl.BlockSpec((tm, d), lambda j, i: (i, 0)), pl.BlockSpec((tn, d), lambda j, i: (j, 0)),
                  pl.BlockSpec((tn, d), lambda j, i: (j + nb, 0))],
        out_specs=[pl.BlockSpec((tm, tn), lambda j, i: (i, j))] * 2,
        out_shape=[jax.ShapeDtypeStruct((s, f), F32)] * 2,
        compiler_params=_params(("parallel", "parallel"), VMEM_BIG),
    )(h2, w_up_t, w_up_t)


def _down_loss(act, w_down, x1, target):
    s, d = x1.shape
    f = act.shape[1]
    tm = _tile(s, 256, 8)

    def body(a_ref, w_ref, x1_ref, t_ref, loss_ref, dyf_ref, dyb_ref):
        @pl.when(pl.program_id(0) == 0)
        def _():
            loss_ref[...] = jnp.zeros_like(loss_ref)

        diff = x1_ref[...] + _dot(a_ref[...], w_ref[...]) - t_ref[...]
        sq = jnp.sum(jnp.sum(diff * diff, axis=1, keepdims=True), axis=0, keepdims=True)
        loss_ref[...] += jnp.broadcast_to(sq * (0.5 / d), loss_ref.shape)
        dy = diff * (1.0 / d)
        dyf_ref[...] = dy
        dyb_ref[...] = dy.astype(BF16)

    row = lambda w: pl.BlockSpec((tm, w), lambda i: (i, 0))
    return pl.pallas_call(
        body, name="down_loss", grid=(s // tm,),
        in_specs=[row(f), pl.BlockSpec((f, d), lambda i: (0, 0)), row(d), row(d)],
        out_specs=[pl.BlockSpec((8, 128), lambda i: (0, 0)), row(d), row(d)],
        out_shape=[jax.ShapeDtypeStruct((8, 128), F32), jax.ShapeDtypeStruct((s, d), F32),
                   jax.ShapeDtypeStruct((s, d), BF16)],
        compiler_params=_params(("arbitrary",), VMEM_BIG),
    )(act, w_down, x1, target)


def _norm_bwd_tail(dh, xv, g, resid, d):
    r = lax.rsqrt(jnp.mean(xv * xv, axis=-1, keepdims=True) + EPS)
    xh = xv * r
    gd = dh * g
    dx = r * (gd - xh * (jnp.sum(gd * xh, axis=-1, keepdims=True) * (1.0 / d)))
    return resid + dx, jnp.sum(dh * xh, axis=0, keepdims=True)


def _up_bwd(dg, dv, w_up_t, x1, dyf, g2):
    s, d = x1.shape
    f = dg.shape[1]
    tm = _tile(s, 256, 8)

    def body(dg_ref, dv_ref, w_ref, x1_ref, dy_ref, g_ref, dxf_ref, dxb_ref, gg_ref):
        @pl.when(pl.program_id(0) == 0)
        def _():
            gg_ref[...] = jnp.zeros_like(gg_ref)

        dh = _dot(dg_ref[...], w_ref[:f, :]) + _dot(dv_ref[...], w_ref[f:, :])
        dx, gg = _norm_bwd_tail(dh, x1_ref[...], g_ref[...], dy_ref[...], d)
        dxf_ref[...] = dx
        dxb_ref[...] = dx.astype(BF16)
        gg_ref[...] += gg

    row = lambda w: pl.BlockSpec((tm, w), lambda i: (i, 0))
    full = lambda a: pl.BlockSpec(a.shape, lambda i: (0, 0))
    return pl.pallas_call(
        body, name="up_bwd", grid=(s // tm,),
        in_specs=[row(f), row(f), full(w_up_t), row(d), row(d), full(g2)],
        out_specs=[row(d), row(d), pl.BlockSpec((1, d), lambda i: (0, 0))],
        out_shape=[jax.ShapeDtypeStruct((s, d), F32), jax.ShapeDtypeStruct((s, d), BF16),
                   jax.ShapeDtypeStruct((1, d), F32)],
        compiler_params=_params(("arbitrary",), VMEM_BIG),
    )(dg, dv, w_up_t, x1, dyf, g2)


def _proj_bwd(d_a, d_qkv, w_in_t, x, dx1, g1, rider=None):
    s, d = x.shape
    na, nq = d_a.shape[1], d_qkv.shape[1]
    tm = _tile(s, 256, 8)

    def body(da_ref, dq_ref, w_ref, x_ref, r_ref, g_ref, gx_ref, gg_ref):
        @pl.when(pl.program_id(0) == 0)
        def _():
            gg_ref[...] = jnp.zeros_like(gg_ref)

        dh = _dot(da_ref[...], w_ref[:na, :]) + _dot(dq_ref[...], w_ref[na:, :])
        dx, gg = _norm_bwd_tail(dh, x_ref[...], g_ref[...], r_ref[...], d)
        gx_ref[...] = dx
        gg_ref[...] += gg

    row = lambda w: pl.BlockSpec((tm, w), lambda i: (i, 0))
    full = lambda a: pl.BlockSpec(a.shape, lambda i: (0, 0))
    outs, carried = _call(
        body, rider, name="proj_bwd", grid=(s // tm,),
        in_specs=[row(na), row(nq), full(w_in_t), row(d), row(d), full(g1)],
        out_specs=[row(d), pl.BlockSpec((1, d), lambda i: (0, 0))],
        out_shape=[jax.ShapeDtypeStruct((s, d), F32), jax.ShapeDtypeStruct((1, d), F32)],
        compiler_params=_params(("arbitrary",), VMEM_BIG),
        args=(d_a, d_qkv, w_in_t, x, dx1, g1))
    return (*outs, carried)


CONV_CHUNK = 64


def _glu(av):
    return av[:, :CONV_CH] * _sigmoid(av[:, CONV_CH:])


def _layer_norm_stats(u1):
    mu = jnp.mean(u1, axis=-1, keepdims=True)
    cen = u1 - mu
    rstd = lax.rsqrt(jnp.mean(cen * cen, axis=-1, keepdims=True) + EPS)
    return cen * rstd, rstd


def _conv_fwd(a, cw, cb, cg, cbeta, rider=None):
    s = a.shape[0]
    tm = _tile(s, 256, CONV_CHUNK)
    hb = tm // CONV_HALO

    def body(a_ref, ap_ref, cw_ref, cb_ref, cg_ref, cbeta_ref, u1_ref, u_ref, ubuf):
        i = pl.program_id(0)
        ubuf[0:CONV_HALO, :] = jnp.where(i > 0, _glu(ap_ref[...]), 0.0)
        ubuf[CONV_HALO:, :] = _glu(a_ref[...])
        for c0 in range(0, tm, CONV_CHUNK):
            acc = jnp.broadcast_to(cb_ref[...], (CONV_CHUNK, CONV_CH))
            for k in range(CONV_K):
                acc = acc + cw_ref[k:k + 1, :] * ubuf[c0 + 2 + k:c0 + 2 + k + CONV_CHUNK, :]
            u1_ref[c0:c0 + CONV_CHUNK, :] = acc
        xh, _ = _layer_norm_stats(u1_ref[...])
        z = xh * cg_ref[...] + cbeta_ref[...]
        u_ref[...] = (z * _sigmoid(z)).astype(BF16)

    full = lambda t: pl.BlockSpec(t.shape, lambda i: (0, 0))
    outs, carried = _call(
        body, rider, name="conv_fwd", grid=(s // tm,),
        in_specs=[pl.BlockSpec((tm, 2 * CONV_CH), lambda i: (i, 0)),
                  pl.BlockSpec((CONV_HALO, 2 * CONV_CH), lambda i: (jnp.maximum(i * hb - 1, 0), 0)),
                  full(cw), full(cb), full(cg), full(cbeta)],
        out_specs=[pl.BlockSpec((tm, CONV_CH), lambda i: (i, 0))] * 2,
        out_shape=[jax.ShapeDtypeStruct((s, CONV_CH), F32), jax.ShapeDtypeStruct((s, CONV_CH), BF16)],
        scratch_shapes=[pltpu.VMEM((CONV_HALO + tm, CONV_CH), F32)],
        compiler_params=_params(("arbitrary",), VMEM_BIG),
        args=(a, a, cw, cb, cg, cbeta))
    return (*outs, carried)


def _conv_bwd(a, u1, d_cat, cw, cg, cbeta, rider=None):
    s = a.shape[0]
    tm = _tile(s, 256, CONV_CHUNK)
    hb = tm // CONV_HALO
    last_halo = s // CONV_HALO - 1
    nt = s // tm
    te = tm + CONV_HALO

    def body(a_ref, ap_ref, u1_ref, u1n_ref, du_ref, dun_ref, cw_ref, cg_ref, cbeta_ref,
             da_ref, gw_ref, gb_ref, gg_ref, gbeta_ref, ubuf, dbuf):
        i = pl.program_id(0)

        @pl.when(i == 0)
        def _():
            gw_ref[...] = jnp.zeros_like(gw_ref)
            gb_ref[...] = jnp.zeros_like(gb_ref)
            gg_ref[...] = jnp.zeros_like(gg_ref)
            gbeta_ref[...] = jnp.zeros_like(gbeta_ref)

        def du1_of(u1, du):
            xh, rstd = _layer_norm_stats(u1)
            z = xh * cg_ref[...] + cbeta_ref[...]
            sz = _sigmoid(z)
            dz = du * (sz * (1.0 + z * (1.0 - sz)))
            dxh = dz * cg_ref[...]
            du1 = rstd * (dxh - jnp.mean(dxh, axis=-1, keepdims=True)
                          - xh * jnp.mean(dxh * xh, axis=-1, keepdims=True))
            return du1, dz, xh

        du1, dz, xh = du1_of(u1_ref[...], du_ref[...])
        gg_ref[...] += jnp.sum(dz * xh, axis=0, keepdims=True)
        gbeta_ref[...] += jnp.sum(dz, axis=0, keepdims=True)
        gb_ref[...] += jnp.sum(du1, axis=0, keepdims=True)
        dbuf[0:tm, :] = du1
        du1n, _, _ = du1_of(u1n_ref[...], jnp.where(i < nt - 1, dun_ref[...], 0.0))
        dbuf[tm:, :] = du1n

        av = a_ref[...]
        ubuf[0:CONV_HALO, :] = jnp.where(i > 0, _glu(ap_ref[...]), 0.0)
        sg = _sigmoid(av[:, CONV_CH:])
        ubuf[CONV_HALO:, :] = av[:, :CONV_CH] * sg

        for k in range(CONV_K):
            part = jnp.zeros((1, CONV_CH), F32)
            for c0 in range(0, tm, CONV_CHUNK):
                prod = dbuf[c0:c0 + CONV_CHUNK, :] * ubuf[c0 + 2 + k:c0 + 2 + k + CONV_CHUNK, :]
                part = part + jnp.sum(prod, axis=0, keepdims=True)
            gw_ref[k:k + 1, :] += part

        for c0 in range(0, tm, CONV_CHUNK):
            acc = jnp.zeros((CONV_CHUNK, CONV_CH), F32)
            for k in range(CONV_K):
                off = c0 + CONV_K - 1 - k
                acc = acc + cw_ref[k:k + 1, :] * dbuf[off:off + CONV_CHUNK, :]
            sgc = sg[c0:c0 + CONV_CHUNK, :]
            avc = av[c0:c0 + CONV_CHUNK, :CONV_CH]
            da_ref[c0:c0 + CONV_CHUNK, :CONV_CH] = (acc * sgc).astype(BF16)
            da_ref[c0:c0 + CONV_CHUNK, CONV_CH:] = (acc * avc * sgc * (1.0 - sgc)).astype(BF16)

    full = lambda t: pl.BlockSpec(t.shape, lambda i: (0, 0))
    vec = pl.BlockSpec((1, CONV_CH), lambda i: (0, 0))
    nxt = lambda i: (jnp.minimum((i + 1) * hb, last_halo), 0)
    outs, carried = _call(
        body, rider, name="conv_bwd", grid=(nt,),
        in_specs=[pl.BlockSpec((tm, 2 * CONV_CH), lambda i: (i, 0)),
                  pl.BlockSpec((CONV_HALO, 2 * CONV_CH), lambda i: (jnp.maximum(i * hb - 1, 0), 0)),
                  pl.BlockSpec((tm, CONV_CH), lambda i: (i, 0)), pl.BlockSpec((CONV_HALO, CONV_CH), nxt),
                  pl.BlockSpec((tm, CONV_CH), lambda i: (i, 0)), pl.BlockSpec((CONV_HALO, CONV_CH), nxt),
                  full(cw), full(cg), full(cbeta)],
        out_specs=[pl.BlockSpec((tm, 2 * CONV_CH), lambda i: (i, 0)),
                   pl.BlockSpec((CONV_HALO, CONV_CH), lambda i: (0, 0)), vec, vec, vec],
        out_shape=[jax.ShapeDtypeStruct((s, 2 * CONV_CH), BF16), jax.ShapeDtypeStruct((CONV_HALO, CONV_CH), F32),
                   jax.ShapeDtypeStruct((1, CONV_CH), F32), jax.ShapeDtypeStruct((1, CONV_CH), F32),
                   jax.ShapeDtypeStruct((1, CONV_CH), F32)],
        scratch_shapes=[pltpu.VMEM((CONV_HALO + tm, CONV_CH), F32), pltpu.VMEM((te, CONV_CH), F32)],
        compiler_params=_params(("arbitrary",), VMEM_BIG),
        args=(a, a, u1, u1, d_cat, d_cat, cw, cg, cbeta))
    return (*outs, carried)


def _ff_tiles(s, f):
    return _tile(s, 256, 8), _tile(f, 1408, 128)


FF_CHUNK, FF_LANES = 64, 128


def _row_chunks(n):
    return [(r0, min(FF_CHUNK, n - r0)) for r0 in range(0, n, FF_CHUNK)]


def _fold8(v):
    acc = v[0:8]
    for r in range(8, v.shape[0], 8):
        acc = acc + v[r:r + 8]
    return acc


def _ff_conv(pre_buf, w_ref, b_ref, half, r0, rows, cols):
    acc = b_ref[half:half + 1, cols]
    for k in range(FF_K):
        off = FF_HALO - (FF_K - 1) + k + r0
        acc = acc + w_ref[k, half:half + 1, cols] * pre_buf[half, off:off + rows, cols]
    return acc


def _ff_act(up_g, up_v, fw, fb):
    s, f = up_g.shape
    tm, tc = _ff_tiles(s, f)
    hb = tm // FF_HALO

    def body(g_ref, gp_ref, v_ref, vp_ref, w_ref, b_ref, act_ref, buf):
        i = pl.program_id(1)
        for half, (m_ref, p_ref) in enumerate(((g_ref, gp_ref), (v_ref, vp_ref))):
            buf[half, 0:FF_HALO, :] = jnp.where(i > 0, p_ref[...], 0.0)
            buf[half, FF_HALO:, :] = m_ref[...]
        for c0 in range(0, tc, FF_LANES):
            cols = slice(c0, c0 + FF_LANES)
            for r0, rows in _row_chunks(tm):
                gate = _ff_conv(buf, w_ref, b_ref, 0, r0, rows, cols)
                val = _ff_conv(buf, w_ref, b_ref, 1, r0, rows, cols)
                act_ref[r0:r0 + rows, cols] = (gate * _sigmoid(gate) * val).astype(BF16)

    main = pl.BlockSpec((tm, tc), lambda j, i: (i, j))
    prev = pl.BlockSpec((FF_HALO, tc), lambda j, i: (jnp.maximum(i * hb - 1, 0), j))
    return pl.pallas_call(
        body, name="ff_act", grid=(f // tc, s // tm),
        in_specs=[main, prev, main, prev, pl.BlockSpec((FF_K, 2, tc), lambda j, i: (0, 0, j)),
                  pl.BlockSpec((2, tc), lambda j, i: (0, j))],
        out_specs=main,
        out_shape=jax.ShapeDtypeStruct((s, f), BF16),
        scratch_shapes=[pltpu.VMEM((2, FF_HALO + tm, tc), F32)],
        compiler_params=_params(("parallel", "parallel"), VMEM_BIG),
    )(up_g, up_g, up_v, up_v, fw, fb)


def _ff_bwd(up_g, up_v, d_act, fw, fb, rider=None):
    s, f = up_g.shape
    tm, tc = _ff_tiles(s, f)
    hb = tm // FF_HALO
    nt = s // tm
    last_halo = s // FF_HALO - 1
    te = tm + FF_HALO

    def body(g_ref, gp_ref, gn_ref, v_ref, vp_ref, vn_ref, da_ref, dan_ref, w_ref, b_ref,
             dg_ref, dv_ref, gw_ref, gb_ref, buf, dbuf, dabuf):
        i = pl.program_id(1)

        @pl.when(i == 0)
        def _():
            gw_ref[...] = jnp.zeros_like(gw_ref)
            gb_ref[...] = jnp.zeros_like(gb_ref)

        for half, (m_ref, p_ref, n_ref) in enumerate(((g_ref, gp_ref, gn_ref), (v_ref, vp_ref, vn_ref))):
            buf[half, 0:FF_HALO, :] = jnp.where(i > 0, p_ref[...], 0.0)
            buf[half, FF_HALO:FF_HALO + tm, :] = m_ref[...]
            buf[half, FF_HALO + tm:, :] = n_ref[...]
        dabuf[0:tm, :] = da_ref[...]
        dabuf[tm:, :] = jnp.where(i < nt - 1, dan_ref[...], 0.0)

        for c0 in range(0, tc, FF_LANES):
            cols = slice(c0, c0 + FF_LANES)
            gb = [jnp.zeros((8, FF_LANES), F32) for _ in range(2)]
            gw = [[jnp.zeros((8, FF_LANES), F32) for _ in range(FF_K)] for _ in range(2)]
            for r0, rows in _row_chunks(te):
                taps = [[buf[half, FF_HALO - (FF_K - 1) + k + r0:FF_HALO - (FF_K - 1) + k + r0 + rows, cols]
                         for k in range(FF_K)] for half in range(2)]
                gate, val = [b_ref[half:half + 1, cols] + sum(w_ref[k, half:half + 1, cols] * taps[half][k]
                                                              for k in range(FF_K)) for half in range(2)]
                da = dabuf[r0:r0 + rows, cols]
                sg = _sigmoid(gate)
                dup = [da * val * (sg * (1.0 + gate * (1.0 - sg))), da * (gate * sg)]
                for half in range(2):
                    dbuf[half, r0:r0 + rows, cols] = dup[half]
                    if r0 < tm:
                        gb[half] = gb[half] + _fold8(dup[half])
                        for k in range(FF_K):
                            gw[half][k] = gw[half][k] + _fold8(dup[half] * taps[half][k])
            for half, out_ref in enumerate((dg_ref, dv_ref)):
                gb_ref[half:half + 1, cols] += jnp.sum(gb[half], axis=0, keepdims=True)
                for k in range(FF_K):
                    gw_ref[k, half:half + 1, cols] += jnp.sum(gw[half][k], axis=0, keepdims=True)
                for r0, rows in _row_chunks(tm):
                    acc = jnp.zeros((rows, FF_LANES), F32)
                    for k in range(FF_K):
                        lo = r0 + FF_K - 1 - k
                        acc = acc + w_ref[k, half:half + 1, cols] * dbuf[half, lo:lo + rows, cols]
                    out_ref[r0:r0 + rows, cols] = acc.astype(BF16)

    main = pl.BlockSpec((tm, tc), lambda j, i: (i, j))
    prev = pl.BlockSpec((FF_HALO, tc), lambda j, i: (jnp.maximum(i * hb - 1, 0), j))
    nxt = pl.BlockSpec((FF_HALO, tc), lambda j, i: (jnp.minimum((i + 1) * hb, last_halo), j))
    (dg, dv, gw, gb), carried = _call(
        body, rider, name="ff_bwd", grid=(f // tc, nt),
        in_specs=[main, prev, nxt, main, prev, nxt, main, nxt,
                  pl.BlockSpec((FF_K, 2, tc), lambda j, i: (0, 0, j)), pl.BlockSpec((2, tc), lambda j, i: (0, j))],
        out_specs=[main, main, pl.BlockSpec((FF_K, 2, tc), lambda j, i: (0, 0, j)),
                   pl.BlockSpec((2, tc), lambda j, i: (0, j))],
        out_shape=[jax.ShapeDtypeStruct((s, f), BF16), jax.ShapeDtypeStruct((s, f), BF16),
                   jax.ShapeDtypeStruct((FF_K, 2, f), F32), jax.ShapeDtypeStruct((2, f), F32)],
        scratch_shapes=[pltpu.VMEM((2, FF_HALO + te, tc), F32), pltpu.VMEM((2, te, tc), F32), pltpu.VMEM((te, tc), F32)],
        compiler_params=_params(("arbitrary", "arbitrary"), VMEM_BIG),
        args=(up_g, up_g, up_g, up_v, up_v, up_v, d_act, d_act, fw, fb))
    return dg, dv, gw, gb, carried


ATT_TILE = Q_BLOCK * max(PATTERN_DILATIONS)


def _stream_rows(start, d, n=Q_BLOCK):
    return pl.ds(start, n) if d == 1 else pl.ds(start, n, stride=d)


def _band_geometry():
    qi = lax.broadcasted_iota(jnp.int32, (Q_BLOCK, 2 * Q_BLOCK), 0)
    ki = lax.broadcasted_iota(jnp.int32, (Q_BLOCK, 2 * Q_BLOCK), 1)
    delta = qi + Q_BLOCK - ki
    return (delta >= 0) & (delta <= Q_BLOCK), delta.astype(F32), ki


def _stream_blocks(d):
    out = []
    for r in range(d):
        for b in range(ATT_TILE // (Q_BLOCK * d)):
            start = b * Q_BLOCK * d + r
            out.append((start, start - Q_BLOCK * d if b > 0 else None))
    return out


def _attn_forward(qkv, slopes, rider=None):
    s = qkv.shape[0]
    nt = s // ATT_TILE
    nhp = ATTN_W // Q_BLOCK

    def body(sl_ref, q_ref, k_ref, kp_ref, v_ref, vp_ref, o_ref, l_ref):
        hp, i = pl.program_id(0), pl.program_id(1)
        head0 = lax.broadcasted_iota(jnp.int32, (Q_BLOCK, Q_BLOCK), 1) < HEAD_DIM
        head0_k = lax.broadcasted_iota(jnp.int32, (2 * Q_BLOCK, Q_BLOCK), 1) < HEAD_DIM
        valid, dist, ki = _band_geometry()
        first_key = jnp.where(i == 0, Q_BLOCK, 0)
        for d in PATTERN_DILATIONS:
            biases = [jnp.where(valid, dist * (-sl_ref[2 * hp + hh] * d), NEG) for hh in range(2)]
            for start, prev in _stream_blocks(d):
                rows = _stream_rows(start, d)
                if prev is None:
                    prow = _stream_rows(ATT_TILE - Q_BLOCK * d + start, d)
                    kp, vp = kp_ref[prow, :], vp_ref[prow, :]
                else:
                    prow = _stream_rows(prev, d)
                    kp, vp = k_ref[prow, :], v_ref[prow, :]
                qv = q_ref[rows, :].astype(BF16)
                k2 = jnp.concatenate([kp, k_ref[rows, :]], axis=0).astype(BF16)
                v2 = jnp.concatenate([vp, v_ref[rows, :]], axis=0).astype(BF16)
                res, mxs = [], []
                for hh in range(2):
                    mine = head0 if hh == 0 else jnp.logical_not(head0)
                    mine_k = head0_k if hh == 0 else jnp.logical_not(head0_k)
                    sc = _dot(jnp.where(mine, qv, jnp.zeros_like(qv)), k2, trans_b=True) + biases[hh]
                    if prev is None:
                        sc = jnp.where(ki < first_key, NEG, sc)
                    mx = jnp.max(sc, axis=1, keepdims=True)
                    p = jnp.exp(sc - mx).astype(BF16)
                    res.append(_dot(p, jnp.where(mine_k, v2, jnp.ones_like(v2))))
                    mxs.append(mx)
                num = jnp.where(head0, res[0], res[1])
                den = pltpu.roll(jnp.where(head0, res[1], res[0]), HEAD_DIM, 1)
                o_new = num / den
                l_new = jnp.where(head0, mxs[0], mxs[1]) + jnp.log(den)
                if d != PATTERN_DILATIONS[0]:
                    oa, la = o_ref[rows, :], l_ref[rows, :]
                    mm = jnp.maximum(la, l_new)
                    wa, wn = jnp.exp(la - mm), jnp.exp(l_new - mm)
                    o_new = (wa * oa + wn * o_new) / (wa + wn)
                    l_new = mm + jnp.log(wa + wn)
                o_ref[rows, :] = o_new
                l_ref[rows, :] = l_new

    def col(off):
        return pl.BlockSpec((ATT_TILE, Q_BLOCK), lambda hp, i: (i, off + hp))

    def col_prev(off):
        return pl.BlockSpec((ATT_TILE, Q_BLOCK), lambda hp, i: (jnp.maximum(i - 1, 0), off + hp))

    outs, carried = _call(
        body, rider, name="attn_fwd", grid=(nhp, nt),
        in_specs=[pl.BlockSpec(memory_space=pltpu.SMEM), col(0), col(nhp), col_prev(nhp), col(2 * nhp), col_prev(2 * nhp)],
        out_specs=[col(0), col(0)],
        out_shape=[jax.ShapeDtypeStruct((s, ATTN_W), F32)] * 2,
        compiler_params=_params(("arbitrary", "arbitrary"), VMEM_BIG),
        args=(slopes, qkv, qkv, qkv, qkv, qkv))
    return (*outs, carried)


def _attn_backward(qkv, d_cat, o, lse, slopes, ones2, rider=None):
    s = qkv.shape[0]
    nt = s // ATT_TILE
    nhp = ATTN_W // Q_BLOCK
    tt = ATT_TILE

    def body(sl_ref, q_ref, k_ref, kp_ref, v_ref, vp_ref, do_ref, o_ref, l_ref, ones_ref,
             dq_ref, dk_ref, dv_ref, dkacc, dvacc, dd):
        hp, step = pl.program_id(0), pl.program_id(1)
        tile = nt - 1 - step
        head0 = lax.broadcasted_iota(jnp.int32, (Q_BLOCK, Q_BLOCK), 1) < HEAD_DIM
        valid, dist, ki = _band_geometry()
        first_key = jnp.where(tile == 0, Q_BLOCK, 0)

        @pl.when(step == 0)
        def _():
            dkacc[tt:, :] = jnp.zeros((tt, Q_BLOCK), F32)
            dvacc[tt:, :] = jnp.zeros((tt, Q_BLOCK), F32)

        @pl.when(step > 0)
        def _():
            dkacc[tt:, :] = dkacc[:tt, :]
            dvacc[tt:, :] = dvacc[:tt, :]

        dkacc[:tt, :] = jnp.zeros((tt, Q_BLOCK), F32)
        dvacc[:tt, :] = jnp.zeros((tt, Q_BLOCK), F32)
        dd[...] = _head_sum(do_ref[...] * o_ref[...], ones_ref)

        for d in PATTERN_DILATIONS:
            biases = [jnp.where(valid, dist * (-sl_ref[2 * hp + hh] * d), NEG) for hh in range(2)]
            for start, prev in _stream_blocks(d):
                rows = _stream_rows(start, d)
                if prev is None:
                    prow = _stream_rows(ATT_TILE - Q_BLOCK * d + start, d)
                    kp, vp = kp_ref[prow, :], vp_ref[prow, :]
                else:
                    prow = _stream_rows(prev, d)
                    kp, vp = k_ref[prow, :], v_ref[prow, :]
                acc_rows = _stream_rows(tt + start - Q_BLOCK * d, d, 2 * Q_BLOCK)
                qv = q_ref[rows, :].astype(BF16)
                dov = do_ref[rows, :].astype(BF16)
                lv, ddv = l_ref[rows, :], dd[rows, :]
                lsw, dsw = pltpu.roll(lv, HEAD_DIM, 1), pltpu.roll(ddv, HEAD_DIM, 1)
                k2 = jnp.concatenate([kp, k_ref[rows, :]], axis=0).astype(BF16)
                v2 = jnp.concatenate([vp, v_ref[rows, :]], axis=0).astype(BF16)
                dq_h = []
                dk2 = jnp.zeros((2 * Q_BLOCK, Q_BLOCK), F32)
                dv2 = jnp.zeros((2 * Q_BLOCK, Q_BLOCK), F32)
                for hh in range(2):
                    mine = head0 if hh == 0 else jnp.logical_not(head0)
                    qm = jnp.where(mine, qv, jnp.zeros_like(qv))
                    dom = jnp.where(mine, dov, jnp.zeros_like(dov))
                    lfull, dfull = jnp.where(mine, lv, lsw), jnp.where(mine, ddv, dsw)
                    sc = _dot(qm, k2, trans_b=True) + biases[hh]
                    if prev is None:
                        sc = jnp.where(ki < first_key, NEG, sc)
                    p = jnp.exp(sc - jnp.concatenate([lfull, lfull], axis=1))
                    ds = p * (_dot(dom, v2, trans_b=True) - jnp.concatenate([dfull, dfull], axis=1))
                    dq_h.append(_dot(ds.astype(BF16), k2))
                    dk2 = dk2 + _dot(ds.T.astype(BF16), qm)
                    dv2 = dv2 + _dot(p.T.astype(BF16), dom)
                dq = jnp.where(head0, dq_h[0], dq_h[1])
                if d == PATTERN_DILATIONS[0]:
                    dq_ref[rows, :] = dq
                else:
                    dq_ref[rows, :] += dq
                dkacc[acc_rows, :] += dk2
                dvacc[acc_rows, :] += dv2
        dk_ref[...] = dkacc[tt:, :]
        dv_ref[...] = dvacc[tt:, :]

    def col(off):
        return pl.BlockSpec((tt, Q_BLOCK), lambda hp, st: (nt - 1 - st, off + hp))

    def col_prev(off):
        return pl.BlockSpec((tt, Q_BLOCK), lambda hp, st: (jnp.maximum(nt - 2 - st, 0), off + hp))

    outs, carried = _call(
        body, rider, name="attn_bwd", grid=(nhp, nt),
        in_specs=[pl.BlockSpec(memory_space=pltpu.SMEM), col(0), col(nhp), col_prev(nhp), col(2 * nhp), col_prev(2 * nhp),
                  col(nhp), col(0), col(0), pl.BlockSpec((Q_BLOCK, Q_BLOCK), lambda hp, st: (0, 0))],
        out_specs=[col(0)] * 3,
        out_shape=[jax.ShapeDtypeStruct((s, ATTN_W), F32)] * 3,
        scratch_shapes=[pltpu.VMEM((2 * tt, Q_BLOCK), F32)] * 2 + [pltpu.VMEM((tt, Q_BLOCK), F32)],
        compiler_params=_params(("arbitrary", "arbitrary"), VMEM_BIG),
        args=(slopes, qkv, qkv, qkv, qkv, qkv, d_cat, o, lse, ones2))
    return (*outs, carried)


def _attn_geometry(s, d):
    length = s // d
    tq = _tile(length, 1024, Q_BLOCK)
    return length, d * ATTN_W, tq, length // tq, tq // Q_BLOCK


def _band(slope_ref, head_pair, d):
    qi = lax.broadcasted_iota(jnp.int32, (Q_BLOCK, 2 * Q_BLOCK), 0)
    ki = lax.broadcasted_iota(jnp.int32, (Q_BLOCK, 2 * Q_BLOCK), 1)
    delta = qi + Q_BLOCK - ki
    valid = (delta >= 0) & (delta <= Q_BLOCK)
    dist = delta.astype(F32) * float(d)
    return [jnp.where(valid, dist * (-slope_ref[2 * head_pair + hh]), NEG) for hh in range(2)], ki


def _attn_fwd(qn, kn, v, slopes, d, prev):
    s = qn.shape[0]
    length, width, tq, nt, nqb = _attn_geometry(s, d)
    view = lambda t: t.reshape(length, width)
    merge = prev is not None

    def body(*refs):
        sl_ref, q_ref, k_ref, kh_ref, v_ref, vh_ref = refs[:6]
        if merge:
            oa_ref, la_ref = refs[6:8]
        o_ref, l_ref, kbuf, vbuf = refs[-4:]
        cb, i = pl.program_id(0), pl.program_id(1)
        biases, ki = _band(sl_ref, cb % 4, d)
        head0 = lax.broadcasted_iota(jnp.int32, (Q_BLOCK, Q_BLOCK), 1) < HEAD_DIM
        kbuf[0:Q_BLOCK, :] = kh_ref[...]
        kbuf[Q_BLOCK:, :] = k_ref[...]
        vbuf[0:Q_BLOCK, :] = vh_ref[...]
        vbuf[Q_BLOCK:, :] = v_ref[...]

        def q_block(qb, carry):
            r0 = pl.multiple_of(qb * Q_BLOCK, Q_BLOCK)
            rows = pl.ds(r0, Q_BLOCK)
            qv = q_ref[rows, :]
            k2, v2 = kbuf[pl.ds(r0, 2 * Q_BLOCK), :], vbuf[pl.ds(r0, 2 * Q_BLOCK), :]
            first_key = jnp.where((i == 0) & (qb == 0), Q_BLOCK, 0)
            o_h, l_h = [], []
            for hh in range(2):
                mine = head0 if hh == 0 else jnp.logical_not(head0)
                sc = _dot(jnp.where(mine, qv, jnp.zeros_like(qv)), k2, trans_b=True) + biases[hh]
                sc = jnp.where(ki < first_key, NEG, sc)
                mx = jnp.max(sc, axis=1, keepdims=True)
                p = jnp.exp(sc - mx)
                den = jnp.sum(p, axis=1, keepdims=True)
                o_h.append(_dot(p.astype(BF16), v2) * (1.0 / den))
                l_h.append(jnp.broadcast_to(mx + jnp.log(den), (Q_BLOCK, Q_BLOCK)))
            o_new = jnp.where(head0, o_h[0], o_h[1])
            l_new = jnp.where(head0, l_h[0], l_h[1])
            if merge:
                oa, la = oa_ref[rows, :], la_ref[rows, :]
                mm = jnp.maximum(la, l_new)
                wa, wn = jnp.exp(la - mm), jnp.exp(l_new - mm)
                o_new = (wa * oa + wn * o_new) / (wa + wn)
                l_new = mm + jnp.log(wa + wn)
            o_ref[rows, :] = o_new
            l_ref[rows, :] = l_new
            return carry

        lax.fori_loop(0, nqb, q_block, 0)

    main = pl.BlockSpec((tq, Q_BLOCK), lambda cb, i: (i, cb))
    halo = pl.BlockSpec((Q_BLOCK, Q_BLOCK), lambda cb, i: (jnp.maximum(i * nqb - 1, 0), cb))
    ins = [slopes, view(qn), view(kn), view(kn), view(v), view(v)] + ([view(prev[0]), view(prev[1])] if merge else [])
    o, lse = pl.pallas_call(
        body, name=f"attn_fwd_d{d}", grid=(width // Q_BLOCK, nt),
        in_specs=[pl.BlockSpec(memory_space=pltpu.SMEM), main, main, halo, main, halo] + ([main, main] if merge else []),
        out_specs=[main, main],
        out_shape=[jax.ShapeDtypeStruct((length, width), F32)] * 2,
        scratch_shapes=[pltpu.VMEM((Q_BLOCK + tq, Q_BLOCK), BF16)] * 2,
        compiler_params=_params(("parallel", "parallel"), VMEM_BIG),
    )(*ins)
    return o.reshape(s, ATTN_W), lse.reshape(s, ATTN_W)


def _attn_bwd_prep(d_cat, o, ones):
    s = o.shape[0]
    tm = _tile(s, 512, 8)

    def body(do_ref, o_ref, ones_ref, dob_ref, dd_ref):
        do = do_ref[...]
        dob_ref[...] = do.astype(BF16)
        dd_ref[...] = _head_sum(do * o_ref[...], ones_ref)

    row = pl.BlockSpec((tm, ATTN_W), lambda i: (i, 0))
    return pl.pallas_call(
        body, name="attn_bwd_prep", grid=(s // tm,),
        in_specs=[pl.BlockSpec((tm, ATTN_W), lambda i: (i, 1)), row, pl.BlockSpec(ones.shape, lambda i: (0, 0))],
        out_specs=[row, row],
        out_shape=[jax.ShapeDtypeStruct((s, ATTN_W), BF16), jax.ShapeDtypeStruct((s, ATTN_W), F32)],
        compiler_params=_params(("parallel",), VMEM_BIG),
    )(d_cat, o, ones)


def _attn_bwd(qn, kn, v, dob, lse, dd, slopes, d, acc):
    s = qn.shape[0]
    length, width, tq, nt, nqb = _attn_geometry(s, d)
    view = lambda t: t.reshape(length, width)
    add = acc is not None

    def body(*refs):
        sl_ref, q_ref, k_ref, kh_ref, v_ref, vh_ref, do_ref, l_ref, dd_ref = refs[:9]
        if add:
            dqa_ref, dka_ref, dva_ref = refs[9:12]
        dq_ref, dk_ref, dv_ref, kbuf, vbuf, dkacc, dvacc, kcarry, vcarry = refs[-9:]
        cb, step = pl.program_id(0), pl.program_id(1)
        tile = nt - 1 - step
        biases, ki = _band(sl_ref, cb % 4, d)
        head0 = lax.broadcasted_iota(jnp.int32, (Q_BLOCK, Q_BLOCK), 1) < HEAD_DIM
        kbuf[0:Q_BLOCK, :] = kh_ref[...]
        kbuf[Q_BLOCK:, :] = k_ref[...]
        vbuf[0:Q_BLOCK, :] = vh_ref[...]
        vbuf[Q_BLOCK:, :] = v_ref[...]
        dkacc[...] = jnp.zeros_like(dkacc)
        dvacc[...] = jnp.zeros_like(dvacc)

        @pl.when(step > 0)
        def _():
            dkacc[tq:, :] = kcarry[...]
            dvacc[tq:, :] = vcarry[...]

        def q_block(qb, carry):
            r0 = pl.multiple_of(qb * Q_BLOCK, Q_BLOCK)
            rows, rows2 = pl.ds(r0, Q_BLOCK), pl.ds(r0, 2 * Q_BLOCK)
            qv, dov = q_ref[rows, :], do_ref[rows, :]
            lv, ddv = l_ref[rows, :], dd_ref[rows, :]
            k2, v2 = kbuf[rows2, :], vbuf[rows2, :]
            first_key = jnp.where((tile == 0) & (qb == 0), Q_BLOCK, 0)
            dq_h = []
            dk2 = jnp.zeros((2 * Q_BLOCK, Q_BLOCK), F32)
            dv2 = jnp.zeros((2 * Q_BLOCK, Q_BLOCK), F32)
            for hh in range(2):
                mine = head0 if hh == 0 else jnp.logical_not(head0)
                qm = jnp.where(mine, qv, jnp.zeros_like(qv))
                dom = jnp.where(mine, dov, jnp.zeros_like(dov))
                lcol = jnp.max(jnp.where(mine, lv, -jnp.inf), axis=1, keepdims=True)
                dcol = jnp.max(jnp.where(mine, ddv, -jnp.inf), axis=1, keepdims=True)
                sc = _dot(qm, k2, trans_b=True) + biases[hh]
                sc = jnp.where(ki < first_key, NEG, sc)
                p = jnp.exp(sc - lcol)
                ds = p * (_dot(dom, v2, trans_b=True) - dcol)
                dq_h.append(_dot(ds.astype(BF16), k2))
                dk2 = dk2 + _dot(ds.T.astype(BF16), qm)
                dv2 = dv2 + _dot(p.T.astype(BF16), dom)
            dq = jnp.where(head0, dq_h[0], dq_h[1])
            if add:
                dq = dq + dqa_ref[rows, :]
            dq_ref[rows, :] = dq
            dkacc[rows2, :] += dk2
            dvacc[rows2, :] += dv2
            return carry

        lax.fori_loop(0, nqb, q_block, 0)
        dk, dv = dkacc[Q_BLOCK:, :], dvacc[Q_BLOCK:, :]
        if add:
            dk, dv = dk + dka_ref[...], dv + dva_ref[...]
        dk_ref[...] = dk
        dv_ref[...] = dv
        kcarry[...] = dkacc[0:Q_BLOCK, :]
        vcarry[...] = dvacc[0:Q_BLOCK, :]

    main = pl.BlockSpec((tq, Q_BLOCK), lambda cb, st: (nt - 1 - st, cb))
    halo = pl.BlockSpec((Q_BLOCK, Q_BLOCK), lambda cb, st: (jnp.maximum((nt - 1 - st) * nqb - 1, 0), cb))
    ins = [slopes, view(qn), view(kn), view(kn), view(v), view(v), view(dob), view(lse), view(dd)]
    ins += [view(t) for t in acc] if add else []
    outs = pl.pallas_call(
        body, name=f"attn_bwd_d{d}", grid=(width // Q_BLOCK, nt),
        in_specs=[pl.BlockSpec(memory_space=pltpu.SMEM), main, main, halo, main, halo, main, main, main]
        + ([main] * 3 if add else []),
        out_specs=[main] * 3,
        out_shape=[jax.ShapeDtypeStruct((length, width), F32)] * 3,
        scratch_shapes=[pltpu.VMEM((Q_BLOCK + tq, Q_BLOCK), BF16)] * 2 + [pltpu.VMEM((Q_BLOCK + tq, Q_BLOCK), F32)] * 2
        + [pltpu.VMEM((Q_BLOCK, Q_BLOCK), F32)] * 2,
        compiler_params=_params(("parallel", "arbitrary"), VMEM_BIG),
    )(*ins)
    return tuple(t.reshape(s, ATTN_W) for t in outs)


def _attn_bwd_post(dq, dk, dv, qk, qg, kg, ones):
    s = dq.shape[0]
    tm = _tile(s, 512, 8)
    aw = ATTN_W

    def body(dq_ref, dk_ref, dv_ref, qk_ref, qg_ref, kg_ref, ones_ref, out_ref, gq_ref, gk_ref):
        @pl.when(pl.program_id(0) == 0)
        def _():
            gq_ref[...] = jnp.zeros_like(gq_ref)
            gk_ref[...] = jnp.zeros_like(gk_ref)

        def norm_bwd(dy, raw, g, scale):
            r = lax.rsqrt(_head_sum(raw * raw, ones_ref) * (1.0 / HEAD_DIM) + EPS)
            xh = raw * r
            gd = dy * (g * scale)
            dx = r * (gd - xh * (_head_sum(gd * xh, ones_ref) * (1.0 / HEAD_DIM)))
            gsum = jnp.sum(dy * xh, axis=0, keepdims=True) * scale
            for shift in (256, 128, 64):
                gsum = gsum + pltpu.roll(gsum, shift, 1)
            return dx, gsum

        dxq, gq = norm_bwd(dq_ref[...], qk_ref[:, :aw], qg_ref[...], HEAD_DIM ** -0.5)
        dxk, gk = norm_bwd(dk_ref[...], qk_ref[:, aw:], kg_ref[...], 1.0)
        out_ref[:, :aw] = dxq.astype(BF16)
        out_ref[:, aw:2 * aw] = dxk.astype(BF16)
        out_ref[:, 2 * aw:] = dv_ref[...].astype(BF16)
        gq_ref[...] += gq
        gk_ref[...] += gk

    row = lambda w: pl.BlockSpec((tm, w), lambda i: (i, 0))
    full = lambda t: pl.BlockSpec(t.shape, lambda i: (0, 0))
    vec = pl.BlockSpec((1, aw), lambda i: (0, 0))
    return pl.pallas_call(
        body, name="attn_bwd_post", grid=(s // tm,),
        in_specs=[row(aw), row(aw), row(aw), row(2 * aw), full(qg), full(kg), full(ones)],
        out_specs=[row(3 * aw), vec, vec],
        out_shape=[jax.ShapeDtypeStruct((s, 3 * aw), BF16), jax.ShapeDtypeStruct((1, aw), F32),
                   jax.ShapeDtypeStruct((1, aw), F32)],
        compiler_params=_params(("arbitrary",), VMEM_BIG),
    )(dq, dk, dv, qk, qg, kg, ones)


def _sum_parts(name, parts):
    _, n, w = parts.shape
    tn = _tile(n, 256, 8)

    def body(p_ref, o_ref):
        acc = p_ref[0].astype(F32)
        for j in range(1, N_DEV):
            acc = acc + p_ref[j].astype(F32)
        o_ref[...] = acc

    return pl.pallas_call(
        body, name=name, grid=(n // tn,),
        in_specs=[pl.BlockSpec((N_DEV, tn, w), lambda i: (0, i, 0))],
        out_specs=pl.BlockSpec((tn, w), lambda i: (i, 0)),
        out_shape=jax.ShapeDtypeStruct((n, w), F32),
        compiler_params=_params(("parallel",), VMEM_BIG),
    )(parts)


def _adamw(name, w, g, m, v):
    n, cols = w.shape
    tn = _tile(n, 256, 8)
    c1 = 1.0 - ADAM_B1 ** ADAM_STEP
    c2 = 1.0 - ADAM_B2 ** ADAM_STEP

    def body(w_ref, g_ref, m_ref, v_ref, d_ref, nm_ref, nv_ref):
        gv = g_ref[...]
        nm = ADAM_B1 * m_ref[...] + (1.0 - ADAM_B1) * gv
        nv = ADAM_B2 * v_ref[...] + (1.0 - ADAM_B2) * (gv * gv)
        nm_ref[...] = nm
        nv_ref[...] = nv
        d_ref[...] = -ADAM_LR * ((nm / c1) / (jnp.sqrt(nv / c2) + ADAM_EPS) + ADAM_WD * w_ref[...])

    blk = pl.BlockSpec((tn, cols), lambda i: (i, 0))
    return pl.pallas_call(
        body, name=name, grid=(n // tn,),
        in_specs=[blk] * 4, out_specs=[blk] * 3,
        out_shape=[jax.ShapeDtypeStruct((n, cols), F32)] * 3,
        compiler_params=_params(("parallel",), VMEM_BIG),
    )(w, g, m, v)


def _pack(vectors, width):
    flat = jnp.concatenate([t.reshape(-1) for t in vectors])
    rows = -(-flat.shape[0] // (8 * width)) * 8
    return jnp.pad(flat, (0, rows * width - flat.shape[0])).reshape(rows, width)


def _unpack(packed, shapes):
    flat = packed.reshape(-1)
    out, at = [], 0
    for shp in shapes:
        size = 1
        for dim in shp:
            size *= dim
        out.append(flat[at:at + size].reshape(shp))
        at += size
    return out


def kernel(x, norm1_g, w_in, conv_w, conv_b, cn_g, cn_b, q_norm_g, k_norm_g, w_out, norm2_g, w_up, ffconv_w, ffconv_b, w_down, loss_target, m_norm1_g, m_w_in, m_conv_w, m_conv_b, m_cn_g, m_cn_b, m_q_norm_g, m_k_norm_g, m_w_out, m_norm2_g, m_w_up, m_ffconv_w, m_ffconv_b, m_w_down, v_norm1_g, v_w_in, v_conv_w, v_conv_b, v_cn_g, v_cn_b, v_q_norm_g, v_k_norm_g, v_w_out, v_norm2_g, v_w_up, v_ffconv_w, v_ffconv_b, v_w_down):
    s, d = x.shape[1], x.shape[2]
    f = w_down.shape[0] * N_DEV
    n_in = w_in.shape[1] * N_DEV
    xs, target = x.reshape(s, d), loss_target.reshape(s, d)
    me = _linear((lax.axis_index("x"), lax.axis_index("y"), lax.axis_index("c")))

    g_in, g_filt = _communicate("gather_w_in", _Gather([w_in.T.astype(BF16), _pack([conv_w, ffconv_w], 128)]))
    w_in_t = g_in.reshape(n_in, d)
    filt = g_filt.reshape(N_DEV, -1)
    n_cw = conv_w.size
    cw = filt[:, :n_cw].reshape(N_DEV, CONV_K, -1).transpose(1, 0, 2).reshape(CONV_K, CONV_CH)
    cw = jnp.pad(cw, ((0, CONV_HALO - CONV_K), (0, 0)))
    fw = filt[:, n_cw:n_cw + ffconv_w.size].reshape(N_DEV, FF_K, -1).transpose(1, 0, 2).reshape(FF_K, 2, f)
    fb = ffconv_b.reshape(2, f)
    row = lambda t: t.reshape(1, -1)
    g1, g2 = row(norm1_g), row(norm2_g)
    qg, kg = row(jnp.tile(q_norm_g, N_HEADS)), row(jnp.tile(k_norm_g, N_HEADS))
    lanes = jnp.arange(ATTN_W) // HEAD_DIM
    ones = (lanes[:, None] == lanes[None, :]).astype(BF16)
    slopes = 2.0 ** (-8.0 * jnp.arange(1, N_HEADS + 1, dtype=F32) / N_HEADS)

    h, a, qk, qkv, (g_out,) = _proj_fwd(xs, g1, w_in_t, qg, kg, ones, _Gather([w_out.astype(BF16)]))
    u1, u, (g_up,) = _conv_fwd(a, cw, row(conv_b), row(cn_g), row(cn_b), _Gather([w_up.T.astype(BF16)]))
    o, lse, (g_down,) = _attn_forward(qkv, slopes, _Gather([w_down.astype(BF16)]))
    w_out_f = g_out.reshape(2 * CONV_CH, d)
    w_up_t = g_up.reshape(2 * f, d)
    w_down_f = g_down.reshape(f, d)
    x1, h2 = _mix_out(u, o, w_out_f, xs, g2)
    up_g, up_v = _up_fwd(h2, w_up_t, f)
    act = _ff_act(up_g, up_v, fw, fb)
    loss_acc, dyf, dyb = _down_loss(act, w_down_f, x1, target)
    loss = lax.psum(loss_acc[0, 0], MESH_AXES)

    tk = _tile(s, 512, 8)
    gw_down = _mm_tn("grad_w_down", [act], dyb, _tile(f, 1408, 128), tk)
    d_act = _mm("d_act", dyb, w_down_f, True, F32, _tile(s, 512, 8), _tile(f, 1408, 128))
    blocks = lambda t: _Exchange([t.reshape(N_DEV, t.shape[0] // N_DEV, d)])
    dpre_g, dpre_v, gfw, gfb, (r_down,) = _ff_bwd(up_g, up_v, d_act, fw, fb, blocks(gw_down))
    gw_up = _mm_tn("grad_w_up", [dpre_g, dpre_v], h2, _tile(f, 1408, 128), tk)
    dx1f, dx1b, gg2 = _up_bwd(dpre_g, dpre_v, w_up_t, x1, dyf, g2)
    gw_out = _mm_tn("grad_w_out", [u, o], dx1b, CONV_CH, tk)
    d_cat = _mm("d_cat", dx1b, w_out_f, True, F32, _tile(s, 512, 8), 2 * CONV_CH)
    d_a, gcw, gcb, gcg, gcbeta, (r_out,) = _conv_bwd(a, u1, d_cat, cw, row(cn_g), row(cn_b), blocks(gw_out))
    dq, dk, dv, (r_up,) = _attn_backward(qkv, d_cat, o, lse, slopes, ones[:Q_BLOCK, :Q_BLOCK], blocks(gw_up))
    d_qkv, gqg, gkg = _attn_bwd_post(dq, dk, dv, qk, qg, kg, ones)
    gw_in = _mm_tn("grad_w_in", [d_a, d_qkv], h, 512, tk)
    grad_x, gg1, (r_in,) = _proj_bwd(d_a, d_qkv, w_in_t, xs, dx1f, g1, blocks(gw_in))
    g_w_in_t, g_w_out, g_w_up_t, g_w_down = [_sum_parts(f"sum_grad_{i}", r) for i, r in enumerate((r_in, r_out, r_up, r_down))]
    g_w_in, g_w_up = g_w_in_t.T, g_w_up_t.T

    small_shapes = [(d,), (d,), (CONV_CH,), (CONV_CH,), (CONV_CH,), (ATTN_W,), (ATTN_W,), (2 * f,),
                    (CONV_K, CONV_CH), (FF_K, 2 * f)]
    small = _pack([gg1, gg2, gcb, gcg, gcbeta, gqg, gkg, gfb, gcw[:CONV_K], gfw], 1024)
    (small_all,) = _communicate("gather_small_grads", _Gather([small]))
    sg1, sg2, scb, scg, scbeta, sqg, skg, sfb, scw, sfw = _unpack(_sum_parts("sum_small_grads", small_all), small_shapes)
    cwl, fwl = conv_w.shape[1], ffconv_w.shape[1]
    g_small = [sg1, lax.dynamic_slice_in_dim(scw, me * cwl, cwl, 1), scb, scg, scbeta,
               sqg[:HEAD_DIM], skg[:HEAD_DIM], sg2, lax.dynamic_slice_in_dim(sfw, me * fwl, fwl, 1), sfb]

    w_small = [norm1_g, conv_w, conv_b, cn_g, cn_b, q_norm_g, k_norm_g, norm2_g, ffconv_w, ffconv_b]
    m_small = [m_norm1_g, m_conv_w, m_conv_b, m_cn_g, m_cn_b, m_q_norm_g, m_k_norm_g, m_norm2_g, m_ffconv_w, m_ffconv_b]
    v_small = [v_norm1_g, v_conv_w, v_conv_b, v_cn_g, v_cn_b, v_q_norm_g, v_k_norm_g, v_norm2_g, v_ffconv_w, v_ffconv_b]
    shapes = [t.shape for t in w_small]
    packed = _adamw("adamw_small", *[_pack(ts, 128) for ts in (w_small, g_small, m_small, v_small)])
    d_small, nm_small, nv_small = [_unpack(p, shapes) for p in packed]
    upd = {}
    for name, wt, gt, mt, vt in (("w_in", w_in, g_w_in, m_w_in, v_w_in), ("w_out", w_out, g_w_out, m_w_out, v_w_out),
                                 ("w_up", w_up, g_w_up, m_w_up, v_w_up), ("w_down", w_down, g_w_down, m_w_down, v_w_down)):
        upd[name] = (gt,) + tuple(_adamw("adamw_" + name, wt, gt, mt, vt))
    order = ["norm1_g", "w_in", "conv_w", "conv_b", "cn_g", "cn_b", "q_norm_g", "k_norm_g", "w_out", "norm2_g",
             "w_up", "ffconv_w", "ffconv_b", "w_down"]
    small_names = ["norm1_g", "conv_w", "conv_b", "cn_g", "cn_b", "q_norm_g", "k_norm_g", "norm2_g", "ffconv_w", "ffconv_b"]
    for i, name in enumerate(small_names):
        upd[name] = (g_small[i], d_small[i], nm_small[i], nv_small[i])
    outs = [loss, grad_x.reshape(x.shape)]
    for field in range(4):
        outs += [upd[name][field] for name in order]
    return tuple(outs)
```

```python
import functools

import jax
import jax.numpy as jnp
from jax import lax
from jax.experimental import pallas as pl
from jax.experimental.pallas import tpu as pltpu

F32, BF16 = jnp.float32, jnp.bfloat16
N_DEV = 8
N_HEADS, HEAD_DIM = 8, 64
CONV_CH = 512
ATTN_W = N_HEADS * HEAD_DIM
CONV_K, FF_K = 31, 3
CONV_HALO = 32
FF_HALO = 8
PATTERN_DILATIONS = (1, 4, 16)
Q_BLOCK = 128
EPS = 1e-6
NEG = -1e30
ADAM_LR, ADAM_B1, ADAM_B2, ADAM_EPS, ADAM_WD, ADAM_STEP = 0.001, 0.9, 0.999, 1e-08, 0.01, 10
MESH_AXES = ("x", "y", "c")
VMEM_BIG = 56 * 1024 * 1024
ANY = pl.BlockSpec(memory_space=pl.ANY)


def _params(sem=None, vmem=None):
    return pltpu.CompilerParams(dimension_semantics=sem, vmem_limit_bytes=vmem)


def _dot(a, b, trans_a=False, trans_b=False):
    dims = (((0 if trans_a else 1,), (1 if trans_b else 0,)), ((), ()))
    return lax.dot_general(a, b, dims, preferred_element_type=F32)


def _sigmoid(z):
    return 1.0 / (1.0 + jnp.exp(-z))


def _tile(n, pref, mult):
    if n <= pref:
        return n
    t = (pref // mult) * mult
    while n % t:
        t -= mult
    return t


def _head_sum(v, ones_ref):
    hi = v.astype(BF16)
    lo = (v - hi.astype(F32)).astype(BF16)
    ones = ones_ref[...]
    return _dot(hi, ones) + _dot(lo, ones)


def _linear(p):
    return 4 * p[0] + 2 * p[1] + p[2]


class _Gather:
    def __init__(self, arrs):
        self.arrs = list(arrs)
        self.n = len(self.arrs)
        self.out_shape = [jax.ShapeDtypeStruct((N_DEV,) + a.shape, a.dtype) for a in self.arrs]

    def _setup(self, ins, outs, sems):
        send_sems, recv_sems, local_sems = sems
        x, y, c = lax.axis_index("x"), lax.axis_index("y"), lax.axis_index("c")
        me, sib = (x, y, c), (x, y, 1 - c)
        chips = [(1 - x, y), (x, 1 - y), (1 - x, 1 - y)]

        def copy(a, k, block, to, src=None):
            dst = outs[a].at[_linear(block)]
            return pltpu.make_async_remote_copy(
                src_ref=dst if src is None else src, dst_ref=dst,
                send_sem=send_sems.at[7 * a + k], recv_sem=recv_sems.at[7 * a + k],
                device_id=to, device_id_type=pl.DeviceIdType.MESH)

        mine = [pltpu.make_async_copy(ins[a], outs[a].at[_linear(me)], local_sems.at[a]) for a in range(self.n)]
        first = []
        for a in range(self.n):
            first.append(copy(a, 0, me, sib, src=ins[a]))
            first += [copy(a, 1 + j, me, (*chip, c), src=ins[a]) for j, chip in enumerate(chips)]
        return copy, mine, first, me, sib, chips, c

    def start(self, ins, outs, sems):
        _, mine, first, *_ = self._setup(ins, outs, sems)
        for cp in mine + first:
            cp.start()

    def finish(self, ins, outs, sems):
        copy, mine, first, me, sib, chips, c = self._setup(ins, outs, sems)
        passed = []
        for j, chip in enumerate(chips):
            for a in range(self.n):
                copy(a, 1 + j, (*chip, c), me).wait_recv()
                fwd = copy(a, 4 + j, (*chip, c), sib)
                fwd.start()
                passed.append(fwd)
        for a in range(self.n):
            copy(a, 0, sib, me).wait_recv()
            for j, chip in enumerate(chips):
                copy(a, 4 + j, (*chip, 1 - c), me).wait_recv()
        for cp in first + passed:
            cp.wait_send()
        for cp in mine:
            cp.wait()


class _Exchange:
    def __init__(self, arrs):
        self.arrs = list(arrs)
        self.n = len(self.arrs)
        self.out_shape = [jax.ShapeDtypeStruct(a.shape, a.dtype) for a in self.arrs]

    def _setup(self, ins, outs, sems):
        send_sems, recv_sems, local_sems = sems
        x, y, c = lax.axis_index("x"), lax.axis_index("y"), lax.axis_index("c")
        me = _linear((x, y, c))
        mine = [pltpu.make_async_copy(ins[a].at[me], outs[a].at[me], local_sems.at[a]) for a in range(self.n)]
        sends, recvs = [], []
        for p in range(1, N_DEV):
            peer = (1 - x if p & 4 else x, 1 - y if p & 2 else y, 1 - c if p & 1 else c)
            for a in range(self.n):
                sem = dict(send_sem=send_sems.at[7 * a + p - 1], recv_sem=recv_sems.at[7 * a + p - 1],
                           device_id=peer, device_id_type=pl.DeviceIdType.MESH)
                sends.append(pltpu.make_async_remote_copy(
                    src_ref=ins[a].at[_linear(peer)], dst_ref=outs[a].at[me], **sem))
                recvs.append(pltpu.make_async_remote_copy(
                    src_ref=ins[a].at[_linear(peer)], dst_ref=outs[a].at[_linear(peer)], **sem))
        return mine, sends, recvs

    def start(self, ins, outs, sems):
        mine, sends, _ = self._setup(ins, outs, sems)
        for cp in mine + sends:
            cp.start()

    def finish(self, ins, outs, sems):
        mine, sends, recvs = self._setup(ins, outs, sems)
        for cp in recvs:
            cp.wait_recv()
        for cp in sends:
            cp.wait_send()
        for cp in mine:
            cp.wait()


def _rider_scratch(rider):
    return [pltpu.SemaphoreType.DMA((7 * rider.n,)), pltpu.SemaphoreType.DMA((7 * rider.n,)),
            pltpu.SemaphoreType.DMA((rider.n,))]


def _communicate(name, rider):
    na = rider.n

    def body(*refs):
        ins, outs, sems = refs[:na], refs[na:2 * na], refs[2 * na:]
        rider.start(ins, outs, sems)
        rider.finish(ins, outs, sems)

    return pl.pallas_call(
        body, name=name, out_shape=rider.out_shape, in_specs=[ANY] * na, out_specs=[ANY] * na,
        scratch_shapes=_rider_scratch(rider),
    )(*rider.arrs)


def _call(body, rider=None, *, name, grid, in_specs, out_specs, out_shape, scratch_shapes=(), compiler_params, args):
    if rider is None:
        out = pl.pallas_call(body, name=name, grid=grid, in_specs=in_specs, out_specs=out_specs, out_shape=out_shape,
                             scratch_shapes=list(scratch_shapes), compiler_params=compiler_params)(*args)
        return out, None
    n_in, n_out, n_scr, na = len(in_specs), len(out_specs), len(scratch_shapes), rider.n

    def carried(*refs):
        ins, refs = refs[:n_in], refs[n_in:]
        r_ins, refs = refs[:na], refs[na:]
        outs, refs = refs[:n_out], refs[n_out:]
        r_outs, refs = refs[:na], refs[na:]
        scratch, sems = refs[:n_scr], refs[n_scr:]
        ids = [pl.program_id(ax) for ax in range(len(grid))]
        first, last = ids[0] == 0, ids[0] == grid[0] - 1
        for ax in range(1, len(grid)):
            first, last = first & (ids[ax] == 0), last & (ids[ax] == grid[ax] - 1)

        @pl.when(first)
        def _():
            rider.start(r_ins, r_outs, sems)

        body(*ins, *outs, *scratch)

        @pl.when(last)
        def _():
            rider.finish(r_ins, r_outs, sems)

    out = pl.pallas_call(
        carried, name=name, grid=grid, in_specs=list(in_specs) + [ANY] * na, out_specs=list(out_specs) + [ANY] * na,
        out_shape=list(out_shape) + rider.out_shape, scratch_shapes=list(scratch_shapes) + _rider_scratch(rider),
        compiler_params=compiler_params)(*args, *rider.arrs)
    return out[:n_out], out[n_out:]


def _mm(name, a, w, trans_b, out_dtype, tm, tn):
    m, k = a.shape
    n = w.shape[0] if trans_b else w.shape[1]

    def body(a_ref, w_ref, o_ref):
        o_ref[...] = _dot(a_ref[...], w_ref[...], trans_b=trans_b).astype(o_ref.dtype)

    w_spec = pl.BlockSpec((tn, k), lambda j, i: (j, 0)) if trans_b else pl.BlockSpec((k, tn), lambda j, i: (0, j))
    return pl.pallas_call(
        body, name=name, grid=(n // tn, m // tm),
        in_specs=[pl.BlockSpec((tm, k), lambda j, i: (i, 0)), w_spec],
        out_specs=pl.BlockSpec((tm, tn), lambda j, i: (i, j)),
        out_shape=jax.ShapeDtypeStruct((m, n), out_dtype),
        compiler_params=_params(("parallel", "parallel"), VMEM_BIG),
    )(a, w)


def _mm_tn(name, parts, b, tn, tk):
    s, d = b.shape
    blocks = [p.shape[1] // tn for p in parts]
    starts = [sum(blocks[:i]) for i in range(len(parts))]
    np_ = len(parts)

    nk = s // tk

    def body(*refs):
        a_refs, b_ref, o_ref, acc_ref = refs[:np_], refs[np_], refs[np_ + 1], refs[np_ + 2]
        j, kk = pl.program_id(0), pl.program_id(1)

        @pl.when(kk == 0)
        def _():
            acc_ref[...] = jnp.zeros_like(acc_ref)

        for p in range(np_):
            @pl.when((j >= starts[p]) & (j < starts[p] + blocks[p]))
            def _(p=p):
                acc_ref[...] += _dot(a_refs[p][...].astype(BF16), b_ref[...], trans_a=True)

        @pl.when(kk == nk - 1)
        def _():
            o_ref[...] = acc_ref[...].astype(o_ref.dtype)

    def a_spec(p):
        return pl.BlockSpec((tk, tn), lambda j, kk: (kk, jnp.clip(j - starts[p], 0, blocks[p] - 1)))

    return pl.pallas_call(
        body, name=name, grid=(sum(blocks), nk),
        in_specs=[a_spec(p) for p in range(np_)] + [pl.BlockSpec((tk, d), lambda j, kk: (kk, 0))],
        out_specs=pl.BlockSpec((tn, d), lambda j, kk: (j, 0)),
        out_shape=jax.ShapeDtypeStruct((sum(blocks) * tn, d), BF16),
        scratch_shapes=[pltpu.VMEM((tn, d), F32)],
        compiler_params=_params(("parallel", "arbitrary"), VMEM_BIG),
    )(*parts, b)


def _proj_fwd(x, g1, w_in_t, qg, kg, ones, rider=None):
    s, d = x.shape
    tm = _tile(s, 256, 8)
    c2, aw = 2 * CONV_CH, ATTN_W

    def body(x_ref, g_ref, w_ref, qg_ref, kg_ref, ones_ref, h_ref, a_ref, qk_ref, qkv_ref):
        xv = x_ref[...]
        r = lax.rsqrt(jnp.mean(xv * xv, axis=-1, keepdims=True) + EPS)
        h = (xv * r * g_ref[...]).astype(BF16)
        h_ref[...] = h
        proj = _dot(h, w_ref[...], trans_b=True)
        a_ref[...] = proj[:, :c2]
        qk_ref[...] = proj[:, c2:c2 + 2 * aw]
        q, k = proj[:, c2:c2 + aw], proj[:, c2 + aw:c2 + 2 * aw]
        rq = lax.rsqrt(_head_sum(q * q, ones_ref) * (1.0 / HEAD_DIM) + EPS)
        rk = lax.rsqrt(_head_sum(k * k, ones_ref) * (1.0 / HEAD_DIM) + EPS)
        qkv_ref[:, :aw] = q * rq * qg_ref[...] * (HEAD_DIM ** -0.5)
        qkv_ref[:, aw:2 * aw] = k * rk * kg_ref[...]
        qkv_ref[:, 2 * aw:] = proj[:, c2 + 2 * aw:]

    row = lambda w: pl.BlockSpec((tm, w), lambda i: (i, 0))
    full = lambda a: pl.BlockSpec(a.shape, lambda i: (0, 0))
    outs, carried = _call(
        body, rider, name="proj_fwd", grid=(s // tm,),
        in_specs=[row(d), full(g1), full(w_in_t), full(qg), full(kg), full(ones)],
        out_specs=[row(d), row(c2), row(2 * aw), row(3 * aw)],
        out_shape=[jax.ShapeDtypeStruct((s, d), BF16), jax.ShapeDtypeStruct((s, c2), F32),
                   jax.ShapeDtypeStruct((s, 2 * aw), F32), jax.ShapeDtypeStruct((s, 3 * aw), F32)],
        compiler_params=_params(("arbitrary",), VMEM_BIG),
        args=(x, g1, w_in_t, qg, kg, ones))
    return (*outs, carried)


def _mix_out(u, o, w_out, x, g2):
    s, d = x.shape
    tm = _tile(s, 512, 8)

    def body(u_ref, o_ref, w_ref, x_ref, g_ref, x1_ref, h2_ref):
        x1 = x_ref[...] + _dot(u_ref[...], w_ref[:CONV_CH, :]) + _dot(o_ref[...].astype(BF16), w_ref[CONV_CH:, :])
        x1_ref[...] = x1
        r = lax.rsqrt(jnp.mean(x1 * x1, axis=-1, keepdims=True) + EPS)
        h2_ref[...] = (x1 * r * g_ref[...]).astype(BF16)

    row = lambda w: pl.BlockSpec((tm, w), lambda i: (i, 0))
    full = lambda a: pl.BlockSpec(a.shape, lambda i: (0, 0))
    return pl.pallas_call(
        body, name="mix_out", grid=(s // tm,),
        in_specs=[row(CONV_CH), row(ATTN_W), full(w_out), row(d), full(g2)],
        out_specs=[row(d), row(d)],
        out_shape=[jax.ShapeDtypeStruct((s, d), F32), jax.ShapeDtypeStruct((s, d), BF16)],
        compiler_params=_params(("parallel",), VMEM_BIG),
    )(u, o, w_out, x, g2)


def _up_fwd(h2, w_up_t, f):
    s, d = h2.shape
    tm, tn = _tile(s, 512, 8), _tile(f, 1408, 128)
    nb = f // tn

    def body(h_ref, wg_ref, wv_ref, g_ref, v_ref):
        hv = h_ref[...]
        g_ref[...] = _dot(hv, wg_ref[...], trans_b=True)
        v_ref[...] = _dot(hv, wv_ref[...], trans_b=True)

    return pl.pallas_call(
        body, name="up_fwd", grid=(nb, s // tm),
        in_specs=[pl.BlockSpec((tm, d), lambda j, i: (i, 0)), pl.BlockSpec((tn, d), lambda j, i: (j, 0)),
                  pl.BlockSpec((tn, d), lambda j, i: (j + nb, 0))],
        out_specs=[pl.BlockSpec((tm, tn), lambda j, i: (i, j))] * 2,
        out_shape=[jax.ShapeDtypeStruct((s, f), F32)] * 2,
        compiler_params=_params(("parallel", "parallel"), VMEM_BIG),
    )(h2, w_up_t, w_up_t)


def _down_loss(act, w_down, x1, target):
    s, d = x1.shape
    f = act.shape[1]
    tm = _tile(s, 256, 8)

    def body(a_ref, w_ref, x1_ref, t_ref, loss_ref, dyf_ref, dyb_ref):
        @pl.when(pl.program_id(0) == 0)
        def _():
            loss_ref[...] = jnp.zeros_like(loss_ref)

        diff = x1_ref[...] + _dot(a_ref[...], w_ref[...]) - t_ref[...]
        sq = jnp.sum(jnp.sum(diff * diff, axis=1, keepdims=True), axis=0, keepdims=True)
        loss_ref[...] += jnp.broadcast_to(sq * (0.5 / d), loss_ref.shape)
        dy = diff * (1.0 / d)
        dyf_ref[...] = dy
        dyb_ref[...] = dy.astype(BF16)

    row = lambda w: pl.BlockSpec((tm, w), lambda i: (i, 0))
    return pl.pallas_call(
        body, name="down_loss", grid=(s // tm,),
        in_specs=[row(f), pl.BlockSpec((f, d), lambda i: (0, 0)), row(d), row(d)],
        out_specs=[pl.BlockSpec((8, 128), lambda i: (0, 0)), row(d), row(d)],
        out_shape=[jax.ShapeDtypeStruct((8, 128), F32), jax.ShapeDtypeStruct((s, d), F32),
                   jax.ShapeDtypeStruct((s, d), BF16)],
        compiler_params=_params(("arbitrary",), VMEM_BIG),
    )(act, w_down, x1, target)


def _norm_bwd_tail(dh, xv, g, resid, d):
    r = lax.rsqrt(jnp.mean(xv * xv, axis=-1, keepdims=True) + EPS)
    xh = xv * r
    gd = dh * g
    dx = r * (gd - xh * (jnp.sum(gd * xh, axis=-1, keepdims=True) * (1.0 / d)))
    return resid + dx, jnp.sum(dh * xh, axis=0, keepdims=True)


def _up_bwd(dg, dv, w_up_t, x1, dyf, g2):
    s, d = x1.shape
    f = dg.shape[1]
    tm = _tile(s, 256, 8)

    def body(dg_ref, dv_ref, w_ref, x1_ref, dy_ref, g_ref, dxf_ref, dxb_ref, gg_ref):
        @pl.when(pl.program_id(0) == 0)
        def _():
            gg_ref[...] = jnp.zeros_like(gg_ref)

        dh = _dot(dg_ref[...], w_ref[:f, :]) + _dot(dv_ref[...], w_ref[f:, :])
        dx, gg = _norm_bwd_tail(dh, x1_ref[...], g_ref[...], dy_ref[...], d)
        dxf_ref[...] = dx
        dxb_ref[...] = dx.astype(BF16)
        gg_ref[...] += gg

    row = lambda w: pl.BlockSpec((tm, w), lambda i: (i, 0))
    full = lambda a: pl.BlockSpec(a.shape, lambda i: (0, 0))
    return pl.pallas_call(
        body, name="up_bwd", grid=(s // tm,),
        in_specs=[row(f), row(f), full(w_up_t), row(d), row(d), full(g2)],
        out_specs=[row(d), row(d), pl.BlockSpec((1, d), lambda i: (0, 0))],
        out_shape=[jax.ShapeDtypeStruct((s, d), F32), jax.ShapeDtypeStruct((s, d), BF16),
                   jax.ShapeDtypeStruct((1, d), F32)],
        compiler_params=_params(("arbitrary",), VMEM_BIG),
    )(dg, dv, w_up_t, x1, dyf, g2)


def _proj_bwd(d_a, d_qkv, w_in_t, x, dx1, g1, rider=None):
    s, d = x.shape
    na, nq = d_a.shape[1], d_qkv.shape[1]
    tm = _tile(s, 256, 8)

    def body(da_ref, dq_ref, w_ref, x_ref, r_ref, g_ref, gx_ref, gg_ref):
        @pl.when(pl.program_id(0) == 0)
        def _():
            gg_ref[...] = jnp.zeros_like(gg_ref)

        dh = _dot(da_ref[...], w_ref[:na, :]) + _dot(dq_ref[...], w_ref[na:, :])
        dx, gg = _norm_bwd_tail(dh, x_ref[...], g_ref[...], r_ref[...], d)
        gx_ref[...] = dx
        gg_ref[...] += gg

    row = lambda w: pl.BlockSpec((tm, w), lambda i: (i, 0))
    full = lambda a: pl.BlockSpec(a.shape, lambda i: (0, 0))
    outs, carried = _call(
        body, rider, name="proj_bwd", grid=(s // tm,),
        in_specs=[row(na), row(nq), full(w_in_t), row(d), row(d), full(g1)],
        out_specs=[row(d), pl.BlockSpec((1, d), lambda i: (0, 0))],
        out_shape=[jax.ShapeDtypeStruct((s, d), F32), jax.ShapeDtypeStruct((1, d), F32)],
        compiler_params=_params(("arbitrary",), VMEM_BIG),
        args=(d_a, d_qkv, w_in_t, x, dx1, g1))
    return (*outs, carried)


CONV_CHUNK = 32
SUBLANES = 8


def _glu(av):
    return av[:, :CONV_CH] * _sigmoid(av[:, CONV_CH:])


def _chunks(n, size=CONV_CHUNK):
    return [(r0, min(size, n - r0)) for r0 in range(0, n, size)]


def _fill_shifted(sbuf, rows):
    for r in range(1, SUBLANES):
        for r0, n in _chunks(rows, 64):
            sbuf[r, r0:r0 + n, :] = sbuf[0, r0 + r:r0 + r + n, :]


def _tap(sbuf, offset, r0, rows):
    q, r = divmod(offset, SUBLANES)
    return sbuf[r, SUBLANES * q + r0:SUBLANES * q + r0 + rows, :]


def _layer_norm_stats(u1):
    mu = jnp.mean(u1, axis=-1, keepdims=True)
    cen = u1 - mu
    rstd = lax.rsqrt(jnp.mean(cen * cen, axis=-1, keepdims=True) + EPS)
    return cen * rstd, rstd


def _conv_fwd(a, cw, cb, cg, cbeta, rider=None):
    s = a.shape[0]
    tm = _tile(s, 256, CONV_CHUNK)
    hb = tm // CONV_HALO

    def body(a_ref, ap_ref, cw_ref, cb_ref, cg_ref, cbeta_ref, u1_ref, u_ref, ubuf):
        i = pl.program_id(0)
        ubuf[0, 0:CONV_HALO, :] = jnp.where(i > 0, _glu(ap_ref[...]), 0.0)
        for r0, n in _chunks(tm):
            ubuf[0, CONV_HALO + r0:CONV_HALO + r0 + n, :] = _glu(a_ref[r0:r0 + n, :])
        _fill_shifted(ubuf, tm + CONV_HALO - SUBLANES)
        for r0, n in _chunks(tm):
            acc = jnp.broadcast_to(cb_ref[...], (n, CONV_CH))
            for k in range(CONV_K):
                acc = acc + cw_ref[k:k + 1, :] * _tap(ubuf, 2 + k, r0, n)
            u1_ref[r0:r0 + n, :] = acc
            xh, _ = _layer_norm_stats(acc)
            z = xh * cg_ref[...] + cbeta_ref[...]
            u_ref[r0:r0 + n, :] = (z * _sigmoid(z)).astype(BF16)

    full = lambda t: pl.BlockSpec(t.shape, lambda i: (0, 0))
    outs, carried = _call(
        body, rider, name="conv_fwd", grid=(s // tm,),
        in_specs=[pl.BlockSpec((tm, 2 * CONV_CH), lambda i: (i, 0)),
                  pl.BlockSpec((CONV_HALO, 2 * CONV_CH), lambda i: (jnp.maximum(i * hb - 1, 0), 0)),
                  full(cw), full(cb), full(cg), full(cbeta)],
        out_specs=[pl.BlockSpec((tm, CONV_CH), lambda i: (i, 0))] * 2,
        out_shape=[jax.ShapeDtypeStruct((s, CONV_CH), F32), jax.ShapeDtypeStruct((s, CONV_CH), BF16)],
        scratch_shapes=[pltpu.VMEM((SUBLANES, CONV_HALO + tm, CONV_CH), F32)],
        compiler_params=_params(("arbitrary",), VMEM_BIG),
        args=(a, a, cw, cb, cg, cbeta))
    return (*outs, carried)


def _conv_bwd(a, u1, d_cat, cw, cg, cbeta, rider=None):
    s = a.shape[0]
    tm = _tile(s, 256, CONV_CHUNK)
    hb = tm // CONV_HALO
    last_halo = s // CONV_HALO - 1
    nt = s // tm
    te = tm + CONV_HALO

    def body(a_ref, ap_ref, u1_ref, u1n_ref, du_ref, dun_ref, cw_ref, cg_ref, cbeta_ref,
             da_ref, gw_ref, gb_ref, gg_ref, gbeta_ref, ubuf, dbuf):
        i = pl.program_id(0)

        @pl.when(i == 0)
        def _():
            gw_ref[...] = jnp.zeros_like(gw_ref)
            gb_ref[...] = jnp.zeros_like(gb_ref)
            gg_ref[...] = jnp.zeros_like(gg_ref)
            gbeta_ref[...] = jnp.zeros_like(gbeta_ref)

        def du1_of(u1, du):
            xh, rstd = _layer_norm_stats(u1)
            z = xh * cg_ref[...] + cbeta_ref[...]
            sz = _sigmoid(z)
            dz = du * (sz * (1.0 + z * (1.0 - sz)))
            dxh = dz * cg_ref[...]
            du1 = rstd * (dxh - jnp.mean(dxh, axis=-1, keepdims=True)
                          - xh * jnp.mean(dxh * xh, axis=-1, keepdims=True))
            return du1, dz, xh

        gg, gbeta, gb = [jnp.zeros((SUBLANES, CONV_CH), F32) for _ in range(3)]
        for r0, n in _chunks(tm):
            du1, dz, xh = du1_of(u1_ref[r0:r0 + n, :], du_ref[r0:r0 + n, :])
            gg, gbeta, gb = gg + _fold8(dz * xh), gbeta + _fold8(dz), gb + _fold8(du1)
            dbuf[0, r0:r0 + n, :] = du1
        gg_ref[...] += jnp.sum(gg, axis=0, keepdims=True)
        gbeta_ref[...] += jnp.sum(gbeta, axis=0, keepdims=True)
        gb_ref[...] += jnp.sum(gb, axis=0, keepdims=True)
        du1n, _, _ = du1_of(u1n_ref[...], jnp.where(i < nt - 1, dun_ref[...], 0.0))
        dbuf[0, tm:, :] = du1n
        _fill_shifted(dbuf, tm + CONV_HALO - SUBLANES)

        ubuf[0, 0:CONV_HALO, :] = jnp.where(i > 0, _glu(ap_ref[...]), 0.0)
        for r0, n in _chunks(tm):
            ubuf[0, CONV_HALO + r0:CONV_HALO + r0 + n, :] = _glu(a_ref[r0:r0 + n, :])
        _fill_shifted(ubuf, tm + CONV_HALO - SUBLANES)

        for k in range(CONV_K):
            part = jnp.zeros((SUBLANES, CONV_CH), F32)
            for r0, n in _chunks(tm):
                part = part + _fold8(dbuf[0, r0:r0 + n, :] * _tap(ubuf, 2 + k, r0, n))
            gw_ref[k:k + 1, :] += jnp.sum(part, axis=0, keepdims=True)

        for r0, n in _chunks(tm):
            acc = jnp.zeros((n, CONV_CH), F32)
            for k in range(CONV_K):
                acc = acc + cw_ref[k:k + 1, :] * _tap(dbuf, CONV_K - 1 - k, r0, n)
            avc = a_ref[r0:r0 + n, :CONV_CH]
            sgc = _sigmoid(a_ref[r0:r0 + n, CONV_CH:])
            da_ref[r0:r0 + n, :CONV_CH] = (acc * sgc).astype(BF16)
            da_ref[r0:r0 + n, CONV_CH:] = (acc * avc * sgc * (1.0 - sgc)).astype(BF16)

    full = lambda t: pl.BlockSpec(t.shape, lambda i: (0, 0))
    vec = pl.BlockSpec((1, CONV_CH), lambda i: (0, 0))
    nxt = lambda i: (jnp.minimum((i + 1) * hb, last_halo), 0)
    outs, carried = _call(
        body, rider, name="conv_bwd", grid=(nt,),
        in_specs=[pl.BlockSpec((tm, 2 * CONV_CH), lambda i: (i, 0)),
                  pl.BlockSpec((CONV_HALO, 2 * CONV_CH), lambda i: (jnp.maximum(i * hb - 1, 0), 0)),
                  pl.BlockSpec((tm, CONV_CH), lambda i: (i, 0)), pl.BlockSpec((CONV_HALO, CONV_CH), nxt),
                  pl.BlockSpec((tm, CONV_CH), lambda i: (i, 0)), pl.BlockSpec((CONV_HALO, CONV_CH), nxt),
                  full(cw), full(cg), full(cbeta)],
        out_specs=[pl.BlockSpec((tm, 2 * CONV_CH), lambda i: (i, 0)),
                   pl.BlockSpec((CONV_HALO, CONV_CH), lambda i: (0, 0)), vec, vec, vec],
        out_shape=[jax.ShapeDtypeStruct((s, 2 * CONV_CH), BF16), jax.ShapeDtypeStruct((CONV_HALO, CONV_CH), F32),
                   jax.ShapeDtypeStruct((1, CONV_CH), F32), jax.ShapeDtypeStruct((1, CONV_CH), F32),
                   jax.ShapeDtypeStruct((1, CONV_CH), F32)],
        scratch_shapes=[pltpu.VMEM((SUBLANES, CONV_HALO + tm, CONV_CH), F32), pltpu.VMEM((SUBLANES, te, CONV_CH), F32)],
        compiler_params=_params(("arbitrary",), VMEM_BIG),
        args=(a, a, u1, u1, d_cat, d_cat, cw, cg, cbeta))
    return (*outs, carried)


def _ff_tiles(s, f):
    return _tile(s, 256, 8), _tile(f, 1408, 128)


FF_CHUNK, FF_LANES = 64, 128


def _row_chunks(n):
    return [(r0, min(FF_CHUNK, n - r0)) for r0 in range(0, n, FF_CHUNK)]


def _fold8(v):
    acc = v[0:8]
    for r in range(8, v.shape[0], 8):
        acc = acc + v[r:r + 8]
    return acc


def _ff_conv(pre_buf, w_ref, b_ref, half, r0, rows, cols):
    acc = b_ref[half:half + 1, cols]
    for k in range(FF_K):
        off = FF_HALO - (FF_K - 1) + k + r0
        acc = acc + w_ref[k, half:half + 1, cols] * pre_buf[half, off:off + rows, cols]
    return acc


def _ff_act(up_g, up_v, fw, fb):
    s, f = up_g.shape
    tm, tc = _ff_tiles(s, f)
    hb = tm // FF_HALO

    def body(g_ref, gp_ref, v_ref, vp_ref, w_ref, b_ref, act_ref, buf):
        i = pl.program_id(1)
        for half, (m_ref, p_ref) in enumerate(((g_ref, gp_ref), (v_ref, vp_ref))):
            buf[half, 0:FF_HALO, :] = jnp.where(i > 0, p_ref[...], 0.0)
            buf[half, FF_HALO:, :] = m_ref[...]
        for c0 in range(0, tc, FF_LANES):
            cols = slice(c0, c0 + FF_LANES)
            for r0, rows in _row_chunks(tm):
                gate = _ff_conv(buf, w_ref, b_ref, 0, r0, rows, cols)
                val = _ff_conv(buf, w_ref, b_ref, 1, r0, rows, cols)
                act_ref[r0:r0 + rows, cols] = (gate * _sigmoid(gate) * val).astype(BF16)

    main = pl.BlockSpec((tm, tc), lambda j, i: (i, j))
    prev = pl.BlockSpec((FF_HALO, tc), lambda j, i: (jnp.maximum(i * hb - 1, 0), j))
    return pl.pallas_call(
        body, name="ff_act", grid=(f // tc, s // tm),
        in_specs=[main, prev, main, prev, pl.BlockSpec((FF_K, 2, tc), lambda j, i: (0, 0, j)),
                  pl.BlockSpec((2, tc), lambda j, i: (0, j))],
        out_specs=main,
        out_shape=jax.ShapeDtypeStruct((s, f), BF16),
        scratch_shapes=[pltpu.VMEM((2, FF_HALO + tm, tc), F32)],
        compiler_params=_params(("parallel", "parallel"), VMEM_BIG),
    )(up_g, up_g, up_v, up_v, fw, fb)


def _ff_bwd(up_g, up_v, d_act, fw, fb, rider=None):
    s, f = up_g.shape
    tm, tc = _ff_tiles(s, f)
    hb = tm // FF_HALO
    nt = s // tm
    last_halo = s // FF_HALO - 1
    te = tm + FF_HALO

    def body(g_ref, gp_ref, gn_ref, v_ref, vp_ref, vn_ref, da_ref, dan_ref, w_ref, b_ref,
             dg_ref, dv_ref, gw_ref, gb_ref, buf, dbuf, dabuf):
        i = pl.program_id(1)

        @pl.when(i == 0)
        def _():
            gw_ref[...] = jnp.zeros_like(gw_ref)
            gb_ref[...] = jnp.zeros_like(gb_ref)

        for half, (m_ref, p_ref, n_ref) in enumerate(((g_ref, gp_ref, gn_ref), (v_ref, vp_ref, vn_ref))):
            buf[half, 0:FF_HALO, :] = jnp.where(i > 0, p_ref[...], 0.0)
            buf[half, FF_HALO:FF_HALO + tm, :] = m_ref[...]
            buf[half, FF_HALO + tm:, :] = n_ref[...]
        dabuf[0:tm, :] = da_ref[...]
        dabuf[tm:, :] = jnp.where(i < nt - 1, dan_ref[...], 0.0)

        for c0 in range(0, tc, FF_LANES):
            cols = slice(c0, c0 + FF_LANES)
            gb = [jnp.zeros((8, FF_LANES), F32) for _ in range(2)]
            gw = [[jnp.zeros((8, FF_LANES), F32) for _ in range(FF_K)] for _ in range(2)]
            for r0, rows in _row_chunks(te):
                taps = [[buf[half, FF_HALO - (FF_K - 1) + k + r0:FF_HALO - (FF_K - 1) + k + r0 + rows, cols]
                         for k in range(FF_K)] for half in range(2)]
                gate, val = [b_ref[half:half + 1, cols] + sum(w_ref[k, half:half + 1, cols] * taps[half][k]
                                                              for k in range(FF_K)) for half in range(2)]
                da = dabuf[r0:r0 + rows, cols]
                sg = _sigmoid(gate)
                dup = [da * val * (sg * (1.0 + gate * (1.0 - sg))), da * (gate * sg)]
                for half in range(2):
                    dbuf[half, r0:r0 + rows, cols] = dup[half]
                    if r0 < tm:
                        gb[half] = gb[half] + _fold8(dup[half])
                        for k in range(FF_K):
                            gw[half][k] = gw[half][k] + _fold8(dup[half] * taps[half][k])
            for half, out_ref in enumerate((dg_ref, dv_ref)):
                gb_ref[half:half + 1, cols] += jnp.sum(gb[half], axis=0, keepdims=True)
                for k in range(FF_K):
                    gw_ref[k, half:half + 1, cols] += jnp.sum(gw[half][k], axis=0, keepdims=True)
                for r0, rows in _row_chunks(tm):
                    acc = jnp.zeros((rows, FF_LANES), F32)
                    for k in range(FF_K):
                        lo = r0 + FF_K - 1 - k
                        acc = acc + w_ref[k, half:half + 1, cols] * dbuf[half, lo:lo + rows, cols]
                    out_ref[r0:r0 + rows, cols] = acc.astype(BF16)

    main = pl.BlockSpec((tm, tc), lambda j, i: (i, j))
    prev = pl.BlockSpec((FF_HALO, tc), lambda j, i: (jnp.maximum(i * hb - 1, 0), j))
    nxt = pl.BlockSpec((FF_HALO, tc), lambda j, i: (jnp.minimum((i + 1) * hb, last_halo), j))
    (dg, dv, gw, gb), carried = _call(
        body, rider, name="ff_bwd", grid=(f // tc, nt),
        in_specs=[main, prev, nxt, main, prev, nxt, main, nxt,
                  pl.BlockSpec((FF_K, 2, tc), lambda j, i: (0, 0, j)), pl.BlockSpec((2, tc), lambda j, i: (0, j))],
        out_specs=[main, main, pl.BlockSpec((FF_K, 2, tc), lambda j, i: (0, 0, j)),
                   pl.BlockSpec((2, tc), lambda j, i: (0, j))],
        out_shape=[jax.ShapeDtypeStruct((s, f), BF16), jax.ShapeDtypeStruct((s, f), BF16),
                   jax.ShapeDtypeStruct((FF_K, 2, f), F32), jax.ShapeDtypeStruct((2, f), F32)],
        scratch_shapes=[pltpu.VMEM((2, FF_HALO + te, tc), F32), pltpu.VMEM((2, te, tc), F32), pltpu.VMEM((te, tc), F32)],
        compiler_params=_params(("arbitrary", "arbitrary"), VMEM_BIG),
        args=(up_g, up_g, up_g, up_v, up_v, up_v, d_act, d_act, fw, fb))
    return dg, dv, gw, gb, carried


ATT_TILE = Q_BLOCK * max(PATTERN_DILATIONS)


def _stream_rows(start, d, n=Q_BLOCK):
    return pl.ds(start, n) if d == 1 else pl.ds(start, n, stride=d)


def _band_geometry():
    qi = lax.broadcasted_iota(jnp.int32, (Q_BLOCK, 2 * Q_BLOCK), 0)
    ki = lax.broadcasted_iota(jnp.int32, (Q_BLOCK, 2 * Q_BLOCK), 1)
    delta = qi + Q_BLOCK - ki
    return (delta >= 0) & (delta <= Q_BLOCK), delta.astype(F32), ki


def _stream_blocks(d):
    out = []
    for r in range(d):
        for b in range(ATT_TILE // (Q_BLOCK * d)):
            start = b * Q_BLOCK * d + r
            out.append((start, start - Q_BLOCK * d if b > 0 else None))
    return out


def _attn_forward(qkv, slopes, rider=None):
    s = qkv.shape[0]
    nt = s // ATT_TILE
    nhp = ATTN_W // Q_BLOCK

    def body(sl_ref, q_ref, k_ref, kp_ref, v_ref, vp_ref, o_ref, l_ref):
        hp, i = pl.program_id(0), pl.program_id(1)
        head0 = lax.broadcasted_iota(jnp.int32, (Q_BLOCK, Q_BLOCK), 1) < HEAD_DIM
        head0_k = lax.broadcasted_iota(jnp.int32, (2 * Q_BLOCK, Q_BLOCK), 1) < HEAD_DIM
        valid, dist, ki = _band_geometry()
        first_key = jnp.where(i == 0, Q_BLOCK, 0)
        for d in PATTERN_DILATIONS:
            biases = [jnp.where(valid, dist * (-sl_ref[2 * hp + hh] * d), NEG) for hh in range(2)]
            for start, prev in _stream_blocks(d):
                rows = _stream_rows(start, d)
                if prev is None:
                    prow = _stream_rows(ATT_TILE - Q_BLOCK * d + start, d)
                    kp, vp = kp_ref[prow, :], vp_ref[prow, :]
                else:
                    prow = _stream_rows(prev, d)
                    kp, vp = k_ref[prow, :], v_ref[prow, :]
                qv = q_ref[rows, :].astype(BF16)
                k2 = jnp.concatenate([kp, k_ref[rows, :]], axis=0).astype(BF16)
                v2 = jnp.concatenate([vp, v_ref[rows, :]], axis=0).astype(BF16)
                res, mxs = [], []
                for hh in range(2):
                    mine = head0 if hh == 0 else jnp.logical_not(head0)
                    mine_k = head0_k if hh == 0 else jnp.logical_not(head0_k)
                    sc = _dot(jnp.where(mine, qv, jnp.zeros_like(qv)), k2, trans_b=True) + biases[hh]
                    if prev is None:
                        sc = jnp.where(ki < first_key, NEG, sc)
                    mx = jnp.max(sc, axis=1, keepdims=True)
                    p = jnp.exp(sc - mx).astype(BF16)
                    res.append(_dot(p, jnp.where(mine_k, v2, jnp.ones_like(v2))))
                    mxs.append(mx)
                num = jnp.where(head0, res[0], res[1])
                den = pltpu.roll(jnp.where(head0, res[1], res[0]), HEAD_DIM, 1)
                o_new = num / den
                l_new = jnp.where(head0, mxs[0], mxs[1]) + jnp.log(den)
                if d != PATTERN_DILATIONS[0]:
                    oa, la = o_ref[rows, :], l_ref[rows, :]
                    mm = jnp.maximum(la, l_new)
                    wa, wn = jnp.exp(la - mm), jnp.exp(l_new - mm)
                    o_new = (wa * oa + wn * o_new) / (wa + wn)
                    l_new = mm + jnp.log(wa + wn)
                o_ref[rows, :] = o_new
                l_ref[rows, :] = l_new

    def col(off):
        return pl.BlockSpec((ATT_TILE, Q_BLOCK), lambda hp, i: (i, off + hp))

    def col_prev(off):
        return pl.BlockSpec((ATT_TILE, Q_BLOCK), lambda hp, i: (jnp.maximum(i - 1, 0), off + hp))

    outs, carried = _call(
        body, rider, name="attn_fwd", grid=(nhp, nt),
        in_specs=[pl.BlockSpec(memory_space=pltpu.SMEM), col(0), col(nhp), col_prev(nhp), col(2 * nhp), col_prev(2 * nhp)],
        out_specs=[col(0), col(0)],
        out_shape=[jax.ShapeDtypeStruct((s, ATTN_W), F32)] * 2,
        compiler_params=_params(("arbitrary", "arbitrary"), VMEM_BIG),
        args=(slopes, qkv, qkv, qkv, qkv, qkv))
    return (*outs, carried)


def _attn_backward(qkv, d_cat, o, lse, slopes, ones2, rider=None):
    s = qkv.shape[0]
    nt = s // ATT_TILE
    nhp = ATTN_W // Q_BLOCK
    tt = ATT_TILE

    def body(sl_ref, q_ref, k_ref, kp_ref, v_ref, vp_ref, do_ref, o_ref, l_ref, ones_ref,
             dq_ref, dk_ref, dv_ref, dkacc, dvacc, dd):
        hp, step = pl.program_id(0), pl.program_id(1)
        tile = nt - 1 - step
        head0 = lax.broadcasted_iota(jnp.int32, (Q_BLOCK, Q_BLOCK), 1) < HEAD_DIM
        valid, dist, ki = _band_geometry()
        first_key = jnp.where(tile == 0, Q_BLOCK, 0)

        @pl.when(step == 0)
        def _():
            dkacc[tt:, :] = jnp.zeros((tt, Q_BLOCK), F32)
            dvacc[tt:, :] = jnp.zeros((tt, Q_BLOCK), F32)

        @pl.when(step > 0)
        def _():
            dkacc[tt:, :] = dkacc[:tt, :]
            dvacc[tt:, :] = dvacc[:tt, :]

        dkacc[:tt, :] = jnp.zeros((tt, Q_BLOCK), F32)
        dvacc[:tt, :] = jnp.zeros((tt, Q_BLOCK), F32)
        dd[...] = _head_sum(do_ref[...] * o_ref[...], ones_ref)

        ki2 = jnp.concatenate([ki, ki], axis=0)
        for d in PATTERN_DILATIONS:
            bias2 = jnp.concatenate([jnp.where(valid, dist * (-sl_ref[2 * hp + hh] * d), NEG) for hh in range(2)], axis=0)
            for start, prev in _stream_blocks(d):
                rows = _stream_rows(start, d)
                if prev is None:
                    prow = _stream_rows(ATT_TILE - Q_BLOCK * d + start, d)
                    kp, vp = kp_ref[prow, :], vp_ref[prow, :]
                else:
                    prow = _stream_rows(prev, d)
                    kp, vp = k_ref[prow, :], v_ref[prow, :]
                acc_rows = _stream_rows(tt + start - Q_BLOCK * d, d, 2 * Q_BLOCK)
                qv = q_ref[rows, :].astype(BF16)
                dov = do_ref[rows, :].astype(BF16)
                lv, ddv = l_ref[rows, :], dd[rows, :]
                lsw, dsw = pltpu.roll(lv, HEAD_DIM, 1), pltpu.roll(ddv, HEAD_DIM, 1)
                k2 = jnp.concatenate([kp, k_ref[rows, :]], axis=0).astype(BF16)
                v2 = jnp.concatenate([vp, v_ref[rows, :]], axis=0).astype(BF16)
                zq = jnp.zeros_like(qv)
                q2 = jnp.concatenate([jnp.where(head0, qv, zq), jnp.where(head0, zq, qv)], axis=0)
                do2 = jnp.concatenate([jnp.where(head0, dov, zq), jnp.where(head0, zq, dov)], axis=0)
                lh = jnp.concatenate([jnp.where(head0, lv, lsw), jnp.where(head0, lsw, lv)], axis=0)
                dh = jnp.concatenate([jnp.where(head0, ddv, dsw), jnp.where(head0, dsw, ddv)], axis=0)
                sc = _dot(q2, k2, trans_b=True) + bias2
                if prev is None:
                    sc = jnp.where(ki2 < first_key, NEG, sc)
                p = jnp.exp(sc - jnp.concatenate([lh, lh], axis=1))
                ds = p * (_dot(do2, v2, trans_b=True) - jnp.concatenate([dh, dh], axis=1))
                dq2 = _dot(ds.astype(BF16), k2)
                dk2 = _dot(ds.T.astype(BF16), q2)
                dv2 = _dot(p.T.astype(BF16), do2)
                dq = jnp.where(head0, dq2[:Q_BLOCK], dq2[Q_BLOCK:])
                if d == PATTERN_DILATIONS[0]:
                    dq_ref[rows, :] = dq
                else:
                    dq_ref[rows, :] += dq
                dkacc[acc_rows, :] += dk2
                dvacc[acc_rows, :] += dv2
        dk_ref[...] = dkacc[tt:, :]
        dv_ref[...] = dvacc[tt:, :]

    def col(off):
        return pl.BlockSpec((tt, Q_BLOCK), lambda hp, st: (nt - 1 - st, off + hp))

    def col_prev(off):
        return pl.BlockSpec((tt, Q_BLOCK), lambda hp, st: (jnp.maximum(nt - 2 - st, 0), off + hp))

    outs, carried = _call(
        body, rider, name="attn_bwd", grid=(nhp, nt),
        in_specs=[pl.BlockSpec(memory_space=pltpu.SMEM), col(0), col(nhp), col_prev(nhp), col(2 * nhp), col_prev(2 * nhp),
                  col(nhp), col(0), col(0), pl.BlockSpec((Q_BLOCK, Q_BLOCK), lambda hp, st: (0, 0))],
        out_specs=[col(0)] * 3,
        out_shape=[jax.ShapeDtypeStruct((s, ATTN_W), F32)] * 3,
        scratch_shapes=[pltpu.VMEM((2 * tt, Q_BLOCK), F32)] * 2 + [pltpu.VMEM((tt, Q_BLOCK), F32)],
        compiler_params=_params(("arbitrary", "arbitrary"), VMEM_BIG),
        args=(slopes, qkv, qkv, qkv, qkv, qkv, d_cat, o, lse, ones2))
    return (*outs, carried)


def _attn_geometry(s, d):
    length = s // d
    tq = _tile(length, 1024, Q_BLOCK)
    return length, d * ATTN_W, tq, length // tq, tq // Q_BLOCK


def _band(slope_ref, head_pair, d):
    qi = lax.broadcasted_iota(jnp.int32, (Q_BLOCK, 2 * Q_BLOCK), 0)
    ki = lax.broadcasted_iota(jnp.int32, (Q_BLOCK, 2 * Q_BLOCK), 1)
    delta = qi + Q_BLOCK - ki
    valid = (delta >= 0) & (delta <= Q_BLOCK)
    dist = delta.astype(F32) * float(d)
    return [jnp.where(valid, dist * (-slope_ref[2 * head_pair + hh]), NEG) for hh in range(2)], ki


def _attn_fwd(qn, kn, v, slopes, d, prev):
    s = qn.shape[0]
    length, width, tq, nt, nqb = _attn_geometry(s, d)
    view = lambda t: t.reshape(length, width)
    merge = prev is not None

    def body(*refs):
        sl_ref, q_ref, k_ref, kh_ref, v_ref, vh_ref = refs[:6]
        if merge:
            oa_ref, la_ref = refs[6:8]
        o_ref, l_ref, kbuf, vbuf = refs[-4:]
        cb, i = pl.program_id(0), pl.program_id(1)
        biases, ki = _band(sl_ref, cb % 4, d)
        head0 = lax.broadcasted_iota(jnp.int32, (Q_BLOCK, Q_BLOCK), 1) < HEAD_DIM
        kbuf[0:Q_BLOCK, :] = kh_ref[...]
        kbuf[Q_BLOCK:, :] = k_ref[...]
        vbuf[0:Q_BLOCK, :] = vh_ref[...]
        vbuf[Q_BLOCK:, :] = v_ref[...]

        def q_block(qb, carry):
            r0 = pl.multiple_of(qb * Q_BLOCK, Q_BLOCK)
            rows = pl.ds(r0, Q_BLOCK)
            qv = q_ref[rows, :]
            k2, v2 = kbuf[pl.ds(r0, 2 * Q_BLOCK), :], vbuf[pl.ds(r0, 2 * Q_BLOCK), :]
            first_key = jnp.where((i == 0) & (qb == 0), Q_BLOCK, 0)
            o_h, l_h = [], []
            for hh in range(2):
                mine = head0 if hh == 0 else jnp.logical_not(head0)
                sc = _dot(jnp.where(mine, qv, jnp.zeros_like(qv)), k2, trans_b=True) + biases[hh]
                sc = jnp.where(ki < first_key, NEG, sc)
                mx = jnp.max(sc, axis=1, keepdims=True)
                p = jnp.exp(sc - mx)
                den = jnp.sum(p, axis=1, keepdims=True)
                o_h.append(_dot(p.astype(BF16), v2) * (1.0 / den))
                l_h.append(jnp.broadcast_to(mx + jnp.log(den), (Q_BLOCK, Q_BLOCK)))
            o_new = jnp.where(head0, o_h[0], o_h[1])
            l_new = jnp.where(head0, l_h[0], l_h[1])
            if merge:
                oa, la = oa_ref[rows, :], la_ref[rows, :]
                mm = jnp.maximum(la, l_new)
                wa, wn = jnp.exp(la - mm), jnp.exp(l_new - mm)
                o_new = (wa * oa + wn * o_new) / (wa + wn)
                l_new = mm + jnp.log(wa + wn)
            o_ref[rows, :] = o_new
            l_ref[rows, :] = l_new
            return carry

        lax.fori_loop(0, nqb, q_block, 0)

    main = pl.BlockSpec((tq, Q_BLOCK), lambda cb, i: (i, cb))
    halo = pl.BlockSpec((Q_BLOCK, Q_BLOCK), lambda cb, i: (jnp.maximum(i * nqb - 1, 0), cb))
    ins = [slopes, view(qn), view(kn), view(kn), view(v), view(v)] + ([view(prev[0]), view(prev[1])] if merge else [])
    o, lse = pl.pallas_call(
        body, name=f"attn_fwd_d{d}", grid=(width // Q_BLOCK, nt),
        in_specs=[pl.BlockSpec(memory_space=pltpu.SMEM), main, main, halo, main, halo] + ([main, main] if merge else []),
        out_specs=[main, main],
        out_shape=[jax.ShapeDtypeStruct((length, width), F32)] * 2,
        scratch_shapes=[pltpu.VMEM((Q_BLOCK + tq, Q_BLOCK), BF16)] * 2,
        compiler_params=_params(("parallel", "parallel"), VMEM_BIG),
    )(*ins)
    return o.reshape(s, ATTN_W), lse.reshape(s, ATTN_W)


def _attn_bwd_prep(d_cat, o, ones):
    s = o.shape[0]
    tm = _tile(s, 512, 8)

    def body(do_ref, o_ref, ones_ref, dob_ref, dd_ref):
        do = do_ref[...]
        dob_ref[...] = do.astype(BF16)
        dd_ref[...] = _head_sum(do * o_ref[...], ones_ref)

    row = pl.BlockSpec((tm, ATTN_W), lambda i: (i, 0))
    return pl.pallas_call(
        body, name="attn_bwd_prep", grid=(s // tm,),
        in_specs=[pl.BlockSpec((tm, ATTN_W), lambda i: (i, 1)), row, pl.BlockSpec(ones.shape, lambda i: (0, 0))],
        out_specs=[row, row],
        out_shape=[jax.ShapeDtypeStruct((s, ATTN_W), BF16), jax.ShapeDtypeStruct((s, ATTN_W), F32)],
        compiler_params=_params(("parallel",), VMEM_BIG),
    )(d_cat, o, ones)


def _attn_bwd(qn, kn, v, dob, lse, dd, slopes, d, acc):
    s = qn.shape[0]
    length, width, tq, nt, nqb = _attn_geometry(s, d)
    view = lambda t: t.reshape(length, width)
    add = acc is not None

    def body(*refs):
        sl_ref, q_ref, k_ref, kh_ref, v_ref, vh_ref, do_ref, l_ref, dd_ref = refs[:9]
        if add:
            dqa_ref, dka_ref, dva_ref = refs[9:12]
        dq_ref, dk_ref, dv_ref, kbuf, vbuf, dkacc, dvacc, kcarry, vcarry = refs[-9:]
        cb, step = pl.program_id(0), pl.program_id(1)
        tile = nt - 1 - step
        biases, ki = _band(sl_ref, cb % 4, d)
        head0 = lax.broadcasted_iota(jnp.int32, (Q_BLOCK, Q_BLOCK), 1) < HEAD_DIM
        kbuf[0:Q_BLOCK, :] = kh_ref[...]
        kbuf[Q_BLOCK:, :] = k_ref[...]
        vbuf[0:Q_BLOCK, :] = vh_ref[...]
        vbuf[Q_BLOCK:, :] = v_ref[...]
        dkacc[...] = jnp.zeros_like(dkacc)
        dvacc[...] = jnp.zeros_like(dvacc)

        @pl.when(step > 0)
        def _():
            dkacc[tq:, :] = kcarry[...]
            dvacc[tq:, :] = vcarry[...]

        def q_block(qb, carry):
            r0 = pl.multiple_of(qb * Q_BLOCK, Q_BLOCK)
            rows, rows2 = pl.ds(r0, Q_BLOCK), pl.ds(r0, 2 * Q_BLOCK)
            qv, dov = q_ref[rows, :], do_ref[rows, :]
            lv, ddv = l_ref[rows, :], dd_ref[rows, :]
            k2, v2 = kbuf[rows2, :], vbuf[rows2, :]
            first_key = jnp.where((tile == 0) & (qb == 0), Q_BLOCK, 0)
            dq_h = []
            dk2 = jnp.zeros((2 * Q_BLOCK, Q_BLOCK), F32)
            dv2 = jnp.zeros((2 * Q_BLOCK, Q_BLOCK), F32)
            for hh in range(2):
                mine = head0 if hh == 0 else jnp.logical_not(head0)
                qm = jnp.where(mine, qv, jnp.zeros_like(qv))
                dom = jnp.where(mine, dov, jnp.zeros_like(dov))
                lcol = jnp.max(jnp.where(mine, lv, -jnp.inf), axis=1, keepdims=True)
                dcol = jnp.max(jnp.where(mine, ddv, -jnp.inf), axis=1, keepdims=True)
                sc = _dot(qm, k2, trans_b=True) + biases[hh]
                sc = jnp.where(ki < first_key, NEG, sc)
                p = jnp.exp(sc - lcol)
                ds = p * (_dot(dom, v2, trans_b=True) - dcol)
                dq_h.append(_dot(ds.astype(BF16), k2))
                dk2 = dk2 + _dot(ds.T.astype(BF16), qm)
                dv2 = dv2 + _dot(p.T.astype(BF16), dom)
            dq = jnp.where(head0, dq_h[0], dq_h[1])
            if add:
                dq = dq + dqa_ref[rows, :]
            dq_ref[rows, :] = dq
            dkacc[rows2, :] += dk2
            dvacc[rows2, :] += dv2
            return carry

        lax.fori_loop(0, nqb, q_block, 0)
        dk, dv = dkacc[Q_BLOCK:, :], dvacc[Q_BLOCK:, :]
        if add:
            dk, dv = dk + dka_ref[...], dv + dva_ref[...]
        dk_ref[...] = dk
        dv_ref[...] = dv
        kcarry[...] = dkacc[0:Q_BLOCK, :]
        vcarry[...] = dvacc[0:Q_BLOCK, :]

    main = pl.BlockSpec((tq, Q_BLOCK), lambda cb, st: (nt - 1 - st, cb))
    halo = pl.BlockSpec((Q_BLOCK, Q_BLOCK), lambda cb, st: (jnp.maximum((nt - 1 - st) * nqb - 1, 0), cb))
    ins = [slopes, view(qn), view(kn), view(kn), view(v), view(v), view(dob), view(lse), view(dd)]
    ins += [view(t) for t in acc] if add else []
    outs = pl.pallas_call(
        body, name=f"attn_bwd_d{d}", grid=(width // Q_BLOCK, nt),
        in_specs=[pl.BlockSpec(memory_space=pltpu.SMEM), main, main, halo, main, halo, main, main, main]
        + ([main] * 3 if add else []),
        out_specs=[main] * 3,
        out_shape=[jax.ShapeDtypeStruct((length, width), F32)] * 3,
        scratch_shapes=[pltpu.VMEM((Q_BLOCK + tq, Q_BLOCK), BF16)] * 2 + [pltpu.VMEM((Q_BLOCK + tq, Q_BLOCK), F32)] * 2
        + [pltpu.VMEM((Q_BLOCK, Q_BLOCK), F32)] * 2,
        compiler_params=_params(("parallel", "arbitrary"), VMEM_BIG),
    )(*ins)
    return tuple(t.reshape(s, ATTN_W) for t in outs)


def _attn_bwd_post(dq, dk, dv, qk, qg, kg, ones):
    s = dq.shape[0]
    tm = _tile(s, 512, 8)
    aw = ATTN_W

    def body(dq_ref, dk_ref, dv_ref, qk_ref, qg_ref, kg_ref, ones_ref, out_ref, gq_ref, gk_ref):
        @pl.when(pl.program_id(0) == 0)
        def _():
            gq_ref[...] = jnp.zeros_like(gq_ref)
            gk_ref[...] = jnp.zeros_like(gk_ref)

        def norm_bwd(dy, raw, g, scale):
            r = lax.rsqrt(_head_sum(raw * raw, ones_ref) * (1.0 / HEAD_DIM) + EPS)
            xh = raw * r
            gd = dy * (g * scale)
            dx = r * (gd - xh * (_head_sum(gd * xh, ones_ref) * (1.0 / HEAD_DIM)))
            gsum = jnp.sum(dy * xh, axis=0, keepdims=True) * scale
            for shift in (256, 128, 64):
                gsum = gsum + pltpu.roll(gsum, shift, 1)
            return dx, gsum

        dxq, gq = norm_bwd(dq_ref[...], qk_ref[:, :aw], qg_ref[...], HEAD_DIM ** -0.5)
        dxk, gk = norm_bwd(dk_ref[...], qk_ref[:, aw:], kg_ref[...], 1.0)
        out_ref[:, :aw] = dxq.astype(BF16)
        out_ref[:, aw:2 * aw] = dxk.astype(BF16)
        out_ref[:, 2 * aw:] = dv_ref[...].astype(BF16)
        gq_ref[...] += gq
        gk_ref[...] += gk

    row = lambda w: pl.BlockSpec((tm, w), lambda i: (i, 0))
    full = lambda t: pl.BlockSpec(t.shape, lambda i: (0, 0))
    vec = pl.BlockSpec((1, aw), lambda i: (0, 0))
    return pl.pallas_call(
        body, name="attn_bwd_post", grid=(s // tm,),
        in_specs=[row(aw), row(aw), row(aw), row(2 * aw), full(qg), full(kg), full(ones)],
        out_specs=[row(3 * aw), vec, vec],
        out_shape=[jax.ShapeDtypeStruct((s, 3 * aw), BF16), jax.ShapeDtypeStruct((1, aw), F32),
                   jax.ShapeDtypeStruct((1, aw), F32)],
        compiler_params=_params(("arbitrary",), VMEM_BIG),
    )(dq, dk, dv, qk, qg, kg, ones)


def _sum_parts(name, parts):
    _, n, w = parts.shape
    tn = _tile(n, 256, 8)

    def body(p_ref, o_ref):
        acc = p_ref[0].astype(F32)
        for j in range(1, N_DEV):
            acc = acc + p_ref[j].astype(F32)
        o_ref[...] = acc

    return pl.pallas_call(
        body, name=name, grid=(n // tn,),
        in_specs=[pl.BlockSpec((N_DEV, tn, w), lambda i: (0, i, 0))],
        out_specs=pl.BlockSpec((tn, w), lambda i: (i, 0)),
        out_shape=jax.ShapeDtypeStruct((n, w), F32),
        compiler_params=_params(("parallel",), VMEM_BIG),
    )(parts)


def _adamw(name, w, g, m, v):
    n, cols = w.shape
    tn = _tile(n, 256, 8)
    c1 = 1.0 - ADAM_B1 ** ADAM_STEP
    c2 = 1.0 - ADAM_B2 ** ADAM_STEP

    def body(w_ref, g_ref, m_ref, v_ref, d_ref, nm_ref, nv_ref):
        gv = g_ref[...]
        nm = ADAM_B1 * m_ref[...] + (1.0 - ADAM_B1) * gv
        nv = ADAM_B2 * v_ref[...] + (1.0 - ADAM_B2) * (gv * gv)
        nm_ref[...] = nm
        nv_ref[...] = nv
        d_ref[...] = -ADAM_LR * ((nm / c1) / (jnp.sqrt(nv / c2) + ADAM_EPS) + ADAM_WD * w_ref[...])

    blk = pl.BlockSpec((tn, cols), lambda i: (i, 0))
    return pl.pallas_call(
        body, name=name, grid=(n // tn,),
        in_specs=[blk] * 4, out_specs=[blk] * 3,
        out_shape=[jax.ShapeDtypeStruct((n, cols), F32)] * 3,
        compiler_params=_params(("parallel",), VMEM_BIG),
    )(w, g, m, v)


def _pack(vectors, width):
    flat = jnp.concatenate([t.reshape(-1) for t in vectors])
    rows = -(-flat.shape[0] // (8 * width)) * 8
    return jnp.pad(flat, (0, rows * width - flat.shape[0])).reshape(rows, width)


def _unpack(packed, shapes):
    flat = packed.reshape(-1)
    out, at = [], 0
    for shp in shapes:
        size = 1
        for dim in shp:
            size *= dim
        out.append(flat[at:at + size].reshape(shp))
        at += size
    return out


def kernel(x, norm1_g, w_in, conv_w, conv_b, cn_g, cn_b, q_norm_g, k_norm_g, w_out, norm2_g, w_up, ffconv_w, ffconv_b, w_down, loss_target, m_norm1_g, m_w_in, m_conv_w, m_conv_b, m_cn_g, m_cn_b, m_q_norm_g, m_k_norm_g, m_w_out, m_norm2_g, m_w_up, m_ffconv_w, m_ffconv_b, m_w_down, v_norm1_g, v_w_in, v_conv_w, v_conv_b, v_cn_g, v_cn_b, v_q_norm_g, v_k_norm_g, v_w_out, v_norm2_g, v_w_up, v_ffconv_w, v_ffconv_b, v_w_down):
    s, d = x.shape[1], x.shape[2]
    f = w_down.shape[0] * N_DEV
    n_in = w_in.shape[1] * N_DEV
    xs, target = x.reshape(s, d), loss_target.reshape(s, d)
    me = _linear((lax.axis_index("x"), lax.axis_index("y"), lax.axis_index("c")))

    g_in, g_filt = _communicate("gather_w_in", _Gather([w_in.T.astype(BF16), _pack([conv_w, ffconv_w], 128)]))
    w_in_t = g_in.reshape(n_in, d)
    filt = g_filt.reshape(N_DEV, -1)
    n_cw = conv_w.size
    cw = filt[:, :n_cw].reshape(N_DEV, CONV_K, -1).transpose(1, 0, 2).reshape(CONV_K, CONV_CH)
    cw = jnp.pad(cw, ((0, CONV_HALO - CONV_K), (0, 0)))
    fw = filt[:, n_cw:n_cw + ffconv_w.size].reshape(N_DEV, FF_K, -1).transpose(1, 0, 2).reshape(FF_K, 2, f)
    fb = ffconv_b.reshape(2, f)
    row = lambda t: t.reshape(1, -1)
    g1, g2 = row(norm1_g), row(norm2_g)
    qg, kg = row(jnp.tile(q_norm_g, N_HEADS)), row(jnp.tile(k_norm_g, N_HEADS))
    lanes = jnp.arange(ATTN_W) // HEAD_DIM
    ones = (lanes[:, None] == lanes[None, :]).astype(BF16)
    slopes = 2.0 ** (-8.0 * jnp.arange(1, N_HEADS + 1, dtype=F32) / N_HEADS)

    h, a, qk, qkv, (g_out,) = _proj_fwd(xs, g1, w_in_t, qg, kg, ones, _Gather([w_out.astype(BF16)]))
    u1, u, (g_up,) = _conv_fwd(a, cw, row(conv_b), row(cn_g), row(cn_b), _Gather([w_up.T.astype(BF16)]))
    o, lse, (g_down,) = _attn_forward(qkv, slopes, _Gather([w_down.astype(BF16)]))
    w_out_f = g_out.reshape(2 * CONV_CH, d)
    w_up_t = g_up.reshape(2 * f, d)
    w_down_f = g_down.reshape(f, d)
    x1, h2 = _mix_out(u, o, w_out_f, xs, g2)
    up_g, up_v = _up_fwd(h2, w_up_t, f)
    act = _ff_act(up_g, up_v, fw, fb)
    loss_acc, dyf, dyb = _down_loss(act, w_down_f, x1, target)
    loss = lax.psum(loss_acc[0, 0], MESH_AXES)

    tk = _tile(s, 512, 8)
    gw_down = _mm_tn("grad_w_down", [act], dyb, _tile(f, 1408, 128), tk)
    d_act = _mm("d_act", dyb, w_down_f, True, F32, _tile(s, 512, 8), _tile(f, 1408, 128))
    blocks = lambda t: _Exchange([t.reshape(N_DEV, t.shape[0] // N_DEV, d)])
    dpre_g, dpre_v, gfw, gfb, (r_down,) = _ff_bwd(up_g, up_v, d_act, fw, fb, blocks(gw_down))
    gw_up = _mm_tn("grad_w_up", [dpre_g, dpre_v], h2, _tile(f, 1408, 128), tk)
    dx1f, dx1b, gg2 = _up_bwd(dpre_g, dpre_v, w_up_t, x1, dyf, g2)
    gw_out = _mm_tn("grad_w_out", [u, o], dx1b, CONV_CH, tk)
    d_cat = _mm("d_cat", dx1b, w_out_f, True, F32, _tile(s, 512, 8), 2 * CONV_CH)
    d_a, gcw, gcb, gcg, gcbeta, (r_out,) = _conv_bwd(a, u1, d_cat, cw, row(cn_g), row(cn_b), blocks(gw_out))
    dq, dk, dv, (r_up,) = _attn_backward(qkv, d_cat, o, lse, slopes, ones[:Q_BLOCK, :Q_BLOCK], blocks(gw_up))
    d_qkv, gqg, gkg = _attn_bwd_post(dq, dk, dv, qk, qg, kg, ones)
    gw_in = _mm_tn("grad_w_in", [d_a, d_qkv], h, 512, tk)
    grad_x, gg1, (r_in,) = _proj_bwd(d_a, d_qkv, w_in_t, xs, dx1f, g1, blocks(gw_in))
    g_w_in_t, g_w_out, g_w_up_t, g_w_down = [_sum_parts(f"sum_grad_{i}", r) for i, r in enumerate((r_in, r_out, r_up, r_down))]
    g_w_in, g_w_up = g_w_in_t.T, g_w_up_t.T

    small_shapes = [(d,), (d,), (CONV_CH,), (CONV_CH,), (CONV_CH,), (ATTN_W,), (ATTN_W,), (2 * f,),
                    (CONV_K, CONV_CH), (FF_K, 2 * f)]
    small = _pack([gg1, gg2, gcb, gcg, gcbeta, gqg, gkg, gfb, gcw[:CONV_K], gfw], 1024)
    (small_all,) = _communicate("gather_small_grads", _Gather([small]))
    sg1, sg2, scb, scg, scbeta, sqg, skg, sfb, scw, sfw = _unpack(_sum_parts("sum_small_grads", small_all), small_shapes)
    cwl, fwl = conv_w.shape[1], ffconv_w.shape[1]
    g_small = [sg1, lax.dynamic_slice_in_dim(scw, me * cwl, cwl, 1), scb, scg, scbeta,
               sqg[:HEAD_DIM], skg[:HEAD_DIM], sg2, lax.dynamic_slice_in_dim(sfw, me * fwl, fwl, 1), sfb]

    w_small = [norm1_g, conv_w, conv_b, cn_g, cn_b, q_norm_g, k_norm_g, norm2_g, ffconv_w, ffconv_b]
    m_small = [m_norm1_g, m_conv_w, m_conv_b, m_cn_g, m_cn_b, m_q_norm_g, m_k_norm_g, m_norm2_g, m_ffconv_w, m_ffconv_b]
    v_small = [v_norm1_g, v_conv_w, v_conv_b, v_cn_g, v_cn_b, v_q_norm_g, v_k_norm_g, v_norm2_g, v_ffconv_w, v_ffconv_b]
    shapes = [t.shape for t in w_small]
    packed = _adamw("adamw_small", *[_pack(ts, 128) for ts in (w_small, g_small, m_small, v_small)])
    d_small, nm_small, nv_small = [_unpack(p, shapes) for p in packed]
    upd = {}
    for name, wt, gt, mt, vt in (("w_in", w_in, g_w_in, m_w_in, v_w_in), ("w_out", w_out, g_w_out, m_w_out, v_w_out),
                                 ("w_up", w_up, g_w_up, m_w_up, v_w_up), ("w_down", w_down, g_w_down, m_w_down, v_w_down)):
        upd[name] = (gt,) + tuple(_adamw("adamw_" + name, wt, gt, mt, vt))
    order = ["norm1_g", "w_in", "conv_w", "conv_b", "cn_g", "cn_b", "q_norm_g", "k_norm_g", "w_out", "norm2_g",
             "w_up", "ffconv_w", "ffconv_b", "w_down"]
    small_names = ["norm1_g", "conv_w", "conv_b", "cn_g", "cn_b", "q_norm_g", "k_norm_g", "norm2_g", "ffconv_w", "ffconv_b"]
    for i, name in enumerate(small_names):
        upd[name] = (g_small[i], d_small[i], nm_small[i], nv_small[i])
    outs = [loss, grad_x.reshape(x.shape)]
    for field in range(4):
        outs += [upd[name][field] for name in order]
    return tuple(outs)
```

```python
import functools

import jax
import jax.numpy as jnp
from jax import lax
from jax.experimental import pallas as pl
from jax.experimental.pallas import tpu as pltpu

F32, BF16 = jnp.float32, jnp.bfloat16
N_DEV = 8
N_HEADS, HEAD_DIM = 8, 64
CONV_CH = 512
ATTN_W = N_HEADS * HEAD_DIM
CONV_K, FF_K = 31, 3
CONV_HALO = 32
FF_HALO = 8
PATTERN_DILATIONS = (1, 4, 16)
Q_BLOCK = 128
EPS = 1e-6
NEG = -1e30
ADAM_LR, ADAM_B1, ADAM_B2, ADAM_EPS, ADAM_WD, ADAM_STEP = 0.001, 0.9, 0.999, 1e-08, 0.01, 10
MESH_AXES = ("x", "y", "c")
VMEM_BIG = 56 * 1024 * 1024
ANY = pl.BlockSpec(memory_space=pl.ANY)


def _params(sem=None, vmem=None):
    return pltpu.CompilerParams(dimension_semantics=sem, vmem_limit_bytes=vmem)


def _dot(a, b, trans_a=False, trans_b=False):
    dims = (((0 if trans_a else 1,), (1 if trans_b else 0,)), ((), ()))
    return lax.dot_general(a, b, dims, preferred_element_type=F32)


def _sigmoid(z):
    return 1.0 / (1.0 + jnp.exp(-z))


def _tile(n, pref, mult):
    if n <= pref:
        return n
    t = (pref // mult) * mult
    while n % t:
        t -= mult
    return t


def _head_sum(v, ones_ref):
    hi = v.astype(BF16)
    lo = (v - hi.astype(F32)).astype(BF16)
    ones = ones_ref[...]
    return _dot(hi, ones) + _dot(lo, ones)


def _linear(p):
    return 4 * p[0] + 2 * p[1] + p[2]


class _Gather:
    def __init__(self, arrs):
        self.arrs = list(arrs)
        self.n = len(self.arrs)
        self.out_shape = [jax.ShapeDtypeStruct((N_DEV,) + a.shape, a.dtype) for a in self.arrs]

    def _setup(self, ins, outs, sems):
        send_sems, recv_sems, local_sems = sems
        x, y, c = lax.axis_index("x"), lax.axis_index("y"), lax.axis_index("c")
        me, sib = (x, y, c), (x, y, 1 - c)
        chips = [(1 - x, y), (x, 1 - y), (1 - x, 1 - y)]

        def copy(a, k, block, to, src=None):
            dst = outs[a].at[_linear(block)]
            return pltpu.make_async_remote_copy(
                src_ref=dst if src is None else src, dst_ref=dst,
                send_sem=send_sems.at[7 * a + k], recv_sem=recv_sems.at[7 * a + k],
                device_id=to, device_id_type=pl.DeviceIdType.MESH)

        mine = [pltpu.make_async_copy(ins[a], outs[a].at[_linear(me)], local_sems.at[a]) for a in range(self.n)]
        first = []
        for a in range(self.n):
            first.append(copy(a, 0, me, sib, src=ins[a]))
            first += [copy(a, 1 + j, me, (*chip, c), src=ins[a]) for j, chip in enumerate(chips)]
        return copy, mine, first, me, sib, chips, c

    def start(self, ins, outs, sems):
        _, mine, first, *_ = self._setup(ins, outs, sems)
        for cp in mine + first:
            cp.start()

    def finish(self, ins, outs, sems):
        copy, mine, first, me, sib, chips, c = self._setup(ins, outs, sems)
        passed = []
        for j, chip in enumerate(chips):
            for a in range(self.n):
                copy(a, 1 + j, (*chip, c), me).wait_recv()
                fwd = copy(a, 4 + j, (*chip, c), sib)
                fwd.start()
                passed.append(fwd)
        for a in range(self.n):
            copy(a, 0, sib, me).wait_recv()
            for j, chip in enumerate(chips):
                copy(a, 4 + j, (*chip, 1 - c), me).wait_recv()
        for cp in first + passed:
            cp.wait_send()
        for cp in mine:
            cp.wait()


class _Exchange:
    def __init__(self, arrs):
        self.arrs = list(arrs)
        self.n = len(self.arrs)
        self.out_shape = [jax.ShapeDtypeStruct(a.shape, a.dtype) for a in self.arrs]

    def _setup(self, ins, outs, sems):
        send_sems, recv_sems, local_sems = sems
        x, y, c = lax.axis_index("x"), lax.axis_index("y"), lax.axis_index("c")
        me = _linear((x, y, c))
        mine = [pltpu.make_async_copy(ins[a].at[me], outs[a].at[me], local_sems.at[a]) for a in range(self.n)]
        sends, recvs = [], []
        for p in range(1, N_DEV):
            peer = (1 - x if p & 4 else x, 1 - y if p & 2 else y, 1 - c if p & 1 else c)
            for a in range(self.n):
                sem = dict(send_sem=send_sems.at[7 * a + p - 1], recv_sem=recv_sems.at[7 * a + p - 1],
                           device_id=peer, device_id_type=pl.DeviceIdType.MESH)
                sends.append(pltpu.make_async_remote_copy(
                    src_ref=ins[a].at[_linear(peer)], dst_ref=outs[a].at[me], **sem))
                recvs.append(pltpu.make_async_remote_copy(
                    src_ref=ins[a].at[_linear(peer)], dst_ref=outs[a].at[_linear(peer)], **sem))
        return mine, sends, recvs

    def start(self, ins, outs, sems):
        mine, sends, _ = self._setup(ins, outs, sems)
        for cp in mine + sends:
            cp.start()

    def finish(self, ins, outs, sems):
        mine, sends, recvs = self._setup(ins, outs, sems)
        for cp in recvs:
            cp.wait_recv()
        for cp in sends:
            cp.wait_send()
        for cp in mine:
            cp.wait()


def _rider_scratch(rider):
    return [pltpu.SemaphoreType.DMA((7 * rider.n,)), pltpu.SemaphoreType.DMA((7 * rider.n,)),
            pltpu.SemaphoreType.DMA((rider.n,))]


def _communicate(name, rider):
    na = rider.n

    def body(*refs):
        ins, outs, sems = refs[:na], refs[na:2 * na], refs[2 * na:]
        rider.start(ins, outs, sems)
        rider.finish(ins, outs, sems)

    return pl.pallas_call(
        body, name=name, out_shape=rider.out_shape, in_specs=[ANY] * na, out_specs=[ANY] * na,
        scratch_shapes=_rider_scratch(rider),
    )(*rider.arrs)


def _call(body, rider=None, *, name, grid, in_specs, out_specs, out_shape, scratch_shapes=(), compiler_params, args):
    if rider is None:
        out = pl.pallas_call(body, name=name, grid=grid, in_specs=in_specs, out_specs=out_specs, out_shape=out_shape,
                             scratch_shapes=list(scratch_shapes), compiler_params=compiler_params)(*args)
        return out, None
    n_in, n_out, n_scr, na = len(in_specs), len(out_specs), len(scratch_shapes), rider.n

    def carried(*refs):
        ins, refs = refs[:n_in], refs[n_in:]
        r_ins, refs = refs[:na], refs[na:]
        outs, refs = refs[:n_out], refs[n_out:]
        r_outs, refs = refs[:na], refs[na:]
        scratch, sems = refs[:n_scr], refs[n_scr:]
        ids = [pl.program_id(ax) for ax in range(len(grid))]
        first, last = ids[0] == 0, ids[0] == grid[0] - 1
        for ax in range(1, len(grid)):
            first, last = first & (ids[ax] == 0), last & (ids[ax] == grid[ax] - 1)

        @pl.when(first)
        def _():
            rider.start(r_ins, r_outs, sems)

        body(*ins, *outs, *scratch)

        @pl.when(last)
        def _():
            rider.finish(r_ins, r_outs, sems)

    out = pl.pallas_call(
        carried, name=name, grid=grid, in_specs=list(in_specs) + [ANY] * na, out_specs=list(out_specs) + [ANY] * na,
        out_shape=list(out_shape) + rider.out_shape, scratch_shapes=list(scratch_shapes) + _rider_scratch(rider),
        compiler_params=compiler_params)(*args, *rider.arrs)
    return out[:n_out], out[n_out:]


def _mm(name, a, w, trans_b, out_dtype, tm, tn):
    m, k = a.shape
    n = w.shape[0] if trans_b else w.shape[1]

    def body(a_ref, w_ref, o_ref):
        o_ref[...] = _dot(a_ref[...], w_ref[...], trans_b=trans_b).astype(o_ref.dtype)

    w_spec = pl.BlockSpec((tn, k), lambda j, i: (j, 0)) if trans_b else pl.BlockSpec((k, tn), lambda j, i: (0, j))
    return pl.pallas_call(
        body, name=name, grid=(n // tn, m // tm),
        in_specs=[pl.BlockSpec((tm, k), lambda j, i: (i, 0)), w_spec],
        out_specs=pl.BlockSpec((tm, tn), lambda j, i: (i, j)),
        out_shape=jax.ShapeDtypeStruct((m, n), out_dtype),
        compiler_params=_params(("parallel", "parallel"), VMEM_BIG),
    )(a, w)


def _mm_tn(name, groups, b, tk):
    s, d = b.shape
    parts = [p for grp in groups for p in grp]
    owner = [g for g, grp in enumerate(groups) for _ in grp]
    width = sum(p.shape[1] for p in groups[0])
    offsets = []
    for grp in groups:
        at = 0
        for p in grp:
            offsets.append(at)
            at += p.shape[1]
        assert at == width
    np_, nk = len(parts), s // tk

    def body(*refs):
        a_refs, b_ref, o_ref, acc_ref = refs[:np_], refs[np_], refs[np_ + 1], refs[np_ + 2]
        g, kk = pl.program_id(0), pl.program_id(1)

        @pl.when(kk == 0)
        def _():
            acc_ref[...] = jnp.zeros_like(acc_ref)

        for grp in range(len(groups)):
            @pl.when(g == grp)
            def _(grp=grp):
                for p in range(np_):
                    if owner[p] == grp:
                        rows = slice(offsets[p], offsets[p] + parts[p].shape[1])
                        acc_ref[rows, :] += _dot(a_refs[p][...].astype(BF16), b_ref[...], trans_a=True)

        @pl.when(kk == nk - 1)
        def _():
            o_ref[...] = acc_ref[...].astype(o_ref.dtype)

    def a_spec(p):
        return pl.BlockSpec((tk, parts[p].shape[1]), lambda g, kk: (jnp.where(g == owner[p], kk, 0), 0))

    return pl.pallas_call(
        body, name=name, grid=(len(groups), nk),
        in_specs=[a_spec(p) for p in range(np_)] + [pl.BlockSpec((tk, d), lambda g, kk: (kk, 0))],
        out_specs=pl.BlockSpec((width, d), lambda g, kk: (g, 0)),
        out_shape=jax.ShapeDtypeStruct((len(groups) * width, d), BF16),
        scratch_shapes=[pltpu.VMEM((width, d), F32)],
        compiler_params=_params(("arbitrary", "arbitrary"), VMEM_BIG),
    )(*parts, b)


def _proj_fwd(x, g1, w_in_t, qg, kg, ones, rider=None):
    s, d = x.shape
    tm = _tile(s, 512, 8)
    c2, aw = 2 * CONV_CH, ATTN_W

    def body(x_ref, g_ref, w_ref, qg_ref, kg_ref, ones_ref, h_ref, a_ref, qk_ref, qkv_ref):
        xv = x_ref[...]
        r = lax.rsqrt(jnp.mean(xv * xv, axis=-1, keepdims=True) + EPS)
        h = (xv * r * g_ref[...]).astype(BF16)
        h_ref[...] = h
        proj = _dot(h, w_ref[...], trans_b=True)
        a_ref[...] = proj[:, :c2]
        qk_ref[...] = proj[:, c2:c2 + 2 * aw]
        q, k = proj[:, c2:c2 + aw], proj[:, c2 + aw:c2 + 2 * aw]
        rq = lax.rsqrt(_head_sum(q * q, ones_ref) * (1.0 / HEAD_DIM) + EPS)
        rk = lax.rsqrt(_head_sum(k * k, ones_ref) * (1.0 / HEAD_DIM) + EPS)
        qkv_ref[:, :aw] = q * rq * qg_ref[...] * (HEAD_DIM ** -0.5)
        qkv_ref[:, aw:2 * aw] = k * rk * kg_ref[...]
        qkv_ref[:, 2 * aw:] = proj[:, c2 + 2 * aw:]

    row = lambda w: pl.BlockSpec((tm, w), lambda i: (i, 0))
    full = lambda a: pl.BlockSpec(a.shape, lambda i: (0, 0))
    outs, carried = _call(
        body, rider, name="proj_fwd", grid=(s // tm,),
        in_specs=[row(d), full(g1), full(w_in_t), full(qg), full(kg), full(ones)],
        out_specs=[row(d), row(c2), row(2 * aw), row(3 * aw)],
        out_shape=[jax.ShapeDtypeStruct((s, d), BF16), jax.ShapeDtypeStruct((s, c2), F32),
                   jax.ShapeDtypeStruct((s, 2 * aw), F32), jax.ShapeDtypeStruct((s, 3 * aw), F32)],
        compiler_params=_params(("arbitrary",), VMEM_BIG),
        args=(x, g1, w_in_t, qg, kg, ones))
    return (*outs, carried)


def _mix_out(u, o, w_out, x, g2):
    s, d = x.shape
    tm = _tile(s, 512, 8)

    def body(u_ref, o_ref, w_ref, x_ref, g_ref, x1_ref, h2_ref):
        x1 = x_ref[...] + _dot(u_ref[...], w_ref[:CONV_CH, :]) + _dot(o_ref[...].astype(BF16), w_ref[CONV_CH:, :])
        x1_ref[...] = x1
        r = lax.rsqrt(jnp.mean(x1 * x1, axis=-1, keepdims=True) + EPS)
        h2_ref[...] = (x1 * r * g_ref[...]).astype(BF16)

    row = lambda w: pl.BlockSpec((tm, w), lambda i: (i, 0))
    full = lambda a: pl.BlockSpec(a.shape, lambda i: (0, 0))
    return pl.pallas_call(
        body, name="mix_out", grid=(s // tm,),
        in_specs=[row(CONV_CH), row(ATTN_W), full(w_out), row(d), full(g2)],
        out_specs=[row(d), row(d)],
        out_shape=[jax.ShapeDtypeStruct((s, d), F32), jax.ShapeDtypeStruct((s, d), BF16)],
        compiler_params=_params(("parallel",), VMEM_BIG),
    )(u, o, w_out, x, g2)


def _up_fwd(h2, w_up_t, f):
    s, d = h2.shape
    tm, tn = _tile(s, 512, 8), _tile(f, 1408, 128)
    nb = f // tn

    def body(h_ref, wg_ref, wv_ref, g_ref, v_ref):
        hv = h_ref[...]
        g_ref[...] = _dot(hv, wg_ref[...], trans_b=True)
        v_ref[...] = _dot(hv, wv_ref[...], trans_b=True)

    return pl.pallas_call(
        body, name="up_fwd", grid=(nb, s // tm),
        in_specs=[pl.BlockSpec((tm, d), lambda j, i: (i, 0)), pl.BlockSpec((tn, d), lambda j, i: (j, 0)),
                  pl.BlockSpec((tn, d), lambda j, i: (j + nb, 0))],
        out_specs=[pl.BlockSpec((tm, tn), lambda j, i: (i, j))] * 2,
        out_shape=[jax.ShapeDtypeStruct((s, f), F32)] * 2,
        compiler_params=_params(("parallel", "parallel"), VMEM_BIG),
    )(h2, w_up_t, w_up_t)


def _down_loss(act, w_down, x1, target):
    s, d = x1.shape
    f = act.shape[1]
    tm = _tile(s, 512, 8)

    def body(a_ref, w_ref, x1_ref, t_ref, loss_ref, dyf_ref, dyb_ref):
        @pl.when(pl.program_id(0) == 0)
        def _():
            loss_ref[...] = jnp.zeros_like(loss_ref)

        diff = x1_ref[...] + _dot(a_ref[...], w_ref[...]) - t_ref[...]
        sq = jnp.sum(jnp.sum(diff * diff, axis=1, keepdims=True), axis=0, keepdims=True)
        loss_ref[...] += jnp.broadcast_to(sq * (0.5 / d), loss_ref.shape)
        dy = diff * (1.0 / d)
        dyf_ref[...] = dy
        dyb_ref[...] = dy.astype(BF16)

    row = lambda w: pl.BlockSpec((tm, w), lambda i: (i, 0))
    return pl.pallas_call(
        body, name="down_loss", grid=(s // tm,),
        in_specs=[row(f), pl.BlockSpec((f, d), lambda i: (0, 0)), row(d), row(d)],
        out_specs=[pl.BlockSpec((8, 128), lambda i: (0, 0)), row(d), row(d)],
        out_shape=[jax.ShapeDtypeStruct((8, 128), F32), jax.ShapeDtypeStruct((s, d), F32),
                   jax.ShapeDtypeStruct((s, d), BF16)],
        compiler_params=_params(("arbitrary",), VMEM_BIG),
    )(act, w_down, x1, target)


def _norm_bwd_tail(dh, xv, g, resid, d):
    r = lax.rsqrt(jnp.mean(xv * xv, axis=-1, keepdims=True) + EPS)
    xh = xv * r
    gd = dh * g
    dx = r * (gd - xh * (jnp.sum(gd * xh, axis=-1, keepdims=True) * (1.0 / d)))
    return resid + dx, jnp.sum(dh * xh, axis=0, keepdims=True)


def _up_bwd(dg, dv, w_up_t, x1, dyf, g2):
    s, d = x1.shape
    f = dg.shape[1]
    tm = _tile(s, 512, 8)

    def body(dg_ref, dv_ref, w_ref, x1_ref, dy_ref, g_ref, dxf_ref, dxb_ref, gg_ref):
        @pl.when(pl.program_id(0) == 0)
        def _():
            gg_ref[...] = jnp.zeros_like(gg_ref)

        dh = _dot(dg_ref[...], w_ref[:f, :]) + _dot(dv_ref[...], w_ref[f:, :])
        dx, gg = _norm_bwd_tail(dh, x1_ref[...], g_ref[...], dy_ref[...], d)
        dxf_ref[...] = dx
        dxb_ref[...] = dx.astype(BF16)
        gg_ref[...] += gg

    row = lambda w: pl.BlockSpec((tm, w), lambda i: (i, 0))
    full = lambda a: pl.BlockSpec(a.shape, lambda i: (0, 0))
    return pl.pallas_call(
        body, name="up_bwd", grid=(s // tm,),
        in_specs=[row(f), row(f), full(w_up_t), row(d), row(d), full(g2)],
        out_specs=[row(d), row(d), pl.BlockSpec((1, d), lambda i: (0, 0))],
        out_shape=[jax.ShapeDtypeStruct((s, d), F32), jax.ShapeDtypeStruct((s, d), BF16),
                   jax.ShapeDtypeStruct((1, d), F32)],
        compiler_params=_params(("arbitrary",), VMEM_BIG),
    )(dg, dv, w_up_t, x1, dyf, g2)


def _proj_bwd(d_a, d_qkv, w_in_t, x, dx1, g1, rider=None):
    s, d = x.shape
    na, nq = d_a.shape[1], d_qkv.shape[1]
    tm = _tile(s, 512, 8)

    def body(da_ref, dq_ref, w_ref, x_ref, r_ref, g_ref, gx_ref, gg_ref):
        @pl.when(pl.program_id(0) == 0)
        def _():
            gg_ref[...] = jnp.zeros_like(gg_ref)

        dh = _dot(da_ref[...], w_ref[:na, :]) + _dot(dq_ref[...], w_ref[na:, :])
        dx, gg = _norm_bwd_tail(dh, x_ref[...], g_ref[...], r_ref[...], d)
        gx_ref[...] = dx
        gg_ref[...] += gg

    row = lambda w: pl.BlockSpec((tm, w), lambda i: (i, 0))
    full = lambda a: pl.BlockSpec(a.shape, lambda i: (0, 0))
    outs, carried = _call(
        body, rider, name="proj_bwd", grid=(s // tm,),
        in_specs=[row(na), row(nq), full(w_in_t), row(d), row(d), full(g1)],
        out_specs=[row(d), pl.BlockSpec((1, d), lambda i: (0, 0))],
        out_shape=[jax.ShapeDtypeStruct((s, d), F32), jax.ShapeDtypeStruct((1, d), F32)],
        compiler_params=_params(("arbitrary",), VMEM_BIG),
        args=(d_a, d_qkv, w_in_t, x, dx1, g1))
    return (*outs, carried)


CONV_CHUNK = 32
SUBLANES = 8


def _glu(av):
    return av[:, :CONV_CH] * _sigmoid(av[:, CONV_CH:])


def _chunks(n, size=CONV_CHUNK):
    return [(r0, min(size, n - r0)) for r0 in range(0, n, size)]


def _fill_shifted(sbuf, rows):
    for r in range(1, SUBLANES):
        for r0, n in _chunks(rows, 64):
            sbuf[r, r0:r0 + n, :] = sbuf[0, r0 + r:r0 + r + n, :]


def _tap(sbuf, offset, r0, rows):
    q, r = divmod(offset, SUBLANES)
    return sbuf[r, SUBLANES * q + r0:SUBLANES * q + r0 + rows, :]


def _layer_norm_stats(u1):
    mu = jnp.mean(u1, axis=-1, keepdims=True)
    cen = u1 - mu
    rstd = lax.rsqrt(jnp.mean(cen * cen, axis=-1, keepdims=True) + EPS)
    return cen * rstd, rstd


def _conv_fwd(a, cw, cb, cg, cbeta, rider=None):
    s = a.shape[0]
    tm = _tile(s, 256, CONV_CHUNK)
    hb = tm // CONV_HALO

    def body(a_ref, ap_ref, cw_ref, cb_ref, cg_ref, cbeta_ref, u1_ref, u_ref, ubuf):
        i = pl.program_id(0)
        ubuf[0, 0:CONV_HALO, :] = jnp.where(i > 0, _glu(ap_ref[...]), 0.0)
        for r0, n in _chunks(tm):
            ubuf[0, CONV_HALO + r0:CONV_HALO + r0 + n, :] = _glu(a_ref[r0:r0 + n, :])
        _fill_shifted(ubuf, tm + CONV_HALO - SUBLANES)
        for r0, n in _chunks(tm):
            acc = jnp.broadcast_to(cb_ref[...], (n, CONV_CH))
            for k in range(CONV_K):
                acc = acc + cw_ref[k:k + 1, :] * _tap(ubuf, 2 + k, r0, n)
            u1_ref[r0:r0 + n, :] = acc
            xh, _ = _layer_norm_stats(acc)
            z = xh * cg_ref[...] + cbeta_ref[...]
            u_ref[r0:r0 + n, :] = (z * _sigmoid(z)).astype(BF16)

    full = lambda t: pl.BlockSpec(t.shape, lambda i: (0, 0))
    outs, carried = _call(
        body, rider, name="conv_fwd", grid=(s // tm,),
        in_specs=[pl.BlockSpec((tm, 2 * CONV_CH), lambda i: (i, 0)),
                  pl.BlockSpec((CONV_HALO, 2 * CONV_CH), lambda i: (jnp.maximum(i * hb - 1, 0), 0)),
                  full(cw), full(cb), full(cg), full(cbeta)],
        out_specs=[pl.BlockSpec((tm, CONV_CH), lambda i: (i, 0))] * 2,
        out_shape=[jax.ShapeDtypeStruct((s, CONV_CH), F32), jax.ShapeDtypeStruct((s, CONV_CH), BF16)],
        scratch_shapes=[pltpu.VMEM((SUBLANES, CONV_HALO + tm, CONV_CH), F32)],
        compiler_params=_params(("arbitrary",), VMEM_BIG),
        args=(a, a, cw, cb, cg, cbeta))
    return (*outs, carried)


def _conv_bwd(a, u1, d_cat, cw, cg, cbeta, rider=None):
    s = a.shape[0]
    tm = _tile(s, 256, CONV_CHUNK)
    hb = tm // CONV_HALO
    last_halo = s // CONV_HALO - 1
    nt = s // tm
    te = tm + CONV_HALO

    def body(a_ref, ap_ref, u1_ref, u1n_ref, du_ref, dun_ref, cw_ref, cg_ref, cbeta_ref,
             da_ref, gw_ref, gb_ref, gg_ref, gbeta_ref, ubuf, dbuf):
        i = pl.program_id(0)

        @pl.when(i == 0)
        def _():
            gw_ref[...] = jnp.zeros_like(gw_ref)
            gb_ref[...] = jnp.zeros_like(gb_ref)
            gg_ref[...] = jnp.zeros_like(gg_ref)
            gbeta_ref[...] = jnp.zeros_like(gbeta_ref)

        def du1_of(u1, du):
            xh, rstd = _layer_norm_stats(u1)
            z = xh * cg_ref[...] + cbeta_ref[...]
            sz = _sigmoid(z)
            dz = du * (sz * (1.0 + z * (1.0 - sz)))
            dxh = dz * cg_ref[...]
            du1 = rstd * (dxh - jnp.mean(dxh, axis=-1, keepdims=True)
                          - xh * jnp.mean(dxh * xh, axis=-1, keepdims=True))
            return du1, dz, xh

        gg, gbeta, gb = [jnp.zeros((SUBLANES, CONV_CH), F32) for _ in range(3)]
        for r0, n in _chunks(tm):
            du1, dz, xh = du1_of(u1_ref[r0:r0 + n, :], du_ref[r0:r0 + n, :])
            gg, gbeta, gb = gg + _fold8(dz * xh), gbeta + _fold8(dz), gb + _fold8(du1)
            dbuf[0, r0:r0 + n, :] = du1
        gg_ref[...] += jnp.sum(gg, axis=0, keepdims=True)
        gbeta_ref[...] += jnp.sum(gbeta, axis=0, keepdims=True)
        gb_ref[...] += jnp.sum(gb, axis=0, keepdims=True)
        du1n, _, _ = du1_of(u1n_ref[...], jnp.where(i < nt - 1, dun_ref[...], 0.0))
        dbuf[0, tm:, :] = du1n
        _fill_shifted(dbuf, tm + CONV_HALO - SUBLANES)

        ubuf[0, 0:CONV_HALO, :] = jnp.where(i > 0, _glu(ap_ref[...]), 0.0)
        for r0, n in _chunks(tm):
            ubuf[0, CONV_HALO + r0:CONV_HALO + r0 + n, :] = _glu(a_ref[r0:r0 + n, :])
        _fill_shifted(ubuf, tm + CONV_HALO - SUBLANES)

        for k in range(CONV_K):
            part = jnp.zeros((SUBLANES, CONV_CH), F32)
            for r0, n in _chunks(tm):
                part = part + _fold8(dbuf[0, r0:r0 + n, :] * _tap(ubuf, 2 + k, r0, n))
            gw_ref[k:k + 1, :] += jnp.sum(part, axis=0, keepdims=True)

        for r0, n in _chunks(tm):
            acc = jnp.zeros((n, CONV_CH), F32)
            for k in range(CONV_K):
                acc = acc + cw_ref[k:k + 1, :] * _tap(dbuf, CONV_K - 1 - k, r0, n)
            avc = a_ref[r0:r0 + n, :CONV_CH]
            sgc = _sigmoid(a_ref[r0:r0 + n, CONV_CH:])
            da_ref[r0:r0 + n, :CONV_CH] = (acc * sgc).astype(BF16)
            da_ref[r0:r0 + n, CONV_CH:] = (acc * avc * sgc * (1.0 - sgc)).astype(BF16)

    full = lambda t: pl.BlockSpec(t.shape, lambda i: (0, 0))
    vec = pl.BlockSpec((1, CONV_CH), lambda i: (0, 0))
    nxt = lambda i: (jnp.minimum((i + 1) * hb, last_halo), 0)
    outs, carried = _call(
        body, rider, name="conv_bwd", grid=(nt,),
        in_specs=[pl.BlockSpec((tm, 2 * CONV_CH), lambda i: (i, 0)),
                  pl.BlockSpec((CONV_HALO, 2 * CONV_CH), lambda i: (jnp.maximum(i * hb - 1, 0), 0)),
                  pl.BlockSpec((tm, CONV_CH), lambda i: (i, 0)), pl.BlockSpec((CONV_HALO, CONV_CH), nxt),
                  pl.BlockSpec((tm, CONV_CH), lambda i: (i, 0)), pl.BlockSpec((CONV_HALO, CONV_CH), nxt),
                  full(cw), full(cg), full(cbeta)],
        out_specs=[pl.BlockSpec((tm, 2 * CONV_CH), lambda i: (i, 0)),
                   pl.BlockSpec((CONV_HALO, CONV_CH), lambda i: (0, 0)), vec, vec, vec],
        out_shape=[jax.ShapeDtypeStruct((s, 2 * CONV_CH), BF16), jax.ShapeDtypeStruct((CONV_HALO, CONV_CH), F32),
                   jax.ShapeDtypeStruct((1, CONV_CH), F32), jax.ShapeDtypeStruct((1, CONV_CH), F32),
                   jax.ShapeDtypeStruct((1, CONV_CH), F32)],
        scratch_shapes=[pltpu.VMEM((SUBLANES, CONV_HALO + tm, CONV_CH), F32), pltpu.VMEM((SUBLANES, te, CONV_CH), F32)],
        compiler_params=_params(("arbitrary",), VMEM_BIG),
        args=(a, a, u1, u1, d_cat, d_cat, cw, cg, cbeta))
    return (*outs, carried)


def _ff_tiles(s, f):
    return _tile(s, 256, 8), _tile(f, 1408, 128)


FF_CHUNK, FF_LANES = 64, 128
MXU_WIDTH = 256


def _row_chunks(n):
    return [(r0, min(FF_CHUNK, n - r0)) for r0 in range(0, n, FF_CHUNK)]


def _fold8(v):
    acc = v[0:8]
    for r in range(8, v.shape[0], 8):
        acc = acc + v[r:r + 8]
    return acc


def _ff_conv(pre_buf, w_ref, b_ref, half, r0, rows, cols):
    acc = b_ref[half:half + 1, cols]
    for k in range(FF_K):
        off = FF_HALO - (FF_K - 1) + k + r0
        acc = acc + w_ref[k, half:half + 1, cols] * pre_buf[half, off:off + rows, cols]
    return acc


def _ff_act(up_g, up_v, fw, fb):
    s, f = up_g.shape
    tm, tc = _ff_tiles(s, f)
    hb = tm // FF_HALO

    def body(g_ref, gp_ref, v_ref, vp_ref, w_ref, b_ref, act_ref, buf):
        i = pl.program_id(1)
        for half, (m_ref, p_ref) in enumerate(((g_ref, gp_ref), (v_ref, vp_ref))):
            buf[half, 0:FF_HALO, :] = jnp.where(i > 0, p_ref[...], 0.0)
            buf[half, FF_HALO:, :] = m_ref[...]
        for c0 in range(0, tc, FF_LANES):
            cols = slice(c0, c0 + FF_LANES)
            for r0, rows in _row_chunks(tm):
                gate = _ff_conv(buf, w_ref, b_ref, 0, r0, rows, cols)
                val = _ff_conv(buf, w_ref, b_ref, 1, r0, rows, cols)
                act_ref[r0:r0 + rows, cols] = (gate * _sigmoid(gate) * val).astype(BF16)

    main = pl.BlockSpec((tm, tc), lambda j, i: (i, j))
    prev = pl.BlockSpec((FF_HALO, tc), lambda j, i: (jnp.maximum(i * hb - 1, 0), j))
    return pl.pallas_call(
        body, name="ff_act", grid=(f // tc, s // tm),
        in_specs=[main, prev, main, prev, pl.BlockSpec((FF_K, 2, tc), lambda j, i: (0, 0, j)),
                  pl.BlockSpec((2, tc), lambda j, i: (0, j))],
        out_specs=main,
        out_shape=jax.ShapeDtypeStruct((s, f), BF16),
        scratch_shapes=[pltpu.VMEM((2, FF_HALO + tm, tc), F32)],
        compiler_params=_params(("parallel", "parallel"), VMEM_BIG),
    )(up_g, up_g, up_v, up_v, fw, fb)


def _up_fwd_act(h2, w_up_t, fw, fb, f):
    s, d = h2.shape
    tm, tn = _tile(s, 512, 8), _tile(f, 1408, 128)
    nc, nt = f // tn, s // tm

    def body(h_ref, wg_ref, wv_ref, w_ref, b_ref, g_ref, v_ref, act_ref, xbuf):
        @pl.when(pl.program_id(1) == 0)
        def _():
            xbuf[...] = jnp.zeros_like(xbuf)

        strips = [(slice(c0, c0 + FF_LANES), r0, rows) for c0 in range(0, tn, FF_LANES) for r0, rows in _row_chunks(tm)]
        pieces = [(out, wt, slice(c0, min(c0 + MXU_WIDTH, tn))) for out, wt in ((g_ref, wg_ref), (v_ref, wv_ref))
                  for c0 in range(0, tn, MXU_WIDTH)]
        per_piece = -(-len(strips) // len(pieces))
        hv = h_ref[...]
        for n, (out, wt, wcols) in enumerate(pieces):
            out[:, wcols] = _dot(hv, wt[wcols, :], trans_b=True)
            for cols, r0, rows in strips[n * per_piece:(n + 1) * per_piece]:
                gate = _ff_conv(xbuf, w_ref, b_ref, 0, r0, rows, cols)
                val = _ff_conv(xbuf, w_ref, b_ref, 1, r0, rows, cols)
                act_ref[r0:r0 + rows, cols] = (gate * _sigmoid(gate) * val).astype(BF16)
        for half, ref in enumerate((g_ref, v_ref)):
            xbuf[half, 0:FF_HALO, :] = xbuf[half, tm:tm + FF_HALO, :]
            xbuf[half, FF_HALO:, :] = ref[...]

    cur = lambda j, i: (jnp.minimum(i, nt - 1), j)
    return pl.pallas_call(
        body, name="up_fwd_act", grid=(nc, nt + 1),
        in_specs=[pl.BlockSpec((tm, d), lambda j, i: (jnp.minimum(i, nt - 1), 0)),
                  pl.BlockSpec((tn, d), lambda j, i: (j, 0)), pl.BlockSpec((tn, d), lambda j, i: (j + nc, 0)),
                  pl.BlockSpec((FF_K, 2, tn), lambda j, i: (0, 0, j)), pl.BlockSpec((2, tn), lambda j, i: (0, j))],
        out_specs=[pl.BlockSpec((tm, tn), cur), pl.BlockSpec((tm, tn), cur),
                   pl.BlockSpec((tm, tn), lambda j, i: (jnp.maximum(i - 1, 0), j))],
        out_shape=[jax.ShapeDtypeStruct((s, f), F32), jax.ShapeDtypeStruct((s, f), F32), jax.ShapeDtypeStruct((s, f), BF16)],
        scratch_shapes=[pltpu.VMEM((2, FF_HALO + tm, tn), F32)],
        compiler_params=_params(("arbitrary", "arbitrary"), VMEM_BIG),
    )(h2, w_up_t, w_up_t, fw, fb)


def _ff_bwd(up_g, up_v, d_act, fw, fb, rider=None):
    s, f = up_g.shape
    tm, tc = _ff_tiles(s, f)
    hb = tm // FF_HALO
    nt = s // tm
    last_halo = s // FF_HALO - 1
    te = tm + FF_HALO

    def body(g_ref, gp_ref, gn_ref, v_ref, vp_ref, vn_ref, da_ref, dan_ref, w_ref, b_ref,
             dg_ref, dv_ref, gw_ref, gb_ref, buf, dbuf, dabuf):
        i = pl.program_id(1)

        @pl.when(i == 0)
        def _():
            gw_ref[...] = jnp.zeros_like(gw_ref)
            gb_ref[...] = jnp.zeros_like(gb_ref)

        for half, (m_ref, p_ref, n_ref) in enumerate(((g_ref, gp_ref, gn_ref), (v_ref, vp_ref, vn_ref))):
            buf[half, 0:FF_HALO, :] = jnp.where(i > 0, p_ref[...], 0.0)
            buf[half, FF_HALO:FF_HALO + tm, :] = m_ref[...]
            buf[half, FF_HALO + tm:, :] = n_ref[...]
        dabuf[0:tm, :] = da_ref[...]
        dabuf[tm:, :] = jnp.where(i < nt - 1, dan_ref[...], 0.0)

        for c0 in range(0, tc, FF_LANES):
            cols = slice(c0, c0 + FF_LANES)
            gb = [jnp.zeros((8, FF_LANES), F32) for _ in range(2)]
            gw = [[jnp.zeros((8, FF_LANES), F32) for _ in range(FF_K)] for _ in range(2)]
            for r0, rows in _row_chunks(te):
                taps = [[buf[half, FF_HALO - (FF_K - 1) + k + r0:FF_HALO - (FF_K - 1) + k + r0 + rows, cols]
                         for k in range(FF_K)] for half in range(2)]
                gate, val = [b_ref[half:half + 1, cols] + sum(w_ref[k, half:half + 1, cols] * taps[half][k]
                                                              for k in range(FF_K)) for half in range(2)]
                da = dabuf[r0:r0 + rows, cols]
                sg = _sigmoid(gate)
                dup = [da * val * (sg * (1.0 + gate * (1.0 - sg))), da * (gate * sg)]
                for half in range(2):
                    dbuf[half, r0:r0 + rows, cols] = dup[half]
                    if r0 < tm:
                        gb[half] = gb[half] + _fold8(dup[half])
                        for k in range(FF_K):
                            gw[half][k] = gw[half][k] + _fold8(dup[half] * taps[half][k])
            for half, out_ref in enumerate((dg_ref, dv_ref)):
                gb_ref[half:half + 1, cols] += jnp.sum(gb[half], axis=0, keepdims=True)
                for k in range(FF_K):
                    gw_ref[k, half:half + 1, cols] += jnp.sum(gw[half][k], axis=0, keepdims=True)
                for r0, rows in _row_chunks(tm):
                    acc = jnp.zeros((rows, FF_LANES), F32)
                    for k in range(FF_K):
                        lo = r0 + FF_K - 1 - k
                        acc = acc + w_ref[k, half:half + 1, cols] * dbuf[half, lo:lo + rows, cols]
                    out_ref[r0:r0 + rows, cols] = acc.astype(BF16)

    main = pl.BlockSpec((tm, tc), lambda j, i: (i, j))
    prev = pl.BlockSpec((FF_HALO, tc), lambda j, i: (jnp.maximum(i * hb - 1, 0), j))
    nxt = pl.BlockSpec((FF_HALO, tc), lambda j, i: (jnp.minimum((i + 1) * hb, last_halo), j))
    (dg, dv, gw, gb), carried = _call(
        body, rider, name="ff_bwd", grid=(f // tc, nt),
        in_specs=[main, prev, nxt, main, prev, nxt, main, nxt,
                  pl.BlockSpec((FF_K, 2, tc), lambda j, i: (0, 0, j)), pl.BlockSpec((2, tc), lambda j, i: (0, j))],
        out_specs=[main, main, pl.BlockSpec((FF_K, 2, tc), lambda j, i: (0, 0, j)),
                   pl.BlockSpec((2, tc), lambda j, i: (0, j))],
        out_shape=[jax.ShapeDtypeStruct((s, f), BF16), jax.ShapeDtypeStruct((s, f), BF16),
                   jax.ShapeDtypeStruct((FF_K, 2, f), F32), jax.ShapeDtypeStruct((2, f), F32)],
        scratch_shapes=[pltpu.VMEM((2, FF_HALO + te, tc), F32), pltpu.VMEM((2, te, tc), F32), pltpu.VMEM((te, tc), F32)],
        compiler_params=_params(("arbitrary", "arbitrary"), VMEM_BIG),
        args=(up_g, up_g, up_g, up_v, up_v, up_v, d_act, d_act, fw, fb))
    return dg, dv, gw, gb, carried


ATT_TILE = Q_BLOCK * max(PATTERN_DILATIONS)


def _stream_rows(start, d, n=Q_BLOCK):
    return pl.ds(start, n) if d == 1 else pl.ds(start, n, stride=d)


def _band_geometry():
    qi = lax.broadcasted_iota(jnp.int32, (Q_BLOCK, 2 * Q_BLOCK), 0)
    ki = lax.broadcasted_iota(jnp.int32, (Q_BLOCK, 2 * Q_BLOCK), 1)
    delta = qi + Q_BLOCK - ki
    return (delta >= 0) & (delta <= Q_BLOCK), delta.astype(F32), ki


def _stream_blocks(d):
    out = []
    for r in range(d):
        for b in range(ATT_TILE // (Q_BLOCK * d)):
            start = b * Q_BLOCK * d + r
            out.append((start, start - Q_BLOCK * d if b > 0 else None))
    return out


def _attn_forward(qkv, slopes, rider=None):
    s = qkv.shape[0]
    nt = s // ATT_TILE
    nhp = ATTN_W // Q_BLOCK

    def body(sl_ref, q_ref, k_ref, kp_ref, v_ref, vp_ref, o_ref, l_ref):
        hp, i = pl.program_id(0), pl.program_id(1)
        head0 = lax.broadcasted_iota(jnp.int32, (Q_BLOCK, Q_BLOCK), 1) < HEAD_DIM
        head0_k = lax.broadcasted_iota(jnp.int32, (2 * Q_BLOCK, Q_BLOCK), 1) < HEAD_DIM
        valid, dist, ki = _band_geometry()
        first_key = jnp.where(i == 0, Q_BLOCK, 0)
        for d in PATTERN_DILATIONS:
            biases = [jnp.where(valid, dist * (-sl_ref[2 * hp + hh] * d), NEG) for hh in range(2)]
            for start, prev in _stream_blocks(d):
                rows = _stream_rows(start, d)
                if prev is None:
                    prow = _stream_rows(ATT_TILE - Q_BLOCK * d + start, d)
                    kp, vp = kp_ref[prow, :], vp_ref[prow, :]
                else:
                    prow = _stream_rows(prev, d)
                    kp, vp = k_ref[prow, :], v_ref[prow, :]
                qv = q_ref[rows, :].astype(BF16)
                k2 = jnp.concatenate([kp, k_ref[rows, :]], axis=0).astype(BF16)
                v2 = jnp.concatenate([vp, v_ref[rows, :]], axis=0).astype(BF16)
                res, mxs = [], []
                for hh in range(2):
                    mine = head0 if hh == 0 else jnp.logical_not(head0)
                    mine_k = head0_k if hh == 0 else jnp.logical_not(head0_k)
                    sc = _dot(jnp.where(mine, qv, jnp.zeros_like(qv)), k2, trans_b=True) + biases[hh]
                    if prev is None:
                        sc = jnp.where(ki < first_key, NEG, sc)
                    mx = jnp.max(sc, axis=1, keepdims=True)
                    p = jnp.exp(sc - mx).astype(BF16)
                    res.append(_dot(p, jnp.where(mine_k, v2, jnp.ones_like(v2))))
                    mxs.append(mx)
                num = jnp.where(head0, res[0], res[1])
                den = pltpu.roll(jnp.where(head0, res[1], res[0]), HEAD_DIM, 1)
                o_new = num / den
                l_new = jnp.where(head0, mxs[0], mxs[1]) + jnp.log(den)
                if d != PATTERN_DILATIONS[0]:
                    oa, la = o_ref[rows, :], l_ref[rows, :]
                    mm = jnp.maximum(la, l_new)
                    wa, wn = jnp.exp(la - mm), jnp.exp(l_new - mm)
                    o_new = (wa * oa + wn * o_new) / (wa + wn)
                    l_new = mm + jnp.log(wa + wn)
                o_ref[rows, :] = o_new
                l_ref[rows, :] = l_new

    def col(off):
        return pl.BlockSpec((ATT_TILE, Q_BLOCK), lambda hp, i: (i, off + hp))

    def col_prev(off):
        return pl.BlockSpec((ATT_TILE, Q_BLOCK), lambda hp, i: (jnp.maximum(i - 1, 0), off + hp))

    outs, carried = _call(
        body, rider, name="attn_fwd", grid=(nhp, nt),
        in_specs=[pl.BlockSpec(memory_space=pltpu.SMEM), col(0), col(nhp), col_prev(nhp), col(2 * nhp), col_prev(2 * nhp)],
        out_specs=[col(0), col(0)],
        out_shape=[jax.ShapeDtypeStruct((s, ATTN_W), F32)] * 2,
        compiler_params=_params(("arbitrary", "arbitrary"), VMEM_BIG),
        args=(slopes, qkv, qkv, qkv, qkv, qkv))
    return (*outs, carried)


def _attn_backward(qkv, d_cat, o, lse, slopes, ones2, rider=None):
    s = qkv.shape[0]
    nt = s // ATT_TILE
    nhp = ATTN_W // Q_BLOCK
    tt = ATT_TILE

    def body(sl_ref, q_ref, k_ref, kp_ref, v_ref, vp_ref, do_ref, o_ref, l_ref, ones_ref,
             dq_ref, dk_ref, dv_ref, dkacc, dvacc, dd):
        hp, step = pl.program_id(0), pl.program_id(1)
        tile = nt - 1 - step
        head0 = lax.broadcasted_iota(jnp.int32, (Q_BLOCK, Q_BLOCK), 1) < HEAD_DIM
        valid, dist, ki = _band_geometry()
        first_key = jnp.where(tile == 0, Q_BLOCK, 0)

        @pl.when(step == 0)
        def _():
            dkacc[tt:, :] = jnp.zeros((tt, Q_BLOCK), F32)
            dvacc[tt:, :] = jnp.zeros((tt, Q_BLOCK), F32)

        @pl.when(step > 0)
        def _():
            dkacc[tt:, :] = dkacc[:tt, :]
            dvacc[tt:, :] = dvacc[:tt, :]

        dkacc[:tt, :] = jnp.zeros((tt, Q_BLOCK), F32)
        dvacc[:tt, :] = jnp.zeros((tt, Q_BLOCK), F32)
        dd[...] = _head_sum(do_ref[...] * o_ref[...], ones_ref)

        ki2 = jnp.concatenate([ki, ki], axis=0)
        for d in PATTERN_DILATIONS:
            bias2 = jnp.concatenate([jnp.where(valid, dist * (-sl_ref[2 * hp + hh] * d), NEG) for hh in range(2)], axis=0)
            for start, prev in _stream_blocks(d):
                rows = _stream_rows(start, d)
                if prev is None:
                    prow = _stream_rows(ATT_TILE - Q_BLOCK * d + start, d)
                    kp, vp = kp_ref[prow, :], vp_ref[prow, :]
                else:
                    prow = _stream_rows(prev, d)
                    kp, vp = k_ref[prow, :], v_ref[prow, :]
                acc_rows = _stream_rows(tt + start - Q_BLOCK * d, d, 2 * Q_BLOCK)
                qv = q_ref[rows, :].astype(BF16)
                dov = do_ref[rows, :].astype(BF16)
                lv, ddv = l_ref[rows, :], dd[rows, :]
                lsw, dsw = pltpu.roll(lv, HEAD_DIM, 1), pltpu.roll(ddv, HEAD_DIM, 1)
                k2 = jnp.concatenate([kp, k_ref[rows, :]], axis=0).astype(BF16)
                v2 = jnp.concatenate([vp, v_ref[rows, :]], axis=0).astype(BF16)
                zq = jnp.zeros_like(qv)
                q2 = jnp.concatenate([jnp.where(head0, qv, zq), jnp.where(head0, zq, qv)], axis=0)
                do2 = jnp.concatenate([jnp.where(head0, dov, zq), jnp.where(head0, zq, dov)], axis=0)
                lh = jnp.concatenate([jnp.where(head0, lv, lsw), jnp.where(head0, lsw, lv)], axis=0)
                dh = jnp.concatenate([jnp.where(head0, ddv, dsw), jnp.where(head0, dsw, ddv)], axis=0)
                sc = _dot(q2, k2, trans_b=True) + bias2
                if prev is None:
                    sc = jnp.where(ki2 < first_key, NEG, sc)
                p = jnp.exp(sc - jnp.concatenate([lh, lh], axis=1))
                ds = p * (_dot(do2, v2, trans_b=True) - jnp.concatenate([dh, dh], axis=1))
                dq2 = _dot(ds.astype(BF16), k2)
                dk2 = _dot(ds.T.astype(BF16), q2)
                dv2 = _dot(p.T.astype(BF16), do2)
                dq = jnp.where(head0, dq2[:Q_BLOCK], dq2[Q_BLOCK:])
                if d == PATTERN_DILATIONS[0]:
                    dq_ref[rows, :] = dq
                else:
                    dq_ref[rows, :] += dq
                dkacc[acc_rows, :] += dk2
                dvacc[acc_rows, :] += dv2
        dk_ref[...] = dkacc[tt:, :]
        dv_ref[...] = dvacc[tt:, :]

    def col(off):
        return pl.BlockSpec((tt, Q_BLOCK), lambda hp, st: (nt - 1 - st, off + hp))

    def col_prev(off):
        return pl.BlockSpec((tt, Q_BLOCK), lambda hp, st: (jnp.maximum(nt - 2 - st, 0), off + hp))

    outs, carried = _call(
        body, rider, name="attn_bwd", grid=(nhp, nt),
        in_specs=[pl.BlockSpec(memory_space=pltpu.SMEM), col(0), col(nhp), col_prev(nhp), col(2 * nhp), col_prev(2 * nhp),
                  col(nhp), col(0), col(0), pl.BlockSpec((Q_BLOCK, Q_BLOCK), lambda hp, st: (0, 0))],
        out_specs=[col(0)] * 3,
        out_shape=[jax.ShapeDtypeStruct((s, ATTN_W), F32)] * 3,
        scratch_shapes=[pltpu.VMEM((2 * tt, Q_BLOCK), F32)] * 2 + [pltpu.VMEM((tt, Q_BLOCK), F32)],
        compiler_params=_params(("arbitrary", "arbitrary"), VMEM_BIG),
        args=(slopes, qkv, qkv, qkv, qkv, qkv, d_cat, o, lse, ones2))
    return (*outs, carried)


def _attn_geometry(s, d):
    length = s // d
    tq = _tile(length, 1024, Q_BLOCK)
    return length, d * ATTN_W, tq, length // tq, tq // Q_BLOCK


def _band(slope_ref, head_pair, d):
    qi = lax.broadcasted_iota(jnp.int32, (Q_BLOCK, 2 * Q_BLOCK), 0)
    ki = lax.broadcasted_iota(jnp.int32, (Q_BLOCK, 2 * Q_BLOCK), 1)
    delta = qi + Q_BLOCK - ki
    valid = (delta >= 0) & (delta <= Q_BLOCK)
    dist = delta.astype(F32) * float(d)
    return [jnp.where(valid, dist * (-slope_ref[2 * head_pair + hh]), NEG) for hh in range(2)], ki


def _attn_fwd(qn, kn, v, slopes, d, prev):
    s = qn.shape[0]
    length, width, tq, nt, nqb = _attn_geometry(s, d)
    view = lambda t: t.reshape(length, width)
    merge = prev is not None

    def body(*refs):
        sl_ref, q_ref, k_ref, kh_ref, v_ref, vh_ref = refs[:6]
        if merge:
            oa_ref, la_ref = refs[6:8]
        o_ref, l_ref, kbuf, vbuf = refs[-4:]
        cb, i = pl.program_id(0), pl.program_id(1)
        biases, ki = _band(sl_ref, cb % 4, d)
        head0 = lax.broadcasted_iota(jnp.int32, (Q_BLOCK, Q_BLOCK), 1) < HEAD_DIM
        kbuf[0:Q_BLOCK, :] = kh_ref[...]
        kbuf[Q_BLOCK:, :] = k_ref[...]
        vbuf[0:Q_BLOCK, :] = vh_ref[...]
        vbuf[Q_BLOCK:, :] = v_ref[...]

        def q_block(qb, carry):
            r0 = pl.multiple_of(qb * Q_BLOCK, Q_BLOCK)
            rows = pl.ds(r0, Q_BLOCK)
            qv = q_ref[rows, :]
            k2, v2 = kbuf[pl.ds(r0, 2 * Q_BLOCK), :], vbuf[pl.ds(r0, 2 * Q_BLOCK), :]
            first_key = jnp.where((i == 0) & (qb == 0), Q_BLOCK, 0)
            o_h, l_h = [], []
            for hh in range(2):
                mine = head0 if hh == 0 else jnp.logical_not(head0)
                sc = _dot(jnp.where(mine, qv, jnp.zeros_like(qv)), k2, trans_b=True) + biases[hh]
                sc = jnp.where(ki < first_key, NEG, sc)
                mx = jnp.max(sc, axis=1, keepdims=True)
                p = jnp.exp(sc - mx)
                den = jnp.sum(p, axis=1, keepdims=True)
                o_h.append(_dot(p.astype(BF16), v2) * (1.0 / den))
                l_h.append(jnp.broadcast_to(mx + jnp.log(den), (Q_BLOCK, Q_BLOCK)))
            o_new = jnp.where(head0, o_h[0], o_h[1])
            l_new = jnp.where(head0, l_h[0], l_h[1])
            if merge:
                oa, la = oa_ref[rows, :], la_ref[rows, :]
                mm = jnp.maximum(la, l_new)
                wa, wn = jnp.exp(la - mm), jnp.exp(l_new - mm)
                o_new = (wa * oa + wn * o_new) / (wa + wn)
                l_new = mm + jnp.log(wa + wn)
            o_ref[rows, :] = o_new
            l_ref[rows, :] = l_new
            return carry

        lax.fori_loop(0, nqb, q_block, 0)

    main = pl.BlockSpec((tq, Q_BLOCK), lambda cb, i: (i, cb))
    halo = pl.BlockSpec((Q_BLOCK, Q_BLOCK), lambda cb, i: (jnp.maximum(i * nqb - 1, 0), cb))
    ins = [slopes, view(qn), view(kn), view(kn), view(v), view(v)] + ([view(prev[0]), view(prev[1])] if merge else [])
    o, lse = pl.pallas_call(
        body, name=f"attn_fwd_d{d}", grid=(width // Q_BLOCK, nt),
        in_specs=[pl.BlockSpec(memory_space=pltpu.SMEM), main, main, halo, main, halo] + ([main, main] if merge else []),
        out_specs=[main, main],
        out_shape=[jax.ShapeDtypeStruct((length, width), F32)] * 2,
        scratch_shapes=[pltpu.VMEM((Q_BLOCK + tq, Q_BLOCK), BF16)] * 2,
        compiler_params=_params(("parallel", "parallel"), VMEM_BIG),
    )(*ins)
    return o.reshape(s, ATTN_W), lse.reshape(s, ATTN_W)


def _attn_bwd_prep(d_cat, o, ones):
    s = o.shape[0]
    tm = _tile(s, 512, 8)

    def body(do_ref, o_ref, ones_ref, dob_ref, dd_ref):
        do = do_ref[...]
        dob_ref[...] = do.astype(BF16)
        dd_ref[...] = _head_sum(do * o_ref[...], ones_ref)

    row = pl.BlockSpec((tm, ATTN_W), lambda i: (i, 0))
    return pl.pallas_call(
        body, name="attn_bwd_prep", grid=(s // tm,),
        in_specs=[pl.BlockSpec((tm, ATTN_W), lambda i: (i, 1)), row, pl.BlockSpec(ones.shape, lambda i: (0, 0))],
        out_specs=[row, row],
        out_shape=[jax.ShapeDtypeStruct((s, ATTN_W), BF16), jax.ShapeDtypeStruct((s, ATTN_W), F32)],
        compiler_params=_params(("parallel",), VMEM_BIG),
    )(d_cat, o, ones)


def _attn_bwd(qn, kn, v, dob, lse, dd, slopes, d, acc):
    s = qn.shape[0]
    length, width, tq, nt, nqb = _attn_geometry(s, d)
    view = lambda t: t.reshape(length, width)
    add = acc is not None

    def body(*refs):
        sl_ref, q_ref, k_ref, kh_ref, v_ref, vh_ref, do_ref, l_ref, dd_ref = refs[:9]
        if add:
            dqa_ref, dka_ref, dva_ref = refs[9:12]
        dq_ref, dk_ref, dv_ref, kbuf, vbuf, dkacc, dvacc, kcarry, vcarry = refs[-9:]
        cb, step = pl.program_id(0), pl.program_id(1)
        tile = nt - 1 - step
        biases, ki = _band(sl_ref, cb % 4, d)
        head0 = lax.broadcasted_iota(jnp.int32, (Q_BLOCK, Q_BLOCK), 1) < HEAD_DIM
        kbuf[0:Q_BLOCK, :] = kh_ref[...]
        kbuf[Q_BLOCK:, :] = k_ref[...]
        vbuf[0:Q_BLOCK, :] = vh_ref[...]
        vbuf[Q_BLOCK:, :] = v_ref[...]
        dkacc[...] = jnp.zeros_like(dkacc)
        dvacc[...] = jnp.zeros_like(dvacc)

        @pl.when(step > 0)
        def _():
            dkacc[tq:, :] = kcarry[...]
            dvacc[tq:, :] = vcarry[...]

        def q_block(qb, carry):
            r0 = pl.multiple_of(qb * Q_BLOCK, Q_BLOCK)
            rows, rows2 = pl.ds(r0, Q_BLOCK), pl.ds(r0, 2 * Q_BLOCK)
            qv, dov = q_ref[rows, :], do_ref[rows, :]
            lv, ddv = l_ref[rows, :], dd_ref[rows, :]
            k2, v2 = kbuf[rows2, :], vbuf[rows2, :]
            first_key = jnp.where((tile == 0) & (qb == 0), Q_BLOCK, 0)
            dq_h = []
            dk2 = jnp.zeros((2 * Q_BLOCK, Q_BLOCK), F32)
            dv2 = jnp.zeros((2 * Q_BLOCK, Q_BLOCK), F32)
            for hh in range(2):
                mine = head0 if hh == 0 else jnp.logical_not(head0)
                qm = jnp.where(mine, qv, jnp.zeros_like(qv))
                dom = jnp.where(mine, dov, jnp.zeros_like(dov))
                lcol = jnp.max(jnp.where(mine, lv, -jnp.inf), axis=1, keepdims=True)
                dcol = jnp.max(jnp.where(mine, ddv, -jnp.inf), axis=1, keepdims=True)
                sc = _dot(qm, k2, trans_b=True) + biases[hh]
                sc = jnp.where(ki < first_key, NEG, sc)
                p = jnp.exp(sc - lcol)
                ds = p * (_dot(dom, v2, trans_b=True) - dcol)
                dq_h.append(_dot(ds.astype(BF16), k2))
                dk2 = dk2 + _dot(ds.T.astype(BF16), qm)
                dv2 = dv2 + _dot(p.T.astype(BF16), dom)
            dq = jnp.where(head0, dq_h[0], dq_h[1])
            if add:
                dq = dq + dqa_ref[rows, :]
            dq_ref[rows, :] = dq
            dkacc[rows2, :] += dk2
            dvacc[rows2, :] += dv2
            return carry

        lax.fori_loop(0, nqb, q_block, 0)
        dk, dv = dkacc[Q_BLOCK:, :], dvacc[Q_BLOCK:, :]
        if add:
            dk, dv = dk + dka_ref[...], dv + dva_ref[...]
        dk_ref[...] = dk
        dv_ref[...] = dv
        kcarry[...] = dkacc[0:Q_BLOCK, :]
        vcarry[...] = dvacc[0:Q_BLOCK, :]

    main = pl.BlockSpec((tq, Q_BLOCK), lambda cb, st: (nt - 1 - st, cb))
    halo = pl.BlockSpec((Q_BLOCK, Q_BLOCK), lambda cb, st: (jnp.maximum((nt - 1 - st) * nqb - 1, 0), cb))
    ins = [slopes, view(qn), view(kn), view(kn), view(v), view(v), view(dob), view(lse), view(dd)]
    ins += [view(t) for t in acc] if add else []
    outs = pl.pallas_call(
        body, name=f"attn_bwd_d{d}", grid=(width // Q_BLOCK, nt),
        in_specs=[pl.BlockSpec(memory_space=pltpu.SMEM), main, main, halo, main, halo, main, main, main]
        + ([main] * 3 if add else []),
        out_specs=[main] * 3,
        out_shape=[jax.ShapeDtypeStruct((length, width), F32)] * 3,
        scratch_shapes=[pltpu.VMEM((Q_BLOCK + tq, Q_BLOCK), BF16)] * 2 + [pltpu.VMEM((Q_BLOCK + tq, Q_BLOCK), F32)] * 2
        + [pltpu.VMEM((Q_BLOCK, Q_BLOCK), F32)] * 2,
        compiler_params=_params(("parallel", "arbitrary"), VMEM_BIG),
    )(*ins)
    return tuple(t.reshape(s, ATTN_W) for t in outs)


def _attn_bwd_post(dq, dk, dv, qk, qg, kg, ones):
    s = dq.shape[0]
    tm = _tile(s, 512, 8)
    aw = ATTN_W

    def body(dq_ref, dk_ref, dv_ref, qk_ref, qg_ref, kg_ref, ones_ref, out_ref, gq_ref, gk_ref):
        @pl.when(pl.program_id(0) == 0)
        def _():
            gq_ref[...] = jnp.zeros_like(gq_ref)
            gk_ref[...] = jnp.zeros_like(gk_ref)

        def norm_bwd(dy, raw, g, scale):
            r = lax.rsqrt(_head_sum(raw * raw, ones_ref) * (1.0 / HEAD_DIM) + EPS)
            xh = raw * r
            gd = dy * (g * scale)
            dx = r * (gd - xh * (_head_sum(gd * xh, ones_ref) * (1.0 / HEAD_DIM)))
            gsum = jnp.sum(dy * xh, axis=0, keepdims=True) * scale
            for shift in (256, 128, 64):
                gsum = gsum + pltpu.roll(gsum, shift, 1)
            return dx, gsum

        dxq, gq = norm_bwd(dq_ref[...], qk_ref[:, :aw], qg_ref[...], HEAD_DIM ** -0.5)
        dxk, gk = norm_bwd(dk_ref[...], qk_ref[:, aw:], kg_ref[...], 1.0)
        out_ref[:, :aw] = dxq.astype(BF16)
        out_ref[:, aw:2 * aw] = dxk.astype(BF16)
        out_ref[:, 2 * aw:] = dv_ref[...].astype(BF16)
        gq_ref[...] += gq
        gk_ref[...] += gk

    row = lambda w: pl.BlockSpec((tm, w), lambda i: (i, 0))
    full = lambda t: pl.BlockSpec(t.shape, lambda i: (0, 0))
    vec = pl.BlockSpec((1, aw), lambda i: (0, 0))
    return pl.pallas_call(
        body, name="attn_bwd_post", grid=(s // tm,),
        in_specs=[row(aw), row(aw), row(aw), row(2 * aw), full(qg), full(kg), full(ones)],
        out_specs=[row(3 * aw), vec, vec],
        out_shape=[jax.ShapeDtypeStruct((s, 3 * aw), BF16), jax.ShapeDtypeStruct((1, aw), F32),
                   jax.ShapeDtypeStruct((1, aw), F32)],
        compiler_params=_params(("arbitrary",), VMEM_BIG),
    )(dq, dk, dv, qk, qg, kg, ones)


def _sum_parts(name, parts):
    _, n, w = parts.shape
    tn = _tile(n, 256, 8)

    def body(p_ref, o_ref):
        acc = p_ref[0].astype(F32)
        for j in range(1, N_DEV):
            acc = acc + p_ref[j].astype(F32)
        o_ref[...] = acc

    return pl.pallas_call(
        body, name=name, grid=(n // tn,),
        in_specs=[pl.BlockSpec((N_DEV, tn, w), lambda i: (0, i, 0))],
        out_specs=pl.BlockSpec((tn, w), lambda i: (i, 0)),
        out_shape=jax.ShapeDtypeStruct((n, w), F32),
        compiler_params=_params(("parallel",), VMEM_BIG),
    )(parts)


def _adamw(name, w, g, m, v):
    n, cols = w.shape
    tn = _tile(n, 256, 8)
    c1 = 1.0 - ADAM_B1 ** ADAM_STEP
    c2 = 1.0 - ADAM_B2 ** ADAM_STEP

    def body(w_ref, g_ref, m_ref, v_ref, d_ref, nm_ref, nv_ref):
        gv = g_ref[...]
        nm = ADAM_B1 * m_ref[...] + (1.0 - ADAM_B1) * gv
        nv = ADAM_B2 * v_ref[...] + (1.0 - ADAM_B2) * (gv * gv)
        nm_ref[...] = nm
        nv_ref[...] = nv
        d_ref[...] = -ADAM_LR * ((nm / c1) / (jnp.sqrt(nv / c2) + ADAM_EPS) + ADAM_WD * w_ref[...])

    blk = pl.BlockSpec((tn, cols), lambda i: (i, 0))
    return pl.pallas_call(
        body, name=name, grid=(n // tn,),
        in_specs=[blk] * 4, out_specs=[blk] * 3,
        out_shape=[jax.ShapeDtypeStruct((n, cols), F32)] * 3,
        compiler_params=_params(("parallel",), VMEM_BIG),
    )(w, g, m, v)


def _pack(vectors, width):
    flat = jnp.concatenate([t.reshape(-1) for t in vectors])
    rows = -(-flat.shape[0] // (8 * width)) * 8
    return jnp.pad(flat, (0, rows * width - flat.shape[0])).reshape(rows, width)


def _unpack(packed, shapes):
    flat = packed.reshape(-1)
    out, at = [], 0
    for shp in shapes:
        size = 1
        for dim in shp:
            size *= dim
        out.append(flat[at:at + size].reshape(shp))
        at += size
    return out


def kernel(x, norm1_g, w_in, conv_w, conv_b, cn_g, cn_b, q_norm_g, k_norm_g, w_out, norm2_g, w_up, ffconv_w, ffconv_b, w_down, loss_target, m_norm1_g, m_w_in, m_conv_w, m_conv_b, m_cn_g, m_cn_b, m_q_norm_g, m_k_norm_g, m_w_out, m_norm2_g, m_w_up, m_ffconv_w, m_ffconv_b, m_w_down, v_norm1_g, v_w_in, v_conv_w, v_conv_b, v_cn_g, v_cn_b, v_q_norm_g, v_k_norm_g, v_w_out, v_norm2_g, v_w_up, v_ffconv_w, v_ffconv_b, v_w_down):
    s, d = x.shape[1], x.shape[2]
    f = w_down.shape[0] * N_DEV
    n_in = w_in.shape[1] * N_DEV
    xs, target = x.reshape(s, d), loss_target.reshape(s, d)
    me = _linear((lax.axis_index("x"), lax.axis_index("y"), lax.axis_index("c")))

    g_in, g_filt = _communicate("gather_w_in", _Gather([w_in.T.astype(BF16), _pack([conv_w, ffconv_w], 128)]))
    w_in_t = g_in.reshape(n_in, d)
    filt = g_filt.reshape(N_DEV, -1)
    n_cw = conv_w.size
    cw = filt[:, :n_cw].reshape(N_DEV, CONV_K, -1).transpose(1, 0, 2).reshape(CONV_K, CONV_CH)
    cw = jnp.pad(cw, ((0, CONV_HALO - CONV_K), (0, 0)))
    fw = filt[:, n_cw:n_cw + ffconv_w.size].reshape(N_DEV, FF_K, -1).transpose(1, 0, 2).reshape(FF_K, 2, f)
    fb = ffconv_b.reshape(2, f)
    row = lambda t: t.reshape(1, -1)
    g1, g2 = row(norm1_g), row(norm2_g)
    qg, kg = row(jnp.tile(q_norm_g, N_HEADS)), row(jnp.tile(k_norm_g, N_HEADS))
    lanes = jnp.arange(ATTN_W) // HEAD_DIM
    ones = (lanes[:, None] == lanes[None, :]).astype(BF16)
    slopes = 2.0 ** (-8.0 * jnp.arange(1, N_HEADS + 1, dtype=F32) / N_HEADS)

    h, a, qk, qkv, (g_out,) = _proj_fwd(xs, g1, w_in_t, qg, kg, ones, _Gather([w_out.astype(BF16)]))
    u1, u, (g_up,) = _conv_fwd(a, cw, row(conv_b), row(cn_g), row(cn_b), _Gather([w_up.T.astype(BF16)]))
    o, lse, (g_down,) = _attn_forward(qkv, slopes, _Gather([w_down.astype(BF16)]))
    w_out_f = g_out.reshape(2 * CONV_CH, d)
    w_up_t = g_up.reshape(2 * f, d)
    w_down_f = g_down.reshape(f, d)
    x1, h2 = _mix_out(u, o, w_out_f, xs, g2)
    up_g, up_v, act = _up_fwd_act(h2, w_up_t, fw, fb, f)
    loss_acc, dyf, dyb = _down_loss(act, w_down_f, x1, target)

    tk, tk_wide = _tile(s, 1024, 8), _tile(s, 512, 8)
    gw_down = _mm_tn("grad_w_down", [[act]], dyb, tk)
    d_act = _mm("d_act", dyb, w_down_f, True, F32, _tile(s, 512, 8), _tile(f, 1408, 128))
    blocks = lambda t: _Exchange([t.reshape(N_DEV, t.shape[0] // N_DEV, d)])
    dpre_g, dpre_v, gfw, gfb, (r_down,) = _ff_bwd(up_g, up_v, d_act, fw, fb, blocks(gw_down))
    gw_up = _mm_tn("grad_w_up", [[dpre_g], [dpre_v]], h2, tk_wide)
    dx1f, dx1b, gg2 = _up_bwd(dpre_g, dpre_v, w_up_t, x1, dyf, g2)
    gw_out = _mm_tn("grad_w_out", [[u, o]], dx1b, tk)
    d_cat = _mm("d_cat", dx1b, w_out_f, True, F32, _tile(s, 512, 8), 2 * CONV_CH)
    d_a, gcw, gcb, gcg, gcbeta, (r_out,) = _conv_bwd(a, u1, d_cat, cw, row(cn_g), row(cn_b), blocks(gw_out))
    dq, dk, dv, (r_up,) = _attn_backward(qkv, d_cat, o, lse, slopes, ones[:Q_BLOCK, :Q_BLOCK], blocks(gw_up))
    d_qkv, gqg, gkg = _attn_bwd_post(dq, dk, dv, qk, qg, kg, ones)
    gw_in = _mm_tn("grad_w_in", [[d_a, d_qkv]], h, tk)
    grad_x, gg1, (r_in,) = _proj_bwd(d_a, d_qkv, w_in_t, xs, dx1f, g1, blocks(gw_in))
    g_w_in_t, g_w_out, g_w_up_t, g_w_down = [_sum_parts(f"sum_grad_{i}", r) for i, r in enumerate((r_in, r_out, r_up, r_down))]
    g_w_in, g_w_up = g_w_in_t.T, g_w_up_t.T

    small_shapes = [(d,), (d,), (CONV_CH,), (CONV_CH,), (CONV_CH,), (ATTN_W,), (ATTN_W,), (2 * f,),
                    (CONV_K, CONV_CH), (FF_K, 2 * f), (1,)]
    small = _pack([gg1, gg2, gcb, gcg, gcbeta, gqg, gkg, gfb, gcw[:CONV_K], gfw, loss_acc[0:1, 0:1]], 1024)
    (small_all,) = _communicate("gather_small_grads", _Gather([small]))
    sg1, sg2, scb, scg, scbeta, sqg, skg, sfb, scw, sfw, loss_sum = _unpack(
        _sum_parts("sum_small_grads", small_all), small_shapes)
    loss = loss_sum[0]
    cwl, fwl = conv_w.shape[1], ffconv_w.shape[1]
    g_small = [sg1, lax.dynamic_slice_in_dim(scw, me * cwl, cwl, 1), scb, scg, scbeta,
               sqg[:HEAD_DIM], skg[:HEAD_DIM], sg2, lax.dynamic_slice_in_dim(sfw, me * fwl, fwl, 1), sfb]

    w_small = [norm1_g, conv_w, conv_b, cn_g, cn_b, q_norm_g, k_norm_g, norm2_g, ffconv_w, ffconv_b]
    m_small = [m_norm1_g, m_conv_w, m_conv_b, m_cn_g, m_cn_b, m_q_norm_g, m_k_norm_g, m_norm2_g, m_ffconv_w, m_ffconv_b]
    v_small = [v_norm1_g, v_conv_w, v_conv_b, v_cn_g, v_cn_b, v_q_norm_g, v_k_norm_g, v_norm2_g, v_ffconv_w, v_ffconv_b]
    shapes = [t.shape for t in w_small]
    packed = _adamw("adamw_small", *[_pack(ts, 128) for ts in (w_small, g_small, m_small, v_small)])
    d_small, nm_small, nv_small = [_unpack(p, shapes) for p in packed]
    upd = {}
    for name, wt, gt, mt, vt in (("w_in", w_in, g_w_in, m_w_in, v_w_in), ("w_out", w_out, g_w_out, m_w_out, v_w_out),
                                 ("w_up", w_up, g_w_up, m_w_up, v_w_up), ("w_down", w_down, g_w_down, m_w_down, v_w_down)):
        upd[name] = (gt,) + tuple(_adamw("adamw_" + name, wt, gt, mt, vt))
    order = ["norm1_g", "w_in", "conv_w", "conv_b", "cn_g", "cn_b", "q_norm_g", "k_norm_g", "w_out", "norm2_g",
             "w_up", "ffconv_w", "ffconv_b", "w_down"]
    small_names = ["norm1_g", "conv_w", "conv_b", "cn_g", "cn_b", "q_norm_g", "k_norm_g", "norm2_g", "ffconv_w", "ffconv_b"]
    for i, name in enumerate(small_names):
        upd[name] = (g_small[i], d_small[i], nm_small[i], nv_small[i])
    outs = [loss, grad_x.reshape(x.shape)]
    for field in range(4):
        outs += [upd[name][field] for name in order]
    return tuple(outs)
```

```python
import functools

import jax
import jax.numpy as jnp
from jax import lax
from jax.experimental import pallas as pl
from jax.experimental.pallas import tpu as pltpu

F32, BF16 = jnp.float32, jnp.bfloat16
N_DEV = 8
N_HEADS, HEAD_DIM = 8, 64
CONV_CH = 512
ATTN_W = N_HEADS * HEAD_DIM
CONV_K, FF_K = 31, 3
CONV_HALO = 32
FF_HALO = 8
PATTERN_DILATIONS = (1, 4, 16)
Q_BLOCK = 128
EPS = 1e-6
NEG = -1e30
ADAM_LR, ADAM_B1, ADAM_B2, ADAM_EPS, ADAM_WD, ADAM_STEP = 0.001, 0.9, 0.999, 1e-08, 0.01, 10
MESH_AXES = ("x", "y", "c")
VMEM_BIG = 56 * 1024 * 1024
ANY = pl.BlockSpec(memory_space=pl.ANY)


def _params(sem=None, vmem=None):
    return pltpu.CompilerParams(dimension_semantics=sem, vmem_limit_bytes=vmem)


def _dot(a, b, trans_a=False, trans_b=False):
    dims = (((0 if trans_a else 1,), (1 if trans_b else 0,)), ((), ()))
    return lax.dot_general(a, b, dims, preferred_element_type=F32)


def _sigmoid(z):
    return 1.0 / (1.0 + jnp.exp(-z))


def _tile(n, pref, mult):
    if n <= pref:
        return n
    t = (pref // mult) * mult
    while n % t:
        t -= mult
    return t


def _head_sum(v, ones_ref):
    hi = v.astype(BF16)
    lo = (v - hi.astype(F32)).astype(BF16)
    ones = ones_ref[...]
    return _dot(hi, ones) + _dot(lo, ones)


def _linear(p):
    return 4 * p[0] + 2 * p[1] + p[2]


class _Gather:
    def __init__(self, arrs):
        self.arrs = list(arrs)
        self.n = len(self.arrs)
        self.out_shape = [jax.ShapeDtypeStruct((N_DEV,) + a.shape, a.dtype) for a in self.arrs]

    def _setup(self, ins, outs, sems):
        send_sems, recv_sems, local_sems = sems
        x, y, c = lax.axis_index("x"), lax.axis_index("y"), lax.axis_index("c")
        me, sib = (x, y, c), (x, y, 1 - c)
        chips = [(1 - x, y), (x, 1 - y), (1 - x, 1 - y)]

        def copy(a, k, block, to, src=None):
            dst = outs[a].at[_linear(block)]
            return pltpu.make_async_remote_copy(
                src_ref=dst if src is None else src, dst_ref=dst,
                send_sem=send_sems.at[7 * a + k], recv_sem=recv_sems.at[7 * a + k],
                device_id=to, device_id_type=pl.DeviceIdType.MESH)

        mine = [pltpu.make_async_copy(ins[a], outs[a].at[_linear(me)], local_sems.at[a]) for a in range(self.n)]
        first = []
        for a in range(self.n):
            first.append(copy(a, 0, me, sib, src=ins[a]))
            first += [copy(a, 1 + j, me, (*chip, c), src=ins[a]) for j, chip in enumerate(chips)]
        return copy, mine, first, me, sib, chips, c

    def start(self, ins, outs, sems):
        _, mine, first, *_ = self._setup(ins, outs, sems)
        for cp in mine + first:
            cp.start()

    def finish(self, ins, outs, sems):
        copy, mine, first, me, sib, chips, c = self._setup(ins, outs, sems)
        passed = []
        for j, chip in enumerate(chips):
            for a in range(self.n):
                copy(a, 1 + j, (*chip, c), me).wait_recv()
                fwd = copy(a, 4 + j, (*chip, c), sib)
                fwd.start()
                passed.append(fwd)
        for a in range(self.n):
            copy(a, 0, sib, me).wait_recv()
            for j, chip in enumerate(chips):
                copy(a, 4 + j, (*chip, 1 - c), me).wait_recv()
        for cp in first + passed:
            cp.wait_send()
        for cp in mine:
            cp.wait()


class _Exchange:
    def __init__(self, arrs):
        self.arrs = list(arrs)
        self.n = len(self.arrs)
        self.out_shape = [jax.ShapeDtypeStruct(a.shape, a.dtype) for a in self.arrs]

    def _setup(self, ins, outs, sems):
        send_sems, recv_sems, local_sems = sems
        x, y, c = lax.axis_index("x"), lax.axis_index("y"), lax.axis_index("c")
        me = _linear((x, y, c))
        mine = [pltpu.make_async_copy(ins[a].at[me], outs[a].at[me], local_sems.at[a]) for a in range(self.n)]
        sends, recvs = [], []
        for p in range(1, N_DEV):
            peer = (1 - x if p & 4 else x, 1 - y if p & 2 else y, 1 - c if p & 1 else c)
            for a in range(self.n):
                sem = dict(send_sem=send_sems.at[7 * a + p - 1], recv_sem=recv_sems.at[7 * a + p - 1],
                           device_id=peer, device_id_type=pl.DeviceIdType.MESH)
                sends.append(pltpu.make_async_remote_copy(
                    src_ref=ins[a].at[_linear(peer)], dst_ref=outs[a].at[me], **sem))
                recvs.append(pltpu.make_async_remote_copy(
                    src_ref=ins[a].at[_linear(peer)], dst_ref=outs[a].at[_linear(peer)], **sem))
        return mine, sends, recvs

    def start(self, ins, outs, sems):
        mine, sends, _ = self._setup(ins, outs, sems)
        for cp in mine + sends:
            cp.start()

    def finish(self, ins, outs, sems):
        mine, sends, recvs = self._setup(ins, outs, sems)
        for cp in recvs:
            cp.wait_recv()
        for cp in sends:
            cp.wait_send()
        for cp in mine:
            cp.wait()


def _rider_scratch(rider):
    return [pltpu.SemaphoreType.DMA((7 * rider.n,)), pltpu.SemaphoreType.DMA((7 * rider.n,)),
            pltpu.SemaphoreType.DMA((rider.n,))]


def _communicate(name, rider):
    na = rider.n

    def body(*refs):
        ins, outs, sems = refs[:na], refs[na:2 * na], refs[2 * na:]
        rider.start(ins, outs, sems)
        rider.finish(ins, outs, sems)

    return pl.pallas_call(
        body, name=name, out_shape=rider.out_shape, in_specs=[ANY] * na, out_specs=[ANY] * na,
        scratch_shapes=_rider_scratch(rider),
    )(*rider.arrs)


def _call(body, rider=None, *, name, grid, in_specs, out_specs, out_shape, scratch_shapes=(), compiler_params, args):
    if rider is None:
        out = pl.pallas_call(body, name=name, grid=grid, in_specs=in_specs, out_specs=out_specs, out_shape=out_shape,
                             scratch_shapes=list(scratch_shapes), compiler_params=compiler_params)(*args)
        return out, None
    n_in, n_out, n_scr, na = len(in_specs), len(out_specs), len(scratch_shapes), rider.n

    def carried(*refs):
        ins, refs = refs[:n_in], refs[n_in:]
        r_ins, refs = refs[:na], refs[na:]
        outs, refs = refs[:n_out], refs[n_out:]
        r_outs, refs = refs[:na], refs[na:]
        scratch, sems = refs[:n_scr], refs[n_scr:]
        ids = [pl.program_id(ax) for ax in range(len(grid))]
        first, last = ids[0] == 0, ids[0] == grid[0] - 1
        for ax in range(1, len(grid)):
            first, last = first & (ids[ax] == 0), last & (ids[ax] == grid[ax] - 1)

        @pl.when(first)
        def _():
            rider.start(r_ins, r_outs, sems)

        body(*ins, *outs, *scratch)

        @pl.when(last)
        def _():
            rider.finish(r_ins, r_outs, sems)

    out = pl.pallas_call(
        carried, name=name, grid=grid, in_specs=list(in_specs) + [ANY] * na, out_specs=list(out_specs) + [ANY] * na,
        out_shape=list(out_shape) + rider.out_shape, scratch_shapes=list(scratch_shapes) + _rider_scratch(rider),
        compiler_params=compiler_params)(*args, *rider.arrs)
    return out[:n_out], out[n_out:]


def _mm(name, a, w, trans_b, out_dtype, tm, tn):
    m, k = a.shape
    n = w.shape[0] if trans_b else w.shape[1]

    def body(a_ref, w_ref, o_ref):
        o_ref[...] = _dot(a_ref[...], w_ref[...], trans_b=trans_b).astype(o_ref.dtype)

    w_spec = pl.BlockSpec((tn, k), lambda j, i: (j, 0)) if trans_b else pl.BlockSpec((k, tn), lambda j, i: (0, j))
    return pl.pallas_call(
        body, name=name, grid=(n // tn, m // tm),
        in_specs=[pl.BlockSpec((tm, k), lambda j, i: (i, 0)), w_spec],
        out_specs=pl.BlockSpec((tm, tn), lambda j, i: (i, j)),
        out_shape=jax.ShapeDtypeStruct((m, n), out_dtype),
        compiler_params=_params(("parallel", "parallel"), VMEM_BIG),
    )(a, w)


def _mm_tn(name, groups, b, tk):
    s, d = b.shape
    parts = [p for grp in groups for p in grp]
    owner = [g for g, grp in enumerate(groups) for _ in grp]
    width = sum(p.shape[1] for p in groups[0])
    offsets = []
    for grp in groups:
        at = 0
        for p in grp:
            offsets.append(at)
            at += p.shape[1]
        assert at == width
    np_, nk = len(parts), s // tk

    def body(*refs):
        a_refs, b_ref, o_ref, acc_ref = refs[:np_], refs[np_], refs[np_ + 1], refs[np_ + 2]
        g, kk = pl.program_id(0), pl.program_id(1)

        @pl.when(kk == 0)
        def _():
            acc_ref[...] = jnp.zeros_like(acc_ref)

        for grp in range(len(groups)):
            @pl.when(g == grp)
            def _(grp=grp):
                for p in range(np_):
                    if owner[p] == grp:
                        rows = slice(offsets[p], offsets[p] + parts[p].shape[1])
                        acc_ref[rows, :] += _dot(a_refs[p][...].astype(BF16), b_ref[...], trans_a=True)

        @pl.when(kk == nk - 1)
        def _():
            o_ref[...] = acc_ref[...].astype(o_ref.dtype)

    def a_spec(p):
        return pl.BlockSpec((tk, parts[p].shape[1]), lambda g, kk: (jnp.where(g == owner[p], kk, 0), 0))

    return pl.pallas_call(
        body, name=name, grid=(len(groups), nk),
        in_specs=[a_spec(p) for p in range(np_)] + [pl.BlockSpec((tk, d), lambda g, kk: (kk, 0))],
        out_specs=pl.BlockSpec((width, d), lambda g, kk: (g, 0)),
        out_shape=jax.ShapeDtypeStruct((len(groups) * width, d), BF16),
        scratch_shapes=[pltpu.VMEM((width, d), F32)],
        compiler_params=_params(("arbitrary", "arbitrary"), VMEM_BIG),
    )(*parts, b)


def _norm_fwd(x, g1, rider=None):
    s, d = x.shape
    tm = _tile(s, 512, 8)

    def body(x_ref, g_ref, h_ref):
        xv = x_ref[...]
        r = lax.rsqrt(jnp.mean(xv * xv, axis=-1, keepdims=True) + EPS)
        h_ref[...] = (xv * r * g_ref[...]).astype(BF16)

    row = pl.BlockSpec((tm, d), lambda i: (i, 0))
    (h,), carried = _call(
        body, rider, name="norm_fwd", grid=(s // tm,),
        in_specs=[row, pl.BlockSpec(g1.shape, lambda i: (0, 0))], out_specs=[row],
        out_shape=[jax.ShapeDtypeStruct((s, d), BF16)],
        compiler_params=_params(("arbitrary",), VMEM_BIG), args=(x, g1))
    return h, carried


def _proj_fwd(h, w_in_t, qg, kg, ones, rider=None):
    s, d = h.shape
    tm = _tile(s, 512, 8)
    c2, aw = 2 * CONV_CH, ATTN_W

    def body(h_ref, w_ref, qg_ref, kg_ref, ones_ref, a_ref, qk_ref, qkv_ref):
        proj = _dot(h_ref[...], w_ref[...], trans_b=True)
        a_ref[...] = proj[:, :c2]
        qk_ref[...] = proj[:, c2:c2 + 2 * aw]
        q, k = proj[:, c2:c2 + aw], proj[:, c2 + aw:c2 + 2 * aw]
        rq = lax.rsqrt(_head_sum(q * q, ones_ref) * (1.0 / HEAD_DIM) + EPS)
        rk = lax.rsqrt(_head_sum(k * k, ones_ref) * (1.0 / HEAD_DIM) + EPS)
        qkv_ref[:, :aw] = q * rq * qg_ref[...] * (HEAD_DIM ** -0.5)
        qkv_ref[:, aw:2 * aw] = k * rk * kg_ref[...]
        qkv_ref[:, 2 * aw:] = proj[:, c2 + 2 * aw:]

    row = lambda w: pl.BlockSpec((tm, w), lambda i: (i, 0))
    full = lambda a: pl.BlockSpec(a.shape, lambda i: (0, 0))
    outs, carried = _call(
        body, rider, name="proj_fwd", grid=(s // tm,),
        in_specs=[row(d), full(w_in_t), full(qg), full(kg), full(ones)],
        out_specs=[row(c2), row(2 * aw), row(3 * aw)],
        out_shape=[jax.ShapeDtypeStruct((s, c2), F32), jax.ShapeDtypeStruct((s, 2 * aw), F32),
                   jax.ShapeDtypeStruct((s, 3 * aw), F32)],
        compiler_params=_params(("arbitrary",), VMEM_BIG),
        args=(h, w_in_t, qg, kg, ones))
    return (*outs, carried)


def _mix_out(u, o, w_out, x, g2):
    s, d = x.shape
    tm = _tile(s, 512, 8)

    def body(u_ref, o_ref, w_ref, x_ref, g_ref, x1_ref, h2_ref):
        x1 = x_ref[...] + _dot(u_ref[...], w_ref[:CONV_CH, :]) + _dot(o_ref[...].astype(BF16), w_ref[CONV_CH:, :])
        x1_ref[...] = x1
        r = lax.rsqrt(jnp.mean(x1 * x1, axis=-1, keepdims=True) + EPS)
        h2_ref[...] = (x1 * r * g_ref[...]).astype(BF16)

    row = lambda w: pl.BlockSpec((tm, w), lambda i: (i, 0))
    full = lambda a: pl.BlockSpec(a.shape, lambda i: (0, 0))
    return pl.pallas_call(
        body, name="mix_out", grid=(s // tm,),
        in_specs=[row(CONV_CH), row(ATTN_W), full(w_out), row(d), full(g2)],
        out_specs=[row(d), row(d)],
        out_shape=[jax.ShapeDtypeStruct((s, d), F32), jax.ShapeDtypeStruct((s, d), BF16)],
        compiler_params=_params(("parallel",), VMEM_BIG),
    )(u, o, w_out, x, g2)


def _up_fwd(h2, w_up_t, f):
    s, d = h2.shape
    tm, tn = _tile(s, 512, 8), _tile(f, 1408, 128)
    nb = f // tn

    def body(h_ref, wg_ref, wv_ref, g_ref, v_ref):
        hv = h_ref[...]
        g_ref[...] = _dot(hv, wg_ref[...], trans_b=True)
        v_ref[...] = _dot(hv, wv_ref[...], trans_b=True)

    return pl.pallas_call(
        body, name="up_fwd", grid=(nb, s // tm),
        in_specs=[pl.BlockSpec((tm, d), lambda j, i: (i, 0)), pl.BlockSpec((tn, d), lambda j, i: (j, 0)),
                  pl.BlockSpec((tn, d), lambda j, i: (j + nb, 0))],
        out_specs=[pl.BlockSpec((tm, tn), lambda j, i: (i, j))] * 2,
        out_shape=[jax.ShapeDtypeStruct((s, f), F32)] * 2,
        compiler_params=_params(("parallel", "parallel"), VMEM_BIG),
    )(h2, w_up_t, w_up_t)


def _down_loss(act, w_down, x1, target):
    s, d = x1.shape
    f = act.shape[1]
    tm = _tile(s, 512, 8)

    def body(a_ref, w_ref, x1_ref, t_ref, loss_ref, dyf_ref, dyb_ref):
        @pl.when(pl.program_id(0) == 0)
        def _():
            loss_ref[...] = jnp.zeros_like(loss_ref)

        diff = x1_ref[...] + _dot(a_ref[...], w_ref[...]) - t_ref[...]
        sq = jnp.sum(jnp.sum(diff * diff, axis=1, keepdims=True), axis=0, keepdims=True)
        loss_ref[...] += jnp.broadcast_to(sq * (0.5 / d), loss_ref.shape)
        dy = diff * (1.0 / d)
        dyf_ref[...] = dy
        dyb_ref[...] = dy.astype(BF16)

    row = lambda w: pl.BlockSpec((tm, w), lambda i: (i, 0))
    return pl.pallas_call(
        body, name="down_loss", grid=(s // tm,),
        in_specs=[row(f), pl.BlockSpec((f, d), lambda i: (0, 0)), row(d), row(d)],
        out_specs=[pl.BlockSpec((8, 128), lambda i: (0, 0)), row(d), row(d)],
        out_shape=[jax.ShapeDtypeStruct((8, 128), F32), jax.ShapeDtypeStruct((s, d), F32),
                   jax.ShapeDtypeStruct((s, d), BF16)],
        compiler_params=_params(("arbitrary",), VMEM_BIG),
    )(act, w_down, x1, target)


def _norm_bwd_tail(dh, xv, g, resid, d):
    r = lax.rsqrt(jnp.mean(xv * xv, axis=-1, keepdims=True) + EPS)
    xh = xv * r
    gd = dh * g
    dx = r * (gd - xh * (jnp.sum(gd * xh, axis=-1, keepdims=True) * (1.0 / d)))
    return resid + dx, jnp.sum(dh * xh, axis=0, keepdims=True)


def _up_bwd(dg, dv, w_up_t, x1, dyf, g2):
    s, d = x1.shape
    f = dg.shape[1]
    tm = _tile(s, 512, 8)

    def body(dg_ref, dv_ref, w_ref, x1_ref, dy_ref, g_ref, dxf_ref, dxb_ref, gg_ref):
        @pl.when(pl.program_id(0) == 0)
        def _():
            gg_ref[...] = jnp.zeros_like(gg_ref)

        dh = _dot(dg_ref[...], w_ref[:f, :]) + _dot(dv_ref[...], w_ref[f:, :])
        dx, gg = _norm_bwd_tail(dh, x1_ref[...], g_ref[...], dy_ref[...], d)
        dxf_ref[...] = dx
        dxb_ref[...] = dx.astype(BF16)
        gg_ref[...] += gg

    row = lambda w: pl.BlockSpec((tm, w), lambda i: (i, 0))
    full = lambda a: pl.BlockSpec(a.shape, lambda i: (0, 0))
    return pl.pallas_call(
        body, name="up_bwd", grid=(s // tm,),
        in_specs=[row(f), row(f), full(w_up_t), row(d), row(d), full(g2)],
        out_specs=[row(d), row(d), pl.BlockSpec((1, d), lambda i: (0, 0))],
        out_shape=[jax.ShapeDtypeStruct((s, d), F32), jax.ShapeDtypeStruct((s, d), BF16),
                   jax.ShapeDtypeStruct((1, d), F32)],
        compiler_params=_params(("arbitrary",), VMEM_BIG),
    )(dg, dv, w_up_t, x1, dyf, g2)


def _proj_bwd(parts, w_in_t, x, dx1, g1, rider=None):
    s, d = x.shape
    widths = [p.shape[1] for p in parts]
    starts = [sum(widths[:i]) for i in range(len(parts))]
    np_ = len(parts)
    tm = _tile(s, 512, 8)

    def body(*refs):
        p_refs = refs[:np_]
        w_ref, x_ref, r_ref, g_ref, gx_ref, gg_ref = refs[np_:]

        @pl.when(pl.program_id(0) == 0)
        def _():
            gg_ref[...] = jnp.zeros_like(gg_ref)

        dh = _dot(p_refs[0][...], w_ref[:widths[0], :])
        for p in range(1, np_):
            dh = dh + _dot(p_refs[p][...], w_ref[starts[p]:starts[p] + widths[p], :])
        dx, gg = _norm_bwd_tail(dh, x_ref[...], g_ref[...], r_ref[...], d)
        gx_ref[...] = dx
        gg_ref[...] += gg

    row = lambda w: pl.BlockSpec((tm, w), lambda i: (i, 0))
    full = lambda a: pl.BlockSpec(a.shape, lambda i: (0, 0))
    outs, carried = _call(
        body, rider, name="proj_bwd", grid=(s // tm,),
        in_specs=[row(w) for w in widths] + [full(w_in_t), row(d), row(d), full(g1)],
        out_specs=[row(d), pl.BlockSpec((1, d), lambda i: (0, 0))],
        out_shape=[jax.ShapeDtypeStruct((s, d), F32), jax.ShapeDtypeStruct((1, d), F32)],
        compiler_params=_params(("arbitrary",), VMEM_BIG),
        args=(*parts, w_in_t, x, dx1, g1))
    return (*outs, carried)


CONV_CHUNK = 32
SUBLANES = 8


def _glu(av):
    return av[:, :CONV_CH] * _sigmoid(av[:, CONV_CH:])


def _chunks(n, size=CONV_CHUNK):
    return [(r0, min(size, n - r0)) for r0 in range(0, n, size)]


def _fill_shifted(sbuf, rows):
    for r in range(1, SUBLANES):
        for r0, n in _chunks(rows, 64):
            sbuf[r, r0:r0 + n, :] = sbuf[0, r0 + r:r0 + r + n, :]


def _tap(sbuf, offset, r0, rows):
    q, r = divmod(offset, SUBLANES)
    return sbuf[r, SUBLANES * q + r0:SUBLANES * q + r0 + rows, :]


def _layer_norm_stats(u1):
    mu = jnp.mean(u1, axis=-1, keepdims=True)
    cen = u1 - mu
    rstd = lax.rsqrt(jnp.mean(cen * cen, axis=-1, keepdims=True) + EPS)
    return cen * rstd, rstd


def _conv_fwd(a, cw, cb, cg, cbeta, rider=None):
    s = a.shape[0]
    tm = _tile(s, 256, CONV_CHUNK)
    hb = tm // CONV_HALO

    def body(a_ref, ap_ref, cw_ref, cb_ref, cg_ref, cbeta_ref, u1_ref, u_ref, ubuf):
        i = pl.program_id(0)
        ubuf[0, 0:CONV_HALO, :] = jnp.where(i > 0, _glu(ap_ref[...]), 0.0)
        for r0, n in _chunks(tm):
            ubuf[0, CONV_HALO + r0:CONV_HALO + r0 + n, :] = _glu(a_ref[r0:r0 + n, :])
        _fill_shifted(ubuf, tm + CONV_HALO - SUBLANES)
        for r0, n in _chunks(tm):
            acc = jnp.broadcast_to(cb_ref[...], (n, CONV_CH))
            for k in range(CONV_K):
                acc = acc + cw_ref[k:k + 1, :] * _tap(ubuf, 2 + k, r0, n)
            u1_ref[r0:r0 + n, :] = acc
            xh, _ = _layer_norm_stats(acc)
            z = xh * cg_ref[...] + cbeta_ref[...]
            u_ref[r0:r0 + n, :] = (z * _sigmoid(z)).astype(BF16)

    full = lambda t: pl.BlockSpec(t.shape, lambda i: (0, 0))
    outs, carried = _call(
        body, rider, name="conv_fwd", grid=(s // tm,),
        in_specs=[pl.BlockSpec((tm, 2 * CONV_CH), lambda i: (i, 0)),
                  pl.BlockSpec((CONV_HALO, 2 * CONV_CH), lambda i: (jnp.maximum(i * hb - 1, 0), 0)),
                  full(cw), full(cb), full(cg), full(cbeta)],
        out_specs=[pl.BlockSpec((tm, CONV_CH), lambda i: (i, 0))] * 2,
        out_shape=[jax.ShapeDtypeStruct((s, CONV_CH), F32), jax.ShapeDtypeStruct((s, CONV_CH), BF16)],
        scratch_shapes=[pltpu.VMEM((SUBLANES, CONV_HALO + tm, CONV_CH), F32)],
        compiler_params=_params(("arbitrary",), VMEM_BIG),
        args=(a, a, cw, cb, cg, cbeta))
    return (*outs, carried)


def _conv_bwd(a, u1, d_cat, cw, cg, cbeta, rider=None):
    s = a.shape[0]
    tm = _tile(s, 256, CONV_CHUNK)
    hb = tm // CONV_HALO
    last_halo = s // CONV_HALO - 1
    nt = s // tm
    te = tm + CONV_HALO

    def body(a_ref, ap_ref, u1_ref, u1n_ref, du_ref, dun_ref, cw_ref, cg_ref, cbeta_ref,
             da_ref, gw_ref, gb_ref, gg_ref, gbeta_ref, ubuf, dbuf):
        i = pl.program_id(0)

        @pl.when(i == 0)
        def _():
            gw_ref[...] = jnp.zeros_like(gw_ref)
            gb_ref[...] = jnp.zeros_like(gb_ref)
            gg_ref[...] = jnp.zeros_like(gg_ref)
            gbeta_ref[...] = jnp.zeros_like(gbeta_ref)

        def du1_of(u1, du):
            xh, rstd = _layer_norm_stats(u1)
            z = xh * cg_ref[...] + cbeta_ref[...]
            sz = _sigmoid(z)
            dz = du * (sz * (1.0 + z * (1.0 - sz)))
            dxh = dz * cg_ref[...]
            du1 = rstd * (dxh - jnp.mean(dxh, axis=-1, keepdims=True)
                          - xh * jnp.mean(dxh * xh, axis=-1, keepdims=True))
            return du1, dz, xh

        gg, gbeta, gb = [jnp.zeros((SUBLANES, CONV_CH), F32) for _ in range(3)]
        for r0, n in _chunks(tm):
            du1, dz, xh = du1_of(u1_ref[r0:r0 + n, :], du_ref[r0:r0 + n, :])
            gg, gbeta, gb = gg + _fold8(dz * xh), gbeta + _fold8(dz), gb + _fold8(du1)
            dbuf[0, r0:r0 + n, :] = du1
        gg_ref[...] += jnp.sum(gg, axis=0, keepdims=True)
        gbeta_ref[...] += jnp.sum(gbeta, axis=0, keepdims=True)
        gb_ref[...] += jnp.sum(gb, axis=0, keepdims=True)
        du1n, _, _ = du1_of(u1n_ref[...], jnp.where(i < nt - 1, dun_ref[...], 0.0))
        dbuf[0, tm:, :] = du1n
        _fill_shifted(dbuf, tm + CONV_HALO - SUBLANES)

        ubuf[0, 0:CONV_HALO, :] = jnp.where(i > 0, _glu(ap_ref[...]), 0.0)
        for r0, n in _chunks(tm):
            ubuf[0, CONV_HALO + r0:CONV_HALO + r0 + n, :] = _glu(a_ref[r0:r0 + n, :])
        _fill_shifted(ubuf, tm + CONV_HALO - SUBLANES)

        for k in range(CONV_K):
            part = jnp.zeros((SUBLANES, CONV_CH), F32)
            for r0, n in _chunks(tm):
                part = part + _fold8(dbuf[0, r0:r0 + n, :] * _tap(ubuf, 2 + k, r0, n))
            gw_ref[k:k + 1, :] += jnp.sum(part, axis=0, keepdims=True)

        for r0, n in _chunks(tm):
            acc = jnp.zeros((n, CONV_CH), F32)
            for k in range(CONV_K):
                acc = acc + cw_ref[k:k + 1, :] * _tap(dbuf, CONV_K - 1 - k, r0, n)
            avc = a_ref[r0:r0 + n, :CONV_CH]
            sgc = _sigmoid(a_ref[r0:r0 + n, CONV_CH:])
            da_ref[r0:r0 + n, :CONV_CH] = (acc * sgc).astype(BF16)
            da_ref[r0:r0 + n, CONV_CH:] = (acc * avc * sgc * (1.0 - sgc)).astype(BF16)

    full = lambda t: pl.BlockSpec(t.shape, lambda i: (0, 0))
    vec = pl.BlockSpec((1, CONV_CH), lambda i: (0, 0))
    nxt = lambda i: (jnp.minimum((i + 1) * hb, last_halo), 0)
    outs, carried = _call(
        body, rider, name="conv_bwd", grid=(nt,),
        in_specs=[pl.BlockSpec((tm, 2 * CONV_CH), lambda i: (i, 0)),
                  pl.BlockSpec((CONV_HALO, 2 * CONV_CH), lambda i: (jnp.maximum(i * hb - 1, 0), 0)),
                  pl.BlockSpec((tm, CONV_CH), lambda i: (i, 0)), pl.BlockSpec((CONV_HALO, CONV_CH), nxt),
                  pl.BlockSpec((tm, CONV_CH), lambda i: (i, 0)), pl.BlockSpec((CONV_HALO, CONV_CH), nxt),
                  full(cw), full(cg), full(cbeta)],
        out_specs=[pl.BlockSpec((tm, 2 * CONV_CH), lambda i: (i, 0)),
                   pl.BlockSpec((CONV_HALO, CONV_CH), lambda i: (0, 0)), vec, vec, vec],
        out_shape=[jax.ShapeDtypeStruct((s, 2 * CONV_CH), BF16), jax.ShapeDtypeStruct((CONV_HALO, CONV_CH), F32),
                   jax.ShapeDtypeStruct((1, CONV_CH), F32), jax.ShapeDtypeStruct((1, CONV_CH), F32),
                   jax.ShapeDtypeStruct((1, CONV_CH), F32)],
        scratch_shapes=[pltpu.VMEM((SUBLANES, CONV_HALO + tm, CONV_CH), F32), pltpu.VMEM((SUBLANES, te, CONV_CH), F32)],
        compiler_params=_params(("arbitrary",), VMEM_BIG),
        args=(a, a, u1, u1, d_cat, d_cat, cw, cg, cbeta))
    return (*outs, carried)


def _ff_tiles(s, f):
    return _tile(s, 256, 8), _tile(f, 1408, 128)


FF_CHUNK, FF_LANES = 64, 128
MXU_WIDTH = 256


def _row_chunks(n):
    return [(r0, min(FF_CHUNK, n - r0)) for r0 in range(0, n, FF_CHUNK)]


def _fold8(v):
    acc = v[0:8]
    for r in range(8, v.shape[0], 8):
        acc = acc + v[r:r + 8]
    return acc


def _ff_conv(pre_buf, w_ref, b_ref, half, r0, rows, cols):
    acc = b_ref[half:half + 1, cols]
    for k in range(FF_K):
        off = FF_HALO - (FF_K - 1) + k + r0
        acc = acc + w_ref[k, half:half + 1, cols] * pre_buf[half, off:off + rows, cols]
    return acc


def _ff_act(up_g, up_v, fw, fb):
    s, f = up_g.shape
    tm, tc = _ff_tiles(s, f)
    hb = tm // FF_HALO

    def body(g_ref, gp_ref, v_ref, vp_ref, w_ref, b_ref, act_ref, buf):
        i = pl.program_id(1)
        for half, (m_ref, p_ref) in enumerate(((g_ref, gp_ref), (v_ref, vp_ref))):
            buf[half, 0:FF_HALO, :] = jnp.where(i > 0, p_ref[...], 0.0)
            buf[half, FF_HALO:, :] = m_ref[...]
        for c0 in range(0, tc, FF_LANES):
            cols = slice(c0, c0 + FF_LANES)
            for r0, rows in _row_chunks(tm):
                gate = _ff_conv(buf, w_ref, b_ref, 0, r0, rows, cols)
                val = _ff_conv(buf, w_ref, b_ref, 1, r0, rows, cols)
                act_ref[r0:r0 + rows, cols] = (gate * _sigmoid(gate) * val).astype(BF16)

    main = pl.BlockSpec((tm, tc), lambda j, i: (i, j))
    prev = pl.BlockSpec((FF_HALO, tc), lambda j, i: (jnp.maximum(i * hb - 1, 0), j))
    return pl.pallas_call(
        body, name="ff_act", grid=(f // tc, s // tm),
        in_specs=[main, prev, main, prev, pl.BlockSpec((FF_K, 2, tc), lambda j, i: (0, 0, j)),
                  pl.BlockSpec((2, tc), lambda j, i: (0, j))],
        out_specs=main,
        out_shape=jax.ShapeDtypeStruct((s, f), BF16),
        scratch_shapes=[pltpu.VMEM((2, FF_HALO + tm, tc), F32)],
        compiler_params=_params(("parallel", "parallel"), VMEM_BIG),
    )(up_g, up_g, up_v, up_v, fw, fb)


def _up_fwd_act(h2, w_up_t, fw, fb, f):
    s, d = h2.shape
    tm, tn = _tile(s, 512, 8), _tile(f, 1408, 128)
    nc, nt = f // tn, s // tm

    def body(h_ref, wg_ref, wv_ref, w_ref, b_ref, g_ref, v_ref, act_ref, xbuf):
        @pl.when(pl.program_id(1) == 0)
        def _():
            xbuf[...] = jnp.zeros_like(xbuf)

        strips = [(slice(c0, c0 + FF_LANES), r0, rows) for c0 in range(0, tn, FF_LANES) for r0, rows in _row_chunks(tm)]
        pieces = [(out, wt, slice(c0, min(c0 + MXU_WIDTH, tn))) for out, wt in ((g_ref, wg_ref), (v_ref, wv_ref))
                  for c0 in range(0, tn, MXU_WIDTH)]
        per_piece = -(-len(strips) // len(pieces))
        hv = h_ref[...]
        for n, (out, wt, wcols) in enumerate(pieces):
            out[:, wcols] = _dot(hv, wt[wcols, :], trans_b=True)
            for cols, r0, rows in strips[n * per_piece:(n + 1) * per_piece]:
                gate = _ff_conv(xbuf, w_ref, b_ref, 0, r0, rows, cols)
                val = _ff_conv(xbuf, w_ref, b_ref, 1, r0, rows, cols)
                act_ref[r0:r0 + rows, cols] = (gate * _sigmoid(gate) * val).astype(BF16)
        for half, ref in enumerate((g_ref, v_ref)):
            xbuf[half, 0:FF_HALO, :] = xbuf[half, tm:tm + FF_HALO, :]
            xbuf[half, FF_HALO:, :] = ref[...]

    cur = lambda j, i: (jnp.minimum(i, nt - 1), j)
    return pl.pallas_call(
        body, name="up_fwd_act", grid=(nc, nt + 1),
        in_specs=[pl.BlockSpec((tm, d), lambda j, i: (jnp.minimum(i, nt - 1), 0)),
                  pl.BlockSpec((tn, d), lambda j, i: (j, 0)), pl.BlockSpec((tn, d), lambda j, i: (j + nc, 0)),
                  pl.BlockSpec((FF_K, 2, tn), lambda j, i: (0, 0, j)), pl.BlockSpec((2, tn), lambda j, i: (0, j))],
        out_specs=[pl.BlockSpec((tm, tn), cur), pl.BlockSpec((tm, tn), cur),
                   pl.BlockSpec((tm, tn), lambda j, i: (jnp.maximum(i - 1, 0), j))],
        out_shape=[jax.ShapeDtypeStruct((s, f), F32), jax.ShapeDtypeStruct((s, f), F32), jax.ShapeDtypeStruct((s, f), BF16)],
        scratch_shapes=[pltpu.VMEM((2, FF_HALO + tm, tn), F32)],
        compiler_params=_params(("arbitrary", "arbitrary"), VMEM_BIG),
    )(h2, w_up_t, w_up_t, fw, fb)


def _ff_bwd(up_g, up_v, d_act, fw, fb, rider=None):
    s, f = up_g.shape
    tm, tc = _ff_tiles(s, f)
    hb = tm // FF_HALO
    nt = s // tm
    last_halo = s // FF_HALO - 1
    te = tm + FF_HALO

    def body(g_ref, gp_ref, gn_ref, v_ref, vp_ref, vn_ref, da_ref, dan_ref, w_ref, b_ref,
             dg_ref, dv_ref, gw_ref, gb_ref, buf, shifted, dbuf, dshifted, dabuf):
        i = pl.program_id(1)

        @pl.when(i == 0)
        def _():
            gw_ref[...] = jnp.zeros_like(gw_ref)
            gb_ref[...] = jnp.zeros_like(gb_ref)

        for half, (m_ref, p_ref, n_ref) in enumerate(((g_ref, gp_ref, gn_ref), (v_ref, vp_ref, vn_ref))):
            buf[half, 0:FF_HALO, :] = jnp.where(i > 0, p_ref[...], 0.0)
            buf[half, FF_HALO:FF_HALO + tm, :] = m_ref[...]
            buf[half, FF_HALO + tm:, :] = n_ref[...]
            for k in range(FF_K - 1):
                lo = FF_HALO - (FF_K - 1) + k
                for r0, rows in _row_chunks(te):
                    shifted[half, k, r0:r0 + rows, :] = buf[half, lo + r0:lo + r0 + rows, :]
        dabuf[0:tm, :] = da_ref[...]
        dabuf[tm:, :] = jnp.where(i < nt - 1, dan_ref[...], 0.0)

        for c0 in range(0, tc, FF_LANES):
            cols = slice(c0, c0 + FF_LANES)
            gb = [jnp.zeros((8, FF_LANES), F32) for _ in range(2)]
            gw = [[jnp.zeros((8, FF_LANES), F32) for _ in range(FF_K)] for _ in range(2)]
            for r0, rows in _row_chunks(te):
                taps = [[shifted[half, 0, r0:r0 + rows, cols], shifted[half, 1, r0:r0 + rows, cols],
                         buf[half, FF_HALO + r0:FF_HALO + r0 + rows, cols]] for half in range(2)]
                gate, val = [b_ref[half:half + 1, cols] + sum(w_ref[k, half:half + 1, cols] * taps[half][k]
                                                              for k in range(FF_K)) for half in range(2)]
                da = dabuf[r0:r0 + rows, cols]
                sg = _sigmoid(gate)
                dup = [da * val * (sg * (1.0 + gate * (1.0 - sg))), da * (gate * sg)]
                for half in range(2):
                    dbuf[half, r0:r0 + rows, cols] = dup[half]
                    if r0 < tm:
                        gb[half] = gb[half] + _fold8(dup[half])
                        for k in range(FF_K):
                            gw[half][k] = gw[half][k] + _fold8(dup[half] * taps[half][k])
            for half, out_ref in enumerate((dg_ref, dv_ref)):
                gb_ref[half:half + 1, cols] += jnp.sum(gb[half], axis=0, keepdims=True)
                for k in range(FF_K):
                    gw_ref[k, half:half + 1, cols] += jnp.sum(gw[half][k], axis=0, keepdims=True)
                for k in range(1, FF_K):
                    for r0, rows in _row_chunks(tm):
                        dshifted[half, k - 1, r0:r0 + rows, cols] = dbuf[half, k + r0:k + r0 + rows, cols]
                for r0, rows in _row_chunks(tm):
                    acc = (w_ref[2, half:half + 1, cols] * dbuf[half, r0:r0 + rows, cols]
                           + w_ref[1, half:half + 1, cols] * dshifted[half, 0, r0:r0 + rows, cols]
                           + w_ref[0, half:half + 1, cols] * dshifted[half, 1, r0:r0 + rows, cols])
                    out_ref[r0:r0 + rows, cols] = acc.astype(BF16)

    main = pl.BlockSpec((tm, tc), lambda j, i: (i, j))
    prev = pl.BlockSpec((FF_HALO, tc), lambda j, i: (jnp.maximum(i * hb - 1, 0), j))
    nxt = pl.BlockSpec((FF_HALO, tc), lambda j, i: (jnp.minimum((i + 1) * hb, last_halo), j))
    (dg, dv, gw, gb), carried = _call(
        body, rider, name="ff_bwd", grid=(f // tc, nt),
        in_specs=[main, prev, nxt, main, prev, nxt, main, nxt,
                  pl.BlockSpec((FF_K, 2, tc), lambda j, i: (0, 0, j)), pl.BlockSpec((2, tc), lambda j, i: (0, j))],
        out_specs=[main, main, pl.BlockSpec((FF_K, 2, tc), lambda j, i: (0, 0, j)),
                   pl.BlockSpec((2, tc), lambda j, i: (0, j))],
        out_shape=[jax.ShapeDtypeStruct((s, f), BF16), jax.ShapeDtypeStruct((s, f), BF16),
                   jax.ShapeDtypeStruct((FF_K, 2, f), F32), jax.ShapeDtypeStruct((2, f), F32)],
        scratch_shapes=[pltpu.VMEM((2, FF_HALO + te, tc), F32), pltpu.VMEM((2, FF_K - 1, te, tc), F32),
                        pltpu.VMEM((2, te, tc), F32), pltpu.VMEM((2, FF_K - 1, tm, tc), F32), pltpu.VMEM((te, tc), F32)],
        compiler_params=_params(("arbitrary", "arbitrary"), VMEM_BIG),
        args=(up_g, up_g, up_g, up_v, up_v, up_v, d_act, d_act, fw, fb))
    return dg, dv, gw, gb, carried


ATT_TILE = Q_BLOCK * max(PATTERN_DILATIONS)


def _stream_rows(start, d, n=Q_BLOCK):
    return pl.ds(start, n) if d == 1 else pl.ds(start, n, stride=d)


def _band_geometry():
    qi = lax.broadcasted_iota(jnp.int32, (Q_BLOCK, 2 * Q_BLOCK), 0)
    ki = lax.broadcasted_iota(jnp.int32, (Q_BLOCK, 2 * Q_BLOCK), 1)
    delta = qi + Q_BLOCK - ki
    return (delta >= 0) & (delta <= Q_BLOCK), delta.astype(F32), ki


def _stream_blocks(d):
    out = []
    for r in range(d):
        for b in range(ATT_TILE // (Q_BLOCK * d)):
            start = b * Q_BLOCK * d + r
            out.append((start, start - Q_BLOCK * d if b > 0 else None))
    return out


def _attn_forward(qkv, slopes, rider=None):
    s = qkv.shape[0]
    nt = s // ATT_TILE
    nhp = ATTN_W // Q_BLOCK

    def body(sl_ref, q_ref, k_ref, kp_ref, v_ref, vp_ref, o_ref, l_ref):
        hp, i = pl.program_id(0), pl.program_id(1)
        head0 = lax.broadcasted_iota(jnp.int32, (Q_BLOCK, Q_BLOCK), 1) < HEAD_DIM
        head0_k = lax.broadcasted_iota(jnp.int32, (2 * Q_BLOCK, Q_BLOCK), 1) < HEAD_DIM
        valid, dist, ki = _band_geometry()
        first_key = jnp.where(i == 0, Q_BLOCK, 0)
        for d in PATTERN_DILATIONS:
            biases = [jnp.where(valid, dist * (-sl_ref[2 * hp + hh] * d), NEG) for hh in range(2)]
            for start, prev in _stream_blocks(d):
                rows = _stream_rows(start, d)
                if prev is None:
                    prow = _stream_rows(ATT_TILE - Q_BLOCK * d + start, d)
                    kp, vp = kp_ref[prow, :], vp_ref[prow, :]
                else:
                    prow = _stream_rows(prev, d)
                    kp, vp = k_ref[prow, :], v_ref[prow, :]
                qv = q_ref[rows, :].astype(BF16)
                k2 = jnp.concatenate([kp, k_ref[rows, :]], axis=0).astype(BF16)
                v2 = jnp.concatenate([vp, v_ref[rows, :]], axis=0).astype(BF16)
                res, mxs = [], []
                for hh in range(2):
                    mine = head0 if hh == 0 else jnp.logical_not(head0)
                    mine_k = head0_k if hh == 0 else jnp.logical_not(head0_k)
                    sc = _dot(jnp.where(mine, qv, jnp.zeros_like(qv)), k2, trans_b=True) + biases[hh]
                    if prev is None:
                        sc = jnp.where(ki < first_key, NEG, sc)
                    mx = jnp.max(sc, axis=1, keepdims=True)
                    p = jnp.exp(sc - mx).astype(BF16)
                    res.append(_dot(p, jnp.where(mine_k, v2, jnp.ones_like(v2))))
                    mxs.append(mx)
                num = jnp.where(head0, res[0], res[1])
                den = pltpu.roll(jnp.where(head0, res[1], res[0]), HEAD_DIM, 1)
                o_new = num / den
                l_new = jnp.where(head0, mxs[0], mxs[1]) + jnp.log(den)
                if d != PATTERN_DILATIONS[0]:
                    oa, la = o_ref[rows, :], l_ref[rows, :]
                    mm = jnp.maximum(la, l_new)
                    wa, wn = jnp.exp(la - mm), jnp.exp(l_new - mm)
                    o_new = (wa * oa + wn * o_new) / (wa + wn)
                    l_new = mm + jnp.log(wa + wn)
                o_ref[rows, :] = o_new
                l_ref[rows, :] = l_new

    def col(off):
        return pl.BlockSpec((ATT_TILE, Q_BLOCK), lambda hp, i: (i, off + hp))

    def col_prev(off):
        return pl.BlockSpec((ATT_TILE, Q_BLOCK), lambda hp, i: (jnp.maximum(i - 1, 0), off + hp))

    outs, carried = _call(
        body, rider, name="attn_fwd", grid=(nhp, nt),
        in_specs=[pl.BlockSpec(memory_space=pltpu.SMEM), col(0), col(nhp), col_prev(nhp), col(2 * nhp), col_prev(2 * nhp)],
        out_specs=[col(0), col(0)],
        out_shape=[jax.ShapeDtypeStruct((s, ATTN_W), F32)] * 2,
        compiler_params=_params(("arbitrary", "arbitrary"), VMEM_BIG),
        args=(slopes, qkv, qkv, qkv, qkv, qkv))
    return (*outs, carried)


def _attn_backward(qkv, qk, qg, kg, d_cat, o, lse, slopes, ones2, rider=None):
    s = qkv.shape[0]
    nt = s // ATT_TILE
    nhp = ATTN_W // Q_BLOCK
    tt = ATT_TILE

    def body(sl_ref, q_ref, k_ref, kp_ref, v_ref, vp_ref, do_ref, o_ref, l_ref, ones_ref, qraw_ref, kraw_ref,
             qg_ref, kg_ref, dq_ref, dk_ref, dv_ref, gq_ref, gk_ref, dkacc, dvacc, dd, dqacc):
        hp, step = pl.program_id(0), pl.program_id(1)
        tile = nt - 1 - step
        head0 = lax.broadcasted_iota(jnp.int32, (Q_BLOCK, Q_BLOCK), 1) < HEAD_DIM
        valid, dist, ki = _band_geometry()
        first_key = jnp.where(tile == 0, Q_BLOCK, 0)

        @pl.when((step == 0) & (hp == 0))
        def _():
            gq_ref[...] = jnp.zeros_like(gq_ref)
            gk_ref[...] = jnp.zeros_like(gk_ref)

        @pl.when(step == 0)
        def _():
            dkacc[tt:, :] = jnp.zeros((tt, Q_BLOCK), F32)
            dvacc[tt:, :] = jnp.zeros((tt, Q_BLOCK), F32)

        @pl.when(step > 0)
        def _():
            dkacc[tt:, :] = dkacc[:tt, :]
            dvacc[tt:, :] = dvacc[:tt, :]

        dkacc[:tt, :] = jnp.zeros((tt, Q_BLOCK), F32)
        dvacc[:tt, :] = jnp.zeros((tt, Q_BLOCK), F32)
        dd[...] = _head_sum(do_ref[...] * o_ref[...], ones_ref)

        ki2 = jnp.concatenate([ki, ki], axis=0)
        for d in PATTERN_DILATIONS:
            bias2 = jnp.concatenate([jnp.where(valid, dist * (-sl_ref[2 * hp + hh] * d), NEG) for hh in range(2)], axis=0)
            for start, prev in _stream_blocks(d):
                rows = _stream_rows(start, d)
                if prev is None:
                    prow = _stream_rows(ATT_TILE - Q_BLOCK * d + start, d)
                    kp, vp = kp_ref[prow, :], vp_ref[prow, :]
                else:
                    prow = _stream_rows(prev, d)
                    kp, vp = k_ref[prow, :], v_ref[prow, :]
                acc_rows = _stream_rows(tt + start - Q_BLOCK * d, d, 2 * Q_BLOCK)
                qv = q_ref[rows, :].astype(BF16)
                dov = do_ref[rows, :].astype(BF16)
                lv, ddv = l_ref[rows, :], dd[rows, :]
                lsw, dsw = pltpu.roll(lv, HEAD_DIM, 1), pltpu.roll(ddv, HEAD_DIM, 1)
                k2 = jnp.concatenate([kp, k_ref[rows, :]], axis=0).astype(BF16)
                v2 = jnp.concatenate([vp, v_ref[rows, :]], axis=0).astype(BF16)
                zq = jnp.zeros_like(qv)
                q2 = jnp.concatenate([jnp.where(head0, qv, zq), jnp.where(head0, zq, qv)], axis=0)
                do2 = jnp.concatenate([jnp.where(head0, dov, zq), jnp.where(head0, zq, dov)], axis=0)
                lh = jnp.concatenate([jnp.where(head0, lv, lsw), jnp.where(head0, lsw, lv)], axis=0)
                dh = jnp.concatenate([jnp.where(head0, ddv, dsw), jnp.where(head0, dsw, ddv)], axis=0)
                sc = _dot(q2, k2, trans_b=True) + bias2
                if prev is None:
                    sc = jnp.where(ki2 < first_key, NEG, sc)
                p = jnp.exp(sc - jnp.concatenate([lh, lh], axis=1))
                ds = p * (_dot(do2, v2, trans_b=True) - jnp.concatenate([dh, dh], axis=1))
                dq2 = _dot(ds.astype(BF16), k2)
                dk2 = _dot(ds.T.astype(BF16), q2)
                dv2 = _dot(p.T.astype(BF16), do2)
                dq = jnp.where(head0, dq2[:Q_BLOCK], dq2[Q_BLOCK:])
                if d == PATTERN_DILATIONS[0]:
                    dqacc[rows, :] = dq
                else:
                    dqacc[rows, :] += dq
                dkacc[acc_rows, :] += dk2
                dvacc[acc_rows, :] += dv2
        dv_ref[...] = dvacc[tt:, :].astype(BF16)

        def norm_bwd(dy, raw, g, scale):
            r = lax.rsqrt(_head_sum(raw * raw, ones_ref) * (1.0 / HEAD_DIM) + EPS)
            xh = raw * r
            gd = dy * (g * scale)
            dx = r * (gd - xh * (_head_sum(gd * xh, ones_ref) * (1.0 / HEAD_DIM)))
            return dx, _fold8(dy * xh) * scale

        gq, gk = jnp.zeros((SUBLANES, Q_BLOCK), F32), jnp.zeros((SUBLANES, Q_BLOCK), F32)
        for r0, n in _chunks(tt, 256):
            dxq, pq = norm_bwd(dqacc[r0:r0 + n, :], qraw_ref[r0:r0 + n, :], qg_ref[...], HEAD_DIM ** -0.5)
            dxk, pk = norm_bwd(dkacc[tt + r0:tt + r0 + n, :], kraw_ref[r0:r0 + n, :], kg_ref[...], 1.0)
            dq_ref[r0:r0 + n, :] = dxq.astype(BF16)
            dk_ref[r0:r0 + n, :] = dxk.astype(BF16)
            gq, gk = gq + pq, gk + pk
        for acc, out in ((gq, gq_ref), (gk, gk_ref)):
            row = jnp.sum(acc, axis=0, keepdims=True)
            out[...] += row + pltpu.roll(row, HEAD_DIM, 1)

    def col(off):
        return pl.BlockSpec((tt, Q_BLOCK), lambda hp, st: (nt - 1 - st, off + hp))

    def col_prev(off):
        return pl.BlockSpec((tt, Q_BLOCK), lambda hp, st: (jnp.maximum(nt - 2 - st, 0), off + hp))

    gain = pl.BlockSpec((1, Q_BLOCK), lambda hp, st: (0, hp))
    total = pl.BlockSpec((1, Q_BLOCK), lambda hp, st: (0, 0))
    outs, carried = _call(
        body, rider, name="attn_bwd", grid=(nhp, nt),
        in_specs=[pl.BlockSpec(memory_space=pltpu.SMEM), col(0), col(nhp), col_prev(nhp), col(2 * nhp), col_prev(2 * nhp),
                  col(nhp), col(0), col(0), pl.BlockSpec((Q_BLOCK, Q_BLOCK), lambda hp, st: (0, 0)),
                  col(0), col(nhp), gain, gain],
        out_specs=[col(0)] * 3 + [total, total],
        out_shape=[jax.ShapeDtypeStruct((s, ATTN_W), BF16)] * 3 + [jax.ShapeDtypeStruct((1, Q_BLOCK), F32)] * 2,
        scratch_shapes=[pltpu.VMEM((2 * tt, Q_BLOCK), F32)] * 2 + [pltpu.VMEM((tt, Q_BLOCK), F32)] * 2,
        compiler_params=_params(("arbitrary", "arbitrary"), VMEM_BIG),
        args=(slopes, qkv, qkv, qkv, qkv, qkv, d_cat, o, lse, ones2, qk, qk, qg, kg))
    return (*outs, carried)


def _attn_geometry(s, d):
    length = s // d
    tq = _tile(length, 1024, Q_BLOCK)
    return length, d * ATTN_W, tq, length // tq, tq // Q_BLOCK


def _band(slope_ref, head_pair, d):
    qi = lax.broadcasted_iota(jnp.int32, (Q_BLOCK, 2 * Q_BLOCK), 0)
    ki = lax.broadcasted_iota(jnp.int32, (Q_BLOCK, 2 * Q_BLOCK), 1)
    delta = qi + Q_BLOCK - ki
    valid = (delta >= 0) & (delta <= Q_BLOCK)
    dist = delta.astype(F32) * float(d)
    return [jnp.where(valid, dist * (-slope_ref[2 * head_pair + hh]), NEG) for hh in range(2)], ki


def _attn_fwd(qn, kn, v, slopes, d, prev):
    s = qn.shape[0]
    length, width, tq, nt, nqb = _attn_geometry(s, d)
    view = lambda t: t.reshape(length, width)
    merge = prev is not None

    def body(*refs):
        sl_ref, q_ref, k_ref, kh_ref, v_ref, vh_ref = refs[:6]
        if merge:
            oa_ref, la_ref = refs[6:8]
        o_ref, l_ref, kbuf, vbuf = refs[-4:]
        cb, i = pl.program_id(0), pl.program_id(1)
        biases, ki = _band(sl_ref, cb % 4, d)
        head0 = lax.broadcasted_iota(jnp.int32, (Q_BLOCK, Q_BLOCK), 1) < HEAD_DIM
        kbuf[0:Q_BLOCK, :] = kh_ref[...]
        kbuf[Q_BLOCK:, :] = k_ref[...]
        vbuf[0:Q_BLOCK, :] = vh_ref[...]
        vbuf[Q_BLOCK:, :] = v_ref[...]

        def q_block(qb, carry):
            r0 = pl.multiple_of(qb * Q_BLOCK, Q_BLOCK)
            rows = pl.ds(r0, Q_BLOCK)
            qv = q_ref[rows, :]
            k2, v2 = kbuf[pl.ds(r0, 2 * Q_BLOCK), :], vbuf[pl.ds(r0, 2 * Q_BLOCK), :]
            first_key = jnp.where((i == 0) & (qb == 0), Q_BLOCK, 0)
            o_h, l_h = [], []
            for hh in range(2):
                mine = head0 if hh == 0 else jnp.logical_not(head0)
                sc = _dot(jnp.where(mine, qv, jnp.zeros_like(qv)), k2, trans_b=True) + biases[hh]
                sc = jnp.where(ki < first_key, NEG, sc)
                mx = jnp.max(sc, axis=1, keepdims=True)
                p = jnp.exp(sc - mx)
                den = jnp.sum(p, axis=1, keepdims=True)
                o_h.append(_dot(p.astype(BF16), v2) * (1.0 / den))
                l_h.append(jnp.broadcast_to(mx + jnp.log(den), (Q_BLOCK, Q_BLOCK)))
            o_new = jnp.where(head0, o_h[0], o_h[1])
            l_new = jnp.where(head0, l_h[0], l_h[1])
            if merge:
                oa, la = oa_ref[rows, :], la_ref[rows, :]
                mm = jnp.maximum(la, l_new)
                wa, wn = jnp.exp(la - mm), jnp.exp(l_new - mm)
                o_new = (wa * oa + wn * o_new) / (wa + wn)
                l_new = mm + jnp.log(wa + wn)
            o_ref[rows, :] = o_new
            l_ref[rows, :] = l_new
            return carry

        lax.fori_loop(0, nqb, q_block, 0)

    main = pl.BlockSpec((tq, Q_BLOCK), lambda cb, i: (i, cb))
    halo = pl.BlockSpec((Q_BLOCK, Q_BLOCK), lambda cb, i: (jnp.maximum(i * nqb - 1, 0), cb))
    ins = [slopes, view(qn), view(kn), view(kn), view(v), view(v)] + ([view(prev[0]), view(prev[1])] if merge else [])
    o, lse = pl.pallas_call(
        body, name=f"attn_fwd_d{d}", grid=(width // Q_BLOCK, nt),
        in_specs=[pl.BlockSpec(memory_space=pltpu.SMEM), main, main, halo, main, halo] + ([main, main] if merge else []),
        out_specs=[main, main],
        out_shape=[jax.ShapeDtypeStruct((length, width), F32)] * 2,
        scratch_shapes=[pltpu.VMEM((Q_BLOCK + tq, Q_BLOCK), BF16)] * 2,
        compiler_params=_params(("parallel", "parallel"), VMEM_BIG),
    )(*ins)
    return o.reshape(s, ATTN_W), lse.reshape(s, ATTN_W)


def _attn_bwd_prep(d_cat, o, ones):
    s = o.shape[0]
    tm = _tile(s, 512, 8)

    def body(do_ref, o_ref, ones_ref, dob_ref, dd_ref):
        do = do_ref[...]
        dob_ref[...] = do.astype(BF16)
        dd_ref[...] = _head_sum(do * o_ref[...], ones_ref)

    row = pl.BlockSpec((tm, ATTN_W), lambda i: (i, 0))
    return pl.pallas_call(
        body, name="attn_bwd_prep", grid=(s // tm,),
        in_specs=[pl.BlockSpec((tm, ATTN_W), lambda i: (i, 1)), row, pl.BlockSpec(ones.shape, lambda i: (0, 0))],
        out_specs=[row, row],
        out_shape=[jax.ShapeDtypeStruct((s, ATTN_W), BF16), jax.ShapeDtypeStruct((s, ATTN_W), F32)],
        compiler_params=_params(("parallel",), VMEM_BIG),
    )(d_cat, o, ones)


def _attn_bwd(qn, kn, v, dob, lse, dd, slopes, d, acc):
    s = qn.shape[0]
    length, width, tq, nt, nqb = _attn_geometry(s, d)
    view = lambda t: t.reshape(length, width)
    add = acc is not None

    def body(*refs):
        sl_ref, q_ref, k_ref, kh_ref, v_ref, vh_ref, do_ref, l_ref, dd_ref = refs[:9]
        if add:
            dqa_ref, dka_ref, dva_ref = refs[9:12]
        dq_ref, dk_ref, dv_ref, kbuf, vbuf, dkacc, dvacc, kcarry, vcarry = refs[-9:]
        cb, step = pl.program_id(0), pl.program_id(1)
        tile = nt - 1 - step
        biases, ki = _band(sl_ref, cb % 4, d)
        head0 = lax.broadcasted_iota(jnp.int32, (Q_BLOCK, Q_BLOCK), 1) < HEAD_DIM
        kbuf[0:Q_BLOCK, :] = kh_ref[...]
        kbuf[Q_BLOCK:, :] = k_ref[...]
        vbuf[0:Q_BLOCK, :] = vh_ref[...]
        vbuf[Q_BLOCK:, :] = v_ref[...]
        dkacc[...] = jnp.zeros_like(dkacc)
        dvacc[...] = jnp.zeros_like(dvacc)

        @pl.when(step > 0)
        def _():
            dkacc[tq:, :] = kcarry[...]
            dvacc[tq:, :] = vcarry[...]

        def q_block(qb, carry):
            r0 = pl.multiple_of(qb * Q_BLOCK, Q_BLOCK)
            rows, rows2 = pl.ds(r0, Q_BLOCK), pl.ds(r0, 2 * Q_BLOCK)
            qv, dov = q_ref[rows, :], do_ref[rows, :]
            lv, ddv = l_ref[rows, :], dd_ref[rows, :]
            k2, v2 = kbuf[rows2, :], vbuf[rows2, :]
            first_key = jnp.where((tile == 0) & (qb == 0), Q_BLOCK, 0)
            dq_h = []
            dk2 = jnp.zeros((2 * Q_BLOCK, Q_BLOCK), F32)
            dv2 = jnp.zeros((2 * Q_BLOCK, Q_BLOCK), F32)
            for hh in range(2):
                mine = head0 if hh == 0 else jnp.logical_not(head0)
                qm = jnp.where(mine, qv, jnp.zeros_like(qv))
                dom = jnp.where(mine, dov, jnp.zeros_like(dov))
                lcol = jnp.max(jnp.where(mine, lv, -jnp.inf), axis=1, keepdims=True)
                dcol = jnp.max(jnp.where(mine, ddv, -jnp.inf), axis=1, keepdims=True)
                sc = _dot(qm, k2, trans_b=True) + biases[hh]
                sc = jnp.where(ki < first_key, NEG, sc)
                p = jnp.exp(sc - lcol)
                ds = p * (_dot(dom, v2, trans_b=True) - dcol)
                dq_h.append(_dot(ds.astype(BF16), k2))
                dk2 = dk2 + _dot(ds.T.astype(BF16), qm)
                dv2 = dv2 + _dot(p.T.astype(BF16), dom)
            dq = jnp.where(head0, dq_h[0], dq_h[1])
            if add:
                dq = dq + dqa_ref[rows, :]
            dq_ref[rows, :] = dq
            dkacc[rows2, :] += dk2
            dvacc[rows2, :] += dv2
            return carry

        lax.fori_loop(0, nqb, q_block, 0)
        dk, dv = dkacc[Q_BLOCK:, :], dvacc[Q_BLOCK:, :]
        if add:
            dk, dv = dk + dka_ref[...], dv + dva_ref[...]
        dk_ref[...] = dk
        dv_ref[...] = dv
        kcarry[...] = dkacc[0:Q_BLOCK, :]
        vcarry[...] = dvacc[0:Q_BLOCK, :]

    main = pl.BlockSpec((tq, Q_BLOCK), lambda cb, st: (nt - 1 - st, cb))
    halo = pl.BlockSpec((Q_BLOCK, Q_BLOCK), lambda cb, st: (jnp.maximum((nt - 1 - st) * nqb - 1, 0), cb))
    ins = [slopes, view(qn), view(kn), view(kn), view(v), view(v), view(dob), view(lse), view(dd)]
    ins += [view(t) for t in acc] if add else []
    outs = pl.pallas_call(
        body, name=f"attn_bwd_d{d}", grid=(width // Q_BLOCK, nt),
        in_specs=[pl.BlockSpec(memory_space=pltpu.SMEM), main, main, halo, main, halo, main, main, main]
        + ([main] * 3 if add else []),
        out_specs=[main] * 3,
        out_shape=[jax.ShapeDtypeStruct((length, width), F32)] * 3,
        scratch_shapes=[pltpu.VMEM((Q_BLOCK + tq, Q_BLOCK), BF16)] * 2 + [pltpu.VMEM((Q_BLOCK + tq, Q_BLOCK), F32)] * 2
        + [pltpu.VMEM((Q_BLOCK, Q_BLOCK), F32)] * 2,
        compiler_params=_params(("parallel", "arbitrary"), VMEM_BIG),
    )(*ins)
    return tuple(t.reshape(s, ATTN_W) for t in outs)


def _attn_bwd_post(dq, dk, dv, qk, qg, kg, ones):
    s = dq.shape[0]
    tm = _tile(s, 512, 8)
    aw = ATTN_W

    def body(dq_ref, dk_ref, dv_ref, qk_ref, qg_ref, kg_ref, ones_ref, out_ref, gq_ref, gk_ref):
        @pl.when(pl.program_id(0) == 0)
        def _():
            gq_ref[...] = jnp.zeros_like(gq_ref)
            gk_ref[...] = jnp.zeros_like(gk_ref)

        def norm_bwd(dy, raw, g, scale):
            r = lax.rsqrt(_head_sum(raw * raw, ones_ref) * (1.0 / HEAD_DIM) + EPS)
            xh = raw * r
            gd = dy * (g * scale)
            dx = r * (gd - xh * (_head_sum(gd * xh, ones_ref) * (1.0 / HEAD_DIM)))
            gsum = jnp.sum(dy * xh, axis=0, keepdims=True) * scale
            for shift in (256, 128, 64):
                gsum = gsum + pltpu.roll(gsum, shift, 1)
            return dx, gsum

        dxq, gq = norm_bwd(dq_ref[...], qk_ref[:, :aw], qg_ref[...], HEAD_DIM ** -0.5)
        dxk, gk = norm_bwd(dk_ref[...], qk_ref[:, aw:], kg_ref[...], 1.0)
        out_ref[:, :aw] = dxq.astype(BF16)
        out_ref[:, aw:2 * aw] = dxk.astype(BF16)
        out_ref[:, 2 * aw:] = dv_ref[...].astype(BF16)
        gq_ref[...] += gq
        gk_ref[...] += gk

    row = lambda w: pl.BlockSpec((tm, w), lambda i: (i, 0))
    full = lambda t: pl.BlockSpec(t.shape, lambda i: (0, 0))
    vec = pl.BlockSpec((1, aw), lambda i: (0, 0))
    return pl.pallas_call(
        body, name="attn_bwd_post", grid=(s // tm,),
        in_specs=[row(aw), row(aw), row(aw), row(2 * aw), full(qg), full(kg), full(ones)],
        out_specs=[row(3 * aw), vec, vec],
        out_shape=[jax.ShapeDtypeStruct((s, 3 * aw), BF16), jax.ShapeDtypeStruct((1, aw), F32),
                   jax.ShapeDtypeStruct((1, aw), F32)],
        compiler_params=_params(("arbitrary",), VMEM_BIG),
    )(dq, dk, dv, qk, qg, kg, ones)


def _sum_parts(name, parts):
    _, n, w = parts.shape
    tn = _tile(n, 256, 8)

    def body(p_ref, o_ref):
        acc = p_ref[0].astype(F32)
        for j in range(1, N_DEV):
            acc = acc + p_ref[j].astype(F32)
        o_ref[...] = acc

    return pl.pallas_call(
        body, name=name, grid=(n // tn,),
        in_specs=[pl.BlockSpec((N_DEV, tn, w), lambda i: (0, i, 0))],
        out_specs=pl.BlockSpec((tn, w), lambda i: (i, 0)),
        out_shape=jax.ShapeDtypeStruct((n, w), F32),
        compiler_params=_params(("parallel",), VMEM_BIG),
    )(parts)


def _adamw(name, w, g, m, v):
    n, cols = w.shape
    tn = _tile(n, 256, 8)
    c1 = 1.0 - ADAM_B1 ** ADAM_STEP
    c2 = 1.0 - ADAM_B2 ** ADAM_STEP

    def body(w_ref, g_ref, m_ref, v_ref, d_ref, nm_ref, nv_ref):
        gv = g_ref[...]
        nm = ADAM_B1 * m_ref[...] + (1.0 - ADAM_B1) * gv
        nv = ADAM_B2 * v_ref[...] + (1.0 - ADAM_B2) * (gv * gv)
        nm_ref[...] = nm
        nv_ref[...] = nv
        d_ref[...] = -ADAM_LR * ((nm / c1) / (jnp.sqrt(nv / c2) + ADAM_EPS) + ADAM_WD * w_ref[...])

    blk = pl.BlockSpec((tn, cols), lambda i: (i, 0))
    return pl.pallas_call(
        body, name=name, grid=(n // tn,),
        in_specs=[blk] * 4, out_specs=[blk] * 3,
        out_shape=[jax.ShapeDtypeStruct((n, cols), F32)] * 3,
        compiler_params=_params(("parallel",), VMEM_BIG),
    )(w, g, m, v)


def _pack(vectors, width):
    flat = jnp.concatenate([t.reshape(-1) for t in vectors])
    rows = -(-flat.shape[0] // (8 * width)) * 8
    return jnp.pad(flat, (0, rows * width - flat.shape[0])).reshape(rows, width)


def _unpack(packed, shapes):
    flat = packed.reshape(-1)
    out, at = [], 0
    for shp in shapes:
        size = 1
        for dim in shp:
            size *= dim
        out.append(flat[at:at + size].reshape(shp))
        at += size
    return out


def kernel(x, norm1_g, w_in, conv_w, conv_b, cn_g, cn_b, q_norm_g, k_norm_g, w_out, norm2_g, w_up, ffconv_w, ffconv_b, w_down, loss_target, m_norm1_g, m_w_in, m_conv_w, m_conv_b, m_cn_g, m_cn_b, m_q_norm_g, m_k_norm_g, m_w_out, m_norm2_g, m_w_up, m_ffconv_w, m_ffconv_b, m_w_down, v_norm1_g, v_w_in, v_conv_w, v_conv_b, v_cn_g, v_cn_b, v_q_norm_g, v_k_norm_g, v_w_out, v_norm2_g, v_w_up, v_ffconv_w, v_ffconv_b, v_w_down):
    s, d = x.shape[1], x.shape[2]
    f = w_down.shape[0] * N_DEV
    n_in = w_in.shape[1] * N_DEV
    xs, target = x.reshape(s, d), loss_target.reshape(s, d)
    me = _linear((lax.axis_index("x"), lax.axis_index("y"), lax.axis_index("c")))

    row = lambda t: t.reshape(1, -1)
    g1, g2 = row(norm1_g), row(norm2_g)
    h, (g_in, g_filt) = _norm_fwd(xs, g1, _Gather([w_in.T.astype(BF16), _pack([conv_w, ffconv_w], 128)]))
    w_in_t = g_in.reshape(n_in, d)
    filt = g_filt.reshape(N_DEV, -1)
    n_cw = conv_w.size
    cw = filt[:, :n_cw].reshape(N_DEV, CONV_K, -1).transpose(1, 0, 2).reshape(CONV_K, CONV_CH)
    cw = jnp.pad(cw, ((0, CONV_HALO - CONV_K), (0, 0)))
    fw = filt[:, n_cw:n_cw + ffconv_w.size].reshape(N_DEV, FF_K, -1).transpose(1, 0, 2).reshape(FF_K, 2, f)
    fb = ffconv_b.reshape(2, f)
    qg, kg =row(jnp.tile(q_norm_g, N_HEADS)), row(jnp.tile(k_norm_g, N_HEADS))
    lanes = jnp.arange(ATTN_W) // HEAD_DIM
    ones = (lanes[:, None] == lanes[None, :]).astype(BF16)
    slopes = 2.0 ** (-8.0 * jnp.arange(1, N_HEADS + 1, dtype=F32) / N_HEADS)

    a, qk, qkv, (g_out,) = _proj_fwd(h, w_in_t, qg, kg, ones, _Gather([w_out.astype(BF16)]))
    u1, u, (g_up,) = _conv_fwd(a, cw, row(conv_b), row(cn_g), row(cn_b), _Gather([w_up.T.astype(BF16)]))
    o, lse, (g_down,) = _attn_forward(qkv, slopes, _Gather([w_down.astype(BF16)]))
    w_out_f = g_out.reshape(2 * CONV_CH, d)
    w_up_t = g_up.reshape(2 * f, d)
    w_down_f = g_down.reshape(f, d)
    x1, h2 = _mix_out(u, o, w_out_f, xs, g2)
    up_g, up_v, act = _up_fwd_act(h2, w_up_t, fw, fb, f)
    loss_acc, dyf, dyb = _down_loss(act, w_down_f, x1, target)

    tk, tk_wide = _tile(s, 1024, 8), _tile(s, 512, 8)
    gw_down = _mm_tn("grad_w_down", [[act]], dyb, tk)
    d_act = _mm("d_act", dyb, w_down_f, True, F32, _tile(s, 512, 8), _tile(f, 1408, 128))
    blocks = lambda t: _Exchange([t.reshape(N_DEV, t.shape[0] // N_DEV, d)])
    dpre_g, dpre_v, gfw, gfb, (r_down,) = _ff_bwd(up_g, up_v, d_act, fw, fb, blocks(gw_down))
    gw_up = _mm_tn("grad_w_up", [[dpre_g], [dpre_v]], h2, tk_wide)
    dx1f, dx1b, gg2 = _up_bwd(dpre_g, dpre_v, w_up_t, x1, dyf, g2)
    gw_out = _mm_tn("grad_w_out", [[u, o]], dx1b, tk)
    d_cat = _mm("d_cat", dx1b, w_out_f, True, F32, _tile(s, 512, 8), 2 * CONV_CH)
    d_a, gcw, gcb, gcg, gcbeta, (r_out,) = _conv_bwd(a, u1, d_cat, cw, row(cn_g), row(cn_b), blocks(gw_out))
    dq, dk, dv, gqg, gkg, (r_up,) = _attn_backward(qkv, qk, qg, kg, d_cat, o, lse, slopes, ones[:Q_BLOCK, :Q_BLOCK],
                                                   blocks(gw_up))
    d_proj = [d_a, dq, dk, dv]
    gw_in = _mm_tn("grad_w_in", [d_proj], h, tk)
    grad_x, gg1, (r_in,) = _proj_bwd(d_proj, w_in_t, xs, dx1f, g1, blocks(gw_in))
    g_w_in_t, g_w_out, g_w_up_t, g_w_down = [_sum_parts(f"sum_grad_{i}", r) for i, r in enumerate((r_in, r_out, r_up, r_down))]
    g_w_in, g_w_up = g_w_in_t.T, g_w_up_t.T

    small_shapes = [(d,), (d,), (CONV_CH,), (CONV_CH,), (CONV_CH,), (Q_BLOCK,), (Q_BLOCK,), (2 * f,),
                    (CONV_K, CONV_CH), (FF_K, 2 * f), (1,)]
    small = _pack([gg1, gg2, gcb, gcg, gcbeta, gqg, gkg, gfb, gcw[:CONV_K], gfw, loss_acc[0:1, 0:1]], 1024)
    (small_all,) = _communicate("gather_small_grads", _Gather([small]))
    sg1, sg2, scb, scg, scbeta, sqg, skg, sfb, scw, sfw, loss_sum = _unpack(
        _sum_parts("sum_small_grads", small_all), small_shapes)
    loss = loss_sum[0]
    cwl, fwl = conv_w.shape[1], ffconv_w.shape[1]
    g_small = [sg1, lax.dynamic_slice_in_dim(scw, me * cwl, cwl, 1), scb, scg, scbeta,
               sqg[:HEAD_DIM], skg[:HEAD_DIM], sg2, lax.dynamic_slice_in_dim(sfw, me * fwl, fwl, 1), sfb]

    w_small = [norm1_g, conv_w, conv_b, cn_g, cn_b, q_norm_g, k_norm_g, norm2_g, ffconv_w, ffconv_b]
    m_small = [m_norm1_g, m_conv_w, m_conv_b, m_cn_g, m_cn_b, m_q_norm_g, m_k_norm_g, m_norm2_g, m_ffconv_w, m_ffconv_b]
    v_small = [v_norm1_g, v_conv_w, v_conv_b, v_cn_g, v_cn_b, v_q_norm_g, v_k_norm_g, v_norm2_g, v_ffconv_w, v_ffconv_b]
    shapes = [t.shape for t in w_small]
    packed = _adamw("adamw_small", *[_pack(ts, 128) for ts in (w_small, g_small, m_small, v_small)])
    d_small, nm_small, nv_small = [_unpack(p, shapes) for p in packed]
    upd = {}
    for name, wt, gt, mt, vt in (("w_in", w_in, g_w_in, m_w_in, v_w_in), ("w_out", w_out, g_w_out, m_w_out, v_w_out),
                                 ("w_up", w_up, g_w_up, m_w_up, v_w_up), ("w_down", w_down, g_w_down, m_w_down, v_w_down)):
        upd[name] = (gt,) + tuple(_adamw("adamw_" + name, wt, gt, mt, vt))
    order = ["norm1_g", "w_in", "conv_w", "conv_b", "cn_g", "cn_b", "q_norm_g", "k_norm_g", "w_out", "norm2_g",
             "w_up", "ffconv_w", "ffconv_b", "w_down"]
    small_names = ["norm1_g", "conv_w", "conv_b", "cn_g", "cn_b", "q_norm_g", "k_norm_g", "norm2_g", "ffconv_w", "ffconv_b"]
    for i, name in enumerate(small_names):
        upd[name] = (g_small[i], d_small[i], nm_small[i], nv_small[i])
    outs = [loss, grad_x.reshape(x.shape)]
    for field in range(4):
        outs += [upd[name][field] for name in order]
    return tuple(outs)
```

```python
import jax
import jax.numpy as jnp
from jax import lax
from jax.experimental import pallas as pl
from jax.experimental.pallas import tpu as pltpu

F32, BF16 = jnp.float32, jnp.bfloat16
N_DEV = 8
N_HEADS, HEAD_DIM = 8, 64
CONV_CH = 512
ATTN_W = N_HEADS * HEAD_DIM
CONV_K, FF_K = 31, 3
CONV_HALO = 32
FF_HALO = 8
PATTERN_DILATIONS = (16, 4, 1)
Q_BLOCK = 128
EPS = 1e-6
NEG = -1e30
ADAM_LR, ADAM_B1, ADAM_B2, ADAM_EPS, ADAM_WD, ADAM_STEP = 0.001, 0.9, 0.999, 1e-08, 0.01, 10
VMEM_BIG = 56 * 1024 * 1024
ANY = pl.BlockSpec(memory_space=pl.ANY)


def _params(sem=None, vmem=None):
    return pltpu.CompilerParams(dimension_semantics=sem, vmem_limit_bytes=vmem)


def _dot(a, b, trans_a=False, trans_b=False):
    dims = (((0 if trans_a else 1,), (1 if trans_b else 0,)), ((), ()))
    return lax.dot_general(a, b, dims, preferred_element_type=F32)


def _sigmoid(z):
    return 1.0 / (1.0 + jnp.exp(-z))


def _tile(n, pref, mult):
    if n <= pref:
        return n
    t = (pref // mult) * mult
    while n % t:
        t -= mult
    return t


def _head_sum(v, ones_ref):
    hi = v.astype(BF16)
    lo = (v - hi.astype(F32)).astype(BF16)
    ones = ones_ref[...]
    return _dot(hi, ones) + _dot(lo, ones)


def _linear(p):
    return 4 * p[0] + 2 * p[1] + p[2]


class _Gather:
    def __init__(self, arrs):
        self.arrs = list(arrs)
        self.n = len(self.arrs)
        self.out_shape = [jax.ShapeDtypeStruct((N_DEV,) + a.shape, a.dtype) for a in self.arrs]

    def _setup(self, ins, outs, sems):
        send_sems, recv_sems, local_sems = sems
        x, y, c = lax.axis_index("x"), lax.axis_index("y"), lax.axis_index("c")
        me, sib = (x, y, c), (x, y, 1 - c)
        chips = [(1 - x, y), (x, 1 - y), (1 - x, 1 - y)]

        def copy(a, k, block, to, src=None):
            dst = outs[a].at[_linear(block)]
            return pltpu.make_async_remote_copy(
                src_ref=dst if src is None else src, dst_ref=dst,
                send_sem=send_sems.at[7 * a + k], recv_sem=recv_sems.at[7 * a + k],
                device_id=to, device_id_type=pl.DeviceIdType.MESH)

        mine = [pltpu.make_async_copy(ins[a], outs[a].at[_linear(me)], local_sems.at[a]) for a in range(self.n)]
        first = []
        for a in range(self.n):
            first.append(copy(a, 0, me, sib, src=ins[a]))
            first += [copy(a, 1 + j, me, (*chip, c), src=ins[a]) for j, chip in enumerate(chips)]
        return copy, mine, first, me, sib, chips, c

    def start(self, ins, outs, sems):
        _, mine, first, *_ = self._setup(ins, outs, sems)
        for cp in mine + first:
            cp.start()

    def finish(self, ins, outs, sems):
        copy, mine, first, me, sib, chips, c = self._setup(ins, outs, sems)
        passed = []
        for j, chip in enumerate(chips):
            for a in range(self.n):
                copy(a, 1 + j, (*chip, c), me).wait_recv()
                fwd = copy(a, 4 + j, (*chip, c), sib)
                fwd.start()
                passed.append(fwd)
        for a in range(self.n):
            copy(a, 0, sib, me).wait_recv()
            for j, chip in enumerate(chips):
                copy(a, 4 + j, (*chip, 1 - c), me).wait_recv()
        for cp in first + passed:
            cp.wait_send()
        for cp in mine:
            cp.wait()


class _Exchange:
    def __init__(self, arrs):
        self.arrs = list(arrs)
        self.n = len(self.arrs)
        self.out_shape = [jax.ShapeDtypeStruct(a.shape, a.dtype) for a in self.arrs]

    def _setup(self, ins, outs, sems):
        send_sems, recv_sems, local_sems = sems
        x, y, c = lax.axis_index("x"), lax.axis_index("y"), lax.axis_index("c")
        me = _linear((x, y, c))
        mine = [pltpu.make_async_copy(ins[a].at[me], outs[a].at[me], local_sems.at[a]) for a in range(self.n)]
        sends, recvs = [], []
        for p in range(1, N_DEV):
            peer = (1 - x if p & 4 else x, 1 - y if p & 2 else y, 1 - c if p & 1 else c)
            for a in range(self.n):
                sem = dict(send_sem=send_sems.at[7 * a + p - 1], recv_sem=recv_sems.at[7 * a + p - 1],
                           device_id=peer, device_id_type=pl.DeviceIdType.MESH)
                sends.append(pltpu.make_async_remote_copy(
                    src_ref=ins[a].at[_linear(peer)], dst_ref=outs[a].at[me], **sem))
                recvs.append(pltpu.make_async_remote_copy(
                    src_ref=ins[a].at[_linear(peer)], dst_ref=outs[a].at[_linear(peer)], **sem))
        return mine, sends, recvs

    def start(self, ins, outs, sems):
        mine, sends, _ = self._setup(ins, outs, sems)
        for cp in mine + sends:
            cp.start()

    def finish(self, ins, outs, sems):
        mine, sends, recvs = self._setup(ins, outs, sems)
        for cp in recvs:
            cp.wait_recv()
        for cp in sends:
            cp.wait_send()
        for cp in mine:
            cp.wait()


def _rider_scratch(rider):
    return [pltpu.SemaphoreType.DMA((7 * rider.n,)), pltpu.SemaphoreType.DMA((7 * rider.n,)),
            pltpu.SemaphoreType.DMA((rider.n,))]


def _communicate(name, rider):
    na = rider.n

    def body(*refs):
        ins, outs, sems = refs[:na], refs[na:2 * na], refs[2 * na:]
        rider.start(ins, outs, sems)
        rider.finish(ins, outs, sems)

    return pl.pallas_call(
        body, name=name, out_shape=rider.out_shape, in_specs=[ANY] * na, out_specs=[ANY] * na,
        scratch_shapes=_rider_scratch(rider),
    )(*rider.arrs)


def _call(body, rider=None, *, name, grid, in_specs, out_specs, out_shape, scratch_shapes=(), compiler_params, args):
    if rider is None:
        out = pl.pallas_call(body, name=name, grid=grid, in_specs=in_specs, out_specs=out_specs, out_shape=out_shape,
                             scratch_shapes=list(scratch_shapes), compiler_params=compiler_params)(*args)
        return out, None
    n_in, n_out, n_scr, na = len(in_specs), len(out_specs), len(scratch_shapes), rider.n

    def carried(*refs):
        ins, refs = refs[:n_in], refs[n_in:]
        r_ins, refs = refs[:na], refs[na:]
        outs, refs = refs[:n_out], refs[n_out:]
        r_outs, refs = refs[:na], refs[na:]
        scratch, sems = refs[:n_scr], refs[n_scr:]
        ids = [pl.program_id(ax) for ax in range(len(grid))]
        first, last = ids[0] == 0, ids[0] == grid[0] - 1
        for ax in range(1, len(grid)):
            first, last = first & (ids[ax] == 0), last & (ids[ax] == grid[ax] - 1)

        @pl.when(first)
        def _():
            rider.start(r_ins, r_outs, sems)

        body(*ins, *outs, *scratch)

        @pl.when(last)
        def _():
            rider.finish(r_ins, r_outs, sems)

    out = pl.pallas_call(
        carried, name=name, grid=grid, in_specs=list(in_specs) + [ANY] * na, out_specs=list(out_specs) + [ANY] * na,
        out_shape=list(out_shape) + rider.out_shape, scratch_shapes=list(scratch_shapes) + _rider_scratch(rider),
        compiler_params=compiler_params)(*args, *rider.arrs)
    return out[:n_out], out[n_out:]


def _mm(name, a, w, trans_b, out_dtype, tm, tn):
    m, k = a.shape
    n = w.shape[0] if trans_b else w.shape[1]

    def body(a_ref, w_ref, o_ref):
        o_ref[...] = _dot(a_ref[...], w_ref[...], trans_b=trans_b).astype(o_ref.dtype)

    w_spec = pl.BlockSpec((tn, k), lambda j, i: (j, 0)) if trans_b else pl.BlockSpec((k, tn), lambda j, i: (0, j))
    return pl.pallas_call(
        body, name=name, grid=(n // tn, m // tm),
        in_specs=[pl.BlockSpec((tm, k), lambda j, i: (i, 0)), w_spec],
        out_specs=pl.BlockSpec((tm, tn), lambda j, i: (i, j)),
        out_shape=jax.ShapeDtypeStruct((m, n), out_dtype),
        compiler_params=_params(("parallel", "parallel"), VMEM_BIG),
    )(a, w)


def _mm_tn(name, groups, b, tk):
    s, d = b.shape
    parts = [p for grp in groups for p in grp]
    owner = [g for g, grp in enumerate(groups) for _ in grp]
    width = sum(p.shape[1] for p in groups[0])
    offsets = []
    for grp in groups:
        at = 0
        for p in grp:
            offsets.append(at)
            at += p.shape[1]
        assert at == width
    np_, nk = len(parts), s // tk

    def body(*refs):
        a_refs, b_ref, o_ref, acc_ref = refs[:np_], refs[np_], refs[np_ + 1], refs[np_ + 2]
        g, kk = pl.program_id(0), pl.program_id(1)

        @pl.when(kk == 0)
        def _():
            acc_ref[...] = jnp.zeros_like(acc_ref)

        for grp in range(len(groups)):
            @pl.when(g == grp)
            def _(grp=grp):
                for p in range(np_):
                    if owner[p] == grp:
                        rows = slice(offsets[p], offsets[p] + parts[p].shape[1])
                        acc_ref[rows, :] += _dot(a_refs[p][...].astype(BF16), b_ref[...], trans_a=True)

        @pl.when(kk == nk - 1)
        def _():
            o_ref[...] = acc_ref[...].astype(o_ref.dtype)

    def a_spec(p):
        return pl.BlockSpec((tk, parts[p].shape[1]), lambda g, kk: (jnp.where(g == owner[p], kk, 0), 0))

    return pl.pallas_call(
        body, name=name, grid=(len(groups), nk),
        in_specs=[a_spec(p) for p in range(np_)] + [pl.BlockSpec((tk, d), lambda g, kk: (kk, 0))],
        out_specs=pl.BlockSpec((width, d), lambda g, kk: (g, 0)),
        out_shape=jax.ShapeDtypeStruct((len(groups) * width, d), BF16),
        scratch_shapes=[pltpu.VMEM((width, d), F32)],
        compiler_params=_params(("arbitrary", "arbitrary"), VMEM_BIG),
    )(*parts, b)


def _norm_fwd(x, g1, rider=None):
    s, d = x.shape
    tm = _tile(s, 512, 8)

    def body(x_ref, g_ref, h_ref):
        xv = x_ref[...]
        r = lax.rsqrt(jnp.mean(xv * xv, axis=-1, keepdims=True) + EPS)
        h_ref[...] = (xv * r * g_ref[...]).astype(BF16)

    row = pl.BlockSpec((tm, d), lambda i: (i, 0))
    (h,), carried = _call(
        body, rider, name="norm_fwd", grid=(s // tm,),
        in_specs=[row, pl.BlockSpec(g1.shape, lambda i: (0, 0))], out_specs=[row],
        out_shape=[jax.ShapeDtypeStruct((s, d), BF16)],
        compiler_params=_params(("arbitrary",), VMEM_BIG), args=(x, g1))
    return h, carried


def _proj_fwd(h, w_in_t, qg, kg, ones, rider=None):
    s, d = h.shape
    tm = _tile(s, 512, 8)
    c2, aw = 2 * CONV_CH, ATTN_W

    def body(h_ref, w_ref, qg_ref, kg_ref, ones_ref, a_ref, qk_ref, qkv_ref):
        proj = _dot(h_ref[...], w_ref[...], trans_b=True)
        a_ref[...] = proj[:, :c2]
        qk_ref[...] = proj[:, c2:c2 + 2 * aw]
        q, k = proj[:, c2:c2 + aw], proj[:, c2 + aw:c2 + 2 * aw]
        rq = lax.rsqrt(_head_sum(q * q, ones_ref) * (1.0 / HEAD_DIM) + EPS)
        rk = lax.rsqrt(_head_sum(k * k, ones_ref) * (1.0 / HEAD_DIM) + EPS)
        qkv_ref[:, :aw] = q * rq * qg_ref[...] * (HEAD_DIM ** -0.5)
        qkv_ref[:, aw:2 * aw] = k * rk * kg_ref[...]
        qkv_ref[:, 2 * aw:] = proj[:, c2 + 2 * aw:]

    row = lambda w: pl.BlockSpec((tm, w), lambda i: (i, 0))
    full = lambda a: pl.BlockSpec(a.shape, lambda i: (0, 0))
    outs, carried = _call(
        body, rider, name="proj_fwd", grid=(s // tm,),
        in_specs=[row(d), full(w_in_t), full(qg), full(kg), full(ones)],
        out_specs=[row(c2), row(2 * aw), row(3 * aw)],
        out_shape=[jax.ShapeDtypeStruct((s, c2), F32), jax.ShapeDtypeStruct((s, 2 * aw), F32),
                   jax.ShapeDtypeStruct((s, 3 * aw), F32)],
        compiler_params=_params(("arbitrary",), VMEM_BIG),
        args=(h, w_in_t, qg, kg, ones))
    return (*outs, carried)


def _mix_out(u, o, w_out, x, g2):
    s, d = x.shape
    tm = _tile(s, 512, 8)

    def body(u_ref, o_ref, w_ref, x_ref, g_ref, x1_ref, h2_ref):
        x1 = x_ref[...] + _dot(u_ref[...], w_ref[:CONV_CH, :]) + _dot(o_ref[...].astype(BF16), w_ref[CONV_CH:, :])
        x1_ref[...] = x1
        r = lax.rsqrt(jnp.mean(x1 * x1, axis=-1, keepdims=True) + EPS)
        h2_ref[...] = (x1 * r * g_ref[...]).astype(BF16)

    row = lambda w: pl.BlockSpec((tm, w), lambda i: (i, 0))
    full = lambda a: pl.BlockSpec(a.shape, lambda i: (0, 0))
    return pl.pallas_call(
        body, name="mix_out", grid=(s // tm,),
        in_specs=[row(CONV_CH), row(ATTN_W), full(w_out), row(d), full(g2)],
        out_specs=[row(d), row(d)],
        out_shape=[jax.ShapeDtypeStruct((s, d), F32), jax.ShapeDtypeStruct((s, d), BF16)],
        compiler_params=_params(("parallel",), VMEM_BIG),
    )(u, o, w_out, x, g2)


def _down_loss(act, w_down, x1, target):
    s, d = x1.shape
    f = act.shape[1]
    tm = _tile(s, 512, 8)

    def body(a_ref, w_ref, x1_ref, t_ref, loss_ref, dyf_ref, dyb_ref):
        @pl.when(pl.program_id(0) == 0)
        def _():
            loss_ref[...] = jnp.zeros_like(loss_ref)

        diff = x1_ref[...] + _dot(a_ref[...], w_ref[...]) - t_ref[...]
        sq = jnp.sum(jnp.sum(diff * diff, axis=1, keepdims=True), axis=0, keepdims=True)
        loss_ref[...] += jnp.broadcast_to(sq * (0.5 / d), loss_ref.shape)
        dy = diff * (1.0 / d)
        dyf_ref[...] = dy
        dyb_ref[...] = dy.astype(BF16)

    row = lambda w: pl.BlockSpec((tm, w), lambda i: (i, 0))
    return pl.pallas_call(
        body, name="down_loss", grid=(s // tm,),
        in_specs=[row(f), pl.BlockSpec((f, d), lambda i: (0, 0)), row(d), row(d)],
        out_specs=[pl.BlockSpec((8, 128), lambda i: (0, 0)), row(d), row(d)],
        out_shape=[jax.ShapeDtypeStruct((8, 128), F32), jax.ShapeDtypeStruct((s, d), F32),
                   jax.ShapeDtypeStruct((s, d), BF16)],
        compiler_params=_params(("arbitrary",), VMEM_BIG),
    )(act, w_down, x1, target)


def _norm_bwd_tail(dh, xv, g, resid, d):
    r = lax.rsqrt(jnp.mean(xv * xv, axis=-1, keepdims=True) + EPS)
    xh = xv * r
    gd = dh * g
    dx = r * (gd - xh * (jnp.sum(gd * xh, axis=-1, keepdims=True) * (1.0 / d)))
    return resid + dx, jnp.sum(dh * xh, axis=0, keepdims=True)


def _up_bwd(dg, dv, w_up_t, x1, dyf, g2):
    s, d = x1.shape
    f = dg.shape[1]
    tm = _tile(s, 512, 8)

    def body(dg_ref, dv_ref, w_ref, x1_ref, dy_ref, g_ref, dxf_ref, dxb_ref, gg_ref):
        @pl.when(pl.program_id(0) == 0)
        def _():
            gg_ref[...] = jnp.zeros_like(gg_ref)

        dh = _dot(dg_ref[...], w_ref[:f, :]) + _dot(dv_ref[...], w_ref[f:, :])
        dx, gg = _norm_bwd_tail(dh, x1_ref[...], g_ref[...], dy_ref[...], d)
        dxf_ref[...] = dx
        dxb_ref[...] = dx.astype(BF16)
        gg_ref[...] += gg

    row = lambda w: pl.BlockSpec((tm, w), lambda i: (i, 0))
    full = lambda a: pl.BlockSpec(a.shape, lambda i: (0, 0))
    return pl.pallas_call(
        body, name="up_bwd", grid=(s // tm,),
        in_specs=[row(f), row(f), full(w_up_t), row(d), row(d), full(g2)],
        out_specs=[row(d), row(d), pl.BlockSpec((1, d), lambda i: (0, 0))],
        out_shape=[jax.ShapeDtypeStruct((s, d), F32), jax.ShapeDtypeStruct((s, d), BF16),
                   jax.ShapeDtypeStruct((1, d), F32)],
        compiler_params=_params(("arbitrary",), VMEM_BIG),
    )(dg, dv, w_up_t, x1, dyf, g2)


def _proj_bwd(parts, w_in_t, x, dx1, g1, rider=None):
    s, d = x.shape
    widths = [p.shape[1] for p in parts]
    starts = [sum(widths[:i]) for i in range(len(parts))]
    np_ = len(parts)
    tm = _tile(s, 512, 8)

    def body(*refs):
        p_refs = refs[:np_]
        w_ref, x_ref, r_ref, g_ref, gx_ref, gg_ref = refs[np_:]

        @pl.when(pl.program_id(0) == 0)
        def _():
            gg_ref[...] = jnp.zeros_like(gg_ref)

        dh = _dot(p_refs[0][...], w_ref[:widths[0], :])
        for p in range(1, np_):
            dh = dh + _dot(p_refs[p][...], w_ref[starts[p]:starts[p] + widths[p], :])
        dx, gg = _norm_bwd_tail(dh, x_ref[...], g_ref[...], r_ref[...], d)
        gx_ref[...] = dx
        gg_ref[...] += gg

    row = lambda w: pl.BlockSpec((tm, w), lambda i: (i, 0))
    full = lambda a: pl.BlockSpec(a.shape, lambda i: (0, 0))
    outs, carried = _call(
        body, rider, name="proj_bwd", grid=(s // tm,),
        in_specs=[row(w) for w in widths] + [full(w_in_t), row(d), row(d), full(g1)],
        out_specs=[row(d), pl.BlockSpec((1, d), lambda i: (0, 0))],
        out_shape=[jax.ShapeDtypeStruct((s, d), F32), jax.ShapeDtypeStruct((1, d), F32)],
        compiler_params=_params(("arbitrary",), VMEM_BIG),
        args=(*parts, w_in_t, x, dx1, g1))
    return (*outs, carried)


CONV_CHUNK = 32
SUBLANES = 8


def _glu(av):
    return av[:, :CONV_CH] * _sigmoid(av[:, CONV_CH:])


def _chunks(n, size=CONV_CHUNK):
    return [(r0, min(size, n - r0)) for r0 in range(0, n, size)]


def _fill_shifted(sbuf, rows):
    for r in range(1, SUBLANES):
        for r0, n in _chunks(rows, 64):
            sbuf[r, r0:r0 + n, :] = sbuf[0, r0 + r:r0 + r + n, :]


def _tap(sbuf, offset, r0, rows):
    q, r = divmod(offset, SUBLANES)
    return sbuf[r, SUBLANES * q + r0:SUBLANES * q + r0 + rows, :]


def _layer_norm_stats(u1):
    mu = jnp.mean(u1, axis=-1, keepdims=True)
    cen = u1 - mu
    rstd = lax.rsqrt(jnp.mean(cen * cen, axis=-1, keepdims=True) + EPS)
    return cen * rstd, rstd


def _conv_fwd(a, cw, cb, cg, cbeta, rider=None):
    s = a.shape[0]
    tm = _tile(s, 256, CONV_CHUNK)
    hb = tm // CONV_HALO

    def body(a_ref, ap_ref, cw_ref, cb_ref, cg_ref, cbeta_ref, u1_ref, u_ref, ubuf):
        i = pl.program_id(0)
        ubuf[0, 0:CONV_HALO, :] = jnp.where(i > 0, _glu(ap_ref[...]), 0.0)
        for r0, n in _chunks(tm):
            ubuf[0, CONV_HALO + r0:CONV_HALO + r0 + n, :] = _glu(a_ref[r0:r0 + n, :])
        _fill_shifted(ubuf, tm + CONV_HALO - SUBLANES)
        for r0, n in _chunks(tm):
            acc = jnp.broadcast_to(cb_ref[...], (n, CONV_CH))
            for k in range(CONV_K):
                acc = acc + cw_ref[k:k + 1, :] * _tap(ubuf, 2 + k, r0, n)
            u1_ref[r0:r0 + n, :] = acc
            xh, _ = _layer_norm_stats(acc)
            z = xh * cg_ref[...] + cbeta_ref[...]
            u_ref[r0:r0 + n, :] = (z * _sigmoid(z)).astype(BF16)

    full = lambda t: pl.BlockSpec(t.shape, lambda i: (0, 0))
    outs, carried = _call(
        body, rider, name="conv_fwd", grid=(s // tm,),
        in_specs=[pl.BlockSpec((tm, 2 * CONV_CH), lambda i: (i, 0)),
                  pl.BlockSpec((CONV_HALO, 2 * CONV_CH), lambda i: (jnp.maximum(i * hb - 1, 0), 0)),
                  full(cw), full(cb), full(cg), full(cbeta)],
        out_specs=[pl.BlockSpec((tm, CONV_CH), lambda i: (i, 0))] * 2,
        out_shape=[jax.ShapeDtypeStruct((s, CONV_CH), F32), jax.ShapeDtypeStruct((s, CONV_CH), BF16)],
        scratch_shapes=[pltpu.VMEM((SUBLANES, CONV_HALO + tm, CONV_CH), F32)],
        compiler_params=_params(("arbitrary",), VMEM_BIG),
        args=(a, a, cw, cb, cg, cbeta))
    return (*outs, carried)


def _conv_bwd(a, u1, d_cat, cw, cg, cbeta, rider=None):
    s = a.shape[0]
    tm = _tile(s, 256, CONV_CHUNK)
    hb = tm // CONV_HALO
    last_halo = s // CONV_HALO - 1
    nt = s // tm
    te = tm + CONV_HALO

    def body(a_ref, ap_ref, u1_ref, u1n_ref, du_ref, dun_ref, cw_ref, cg_ref, cbeta_ref,
             da_ref, gw_ref, gb_ref, gg_ref, gbeta_ref, ubuf, dbuf):
        i = pl.program_id(0)

        @pl.when(i == 0)
        def _():
            gw_ref[...] = jnp.zeros_like(gw_ref)
            gb_ref[...] = jnp.zeros_like(gb_ref)
            gg_ref[...] = jnp.zeros_like(gg_ref)
            gbeta_ref[...] = jnp.zeros_like(gbeta_ref)

        def du1_of(u1, du):
            xh, rstd = _layer_norm_stats(u1)
            z = xh * cg_ref[...] + cbeta_ref[...]
            sz = _sigmoid(z)
            dz = du * (sz * (1.0 + z * (1.0 - sz)))
            dxh = dz * cg_ref[...]
            du1 = rstd * (dxh - jnp.mean(dxh, axis=-1, keepdims=True)
                          - xh * jnp.mean(dxh * xh, axis=-1, keepdims=True))
            return du1, dz, xh

        gg, gbeta, gb = [jnp.zeros((SUBLANES, CONV_CH), F32) for _ in range(3)]
        for r0, n in _chunks(tm):
            du1, dz, xh = du1_of(u1_ref[r0:r0 + n, :], du_ref[r0:r0 + n, :])
            gg, gbeta, gb = gg + _fold8(dz * xh), gbeta + _fold8(dz), gb + _fold8(du1)
            dbuf[0, r0:r0 + n, :] = du1
        gg_ref[...] += jnp.sum(gg, axis=0, keepdims=True)
        gbeta_ref[...] += jnp.sum(gbeta, axis=0, keepdims=True)
        gb_ref[...] += jnp.sum(gb, axis=0, keepdims=True)
        du1n, _, _ = du1_of(u1n_ref[...], jnp.where(i < nt - 1, dun_ref[...], 0.0))
        dbuf[0, tm:, :] = du1n
        _fill_shifted(dbuf, tm + CONV_HALO - SUBLANES)

        ubuf[0, 0:CONV_HALO, :] = jnp.where(i > 0, _glu(ap_ref[...]), 0.0)
        for r0, n in _chunks(tm):
            ubuf[0, CONV_HALO + r0:CONV_HALO + r0 + n, :] = _glu(a_ref[r0:r0 + n, :])
        _fill_shifted(ubuf, tm + CONV_HALO - SUBLANES)

        for k in range(CONV_K):
            part = jnp.zeros((SUBLANES, CONV_CH), F32)
            for r0, n in _chunks(tm):
                part = part + _fold8(dbuf[0, r0:r0 + n, :] * _tap(ubuf, 2 + k, r0, n))
            gw_ref[k:k + 1, :] += jnp.sum(part, axis=0, keepdims=True)

        for r0, n in _chunks(tm):
            acc = jnp.zeros((n, CONV_CH), F32)
            for k in range(CONV_K):
                acc = acc + cw_ref[k:k + 1, :] * _tap(dbuf, CONV_K - 1 - k, r0, n)
            avc = a_ref[r0:r0 + n, :CONV_CH]
            sgc = _sigmoid(a_ref[r0:r0 + n, CONV_CH:])
            da_ref[r0:r0 + n, :CONV_CH] = (acc * sgc).astype(BF16)
            da_ref[r0:r0 + n, CONV_CH:] = (acc * avc * sgc * (1.0 - sgc)).astype(BF16)

    full = lambda t: pl.BlockSpec(t.shape, lambda i: (0, 0))
    vec = pl.BlockSpec((1, CONV_CH), lambda i: (0, 0))
    nxt = lambda i: (jnp.minimum((i + 1) * hb, last_halo), 0)
    outs, carried = _call(
        body, rider, name="conv_bwd", grid=(nt,),
        in_specs=[pl.BlockSpec((tm, 2 * CONV_CH), lambda i: (i, 0)),
                  pl.BlockSpec((CONV_HALO, 2 * CONV_CH), lambda i: (jnp.maximum(i * hb - 1, 0), 0)),
                  pl.BlockSpec((tm, CONV_CH), lambda i: (i, 0)), pl.BlockSpec((CONV_HALO, CONV_CH), nxt),
                  pl.BlockSpec((tm, CONV_CH), lambda i: (i, 0)), pl.BlockSpec((CONV_HALO, CONV_CH), nxt),
                  full(cw), full(cg), full(cbeta)],
        out_specs=[pl.BlockSpec((tm, 2 * CONV_CH), lambda i: (i, 0)),
                   pl.BlockSpec((CONV_HALO, CONV_CH), lambda i: (0, 0)), vec, vec, vec],
        out_shape=[jax.ShapeDtypeStruct((s, 2 * CONV_CH), BF16), jax.ShapeDtypeStruct((CONV_HALO, CONV_CH), F32),
                   jax.ShapeDtypeStruct((1, CONV_CH), F32), jax.ShapeDtypeStruct((1, CONV_CH), F32),
                   jax.ShapeDtypeStruct((1, CONV_CH), F32)],
        scratch_shapes=[pltpu.VMEM((SUBLANES, CONV_HALO + tm, CONV_CH), F32), pltpu.VMEM((SUBLANES, te, CONV_CH), F32)],
        compiler_params=_params(("arbitrary",), VMEM_BIG),
        args=(a, a, u1, u1, d_cat, d_cat, cw, cg, cbeta))
    return (*outs, carried)


def _ff_tiles(s, f):
    return _tile(s, 256, 8), _tile(f, 1408, 128)


FF_CHUNK, FF_LANES = 64, 128
MXU_WIDTH = 256


def _row_chunks(n):
    return [(r0, min(FF_CHUNK, n - r0)) for r0 in range(0, n, FF_CHUNK)]


def _fold8(v):
    acc = v[0:8]
    for r in range(8, v.shape[0], 8):
        acc = acc + v[r:r + 8]
    return acc


def _ff_conv(pre_buf, w_ref, b_ref, half, r0, rows, cols):
    acc = b_ref[half:half + 1, cols]
    for k in range(FF_K):
        off = FF_HALO - (FF_K - 1) + k + r0
        acc = acc + w_ref[k, half:half + 1, cols] * pre_buf[half, off:off + rows, cols]
    return acc


def _up_fwd_act(h2, w_up_t, fw, fb, f):
    s, d = h2.shape
    tm, tn = _tile(s, 512, 8), _tile(f, 1408, 128)
    nc, nt = f // tn, s // tm

    def body(h_ref, wg_ref, wv_ref, w_ref, b_ref, g_ref, v_ref, act_ref, xbuf):
        @pl.when(pl.program_id(1) == 0)
        def _():
            xbuf[...] = jnp.zeros_like(xbuf)

        strips = [(slice(c0, c0 + FF_LANES), r0, rows) for c0 in range(0, tn, FF_LANES) for r0, rows in _row_chunks(tm)]
        pieces = [(out, wt, slice(c0, min(c0 + MXU_WIDTH, tn))) for out, wt in ((g_ref, wg_ref), (v_ref, wv_ref))
                  for c0 in range(0, tn, MXU_WIDTH)]
        per_piece = -(-len(strips) // len(pieces))
        hv = h_ref[...]
        for n, (out, wt, wcols) in enumerate(pieces):
            out[:, wcols] = _dot(hv, wt[wcols, :], trans_b=True)
            for cols, r0, rows in strips[n * per_piece:(n + 1) * per_piece]:
                gate = _ff_conv(xbuf, w_ref, b_ref, 0, r0, rows, cols)
                val = _ff_conv(xbuf, w_ref, b_ref, 1, r0, rows, cols)
                act_ref[r0:r0 + rows, cols] = (gate * _sigmoid(gate) * val).astype(BF16)
        for half, ref in enumerate((g_ref, v_ref)):
            xbuf[half, 0:FF_HALO, :] = xbuf[half, tm:tm + FF_HALO, :]
            xbuf[half, FF_HALO:, :] = ref[...]

    cur = lambda j, i: (jnp.minimum(i, nt - 1), j)
    return pl.pallas_call(
        body, name="up_fwd_act", grid=(nc, nt + 1),
        in_specs=[pl.BlockSpec((tm, d), lambda j, i: (jnp.minimum(i, nt - 1), 0)),
                  pl.BlockSpec((tn, d), lambda j, i: (j, 0)), pl.BlockSpec((tn, d), lambda j, i: (j + nc, 0)),
                  pl.BlockSpec((FF_K, 2, tn), lambda j, i: (0, 0, j)), pl.BlockSpec((2, tn), lambda j, i: (0, j))],
        out_specs=[pl.BlockSpec((tm, tn), cur), pl.BlockSpec((tm, tn), cur),
                   pl.BlockSpec((tm, tn), lambda j, i: (jnp.maximum(i - 1, 0), j))],
        out_shape=[jax.ShapeDtypeStruct((s, f), F32), jax.ShapeDtypeStruct((s, f), F32), jax.ShapeDtypeStruct((s, f), BF16)],
        scratch_shapes=[pltpu.VMEM((2, FF_HALO + tm, tn), F32)],
        compiler_params=_params(("arbitrary", "arbitrary"), VMEM_BIG),
    )(h2, w_up_t, w_up_t, fw, fb)


def _ff_bwd(up_g, up_v, d_act, fw, fb, rider=None):
    s, f = up_g.shape
    tm, tc = _ff_tiles(s, f)
    hb = tm // FF_HALO
    nt = s // tm
    last_halo = s // FF_HALO - 1
    te = tm + FF_HALO

    def body(g_ref, gp_ref, gn_ref, v_ref, vp_ref, vn_ref, da_ref, dan_ref, w_ref, b_ref,
             dg_ref, dv_ref, gw_ref, gb_ref, buf, shifted, dbuf, dshifted, dabuf):
        i = pl.program_id(1)

        @pl.when(i == 0)
        def _():
            gw_ref[...] = jnp.zeros_like(gw_ref)
            gb_ref[...] = jnp.zeros_like(gb_ref)

        for half, (m_ref, p_ref, n_ref) in enumerate(((g_ref, gp_ref, gn_ref), (v_ref, vp_ref, vn_ref))):
            buf[half, 0:FF_HALO, :] = jnp.where(i > 0, p_ref[...], 0.0)
            buf[half, FF_HALO:FF_HALO + tm, :] = m_ref[...]
            buf[half, FF_HALO + tm:, :] = n_ref[...]
            for k in range(FF_K - 1):
                lo = FF_HALO - (FF_K - 1) + k
                for r0, rows in _row_chunks(te):
                    shifted[half, k, r0:r0 + rows, :] = buf[half, lo + r0:lo + r0 + rows, :]
        dabuf[0:tm, :] = da_ref[...]
        dabuf[tm:, :] = jnp.where(i < nt - 1, dan_ref[...], 0.0)

        for c0 in range(0, tc, FF_LANES):
            cols = slice(c0, c0 + FF_LANES)
            gb = [jnp.zeros((8, FF_LANES), F32) for _ in range(2)]
            gw = [[jnp.zeros((8, FF_LANES), F32) for _ in range(FF_K)] for _ in range(2)]
            for r0, rows in _row_chunks(te):
                taps = [[shifted[half, 0, r0:r0 + rows, cols], shifted[half, 1, r0:r0 + rows, cols],
                         buf[half, FF_HALO + r0:FF_HALO + r0 + rows, cols]] for half in range(2)]
                gate, val = [b_ref[half:half + 1, cols] + sum(w_ref[k, half:half + 1, cols] * taps[half][k]
                                                              for k in range(FF_K)) for half in range(2)]
                da = dabuf[r0:r0 + rows, cols]
                sg = _sigmoid(gate)
                dup = [da * val * (sg * (1.0 + gate * (1.0 - sg))), da * (gate * sg)]
                for half in range(2):
                    dbuf[half, r0:r0 + rows, cols] = dup[half]
                    if r0 < tm:
                        gb[half] = gb[half] + _fold8(dup[half])
                        for k in range(FF_K):
                            gw[half][k] = gw[half][k] + _fold8(dup[half] * taps[half][k])
            for half, out_ref in enumerate((dg_ref, dv_ref)):
                gb_ref[half:half + 1, cols] += jnp.sum(gb[half], axis=0, keepdims=True)
                for k in range(FF_K):
                    gw_ref[k, half:half + 1, cols] += jnp.sum(gw[half][k], axis=0, keepdims=True)
                for k in range(1, FF_K):
                    for r0, rows in _row_chunks(tm):
                        dshifted[half, k - 1, r0:r0 + rows, cols] = dbuf[half, k + r0:k + r0 + rows, cols]
                for r0, rows in _row_chunks(tm):
                    acc = (w_ref[2, half:half + 1, cols] * dbuf[half, r0:r0 + rows, cols]
                           + w_ref[1, half:half + 1, cols] * dshifted[half, 0, r0:r0 + rows, cols]
                           + w_ref[0, half:half + 1, cols] * dshifted[half, 1, r0:r0 + rows, cols])
                    out_ref[r0:r0 + rows, cols] = acc.astype(BF16)

    main = pl.BlockSpec((tm, tc), lambda j, i: (i, j))
    prev = pl.BlockSpec((FF_HALO, tc), lambda j, i: (jnp.maximum(i * hb - 1, 0), j))
    nxt = pl.BlockSpec((FF_HALO, tc), lambda j, i: (jnp.minimum((i + 1) * hb, last_halo), j))
    (dg, dv, gw, gb), carried = _call(
        body, rider, name="ff_bwd", grid=(f // tc, nt),
        in_specs=[main, prev, nxt, main, prev, nxt, main, nxt,
                  pl.BlockSpec((FF_K, 2, tc), lambda j, i: (0, 0, j)), pl.BlockSpec((2, tc), lambda j, i: (0, j))],
        out_specs=[main, main, pl.BlockSpec((FF_K, 2, tc), lambda j, i: (0, 0, j)),
                   pl.BlockSpec((2, tc), lambda j, i: (0, j))],
        out_shape=[jax.ShapeDtypeStruct((s, f), BF16), jax.ShapeDtypeStruct((s, f), BF16),
                   jax.ShapeDtypeStruct((FF_K, 2, f), F32), jax.ShapeDtypeStruct((2, f), F32)],
        scratch_shapes=[pltpu.VMEM((2, FF_HALO + te, tc), F32), pltpu.VMEM((2, FF_K - 1, te, tc), F32),
                        pltpu.VMEM((2, te, tc), F32), pltpu.VMEM((2, FF_K - 1, tm, tc), F32), pltpu.VMEM((te, tc), F32)],
        compiler_params=_params(("arbitrary", "arbitrary"), VMEM_BIG),
        args=(up_g, up_g, up_g, up_v, up_v, up_v, d_act, d_act, fw, fb))
    return dg, dv, gw, gb, carried


ATT_TILE = Q_BLOCK * max(PATTERN_DILATIONS)


def _stream_rows(start, d, n=Q_BLOCK):
    return pl.ds(start, n) if d == 1 else pl.ds(start, n, stride=d)


def _band_geometry():
    qi = lax.broadcasted_iota(jnp.int32, (Q_BLOCK, 2 * Q_BLOCK), 0)
    ki = lax.broadcasted_iota(jnp.int32, (Q_BLOCK, 2 * Q_BLOCK), 1)
    delta = qi + Q_BLOCK - ki
    return (delta >= 0) & (delta <= Q_BLOCK), delta.astype(F32), ki


def _stream_blocks(d):
    out = []
    for r in range(d):
        for b in range(ATT_TILE // (Q_BLOCK * d)):
            start = b * Q_BLOCK * d + r
            out.append((start, start - Q_BLOCK * d if b > 0 else None))
    return out


def _attn_forward(qkv, slopes, rider=None):
    s = qkv.shape[0]
    nt = s // ATT_TILE
    nhp = ATTN_W // Q_BLOCK

    def body(sl_ref, q_ref, k_ref, kp_ref, v_ref, vp_ref, o_ref, l_ref):
        hp, i = pl.program_id(0), pl.program_id(1)
        head0 = lax.broadcasted_iota(jnp.int32, (Q_BLOCK, Q_BLOCK), 1) < HEAD_DIM
        head0_k = lax.broadcasted_iota(jnp.int32, (2 * Q_BLOCK, Q_BLOCK), 1) < HEAD_DIM
        valid, dist, ki = _band_geometry()
        first_key = jnp.where(i == 0, Q_BLOCK, 0)
        for d in PATTERN_DILATIONS:
            biases = [jnp.where(valid, dist * (-sl_ref[2 * hp + hh] * d), NEG) for hh in range(2)]
            for start, prev in _stream_blocks(d):
                rows = _stream_rows(start, d)
                if prev is None:
                    prow = _stream_rows(ATT_TILE - Q_BLOCK * d + start, d)
                    kp, vp = kp_ref[prow, :].astype(BF16), vp_ref[prow, :].astype(BF16)
                else:
                    kp, vp = kc, vc
                kc, vc = k_ref[rows, :].astype(BF16), v_ref[rows, :].astype(BF16)
                qv = q_ref[rows, :].astype(BF16)
                k2 = jnp.concatenate([kp, kc], axis=0)
                v2 = jnp.concatenate([vp, vc], axis=0)
                res, mxs = [], []
                for hh in range(2):
                    mine = head0 if hh == 0 else jnp.logical_not(head0)
                    mine_k = head0_k if hh == 0 else jnp.logical_not(head0_k)
                    sc = _dot(jnp.where(mine, qv, jnp.zeros_like(qv)), k2, trans_b=True) + biases[hh]
                    if prev is None:
                        sc = jnp.where(ki < first_key, NEG, sc)
                    mx = jnp.max(sc, axis=1, keepdims=True)
                    p = jnp.exp(sc - mx).astype(BF16)
                    res.append(_dot(p, jnp.where(mine_k, v2, jnp.ones_like(v2))))
                    mxs.append(mx)
                num = jnp.where(head0, res[0], res[1])
                den = pltpu.roll(jnp.where(head0, res[1], res[0]), HEAD_DIM, 1)
                o_new = num / den
                l_new = jnp.where(head0, mxs[0], mxs[1]) + jnp.log(den)
                if d != PATTERN_DILATIONS[0]:
                    oa, la = o_ref[rows, :], l_ref[rows, :]
                    mm = jnp.maximum(la, l_new)
                    wa, wn = jnp.exp(la - mm), jnp.exp(l_new - mm)
                    o_new = (wa * oa + wn * o_new) / (wa + wn)
                    l_new = mm + jnp.log(wa + wn)
                o_ref[rows, :] = o_new
                l_ref[rows, :] = l_new

    def col(off):
        return pl.BlockSpec((ATT_TILE, Q_BLOCK), lambda hp, i: (i, off + hp))

    def col_prev(off):
        return pl.BlockSpec((ATT_TILE, Q_BLOCK), lambda hp, i: (jnp.maximum(i - 1, 0), off + hp))

    outs, carried = _call(
        body, rider, name="attn_fwd", grid=(nhp, nt),
        in_specs=[pl.BlockSpec(memory_space=pltpu.SMEM), col(0), col(nhp), col_prev(nhp), col(2 * nhp), col_prev(2 * nhp)],
        out_specs=[col(0), col(0)],
        out_shape=[jax.ShapeDtypeStruct((s, ATTN_W), F32)] * 2,
        compiler_params=_params(("arbitrary", "arbitrary"), VMEM_BIG),
        args=(slopes, qkv, qkv, qkv, qkv, qkv))
    return (*outs, carried)


def _attn_backward(qkv, qk, qg, kg, d_cat, o, lse, slopes, ones2, rider=None):
    s = qkv.shape[0]
    nt = s // ATT_TILE
    nhp = ATTN_W // Q_BLOCK
    tt = ATT_TILE

    def body(sl_ref, q_ref, k_ref, kp_ref, v_ref, vp_ref, do_ref, o_ref, l_ref, ones_ref, qraw_ref, kraw_ref,
             qg_ref, kg_ref, dq_ref, dk_ref, dv_ref, gq_ref, gk_ref, dkacc, dvacc, dd, dqacc):
        hp, step = pl.program_id(0), pl.program_id(1)
        tile = nt - 1 - step
        head0 = lax.broadcasted_iota(jnp.int32, (Q_BLOCK, Q_BLOCK), 1) < HEAD_DIM
        valid, dist, ki = _band_geometry()
        first_key = jnp.where(tile == 0, Q_BLOCK, 0)

        @pl.when((step == 0) & (hp == 0))
        def _():
            gq_ref[...] = jnp.zeros_like(gq_ref)
            gk_ref[...] = jnp.zeros_like(gk_ref)

        @pl.when(step == 0)
        def _():
            dkacc[tt:, :] = jnp.zeros((tt, Q_BLOCK), F32)
            dvacc[tt:, :] = jnp.zeros((tt, Q_BLOCK), F32)

        @pl.when(step > 0)
        def _():
            dkacc[tt:, :] = dkacc[:tt, :]
            dvacc[tt:, :] = dvacc[:tt, :]

        dkacc[:tt, :] = jnp.zeros((tt, Q_BLOCK), F32)
        dvacc[:tt, :] = jnp.zeros((tt, Q_BLOCK), F32)
        dd[...] = _head_sum(do_ref[...] * o_ref[...], ones_ref)

        ki2 = jnp.concatenate([ki, ki], axis=0)
        for d in PATTERN_DILATIONS:
            bias2 = jnp.concatenate([jnp.where(valid, dist * (-sl_ref[2 * hp + hh] * d), NEG) for hh in range(2)], axis=0)
            for start, prev in _stream_blocks(d):
                rows = _stream_rows(start, d)
                if prev is None:
                    prow = _stream_rows(ATT_TILE - Q_BLOCK * d + start, d)
                    kp, vp = kp_ref[prow, :].astype(BF16), vp_ref[prow, :].astype(BF16)
                else:
                    kp, vp = kc, vc
                kc, vc = k_ref[rows, :].astype(BF16), v_ref[rows, :].astype(BF16)
                acc_rows = _stream_rows(tt + start - Q_BLOCK * d, d, 2 * Q_BLOCK)
                qv = q_ref[rows, :].astype(BF16)
                dov = do_ref[rows, :].astype(BF16)
                lv, ddv = l_ref[rows, :], dd[rows, :]
                lsw, dsw = pltpu.roll(lv, HEAD_DIM, 1), pltpu.roll(ddv, HEAD_DIM, 1)
                k2 = jnp.concatenate([kp, kc], axis=0)
                v2 = jnp.concatenate([vp, vc], axis=0)
                zq = jnp.zeros_like(qv)
                q2 = jnp.concatenate([jnp.where(head0, qv, zq), jnp.where(head0, zq, qv)], axis=0)
                do2 = jnp.concatenate([jnp.where(head0, dov, zq), jnp.where(head0, zq, dov)], axis=0)
                lh = jnp.concatenate([jnp.where(head0, lv, lsw), jnp.where(head0, lsw, lv)], axis=0)
                dh = jnp.concatenate([jnp.where(head0, ddv, dsw), jnp.where(head0, dsw, ddv)], axis=0)
                sc = _dot(q2, k2, trans_b=True) + bias2
                if prev is None:
                    sc = jnp.where(ki2 < first_key, NEG, sc)
                p = jnp.exp(sc - jnp.concatenate([lh, lh], axis=1))
                ds = p * (_dot(do2, v2, trans_b=True) - jnp.concatenate([dh, dh], axis=1))
                dq2 = _dot(ds.astype(BF16), k2)
                dk2 = _dot(ds.T.astype(BF16), q2)
                dv2 = _dot(p.T.astype(BF16), do2)
                dq = jnp.where(head0, dq2[:Q_BLOCK], dq2[Q_BLOCK:])
                if d == PATTERN_DILATIONS[0]:
                    dqacc[rows, :] = dq
                else:
                    dqacc[rows, :] += dq
                dkacc[acc_rows, :] += dk2
                dvacc[acc_rows, :] += dv2
        dv_ref[...] = dvacc[tt:, :].astype(BF16)

        def norm_bwd(dy, raw, g, scale):
            r = lax.rsqrt(_head_sum(raw * raw, ones_ref) * (1.0 / HEAD_DIM) + EPS)
            xh = raw * r
            gd = dy * (g * scale)
            dx = r * (gd - xh * (_head_sum(gd * xh, ones_ref) * (1.0 / HEAD_DIM)))
            return dx, _fold8(dy * xh) * scale

        gq, gk = jnp.zeros((SUBLANES, Q_BLOCK), F32), jnp.zeros((SUBLANES, Q_BLOCK), F32)
        for r0, n in _chunks(tt, 256):
            dxq, pq = norm_bwd(dqacc[r0:r0 + n, :], qraw_ref[r0:r0 + n, :], qg_ref[...], HEAD_DIM ** -0.5)
            dxk, pk = norm_bwd(dkacc[tt + r0:tt + r0 + n, :], kraw_ref[r0:r0 + n, :], kg_ref[...], 1.0)
            dq_ref[r0:r0 + n, :] = dxq.astype(BF16)
            dk_ref[r0:r0 + n, :] = dxk.astype(BF16)
            gq, gk = gq + pq, gk + pk
        for acc, out in ((gq, gq_ref), (gk, gk_ref)):
            row = jnp.sum(acc, axis=0, keepdims=True)
            out[...] += row + pltpu.roll(row, HEAD_DIM, 1)

    def col(off):
        return pl.BlockSpec((tt, Q_BLOCK), lambda hp, st: (nt - 1 - st, off + hp))

    def col_prev(off):
        return pl.BlockSpec((tt, Q_BLOCK), lambda hp, st: (jnp.maximum(nt - 2 - st, 0), off + hp))

    gain = pl.BlockSpec((1, Q_BLOCK), lambda hp, st: (0, hp))
    total = pl.BlockSpec((1, Q_BLOCK), lambda hp, st: (0, 0))
    outs, carried = _call(
        body, rider, name="attn_bwd", grid=(nhp, nt),
        in_specs=[pl.BlockSpec(memory_space=pltpu.SMEM), col(0), col(nhp), col_prev(nhp), col(2 * nhp), col_prev(2 * nhp),
                  col(nhp), col(0), col(0), pl.BlockSpec((Q_BLOCK, Q_BLOCK), lambda hp, st: (0, 0)),
                  col(0), col(nhp), gain, gain],
        out_specs=[col(0)] * 3 + [total, total],
        out_shape=[jax.ShapeDtypeStruct((s, ATTN_W), BF16)] * 3 + [jax.ShapeDtypeStruct((1, Q_BLOCK), F32)] * 2,
        scratch_shapes=[pltpu.VMEM((2 * tt, Q_BLOCK), F32)] * 2 + [pltpu.VMEM((tt, Q_BLOCK), F32)] * 2,
        compiler_params=_params(("arbitrary", "arbitrary"), VMEM_BIG),
        args=(slopes, qkv, qkv, qkv, qkv, qkv, d_cat, o, lse, ones2, qk, qk, qg, kg))
    return (*outs, carried)


def _sum_parts(name, parts):
    _, n, w = parts.shape
    tn = _tile(n, 256, 8)

    def body(p_ref, o_ref):
        acc = p_ref[0].astype(F32)
        for j in range(1, N_DEV):
            acc = acc + p_ref[j].astype(F32)
        o_ref[...] = acc

    return pl.pallas_call(
        body, name=name, grid=(n // tn,),
        in_specs=[pl.BlockSpec((N_DEV, tn, w), lambda i: (0, i, 0))],
        out_specs=pl.BlockSpec((tn, w), lambda i: (i, 0)),
        out_shape=jax.ShapeDtypeStruct((n, w), F32),
        compiler_params=_params(("parallel",), VMEM_BIG),
    )(parts)


def _adamw(name, w, g, m, v):
    n, cols = w.shape
    tn = _tile(n, 256, 8)
    c1 = 1.0 - ADAM_B1 ** ADAM_STEP
    c2 = 1.0 - ADAM_B2 ** ADAM_STEP

    def body(w_ref, g_ref, m_ref, v_ref, d_ref, nm_ref, nv_ref):
        gv = g_ref[...]
        nm = ADAM_B1 * m_ref[...] + (1.0 - ADAM_B1) * gv
        nv = ADAM_B2 * v_ref[...] + (1.0 - ADAM_B2) * (gv * gv)
        nm_ref[...] = nm
        nv_ref[...] = nv
        d_ref[...] = -ADAM_LR * ((nm / c1) / (jnp.sqrt(nv / c2) + ADAM_EPS) + ADAM_WD * w_ref[...])

    blk = pl.BlockSpec((tn, cols), lambda i: (i, 0))
    return pl.pallas_call(
        body, name=name, grid=(n // tn,),
        in_specs=[blk] * 4, out_specs=[blk] * 3,
        out_shape=[jax.ShapeDtypeStruct((n, cols), F32)] * 3,
        compiler_params=_params(("parallel",), VMEM_BIG),
    )(w, g, m, v)


def _pack(vectors, width):
    flat = jnp.concatenate([t.reshape(-1) for t in vectors])
    rows = -(-flat.shape[0] // (8 * width)) * 8
    return jnp.pad(flat, (0, rows * width - flat.shape[0])).reshape(rows, width)


def _unpack(packed, shapes):
    flat = packed.reshape(-1)
    out, at = [], 0
    for shp in shapes:
        size = 1
        for dim in shp:
            size *= dim
        out.append(flat[at:at + size].reshape(shp))
        at += size
    return out


def kernel(x, norm1_g, w_in, conv_w, conv_b, cn_g, cn_b, q_norm_g, k_norm_g, w_out, norm2_g, w_up, ffconv_w, ffconv_b, w_down, loss_target, m_norm1_g, m_w_in, m_conv_w, m_conv_b, m_cn_g, m_cn_b, m_q_norm_g, m_k_norm_g, m_w_out, m_norm2_g, m_w_up, m_ffconv_w, m_ffconv_b, m_w_down, v_norm1_g, v_w_in, v_conv_w, v_conv_b, v_cn_g, v_cn_b, v_q_norm_g, v_k_norm_g, v_w_out, v_norm2_g, v_w_up, v_ffconv_w, v_ffconv_b, v_w_down):
    s, d = x.shape[1], x.shape[2]
    f = w_down.shape[0] * N_DEV
    n_in = w_in.shape[1] * N_DEV
    xs, target = x.reshape(s, d), loss_target.reshape(s, d)
    me = _linear((lax.axis_index("x"), lax.axis_index("y"), lax.axis_index("c")))

    row = lambda t: t.reshape(1, -1)
    g1, g2 = row(norm1_g), row(norm2_g)
    h, (g_in, g_filt) = _norm_fwd(xs, g1, _Gather([w_in.T.astype(BF16), _pack([conv_w, ffconv_w], 128)]))
    w_in_t = g_in.reshape(n_in, d)
    filt = g_filt.reshape(N_DEV, -1)
    n_cw = conv_w.size
    cw = filt[:, :n_cw].reshape(N_DEV, CONV_K, -1).transpose(1, 0, 2).reshape(CONV_K, CONV_CH)
    cw = jnp.pad(cw, ((0, CONV_HALO - CONV_K), (0, 0)))
    fw = filt[:, n_cw:n_cw + ffconv_w.size].reshape(N_DEV, FF_K, -1).transpose(1, 0, 2).reshape(FF_K, 2, f)
    fb = ffconv_b.reshape(2, f)
    qg, kg =row(jnp.tile(q_norm_g, N_HEADS)), row(jnp.tile(k_norm_g, N_HEADS))
    lanes = jnp.arange(ATTN_W) // HEAD_DIM
    ones = (lanes[:, None] == lanes[None, :]).astype(BF16)
    slopes = 2.0 ** (-8.0 * jnp.arange(1, N_HEADS + 1, dtype=F32) / N_HEADS)

    a, qk, qkv, (g_out,) = _proj_fwd(h, w_in_t, qg, kg, ones, _Gather([w_out.astype(BF16)]))
    u1, u, (g_up,) = _conv_fwd(a, cw, row(conv_b), row(cn_g), row(cn_b), _Gather([w_up.T.astype(BF16)]))
    o, lse, (g_down,) = _attn_forward(qkv, slopes, _Gather([w_down.astype(BF16)]))
    w_out_f = g_out.reshape(2 * CONV_CH, d)
    w_up_t = g_up.reshape(2 * f, d)
    w_down_f = g_down.reshape(f, d)
    x1, h2 = _mix_out(u, o, w_out_f, xs, g2)
    up_g, up_v, act = _up_fwd_act(h2, w_up_t, fw, fb, f)
    loss_acc, dyf, dyb = _down_loss(act, w_down_f, x1, target)

    tk, tk_wide = _tile(s, 1024, 8), _tile(s, 512, 8)
    gw_down = _mm_tn("grad_w_down", [[act]], dyb, tk)
    d_act = _mm("d_act", dyb, w_down_f, True, F32, _tile(s, 512, 8), _tile(f, 1408, 128))
    blocks = lambda t: _Exchange([t.reshape(N_DEV, t.shape[0] // N_DEV, d)])
    dpre_g, dpre_v, gfw, gfb, (r_down,) = _ff_bwd(up_g, up_v, d_act, fw, fb, blocks(gw_down))
    gw_up = _mm_tn("grad_w_up", [[dpre_g], [dpre_v]], h2, tk_wide)
    dx1f, dx1b, gg2 = _up_bwd(dpre_g, dpre_v, w_up_t, x1, dyf, g2)
    gw_out = _mm_tn("grad_w_out", [[u, o]], dx1b, tk)
    d_cat = _mm("d_cat", dx1b, w_out_f, True, F32, _tile(s, 512, 8), 2 * CONV_CH)
    d_a, gcw, gcb, gcg, gcbeta, (r_out,) = _conv_bwd(a, u1, d_cat, cw, row(cn_g), row(cn_b), blocks(gw_out))
    dq, dk, dv, gqg, gkg, (r_up,) = _attn_backward(qkv, qk, qg, kg, d_cat, o, lse, slopes, ones[:Q_BLOCK, :Q_BLOCK],
                                                   blocks(gw_up))
    d_proj = [d_a, dq, dk, dv]
    gw_in = _mm_tn("grad_w_in", [d_proj], h, tk)
    grad_x, gg1, (r_in,) = _proj_bwd(d_proj, w_in_t, xs, dx1f, g1, blocks(gw_in))
    g_w_in_t, g_w_out, g_w_up_t, g_w_down = [_sum_parts(f"sum_grad_{i}", r) for i, r in enumerate((r_in, r_out, r_up, r_down))]
    g_w_in, g_w_up = g_w_in_t.T, g_w_up_t.T

    small_shapes = [(d,), (d,), (CONV_CH,), (CONV_CH,), (CONV_CH,), (Q_BLOCK,), (Q_BLOCK,), (2 * f,),
                    (CONV_K, CONV_CH), (FF_K, 2 * f), (1,)]
    small = _pack([gg1, gg2, gcb, gcg, gcbeta, gqg, gkg, gfb, gcw[:CONV_K], gfw, loss_acc[0:1, 0:1]], 1024)
    (small_all,) = _communicate("gather_small_grads", _Gather([small]))
    sg1, sg2, scb, scg, scbeta, sqg, skg, sfb, scw, sfw, loss_sum = _unpack(
        _sum_parts("sum_small_grads", small_all), small_shapes)
    loss = loss_sum[0]
    cwl, fwl = conv_w.shape[1], ffconv_w.shape[1]
    g_small = [sg1, lax.dynamic_slice_in_dim(scw, me * cwl, cwl, 1), scb, scg, scbeta,
               sqg[:HEAD_DIM], skg[:HEAD_DIM], sg2, lax.dynamic_slice_in_dim(sfw, me * fwl, fwl, 1), sfb]

    w_small = [norm1_g, conv_w, conv_b, cn_g, cn_b, q_norm_g, k_norm_g, norm2_g, ffconv_w, ffconv_b]
    m_small = [m_norm1_g, m_conv_w, m_conv_b, m_cn_g, m_cn_b, m_q_norm_g, m_k_norm_g, m_norm2_g, m_ffconv_w, m_ffconv_b]
    v_small = [v_norm1_g, v_conv_w, v_conv_b, v_cn_g, v_cn_b, v_q_norm_g, v_k_norm_g, v_norm2_g, v_ffconv_w, v_ffconv_b]
    shapes = [t.shape for t in w_small]
    packed = _adamw("adamw_small", *[_pack(ts, 128) for ts in (w_small, g_small, m_small, v_small)])
    d_small, nm_small, nv_small = [_unpack(p, shapes) for p in packed]
    upd = {}
    for name, wt, gt, mt, vt in (("w_in", w_in, g_w_in, m_w_in, v_w_in), ("w_out", w_out, g_w_out, m_w_out, v_w_out),
                                 ("w_up", w_up, g_w_up, m_w_up, v_w_up), ("w_down", w_down, g_w_down, m_w_down, v_w_down)):
        upd[name] = (gt,) + tuple(_adamw("adamw_" + name, wt, gt, mt, vt))
    order = ["norm1_g", "w_in", "conv_w", "conv_b", "cn_g", "cn_b", "q_norm_g", "k_norm_g", "w_out", "norm2_g",
             "w_up", "ffconv_w", "ffconv_b", "w_down"]
    small_names = ["norm1_g", "conv_w", "conv_b", "cn_g", "cn_b", "q_norm_g", "k_norm_g", "norm2_g", "ffconv_w", "ffconv_b"]
    for i, name in enumerate(small_names):
        upd[name] = (g_small[i], d_small[i], nm_small[i], nv_small[i])
    outs = [loss, grad_x.reshape(x.shape)]
    for field in range(4):
        outs += [upd[name][field] for name in order]
    return tuple(outs)
```

```python
import jax
import jax.numpy as jnp
from jax import lax
from jax.experimental import pallas as pl
from jax.experimental.pallas import tpu as pltpu

F32, BF16 = jnp.float32, jnp.bfloat16
N_DEV = 8
N_HEADS, HEAD_DIM = 8, 64
CONV_CH = 512
ATTN_W = N_HEADS * HEAD_DIM
CONV_K, FF_K = 31, 3
CONV_HALO = 32
FF_HALO = 8
PATTERN_DILATIONS = (16, 4, 1)
Q_BLOCK = 128
EPS = 1e-6
NEG = -1e30
ADAM_LR, ADAM_B1, ADAM_B2, ADAM_EPS, ADAM_WD, ADAM_STEP = 0.001, 0.9, 0.999, 1e-08, 0.01, 10
VMEM_BIG = 56 * 1024 * 1024
ANY = pl.BlockSpec(memory_space=pl.ANY)


def _params(sem=None, vmem=None):
    return pltpu.CompilerParams(dimension_semantics=sem, vmem_limit_bytes=vmem)


def _dot(a, b, trans_a=False, trans_b=False):
    dims = (((0 if trans_a else 1,), (1 if trans_b else 0,)), ((), ()))
    return lax.dot_general(a, b, dims, preferred_element_type=F32)


def _sigmoid(z):
    return 1.0 / (1.0 + jnp.exp(-z))


def _tile(n, pref, mult):
    if n <= pref:
        return n
    t = (pref // mult) * mult
    while n % t:
        t -= mult
    return t


def _head_sum(v, ones_ref):
    hi = v.astype(BF16)
    lo = (v - hi.astype(F32)).astype(BF16)
    ones = ones_ref[...]
    return _dot(hi, ones) + _dot(lo, ones)


def _linear(p):
    return 4 * p[0] + 2 * p[1] + p[2]


class _Gather:
    def __init__(self, arrs):
        self.arrs = list(arrs)
        self.n = len(self.arrs)
        self.out_shape = [jax.ShapeDtypeStruct((N_DEV,) + a.shape, a.dtype) for a in self.arrs]

    def _setup(self, ins, outs, sems):
        send_sems, recv_sems, local_sems = sems
        x, y, c = lax.axis_index("x"), lax.axis_index("y"), lax.axis_index("c")
        me, sib = (x, y, c), (x, y, 1 - c)
        chips = [(1 - x, y), (x, 1 - y), (1 - x, 1 - y)]

        def copy(a, k, block, to, src=None):
            dst = outs[a].at[_linear(block)]
            return pltpu.make_async_remote_copy(
                src_ref=dst if src is None else src, dst_ref=dst,
                send_sem=send_sems.at[7 * a + k], recv_sem=recv_sems.at[7 * a + k],
                device_id=to, device_id_type=pl.DeviceIdType.MESH)

        mine = [pltpu.make_async_copy(ins[a], outs[a].at[_linear(me)], local_sems.at[a]) for a in range(self.n)]
        first = []
        for a in range(self.n):
            first.append(copy(a, 0, me, sib, src=ins[a]))
            first += [copy(a, 1 + j, me, (*chip, c), src=ins[a]) for j, chip in enumerate(chips)]
        return copy, mine, first, me, sib, chips, c

    def start(self, ins, outs, sems):
        _, mine, first, *_ = self._setup(ins, outs, sems)
        for cp in mine + first:
            cp.start()

    def finish(self, ins, outs, sems):
        copy, mine, first, me, sib, chips, c = self._setup(ins, outs, sems)
        passed = []
        for j, chip in enumerate(chips):
            for a in range(self.n):
                copy(a, 1 + j, (*chip, c), me).wait_recv()
                fwd = copy(a, 4 + j, (*chip, c), sib)
                fwd.start()
                passed.append(fwd)
        for a in range(self.n):
            copy(a, 0, sib, me).wait_recv()
            for j, chip in enumerate(chips):
                copy(a, 4 + j, (*chip, 1 - c), me).wait_recv()
        for cp in first + passed:
            cp.wait_send()
        for cp in mine:
            cp.wait()


class _Exchange:
    def __init__(self, arrs):
        self.arrs = list(arrs)
        self.n = len(self.arrs)
        self.out_shape = [jax.ShapeDtypeStruct(a.shape, a.dtype) for a in self.arrs]

    def _setup(self, ins, outs, sems):
        send_sems, recv_sems, local_sems = sems
        x, y, c = lax.axis_index("x"), lax.axis_index("y"), lax.axis_index("c")
        me = _linear((x, y, c))
        mine = [pltpu.make_async_copy(ins[a].at[me], outs[a].at[me], local_sems.at[a]) for a in range(self.n)]
        sends, recvs = [], []
        for p in range(1, N_DEV):
            peer = (1 - x if p & 4 else x, 1 - y if p & 2 else y, 1 - c if p & 1 else c)
            for a in range(self.n):
                sem = dict(send_sem=send_sems.at[7 * a + p - 1], recv_sem=recv_sems.at[7 * a + p - 1],
                           device_id=peer, device_id_type=pl.DeviceIdType.MESH)
                sends.append(pltpu.make_async_remote_copy(
                    src_ref=ins[a].at[_linear(peer)], dst_ref=outs[a].at[me], **sem))
                recvs.append(pltpu.make_async_remote_copy(
                    src_ref=ins[a].at[_linear(peer)], dst_ref=outs[a].at[_linear(peer)], **sem))
        return mine, sends, recvs

    def start(self, ins, outs, sems):
        mine, sends, _ = self._setup(ins, outs, sems)
        for cp in mine + sends:
            cp.start()

    def finish(self, ins, outs, sems):
        mine, sends, recvs = self._setup(ins, outs, sems)
        for cp in recvs:
            cp.wait_recv()
        for cp in sends:
            cp.wait_send()
        for cp in mine:
            cp.wait()


def _rider_scratch(rider):
    return [pltpu.SemaphoreType.DMA((7 * rider.n,)), pltpu.SemaphoreType.DMA((7 * rider.n,)),
            pltpu.SemaphoreType.DMA((rider.n,))]


def _communicate(name, rider):
    na = rider.n

    def body(*refs):
        ins, outs, sems = refs[:na], refs[na:2 * na], refs[2 * na:]
        rider.start(ins, outs, sems)
        rider.finish(ins, outs, sems)

    return pl.pallas_call(
        body, name=name, out_shape=rider.out_shape, in_specs=[ANY] * na, out_specs=[ANY] * na,
        scratch_shapes=_rider_scratch(rider),
    )(*rider.arrs)


def _call(body, rider=None, *, name, grid, in_specs, out_specs, out_shape, scratch_shapes=(), compiler_params, args):
    if rider is None:
        out = pl.pallas_call(body, name=name, grid=grid, in_specs=in_specs, out_specs=out_specs, out_shape=out_shape,
                             scratch_shapes=list(scratch_shapes), compiler_params=compiler_params)(*args)
        return out, None
    n_in, n_out, n_scr, na = len(in_specs), len(out_specs), len(scratch_shapes), rider.n

    def carried(*refs):
        ins, refs = refs[:n_in], refs[n_in:]
        r_ins, refs = refs[:na], refs[na:]
        outs, refs = refs[:n_out], refs[n_out:]
        r_outs, refs = refs[:na], refs[na:]
        scratch, sems = refs[:n_scr], refs[n_scr:]
        ids = [pl.program_id(ax) for ax in range(len(grid))]
        first, last = ids[0] == 0, ids[0] == grid[0] - 1
        for ax in range(1, len(grid)):
            first, last = first & (ids[ax] == 0), last & (ids[ax] == grid[ax] - 1)

        @pl.when(first)
        def _():
            rider.start(r_ins, r_outs, sems)

        body(*ins, *outs, *scratch)

        @pl.when(last)
        def _():
            rider.finish(r_ins, r_outs, sems)

    out = pl.pallas_call(
        carried, name=name, grid=grid, in_specs=list(in_specs) + [ANY] * na, out_specs=list(out_specs) + [ANY] * na,
        out_shape=list(out_shape) + rider.out_shape, scratch_shapes=list(scratch_shapes) + _rider_scratch(rider),
        compiler_params=compiler_params)(*args, *rider.arrs)
    return out[:n_out], out[n_out:]


def _mm(name, a, w, trans_b, out_dtype, tm, tn):
    m, k = a.shape
    n = w.shape[0] if trans_b else w.shape[1]

    def body(a_ref, w_ref, o_ref):
        o_ref[...] = _dot(a_ref[...], w_ref[...], trans_b=trans_b).astype(o_ref.dtype)

    w_spec = pl.BlockSpec((tn, k), lambda j, i: (j, 0)) if trans_b else pl.BlockSpec((k, tn), lambda j, i: (0, j))
    return pl.pallas_call(
        body, name=name, grid=(n // tn, m // tm),
        in_specs=[pl.BlockSpec((tm, k), lambda j, i: (i, 0)), w_spec],
        out_specs=pl.BlockSpec((tm, tn), lambda j, i: (i, j)),
        out_shape=jax.ShapeDtypeStruct((m, n), out_dtype),
        compiler_params=_params(("parallel", "parallel"), VMEM_BIG),
    )(a, w)


def _mm_tn(name, groups, b, tk):
    s, d = b.shape
    parts = [p for grp in groups for p in grp]
    owner = [g for g, grp in enumerate(groups) for _ in grp]
    width = sum(p.shape[1] for p in groups[0])
    offsets = []
    for grp in groups:
        at = 0
        for p in grp:
            offsets.append(at)
            at += p.shape[1]
        assert at == width
    np_, nk = len(parts), s // tk

    def body(*refs):
        a_refs, b_ref, o_ref, acc_ref = refs[:np_], refs[np_], refs[np_ + 1], refs[np_ + 2]
        g, kk = pl.program_id(0), pl.program_id(1)

        @pl.when(kk == 0)
        def _():
            acc_ref[...] = jnp.zeros_like(acc_ref)

        for grp in range(len(groups)):
            @pl.when(g == grp)
            def _(grp=grp):
                for p in range(np_):
                    if owner[p] == grp:
                        rows = slice(offsets[p], offsets[p] + parts[p].shape[1])
                        acc_ref[rows, :] += _dot(a_refs[p][...].astype(BF16), b_ref[...], trans_a=True)

        @pl.when(kk == nk - 1)
        def _():
            o_ref[...] = acc_ref[...].astype(o_ref.dtype)

    def a_spec(p):
        return pl.BlockSpec((tk, parts[p].shape[1]), lambda g, kk: (jnp.where(g == owner[p], kk, 0), 0))

    return pl.pallas_call(
        body, name=name, grid=(len(groups), nk),
        in_specs=[a_spec(p) for p in range(np_)] + [pl.BlockSpec((tk, d), lambda g, kk: (kk, 0))],
        out_specs=pl.BlockSpec((width, d), lambda g, kk: (g, 0)),
        out_shape=jax.ShapeDtypeStruct((len(groups) * width, d), BF16),
        scratch_shapes=[pltpu.VMEM((width, d), F32)],
        compiler_params=_params(("arbitrary", "arbitrary"), VMEM_BIG),
    )(*parts, b)


def _norm_fwd(x, g1, rider=None):
    s, d = x.shape
    tm = _tile(s, 512, 8)

    def body(x_ref, g_ref, h_ref):
        xv = x_ref[...]
        r = lax.rsqrt(jnp.mean(xv * xv, axis=-1, keepdims=True) + EPS)
        h_ref[...] = (xv * r * g_ref[...]).astype(BF16)

    row = pl.BlockSpec((tm, d), lambda i: (i, 0))
    (h,), carried = _call(
        body, rider, name="norm_fwd", grid=(s // tm,),
        in_specs=[row, pl.BlockSpec(g1.shape, lambda i: (0, 0))], out_specs=[row],
        out_shape=[jax.ShapeDtypeStruct((s, d), BF16)],
        compiler_params=_params(("arbitrary",), VMEM_BIG), args=(x, g1))
    return h, carried


def _proj_fwd(h, w_in_t, qg, kg, ones, rider=None):
    s, d = h.shape
    tm = _tile(s, 512, 8)
    c2, aw = 2 * CONV_CH, ATTN_W

    def body(h_ref, w_ref, qg_ref, kg_ref, ones_ref, a_ref, qk_ref, qkv_ref):
        proj = _dot(h_ref[...], w_ref[...], trans_b=True)
        a_ref[...] = proj[:, :c2]
        qk_ref[...] = proj[:, c2:c2 + 2 * aw]
        q, k = proj[:, c2:c2 + aw], proj[:, c2 + aw:c2 + 2 * aw]
        rq = lax.rsqrt(_head_sum(q * q, ones_ref) * (1.0 / HEAD_DIM) + EPS)
        rk = lax.rsqrt(_head_sum(k * k, ones_ref) * (1.0 / HEAD_DIM) + EPS)
        qkv_ref[:, :aw] = q * rq * qg_ref[...] * (HEAD_DIM ** -0.5)
        qkv_ref[:, aw:2 * aw] = k * rk * kg_ref[...]
        qkv_ref[:, 2 * aw:] = proj[:, c2 + 2 * aw:]

    row = lambda w: pl.BlockSpec((tm, w), lambda i: (i, 0))
    full = lambda a: pl.BlockSpec(a.shape, lambda i: (0, 0))
    outs, carried = _call(
        body, rider, name="proj_fwd", grid=(s // tm,),
        in_specs=[row(d), full(w_in_t), full(qg), full(kg), full(ones)],
        out_specs=[row(c2), row(2 * aw), row(3 * aw)],
        out_shape=[jax.ShapeDtypeStruct((s, c2), F32), jax.ShapeDtypeStruct((s, 2 * aw), F32),
                   jax.ShapeDtypeStruct((s, 3 * aw), F32)],
        compiler_params=_params(("arbitrary",), VMEM_BIG),
        args=(h, w_in_t, qg, kg, ones))
    return (*outs, carried)


def _mix_out(u, o, w_out, x, g2):
    s, d = x.shape
    tm = _tile(s, 512, 8)

    def body(u_ref, o_ref, w_ref, x_ref, g_ref, x1_ref, h2_ref):
        x1 = x_ref[...] + _dot(u_ref[...], w_ref[:CONV_CH, :]) + _dot(o_ref[...].astype(BF16), w_ref[CONV_CH:, :])
        x1_ref[...] = x1
        r = lax.rsqrt(jnp.mean(x1 * x1, axis=-1, keepdims=True) + EPS)
        h2_ref[...] = (x1 * r * g_ref[...]).astype(BF16)

    row = lambda w: pl.BlockSpec((tm, w), lambda i: (i, 0))
    full = lambda a: pl.BlockSpec(a.shape, lambda i: (0, 0))
    return pl.pallas_call(
        body, name="mix_out", grid=(s // tm,),
        in_specs=[row(CONV_CH), row(ATTN_W), full(w_out), row(d), full(g2)],
        out_specs=[row(d), row(d)],
        out_shape=[jax.ShapeDtypeStruct((s, d), F32), jax.ShapeDtypeStruct((s, d), BF16)],
        compiler_params=_params(("parallel",), VMEM_BIG),
    )(u, o, w_out, x, g2)


def _down_loss(act, w_down, x1, target):
    s, d = x1.shape
    f = act.shape[1]
    tm = _tile(s, 512, 8)

    def body(a_ref, w_ref, x1_ref, t_ref, loss_ref, dyf_ref, dyb_ref):
        @pl.when(pl.program_id(0) == 0)
        def _():
            loss_ref[...] = jnp.zeros_like(loss_ref)

        diff = x1_ref[...] + _dot(a_ref[...], w_ref[...]) - t_ref[...]
        sq = jnp.sum(jnp.sum(diff * diff, axis=1, keepdims=True), axis=0, keepdims=True)
        loss_ref[...] += jnp.broadcast_to(sq * (0.5 / d), loss_ref.shape)
        dy = diff * (1.0 / d)
        dyf_ref[...] = dy
        dyb_ref[...] = dy.astype(BF16)

    row = lambda w: pl.BlockSpec((tm, w), lambda i: (i, 0))
    return pl.pallas_call(
        body, name="down_loss", grid=(s // tm,),
        in_specs=[row(f), pl.BlockSpec((f, d), lambda i: (0, 0)), row(d), row(d)],
        out_specs=[pl.BlockSpec((8, 128), lambda i: (0, 0)), row(d), row(d)],
        out_shape=[jax.ShapeDtypeStruct((8, 128), F32), jax.ShapeDtypeStruct((s, d), F32),
                   jax.ShapeDtypeStruct((s, d), BF16)],
        compiler_params=_params(("arbitrary",), VMEM_BIG),
    )(act, w_down, x1, target)


def _norm_bwd_tail(dh, xv, g, resid, d):
    r = lax.rsqrt(jnp.mean(xv * xv, axis=-1, keepdims=True) + EPS)
    xh = xv * r
    gd = dh * g
    dx = r * (gd - xh * (jnp.sum(gd * xh, axis=-1, keepdims=True) * (1.0 / d)))
    return resid + dx, jnp.sum(dh * xh, axis=0, keepdims=True)


def _up_bwd(dg, dv, w_up_t, x1, dyf, g2):
    s, d = x1.shape
    f = dg.shape[1]
    tm = _tile(s, 512, 8)

    def body(dg_ref, dv_ref, w_ref, x1_ref, dy_ref, g_ref, dxf_ref, dxb_ref, gg_ref):
        @pl.when(pl.program_id(0) == 0)
        def _():
            gg_ref[...] = jnp.zeros_like(gg_ref)

        dh = _dot(dg_ref[...], w_ref[:f, :]) + _dot(dv_ref[...], w_ref[f:, :])
        dx, gg = _norm_bwd_tail(dh, x1_ref[...], g_ref[...], dy_ref[...], d)
        dxf_ref[...] = dx
        dxb_ref[...] = dx.astype(BF16)
        gg_ref[...] += gg

    row = lambda w: pl.BlockSpec((tm, w), lambda i: (i, 0))
    full = lambda a: pl.BlockSpec(a.shape, lambda i: (0, 0))
    return pl.pallas_call(
        body, name="up_bwd", grid=(s // tm,),
        in_specs=[row(f), row(f), full(w_up_t), row(d), row(d), full(g2)],
        out_specs=[row(d), row(d), pl.BlockSpec((1, d), lambda i: (0, 0))],
        out_shape=[jax.ShapeDtypeStruct((s, d), F32), jax.ShapeDtypeStruct((s, d), BF16),
                   jax.ShapeDtypeStruct((1, d), F32)],
        compiler_params=_params(("arbitrary",), VMEM_BIG),
    )(dg, dv, w_up_t, x1, dyf, g2)


def _proj_bwd(parts, w_in_t, x, dx1, g1, rider=None):
    s, d = x.shape
    widths = [p.shape[1] for p in parts]
    starts = [sum(widths[:i]) for i in range(len(parts))]
    np_ = len(parts)
    tm = _tile(s, 512, 8)

    def body(*refs):
        p_refs = refs[:np_]
        w_ref, x_ref, r_ref, g_ref, gx_ref, gg_ref = refs[np_:]

        @pl.when(pl.program_id(0) == 0)
        def _():
            gg_ref[...] = jnp.zeros_like(gg_ref)

        dh = _dot(p_refs[0][...], w_ref[:widths[0], :])
        for p in range(1, np_):
            dh = dh + _dot(p_refs[p][...], w_ref[starts[p]:starts[p] + widths[p], :])
        dx, gg = _norm_bwd_tail(dh, x_ref[...], g_ref[...], r_ref[...], d)
        gx_ref[...] = dx
        gg_ref[...] += gg

    row = lambda w: pl.BlockSpec((tm, w), lambda i: (i, 0))
    full = lambda a: pl.BlockSpec(a.shape, lambda i: (0, 0))
    outs, carried = _call(
        body, rider, name="proj_bwd", grid=(s // tm,),
        in_specs=[row(w) for w in widths] + [full(w_in_t), row(d), row(d), full(g1)],
        out_specs=[row(d), pl.BlockSpec((1, d), lambda i: (0, 0))],
        out_shape=[jax.ShapeDtypeStruct((s, d), F32), jax.ShapeDtypeStruct((1, d), F32)],
        compiler_params=_params(("arbitrary",), VMEM_BIG),
        args=(*parts, w_in_t, x, dx1, g1))
    return (*outs, carried)


CONV_CHUNK = 32
SUBLANES = 8


def _glu(av):
    return av[:, :CONV_CH] * _sigmoid(av[:, CONV_CH:])


def _chunks(n, size=CONV_CHUNK):
    return [(r0, min(size, n - r0)) for r0 in range(0, n, size)]


def _fill_shifted(sbuf, rows):
    for r in range(1, SUBLANES):
        for r0, n in _chunks(rows, 64):
            sbuf[r, r0:r0 + n, :] = sbuf[0, r0 + r:r0 + r + n, :]


def _tap(sbuf, offset, r0, rows):
    q, r = divmod(offset, SUBLANES)
    return sbuf[r, SUBLANES * q + r0:SUBLANES * q + r0 + rows, :]


def _layer_norm_stats(u1):
    mu = jnp.mean(u1, axis=-1, keepdims=True)
    cen = u1 - mu
    rstd = lax.rsqrt(jnp.mean(cen * cen, axis=-1, keepdims=True) + EPS)
    return cen * rstd, rstd


def _conv_fwd(a, cw, cb, cg, cbeta, rider=None):
    s = a.shape[0]
    tm = _tile(s, 256, CONV_CHUNK)
    hb = tm // CONV_HALO

    def body(a_ref, ap_ref, cw_ref, cb_ref, cg_ref, cbeta_ref, u1_ref, u_ref, ubuf):
        i = pl.program_id(0)
        ubuf[0, 0:CONV_HALO, :] = jnp.where(i > 0, _glu(ap_ref[...]), 0.0)
        for r0, n in _chunks(tm):
            ubuf[0, CONV_HALO + r0:CONV_HALO + r0 + n, :] = _glu(a_ref[r0:r0 + n, :])
        _fill_shifted(ubuf, tm + CONV_HALO - SUBLANES)
        for r0, n in _chunks(tm):
            acc = jnp.broadcast_to(cb_ref[...], (n, CONV_CH))
            for k in range(CONV_K):
                acc = acc + cw_ref[k:k + 1, :] * _tap(ubuf, 2 + k, r0, n)
            u1_ref[r0:r0 + n, :] = acc
            xh, _ = _layer_norm_stats(acc)
            z = xh * cg_ref[...] + cbeta_ref[...]
            u_ref[r0:r0 + n, :] = (z * _sigmoid(z)).astype(BF16)

    full = lambda t: pl.BlockSpec(t.shape, lambda i: (0, 0))
    outs, carried = _call(
        body, rider, name="conv_fwd", grid=(s // tm,),
        in_specs=[pl.BlockSpec((tm, 2 * CONV_CH), lambda i: (i, 0)),
                  pl.BlockSpec((CONV_HALO, 2 * CONV_CH), lambda i: (jnp.maximum(i * hb - 1, 0), 0)),
                  full(cw), full(cb), full(cg), full(cbeta)],
        out_specs=[pl.BlockSpec((tm, CONV_CH), lambda i: (i, 0))] * 2,
        out_shape=[jax.ShapeDtypeStruct((s, CONV_CH), F32), jax.ShapeDtypeStruct((s, CONV_CH), BF16)],
        scratch_shapes=[pltpu.VMEM((SUBLANES, CONV_HALO + tm, CONV_CH), F32)],
        compiler_params=_params(("arbitrary",), VMEM_BIG),
        args=(a, a, cw, cb, cg, cbeta))
    return (*outs, carried)


def _conv_bwd(a, u1, d_cat, cw, cg, cbeta, rider=None):
    s = a.shape[0]
    tm = _tile(s, 256, CONV_CHUNK)
    hb = tm // CONV_HALO
    last_halo = s // CONV_HALO - 1
    nt = s // tm
    te = tm + CONV_HALO

    def body(a_ref, ap_ref, u1_ref, u1n_ref, du_ref, dun_ref, cw_ref, cg_ref, cbeta_ref,
             da_ref, gw_ref, gb_ref, gg_ref, gbeta_ref, ubuf, dbuf):
        i = pl.program_id(0)

        @pl.when(i == 0)
        def _():
            gw_ref[...] = jnp.zeros_like(gw_ref)
            gb_ref[...] = jnp.zeros_like(gb_ref)
            gg_ref[...] = jnp.zeros_like(gg_ref)
            gbeta_ref[...] = jnp.zeros_like(gbeta_ref)

        def du1_of(u1, du):
            xh, rstd = _layer_norm_stats(u1)
            z = xh * cg_ref[...] + cbeta_ref[...]
            sz = _sigmoid(z)
            dz = du * (sz * (1.0 + z * (1.0 - sz)))
            dxh = dz * cg_ref[...]
            du1 = rstd * (dxh - jnp.mean(dxh, axis=-1, keepdims=True)
                          - xh * jnp.mean(dxh * xh, axis=-1, keepdims=True))
            return du1, dz, xh

        gg, gbeta, gb = [jnp.zeros((SUBLANES, CONV_CH), F32) for _ in range(3)]
        for r0, n in _chunks(tm):
            du1, dz, xh = du1_of(u1_ref[r0:r0 + n, :], du_ref[r0:r0 + n, :])
            gg, gbeta, gb = gg + _fold8(dz * xh), gbeta + _fold8(dz), gb + _fold8(du1)
            dbuf[0, r0:r0 + n, :] = du1
        gg_ref[...] += jnp.sum(gg, axis=0, keepdims=True)
        gbeta_ref[...] += jnp.sum(gbeta, axis=0, keepdims=True)
        gb_ref[...] += jnp.sum(gb, axis=0, keepdims=True)
        du1n, _, _ = du1_of(u1n_ref[...], jnp.where(i < nt - 1, dun_ref[...], 0.0))
        dbuf[0, tm:, :] = du1n
        _fill_shifted(dbuf, tm + CONV_HALO - SUBLANES)

        ubuf[0, 0:CONV_HALO, :] = jnp.where(i > 0, _glu(ap_ref[...]), 0.0)
        for r0, n in _chunks(tm):
            ubuf[0, CONV_HALO + r0:CONV_HALO + r0 + n, :] = _glu(a_ref[r0:r0 + n, :])
        _fill_shifted(ubuf, tm + CONV_HALO - SUBLANES)

        for k in range(CONV_K):
            part = jnp.zeros((SUBLANES, CONV_CH), F32)
            for r0, n in _chunks(tm):
                part = part + _fold8(dbuf[0, r0:r0 + n, :] * _tap(ubuf, 2 + k, r0, n))
            gw_ref[k:k + 1, :] += jnp.sum(part, axis=0, keepdims=True)

        for r0, n in _chunks(tm):
            acc = jnp.zeros((n, CONV_CH), F32)
            for k in range(CONV_K):
                acc = acc + cw_ref[k:k + 1, :] * _tap(dbuf, CONV_K - 1 - k, r0, n)
            avc = a_ref[r0:r0 + n, :CONV_CH]
            sgc = _sigmoid(a_ref[r0:r0 + n, CONV_CH:])
            da_ref[r0:r0 + n, :CONV_CH] = (acc * sgc).astype(BF16)
            da_ref[r0:r0 + n, CONV_CH:] = (acc * avc * sgc * (1.0 - sgc)).astype(BF16)

    full = lambda t: pl.BlockSpec(t.shape, lambda i: (0, 0))
    vec = pl.BlockSpec((1, CONV_CH), lambda i: (0, 0))
    nxt = lambda i: (jnp.minimum((i + 1) * hb, last_halo), 0)
    outs, carried = _call(
        body, rider, name="conv_bwd", grid=(nt,),
        in_specs=[pl.BlockSpec((tm, 2 * CONV_CH), lambda i: (i, 0)),
                  pl.BlockSpec((CONV_HALO, 2 * CONV_CH), lambda i: (jnp.maximum(i * hb - 1, 0), 0)),
                  pl.BlockSpec((tm, CONV_CH), lambda i: (i, 0)), pl.BlockSpec((CONV_HALO, CONV_CH), nxt),
                  pl.BlockSpec((tm, CONV_CH), lambda i: (i, 0)), pl.BlockSpec((CONV_HALO, CONV_CH), nxt),
                  full(cw), full(cg), full(cbeta)],
        out_specs=[pl.BlockSpec((tm, 2 * CONV_CH), lambda i: (i, 0)),
                   pl.BlockSpec((CONV_HALO, CONV_CH), lambda i: (0, 0)), vec, vec, vec],
        out_shape=[jax.ShapeDtypeStruct((s, 2 * CONV_CH), BF16), jax.ShapeDtypeStruct((CONV_HALO, CONV_CH), F32),
                   jax.ShapeDtypeStruct((1, CONV_CH), F32), jax.ShapeDtypeStruct((1, CONV_CH), F32),
                   jax.ShapeDtypeStruct((1, CONV_CH), F32)],
        scratch_shapes=[pltpu.VMEM((SUBLANES, CONV_HALO + tm, CONV_CH), F32), pltpu.VMEM((SUBLANES, te, CONV_CH), F32)],
        compiler_params=_params(("arbitrary",), VMEM_BIG),
        args=(a, a, u1, u1, d_cat, d_cat, cw, cg, cbeta))
    return (*outs, carried)


def _ff_tiles(s, f):
    return _tile(s, 256, 8), _tile(f, 1408, 128)


FF_CHUNK, FF_LANES = 64, 128
MXU_WIDTH = 256
FWD_STRIP_ROWS = 256


def _row_chunks(n):
    return [(r0, min(FF_CHUNK, n - r0)) for r0 in range(0, n, FF_CHUNK)]


def _fold8(v):
    acc = v[0:8]
    for r in range(8, v.shape[0], 8):
        acc = acc + v[r:r + 8]
    return acc


def _ff_conv(pre_buf, w_ref, b_ref, half, r0, rows, cols):
    acc = b_ref[half:half + 1, cols]
    for k in range(FF_K):
        off = FF_HALO - (FF_K - 1) + k + r0
        acc = acc + w_ref[k, half:half + 1, cols] * pre_buf[half, off:off + rows, cols]
    return acc


def _up_fwd_act(h2, w_up_t, fw, fb, f):
    s, d = h2.shape
    tm, tn = _tile(s, 512, 8), _tile(f, 1408, 128)
    nc, nt = f // tn, s // tm

    def body(h_ref, wg_ref, wv_ref, w_ref, b_ref, g_ref, v_ref, act_ref, xbuf):
        @pl.when(pl.program_id(1) == 0)
        def _():
            xbuf[...] = jnp.zeros_like(xbuf)

        strips = [(slice(c0, c0 + FF_LANES), r0, rows) for c0 in range(0, tn, FF_LANES)
                  for r0, rows in _chunks(tm, FWD_STRIP_ROWS)]
        pieces = [(out, wt, slice(c0, min(c0 + MXU_WIDTH, tn))) for out, wt in ((g_ref, wg_ref), (v_ref, wv_ref))
                  for c0 in range(0, tn, MXU_WIDTH)]
        per_piece = -(-len(strips) // len(pieces))
        hv = h_ref[...]
        for n, (out, wt, wcols) in enumerate(pieces):
            out[:, wcols] = _dot(hv, wt[wcols, :], trans_b=True)
            for cols, r0, rows in strips[n * per_piece:(n + 1) * per_piece]:
                gate = _ff_conv(xbuf, w_ref, b_ref, 0, r0, rows, cols)
                val = _ff_conv(xbuf, w_ref, b_ref, 1, r0, rows, cols)
                act_ref[r0:r0 + rows, cols] = (gate * _sigmoid(gate) * val).astype(BF16)
        for half, ref in enumerate((g_ref, v_ref)):
            xbuf[half, 0:FF_HALO, :] = xbuf[half, tm:tm + FF_HALO, :]
            xbuf[half, FF_HALO:, :] = ref[...]

    cur = lambda j, i: (jnp.minimum(i, nt - 1), j)
    return pl.pallas_call(
        body, name="up_fwd_act", grid=(nc, nt + 1),
        in_specs=[pl.BlockSpec((tm, d), lambda j, i: (jnp.minimum(i, nt - 1), 0)),
                  pl.BlockSpec((tn, d), lambda j, i: (j, 0)), pl.BlockSpec((tn, d), lambda j, i: (j + nc, 0)),
                  pl.BlockSpec((FF_K, 2, tn), lambda j, i: (0, 0, j)), pl.BlockSpec((2, tn), lambda j, i: (0, j))],
        out_specs=[pl.BlockSpec((tm, tn), cur), pl.BlockSpec((tm, tn), cur),
                   pl.BlockSpec((tm, tn), lambda j, i: (jnp.maximum(i - 1, 0), j))],
        out_shape=[jax.ShapeDtypeStruct((s, f), F32), jax.ShapeDtypeStruct((s, f), F32), jax.ShapeDtypeStruct((s, f), BF16)],
        scratch_shapes=[pltpu.VMEM((2, FF_HALO + tm, tn), F32)],
        compiler_params=_params(("arbitrary", "arbitrary"), VMEM_BIG),
    )(h2, w_up_t, w_up_t, fw, fb)


def _ff_bwd(up_g, up_v, d_act, fw, fb, rider=None):
    s, f = up_g.shape
    tm, tc = _ff_tiles(s, f)
    hb = tm // FF_HALO
    nt = s // tm
    last_halo = s // FF_HALO - 1
    te = tm + FF_HALO

    def body(g_ref, gp_ref, gn_ref, v_ref, vp_ref, vn_ref, da_ref, dan_ref, w_ref, b_ref,
             dg_ref, dv_ref, gw_ref, gb_ref, buf, shifted, dbuf, dshifted, dabuf):
        i = pl.program_id(1)

        @pl.when(i == 0)
        def _():
            gw_ref[...] = jnp.zeros_like(gw_ref)
            gb_ref[...] = jnp.zeros_like(gb_ref)

        for half, (m_ref, p_ref, n_ref) in enumerate(((g_ref, gp_ref, gn_ref), (v_ref, vp_ref, vn_ref))):
            buf[half, 0:FF_HALO, :] = jnp.where(i > 0, p_ref[...], 0.0)
            buf[half, FF_HALO:FF_HALO + tm, :] = m_ref[...]
            buf[half, FF_HALO + tm:, :] = n_ref[...]
            for k in range(FF_K - 1):
                lo = FF_HALO - (FF_K - 1) + k
                for r0, rows in _row_chunks(te):
                    shifted[half, k, r0:r0 + rows, :] = buf[half, lo + r0:lo + r0 + rows, :]
        dabuf[0:tm, :] = da_ref[...]
        dabuf[tm:, :] = jnp.where(i < nt - 1, dan_ref[...], 0.0)

        for c0 in range(0, tc, FF_LANES):
            cols = slice(c0, c0 + FF_LANES)
            gb = [jnp.zeros((8, FF_LANES), F32) for _ in range(2)]
            gw = [[jnp.zeros((8, FF_LANES), F32) for _ in range(FF_K)] for _ in range(2)]
            for r0, rows in _row_chunks(te):
                taps = [[shifted[half, 0, r0:r0 + rows, cols], shifted[half, 1, r0:r0 + rows, cols],
                         buf[half, FF_HALO + r0:FF_HALO + r0 + rows, cols]] for half in range(2)]
                gate, val = [b_ref[half:half + 1, cols] + sum(w_ref[k, half:half + 1, cols] * taps[half][k]
                                                              for k in range(FF_K)) for half in range(2)]
                da = dabuf[r0:r0 + rows, cols]
                sg = _sigmoid(gate)
                dup = [da * val * (sg * (1.0 + gate * (1.0 - sg))), da * (gate * sg)]
                for half in range(2):
                    dbuf[half, r0:r0 + rows, cols] = dup[half]
                    if r0 < tm:
                        gb[half] = gb[half] + _fold8(dup[half])
                        for k in range(FF_K):
                            gw[half][k] = gw[half][k] + _fold8(dup[half] * taps[half][k])
            for half, out_ref in enumerate((dg_ref, dv_ref)):
                gb_ref[half:half + 1, cols] += jnp.sum(gb[half], axis=0, keepdims=True)
                for k in range(FF_K):
                    gw_ref[k, half:half + 1, cols] += jnp.sum(gw[half][k], axis=0, keepdims=True)
                for k in range(1, FF_K):
                    for r0, rows in _row_chunks(tm):
                        dshifted[half, k - 1, r0:r0 + rows, cols] = dbuf[half, k + r0:k + r0 + rows, cols]
                for r0, rows in _row_chunks(tm):
                    acc = (w_ref[2, half:half + 1, cols] * dbuf[half, r0:r0 + rows, cols]
                           + w_ref[1, half:half + 1, cols] * dshifted[half, 0, r0:r0 + rows, cols]
                           + w_ref[0, half:half + 1, cols] * dshifted[half, 1, r0:r0 + rows, cols])
                    out_ref[r0:r0 + rows, cols] = acc.astype(BF16)

    main = pl.BlockSpec((tm, tc), lambda j, i: (i, j))
    prev = pl.BlockSpec((FF_HALO, tc), lambda j, i: (jnp.maximum(i * hb - 1, 0), j))
    nxt = pl.BlockSpec((FF_HALO, tc), lambda j, i: (jnp.minimum((i + 1) * hb, last_halo), j))
    (dg, dv, gw, gb), carried = _call(
        body, rider, name="ff_bwd", grid=(f // tc, nt),
        in_specs=[main, prev, nxt, main, prev, nxt, main, nxt,
                  pl.BlockSpec((FF_K, 2, tc), lambda j, i: (0, 0, j)), pl.BlockSpec((2, tc), lambda j, i: (0, j))],
        out_specs=[main, main, pl.BlockSpec((FF_K, 2, tc), lambda j, i: (0, 0, j)),
                   pl.BlockSpec((2, tc), lambda j, i: (0, j))],
        out_shape=[jax.ShapeDtypeStruct((s, f), BF16), jax.ShapeDtypeStruct((s, f), BF16),
                   jax.ShapeDtypeStruct((FF_K, 2, f), F32), jax.ShapeDtypeStruct((2, f), F32)],
        scratch_shapes=[pltpu.VMEM((2, FF_HALO + te, tc), F32), pltpu.VMEM((2, FF_K - 1, te, tc), F32),
                        pltpu.VMEM((2, te, tc), F32), pltpu.VMEM((2, FF_K - 1, tm, tc), F32), pltpu.VMEM((te, tc), F32)],
        compiler_params=_params(("arbitrary", "arbitrary"), VMEM_BIG),
        args=(up_g, up_g, up_g, up_v, up_v, up_v, d_act, d_act, fw, fb))
    return dg, dv, gw, gb, carried


ATT_TILE = Q_BLOCK * max(PATTERN_DILATIONS)


def _stream_rows(start, d, n=Q_BLOCK):
    return pl.ds(start, n) if d == 1 else pl.ds(start, n, stride=d)


def _band_geometry():
    qi = lax.broadcasted_iota(jnp.int32, (Q_BLOCK, 2 * Q_BLOCK), 0)
    ki = lax.broadcasted_iota(jnp.int32, (Q_BLOCK, 2 * Q_BLOCK), 1)
    delta = qi + Q_BLOCK - ki
    return (delta >= 0) & (delta <= Q_BLOCK), delta.astype(F32), ki


def _stream_blocks(d):
    out, nb = [], ATT_TILE // (Q_BLOCK * d)
    for r in range(d):
        for b in range(nb):
            start = b * Q_BLOCK * d + r
            out.append((start, start - Q_BLOCK * d if b > 0 else None, b == nb - 1))
    return out


def _attn_forward(qkv, slopes, rider=None):
    s = qkv.shape[0]
    nt = s // ATT_TILE
    nhp = ATTN_W // Q_BLOCK

    def body(sl_ref, q_ref, k_ref, kp_ref, v_ref, vp_ref, o_ref, l_ref):
        hp, i = pl.program_id(0), pl.program_id(1)
        head0 = lax.broadcasted_iota(jnp.int32, (Q_BLOCK, Q_BLOCK), 1) < HEAD_DIM
        head0_k = lax.broadcasted_iota(jnp.int32, (2 * Q_BLOCK, Q_BLOCK), 1) < HEAD_DIM
        valid, dist, ki = _band_geometry()
        first_key = jnp.where(i == 0, Q_BLOCK, 0)
        for d in PATTERN_DILATIONS:
            biases = [jnp.where(valid, dist * (-sl_ref[2 * hp + hh] * d), NEG) for hh in range(2)]
            for start, prev, _ in _stream_blocks(d):
                rows = _stream_rows(start, d)
                if prev is None:
                    prow = _stream_rows(ATT_TILE - Q_BLOCK * d + start, d)
                    kp, vp = kp_ref[prow, :].astype(BF16), vp_ref[prow, :].astype(BF16)
                else:
                    kp, vp = kc, vc
                kc, vc = k_ref[rows, :].astype(BF16), v_ref[rows, :].astype(BF16)
                qv = q_ref[rows, :].astype(BF16)
                k2 = jnp.concatenate([kp, kc], axis=0)
                v2 = jnp.concatenate([vp, vc], axis=0)
                res, mxs = [], []
                for hh in range(2):
                    mine = head0 if hh == 0 else jnp.logical_not(head0)
                    mine_k = head0_k if hh == 0 else jnp.logical_not(head0_k)
                    sc = _dot(jnp.where(mine, qv, jnp.zeros_like(qv)), k2, trans_b=True) + biases[hh]
                    if prev is None:
                        sc = jnp.where(ki < first_key, NEG, sc)
                    mx = jnp.max(sc, axis=1, keepdims=True)
                    p = jnp.exp(sc - mx).astype(BF16)
                    res.append(_dot(p, jnp.where(mine_k, v2, jnp.ones_like(v2))))
                    mxs.append(mx)
                num = jnp.where(head0, res[0], res[1])
                den = pltpu.roll(jnp.where(head0, res[1], res[0]), HEAD_DIM, 1)
                o_new = num / den
                l_new = jnp.where(head0, mxs[0], mxs[1]) + jnp.log(den)
                if d != PATTERN_DILATIONS[0]:
                    oa, la = o_ref[rows, :], l_ref[rows, :]
                    mm = jnp.maximum(la, l_new)
                    wa, wn = jnp.exp(la - mm), jnp.exp(l_new - mm)
                    o_new = (wa * oa + wn * o_new) / (wa + wn)
                    l_new = mm + jnp.log(wa + wn)
                o_ref[rows, :] = o_new
                l_ref[rows, :] = l_new

    def col(off):
        return pl.BlockSpec((ATT_TILE, Q_BLOCK), lambda hp, i: (i, off + hp))

    def col_prev(off):
        return pl.BlockSpec((ATT_TILE, Q_BLOCK), lambda hp, i: (jnp.maximum(i - 1, 0), off + hp))

    outs, carried = _call(
        body, rider, name="attn_fwd", grid=(nhp, nt),
        in_specs=[pl.BlockSpec(memory_space=pltpu.SMEM), col(0), col(nhp), col_prev(nhp), col(2 * nhp), col_prev(2 * nhp)],
        out_specs=[col(0), col(0)],
        out_shape=[jax.ShapeDtypeStruct((s, ATTN_W), F32)] * 2,
        compiler_params=_params(("arbitrary", "arbitrary"), VMEM_BIG),
        args=(slopes, qkv, qkv, qkv, qkv, qkv))
    return (*outs, carried)


def _attn_backward(qkv, qk, qg, kg, d_cat, o, lse, slopes, ones2, rider=None):
    s = qkv.shape[0]
    nt = s // ATT_TILE
    nhp = ATTN_W // Q_BLOCK
    tt = ATT_TILE

    def body(sl_ref, q_ref, k_ref, kp_ref, v_ref, vp_ref, do_ref, o_ref, l_ref, ones_ref, qraw_ref, kraw_ref,
             qg_ref, kg_ref, dq_ref, dk_ref, dv_ref, gq_ref, gk_ref, dkacc, dvacc, dd, dqacc, kcarry, vcarry):
        hp, step = pl.program_id(0), pl.program_id(1)
        tile = nt - 1 - step
        head0 = lax.broadcasted_iota(jnp.int32, (Q_BLOCK, Q_BLOCK), 1) < HEAD_DIM
        valid, dist, ki = _band_geometry()
        first_key = jnp.where(tile == 0, Q_BLOCK, 0)

        @pl.when((step == 0) & (hp == 0))
        def _():
            gq_ref[...] = jnp.zeros_like(gq_ref)
            gk_ref[...] = jnp.zeros_like(gk_ref)

        @pl.when(step == 0)
        def _():
            kcarry[...] = jnp.zeros_like(kcarry)
            vcarry[...] = jnp.zeros_like(vcarry)

        @pl.when(step > 0)
        def _():
            kcarry[...] = dkacc[:tt, :]
            vcarry[...] = dvacc[:tt, :]

        dd[...] = _head_sum(do_ref[...] * o_ref[...], ones_ref)

        def key_grad(acc, at, value, first):
            if first:
                acc[at, :] = value
            else:
                acc[at, :] += value

        ki2 = jnp.concatenate([ki, ki], axis=0)
        for d in PATTERN_DILATIONS:
            first = d == PATTERN_DILATIONS[0]
            bias2 = jnp.concatenate([jnp.where(valid, dist * (-sl_ref[2 * hp + hh] * d), NEG) for hh in range(2)], axis=0)
            for start, prev, last in _stream_blocks(d):
                rows = _stream_rows(start, d)
                if prev is None:
                    prow = _stream_rows(ATT_TILE - Q_BLOCK * d + start, d)
                    kp, vp = kp_ref[prow, :].astype(BF16), vp_ref[prow, :].astype(BF16)
                else:
                    kp, vp = kc, vc
                kc, vc = k_ref[rows, :].astype(BF16), v_ref[rows, :].astype(BF16)
                qv = q_ref[rows, :].astype(BF16)
                dov = do_ref[rows, :].astype(BF16)
                lv, ddv = l_ref[rows, :], dd[rows, :]
                lsw, dsw = pltpu.roll(lv, HEAD_DIM, 1), pltpu.roll(ddv, HEAD_DIM, 1)
                k2 = jnp.concatenate([kp, kc], axis=0)
                v2 = jnp.concatenate([vp, vc], axis=0)
                zq = jnp.zeros_like(qv)
                q2 = jnp.concatenate([jnp.where(head0, qv, zq), jnp.where(head0, zq, qv)], axis=0)
                do2 = jnp.concatenate([jnp.where(head0, dov, zq), jnp.where(head0, zq, dov)], axis=0)
                lh = jnp.concatenate([jnp.where(head0, lv, lsw), jnp.where(head0, lsw, lv)], axis=0)
                dh = jnp.concatenate([jnp.where(head0, ddv, dsw), jnp.where(head0, dsw, ddv)], axis=0)
                sc = _dot(q2, k2, trans_b=True) + bias2
                if prev is None:
                    sc = jnp.where(ki2 < first_key, NEG, sc)
                p = jnp.exp(sc - jnp.concatenate([lh, lh], axis=1))
                ds = p * (_dot(do2, v2, trans_b=True) - jnp.concatenate([dh, dh], axis=1))
                dq2 = _dot(ds.astype(BF16), k2)
                dk2 = _dot(ds.T.astype(BF16), q2)
                dv2 = _dot(p.T.astype(BF16), do2)
                dq = jnp.where(head0, dq2[:Q_BLOCK], dq2[Q_BLOCK:])
                if first:
                    dqacc[rows, :] = dq
                else:
                    dqacc[rows, :] += dq
                before = _stream_rows(tt + start - Q_BLOCK * d, d)
                if prev is None:
                    key_grad(dkacc, before, dk2[:Q_BLOCK], first)
                    key_grad(dvacc, before, dv2[:Q_BLOCK], first)
                else:
                    key_grad(dkacc, before, kpend + dk2[:Q_BLOCK], first)
                    key_grad(dvacc, before, vpend + dv2[:Q_BLOCK], first)
                kpend, vpend = dk2[Q_BLOCK:], dv2[Q_BLOCK:]
                if last:
                    key_grad(dkacc, _stream_rows(tt + start, d), kpend, first)
                    key_grad(dvacc, _stream_rows(tt + start, d), vpend, first)
        for r0, n in _chunks(tt, 256):
            dv_ref[r0:r0 + n, :] = (dvacc[tt + r0:tt + r0 + n, :] + vcarry[r0:r0 + n, :]).astype(BF16)

        def norm_bwd(dy, raw, g, scale):
            r = lax.rsqrt(_head_sum(raw * raw, ones_ref) * (1.0 / HEAD_DIM) + EPS)
            xh = raw * r
            gd = dy * (g * scale)
            dx = r * (gd - xh * (_head_sum(gd * xh, ones_ref) * (1.0 / HEAD_DIM)))
            return dx, _fold8(dy * xh) * scale

        gq, gk = jnp.zeros((SUBLANES, Q_BLOCK), F32), jnp.zeros((SUBLANES, Q_BLOCK), F32)
        for r0, n in _chunks(tt, 256):
            dxq, pq = norm_bwd(dqacc[r0:r0 + n, :], qraw_ref[r0:r0 + n, :], qg_ref[...], HEAD_DIM ** -0.5)
            dxk, pk = norm_bwd(dkacc[tt + r0:tt + r0 + n, :] + kcarry[r0:r0 + n, :], kraw_ref[r0:r0 + n, :], kg_ref[...], 1.0)
            dq_ref[r0:r0 + n, :] = dxq.astype(BF16)
            dk_ref[r0:r0 + n, :] = dxk.astype(BF16)
            gq, gk = gq + pq, gk + pk
        for acc, out in ((gq, gq_ref), (gk, gk_ref)):
            row = jnp.sum(acc, axis=0, keepdims=True)
            out[...] += row + pltpu.roll(row, HEAD_DIM, 1)

    def col(off):
        return pl.BlockSpec((tt, Q_BLOCK), lambda hp, st: (nt - 1 - st, off + hp))

    def col_prev(off):
        return pl.BlockSpec((tt, Q_BLOCK), lambda hp, st: (jnp.maximum(nt - 2 - st, 0), off + hp))

    gain = pl.BlockSpec((1, Q_BLOCK), lambda hp, st: (0, hp))
    total = pl.BlockSpec((1, Q_BLOCK), lambda hp, st: (0, 0))
    outs, carried = _call(
        body, rider, name="attn_bwd", grid=(nhp, nt),
        in_specs=[pl.BlockSpec(memory_space=pltpu.SMEM), col(0), col(nhp), col_prev(nhp), col(2 * nhp), col_prev(2 * nhp),
                  col(nhp), col(0), col(0), pl.BlockSpec((Q_BLOCK, Q_BLOCK), lambda hp, st: (0, 0)),
                  col(0), col(nhp), gain, gain],
        out_specs=[col(0)] * 3 + [total, total],
        out_shape=[jax.ShapeDtypeStruct((s, ATTN_W), BF16)] * 3 + [jax.ShapeDtypeStruct((1, Q_BLOCK), F32)] * 2,
        scratch_shapes=[pltpu.VMEM((2 * tt, Q_BLOCK), F32)] * 2 + [pltpu.VMEM((tt, Q_BLOCK), F32)] * 4,
        compiler_params=_params(("arbitrary", "arbitrary"), VMEM_BIG),
        args=(slopes, qkv, qkv, qkv, qkv, qkv, d_cat, o, lse, ones2, qk, qk, qg, kg))
    return (*outs, carried)


def _sum_parts(name, parts):
    _, n, w = parts.shape
    tn = _tile(n, 256, 8)

    def body(p_ref, o_ref):
        acc = p_ref[0].astype(F32)
        for j in range(1, N_DEV):
            acc = acc + p_ref[j].astype(F32)
        o_ref[...] = acc

    return pl.pallas_call(
        body, name=name, grid=(n // tn,),
        in_specs=[pl.BlockSpec((N_DEV, tn, w), lambda i: (0, i, 0))],
        out_specs=pl.BlockSpec((tn, w), lambda i: (i, 0)),
        out_shape=jax.ShapeDtypeStruct((n, w), F32),
        compiler_params=_params(("parallel",), VMEM_BIG),
    )(parts)


def _adamw(name, w, g, m, v):
    n, cols = w.shape
    tn = _tile(n, 256, 8)
    c1 = 1.0 - ADAM_B1 ** ADAM_STEP
    c2 = 1.0 - ADAM_B2 ** ADAM_STEP

    def body(w_ref, g_ref, m_ref, v_ref, d_ref, nm_ref, nv_ref):
        gv = g_ref[...]
        nm = ADAM_B1 * m_ref[...] + (1.0 - ADAM_B1) * gv
        nv = ADAM_B2 * v_ref[...] + (1.0 - ADAM_B2) * (gv * gv)
        nm_ref[...] = nm
        nv_ref[...] = nv
        d_ref[...] = -ADAM_LR * ((nm / c1) / (jnp.sqrt(nv / c2) + ADAM_EPS) + ADAM_WD * w_ref[...])

    blk = pl.BlockSpec((tn, cols), lambda i: (i, 0))
    return pl.pallas_call(
        body, name=name, grid=(n // tn,),
        in_specs=[blk] * 4, out_specs=[blk] * 3,
        out_shape=[jax.ShapeDtypeStruct((n, cols), F32)] * 3,
        compiler_params=_params(("parallel",), VMEM_BIG),
    )(w, g, m, v)


def _pack(vectors, width):
    flat = jnp.concatenate([t.reshape(-1) for t in vectors])
    rows = -(-flat.shape[0] // (8 * width)) * 8
    return jnp.pad(flat, (0, rows * width - flat.shape[0])).reshape(rows, width)


def _unpack(packed, shapes):
    flat = packed.reshape(-1)
    out, at = [], 0
    for shp in shapes:
        size = 1
        for dim in shp:
            size *= dim
        out.append(flat[at:at + size].reshape(shp))
        at += size
    return out


def kernel(x, norm1_g, w_in, conv_w, conv_b, cn_g, cn_b, q_norm_g, k_norm_g, w_out, norm2_g, w_up, ffconv_w, ffconv_b, w_down, loss_target, m_norm1_g, m_w_in, m_conv_w, m_conv_b, m_cn_g, m_cn_b, m_q_norm_g, m_k_norm_g, m_w_out, m_norm2_g, m_w_up, m_ffconv_w, m_ffconv_b, m_w_down, v_norm1_g, v_w_in, v_conv_w, v_conv_b, v_cn_g, v_cn_b, v_q_norm_g, v_k_norm_g, v_w_out, v_norm2_g, v_w_up, v_ffconv_w, v_ffconv_b, v_w_down):
    s, d = x.shape[1], x.shape[2]
    f = w_down.shape[0] * N_DEV
    n_in = w_in.shape[1] * N_DEV
    xs, target = x.reshape(s, d), loss_target.reshape(s, d)
    me = _linear((lax.axis_index("x"), lax.axis_index("y"), lax.axis_index("c")))

    row = lambda t: t.reshape(1, -1)
    g1, g2 = row(norm1_g), row(norm2_g)
    h, (g_in, g_filt) = _norm_fwd(xs, g1, _Gather([w_in.T.astype(BF16), _pack([conv_w, ffconv_w], 128)]))
    w_in_t = g_in.reshape(n_in, d)
    filt = g_filt.reshape(N_DEV, -1)
    n_cw = conv_w.size
    cw = filt[:, :n_cw].reshape(N_DEV, CONV_K, -1).transpose(1, 0, 2).reshape(CONV_K, CONV_CH)
    cw = jnp.pad(cw, ((0, CONV_HALO - CONV_K), (0, 0)))
    fw = filt[:, n_cw:n_cw + ffconv_w.size].reshape(N_DEV, FF_K, -1).transpose(1, 0, 2).reshape(FF_K, 2, f)
    fb = ffconv_b.reshape(2, f)
    qg, kg =row(jnp.tile(q_norm_g, N_HEADS)), row(jnp.tile(k_norm_g, N_HEADS))
    lanes = jnp.arange(ATTN_W) // HEAD_DIM
    ones = (lanes[:, None] == lanes[None, :]).astype(BF16)
    slopes = 2.0 ** (-8.0 * jnp.arange(1, N_HEADS + 1, dtype=F32) / N_HEADS)

    a, qk, qkv, (g_out,) = _proj_fwd(h, w_in_t, qg, kg, ones, _Gather([w_out.astype(BF16)]))
    u1, u, (g_up,) = _conv_fwd(a, cw, row(conv_b), row(cn_g), row(cn_b), _Gather([w_up.T.astype(BF16)]))
    o, lse, (g_down,) = _attn_forward(qkv, slopes, _Gather([w_down.astype(BF16)]))
    w_out_f = g_out.reshape(2 * CONV_CH, d)
    w_up_t = g_up.reshape(2 * f, d)
    w_down_f = g_down.reshape(f, d)
    x1, h2 = _mix_out(u, o, w_out_f, xs, g2)
    up_g, up_v, act = _up_fwd_act(h2, w_up_t, fw, fb, f)
    loss_acc, dyf, dyb = _down_loss(act, w_down_f, x1, target)

    tk, tk_wide = _tile(s, 1024, 8), _tile(s, 512, 8)
    gw_down = _mm_tn("grad_w_down", [[act]], dyb, tk)
    d_act = _mm("d_act", dyb, w_down_f, True, F32, _tile(s, 512, 8), _tile(f, 1408, 128))
    blocks = lambda t: _Exchange([t.reshape(N_DEV, t.shape[0] // N_DEV, d)])
    dpre_g, dpre_v, gfw, gfb, (r_down,) = _ff_bwd(up_g, up_v, d_act, fw, fb, blocks(gw_down))
    gw_up = _mm_tn("grad_w_up", [[dpre_g], [dpre_v]], h2, tk_wide)
    dx1f, dx1b, gg2 = _up_bwd(dpre_g, dpre_v, w_up_t, x1, dyf, g2)
    gw_out = _mm_tn("grad_w_out", [[u, o]], dx1b, tk)
    d_cat = _mm("d_cat", dx1b, w_out_f, True, F32, _tile(s, 512, 8), 2 * CONV_CH)
    d_a, gcw, gcb, gcg, gcbeta, (r_out,) = _conv_bwd(a, u1, d_cat, cw, row(cn_g), row(cn_b), blocks(gw_out))
    dq, dk, dv, gqg, gkg, (r_up,) = _attn_backward(qkv, qk, qg, kg, d_cat, o, lse, slopes, ones[:Q_BLOCK, :Q_BLOCK],
                                                   blocks(gw_up))
    d_proj = [d_a, dq, dk, dv]
    gw_in = _mm_tn("grad_w_in", [d_proj], h, tk)
    grad_x, gg1, (r_in,) = _proj_bwd(d_proj, w_in_t, xs, dx1f, g1, blocks(gw_in))
    g_w_in_t, g_w_out, g_w_up_t, g_w_down = [_sum_parts(f"sum_grad_{i}", r) for i, r in enumerate((r_in, r_out, r_up, r_down))]
    g_w_in, g_w_up = g_w_in_t.T, g_w_up_t.T

    small_shapes = [(d,), (d,), (CONV_CH,), (CONV_CH,), (CONV_CH,), (Q_BLOCK,), (Q_BLOCK,), (2 * f,),
                    (CONV_K, CONV_CH), (FF_K, 2 * f), (1,)]
    small = _pack([gg1, gg2, gcb, gcg, gcbeta, gqg, gkg, gfb, gcw[:CONV_K], gfw, loss_acc[0:1, 0:1]], 1024)
    (small_all,) = _communicate("gather_small_grads", _Gather([small]))
    sg1, sg2, scb, scg, scbeta, sqg, skg, sfb, scw, sfw, loss_sum = _unpack(
        _sum_parts("sum_small_grads", small_all), small_shapes)
    loss = loss_sum[0]
    cwl, fwl = conv_w.shape[1], ffconv_w.shape[1]
    g_small = [sg1, lax.dynamic_slice_in_dim(scw, me * cwl, cwl, 1), scb, scg, scbeta,
               sqg[:HEAD_DIM], skg[:HEAD_DIM], sg2, lax.dynamic_slice_in_dim(sfw, me * fwl, fwl, 1), sfb]

    w_small = [norm1_g, conv_w, conv_b, cn_g, cn_b, q_norm_g, k_norm_g, norm2_g, ffconv_w, ffconv_b]
    m_small = [m_norm1_g, m_conv_w, m_conv_b, m_cn_g, m_cn_b, m_q_norm_g, m_k_norm_g, m_norm2_g, m_ffconv_w, m_ffconv_b]
    v_small = [v_norm1_g, v_conv_w, v_conv_b, v_cn_g, v_cn_b, v_q_norm_g, v_k_norm_g, v_norm2_g, v_ffconv_w, v_ffconv_b]
    shapes = [t.shape for t in w_small]
    packed = _adamw("adamw_small", *[_pack(ts, 128) for ts in (w_small, g_small, m_small, v_small)])
    d_small, nm_small, nv_small = [_unpack(p, shapes) for p in packed]
    upd = {}
    for name, wt, gt, mt, vt in (("w_in", w_in, g_w_in, m_w_in, v_w_in), ("w_out", w_out, g_w_out, m_w_out, v_w_out),
                                 ("w_up", w_up, g_w_up, m_w_up, v_w_up), ("w_down", w_down, g_w_down, m_w_down, v_w_down)):
        upd[name] = (gt,) + tuple(_adamw("adamw_" + name, wt, gt, mt, vt))
    order = ["norm1_g", "w_in", "conv_w", "conv_b", "cn_g", "cn_b", "q_norm_g", "k_norm_g", "w_out", "norm2_g",
             "w_up", "ffconv_w", "ffconv_b", "w_down"]
    small_names = ["norm1_g", "conv_w", "conv_b", "cn_g", "cn_b", "q_norm_g", "k_norm_g", "norm2_g", "ffconv_w", "ffconv_b"]
    for i, name in enumerate(small_names):
        upd[name] = (g_small[i], d_small[i], nm_small[i], nv_small[i])
    outs = [loss, grad_x.reshape(x.shape)]
    for field in range(4):
        outs += [upd[name][field] for name in order]
    return tuple(outs)
```

```python
import jax
import jax.numpy as jnp
from jax import lax
from jax.experimental import pallas as pl
from jax.experimental.pallas import tpu as pltpu

F32, BF16 = jnp.float32, jnp.bfloat16
N_DEV = 8
N_HEADS, HEAD_DIM = 8, 64
CONV_CH = 512
ATTN_W = N_HEADS * HEAD_DIM
CONV_K, FF_K = 31, 3
CONV_HALO = 32
FF_HALO = 8
PATTERN_DILATIONS = (16, 4, 1)
Q_BLOCK = 128
EPS = 1e-6
NEG = -1e30
ADAM_LR, ADAM_B1, ADAM_B2, ADAM_EPS, ADAM_WD, ADAM_STEP = 0.001, 0.9, 0.999, 1e-08, 0.01, 10
VMEM_BIG = 56 * 1024 * 1024
ANY = pl.BlockSpec(memory_space=pl.ANY)


def _params(sem=None, vmem=None):
    return pltpu.CompilerParams(dimension_semantics=sem, vmem_limit_bytes=vmem)


def _dot(a, b, trans_a=False, trans_b=False):
    dims = (((0 if trans_a else 1,), (1 if trans_b else 0,)), ((), ()))
    return lax.dot_general(a, b, dims, preferred_element_type=F32)


def _sigmoid(z):
    return 1.0 / (1.0 + jnp.exp(-z))


def _tile(n, pref, mult):
    if n <= pref:
        return n
    t = (pref // mult) * mult
    while n % t:
        t -= mult
    return t


def _head_sum(v, ones_ref):
    hi = v.astype(BF16)
    lo = (v - hi.astype(F32)).astype(BF16)
    ones = ones_ref[...]
    return _dot(hi, ones) + _dot(lo, ones)


def _linear(p):
    return 4 * p[0] + 2 * p[1] + p[2]


class _Gather:
    def __init__(self, arrs):
        self.arrs = list(arrs)
        self.n = len(self.arrs)
        self.out_shape = [jax.ShapeDtypeStruct((N_DEV,) + a.shape, a.dtype) for a in self.arrs]

    def _setup(self, ins, outs, sems):
        send_sems, recv_sems, local_sems = sems
        x, y, c = lax.axis_index("x"), lax.axis_index("y"), lax.axis_index("c")
        me, sib = (x, y, c), (x, y, 1 - c)
        chips = [(1 - x, y), (x, 1 - y), (1 - x, 1 - y)]

        def copy(a, k, block, to, src=None):
            dst = outs[a].at[_linear(block)]
            return pltpu.make_async_remote_copy(
                src_ref=dst if src is None else src, dst_ref=dst,
                send_sem=send_sems.at[7 * a + k], recv_sem=recv_sems.at[7 * a + k],
                device_id=to, device_id_type=pl.DeviceIdType.MESH)

        mine = [pltpu.make_async_copy(ins[a], outs[a].at[_linear(me)], local_sems.at[a]) for a in range(self.n)]
        first = []
        for a in range(self.n):
            first.append(copy(a, 0, me, sib, src=ins[a]))
            first += [copy(a, 1 + j, me, (*chip, c), src=ins[a]) for j, chip in enumerate(chips)]
        return copy, mine, first, me, sib, chips, c

    def start(self, ins, outs, sems):
        _, mine, first, *_ = self._setup(ins, outs, sems)
        for cp in mine + first:
            cp.start()

    def finish(self, ins, outs, sems):
        copy, mine, first, me, sib, chips, c = self._setup(ins, outs, sems)
        passed = []
        for j, chip in enumerate(chips):
            for a in range(self.n):
                copy(a, 1 + j, (*chip, c), me).wait_recv()
                fwd = copy(a, 4 + j, (*chip, c), sib)
                fwd.start()
                passed.append(fwd)
        for a in range(self.n):
            copy(a, 0, sib, me).wait_recv()
            for j, chip in enumerate(chips):
                copy(a, 4 + j, (*chip, 1 - c), me).wait_recv()
        for cp in first + passed:
            cp.wait_send()
        for cp in mine:
            cp.wait()


class _Exchange:
    def __init__(self, arrs):
        self.arrs = list(arrs)
        self.n = len(self.arrs)
        self.out_shape = [jax.ShapeDtypeStruct(a.shape, a.dtype) for a in self.arrs]

    def _setup(self, ins, outs, sems):
        send_sems, recv_sems, local_sems = sems
        x, y, c = lax.axis_index("x"), lax.axis_index("y"), lax.axis_index("c")
        me = _linear((x, y, c))
        mine = [pltpu.make_async_copy(ins[a].at[me], outs[a].at[me], local_sems.at[a]) for a in range(self.n)]
        sends, recvs = [], []
        for p in range(1, N_DEV):
            peer = (1 - x if p & 4 else x, 1 - y if p & 2 else y, 1 - c if p & 1 else c)
            for a in range(self.n):
                sem = dict(send_sem=send_sems.at[7 * a + p - 1], recv_sem=recv_sems.at[7 * a + p - 1],
                           device_id=peer, device_id_type=pl.DeviceIdType.MESH)
                sends.append(pltpu.make_async_remote_copy(
                    src_ref=ins[a].at[_linear(peer)], dst_ref=outs[a].at[me], **sem))
                recvs.append(pltpu.make_async_remote_copy(
                    src_ref=ins[a].at[_linear(peer)], dst_ref=outs[a].at[_linear(peer)], **sem))
        return mine, sends, recvs

    def start(self, ins, outs, sems):
        mine, sends, _ = self._setup(ins, outs, sems)
        for cp in mine + sends:
            cp.start()

    def finish(self, ins, outs, sems):
        mine, sends, recvs = self._setup(ins, outs, sems)
        for cp in recvs:
            cp.wait_recv()
        for cp in sends:
            cp.wait_send()
        for cp in mine:
            cp.wait()


def _rider_scratch(rider):
    return [pltpu.SemaphoreType.DMA((7 * rider.n,)), pltpu.SemaphoreType.DMA((7 * rider.n,)),
            pltpu.SemaphoreType.DMA((rider.n,))]


def _communicate(name, rider):
    na = rider.n

    def body(*refs):
        ins, outs, sems = refs[:na], refs[na:2 * na], refs[2 * na:]
        rider.start(ins, outs, sems)
        rider.finish(ins, outs, sems)

    return pl.pallas_call(
        body, name=name, out_shape=rider.out_shape, in_specs=[ANY] * na, out_specs=[ANY] * na,
        scratch_shapes=_rider_scratch(rider),
    )(*rider.arrs)


def _call(body, rider=None, *, name, grid, in_specs, out_specs, out_shape, scratch_shapes=(), compiler_params, args):
    if rider is None:
        out = pl.pallas_call(body, name=name, grid=grid, in_specs=in_specs, out_specs=out_specs, out_shape=out_shape,
                             scratch_shapes=list(scratch_shapes), compiler_params=compiler_params)(*args)
        return out, None
    n_in, n_out, n_scr, na = len(in_specs), len(out_specs), len(scratch_shapes), rider.n

    def carried(*refs):
        ins, refs = refs[:n_in], refs[n_in:]
        r_ins, refs = refs[:na], refs[na:]
        outs, refs = refs[:n_out], refs[n_out:]
        r_outs, refs = refs[:na], refs[na:]
        scratch, sems = refs[:n_scr], refs[n_scr:]
        ids = [pl.program_id(ax) for ax in range(len(grid))]
        first, last = ids[0] == 0, ids[0] == grid[0] - 1
        for ax in range(1, len(grid)):
            first, last = first & (ids[ax] == 0), last & (ids[ax] == grid[ax] - 1)

        @pl.when(first)
        def _():
            rider.start(r_ins, r_outs, sems)

        body(*ins, *outs, *scratch)

        @pl.when(last)
        def _():
            rider.finish(r_ins, r_outs, sems)

    out = pl.pallas_call(
        carried, name=name, grid=grid, in_specs=list(in_specs) + [ANY] * na, out_specs=list(out_specs) + [ANY] * na,
        out_shape=list(out_shape) + rider.out_shape, scratch_shapes=list(scratch_shapes) + _rider_scratch(rider),
        compiler_params=compiler_params)(*args, *rider.arrs)
    return out[:n_out], out[n_out:]


def _mm(name, a, w, trans_b, out_dtype, tm, tn):
    m, k = a.shape
    n = w.shape[0] if trans_b else w.shape[1]

    def body(a_ref, w_ref, o_ref):
        o_ref[...] = _dot(a_ref[...], w_ref[...], trans_b=trans_b).astype(o_ref.dtype)

    w_spec = pl.BlockSpec((tn, k), lambda j, i: (j, 0)) if trans_b else pl.BlockSpec((k, tn), lambda j, i: (0, j))
    return pl.pallas_call(
        body, name=name, grid=(n // tn, m // tm),
        in_specs=[pl.BlockSpec((tm, k), lambda j, i: (i, 0)), w_spec],
        out_specs=pl.BlockSpec((tm, tn), lambda j, i: (i, j)),
        out_shape=jax.ShapeDtypeStruct((m, n), out_dtype),
        compiler_params=_params(("parallel", "parallel"), VMEM_BIG),
    )(a, w)


def _mm_tn(name, groups, b, tk):
    s, d = b.shape
    parts = [p for grp in groups for p in grp]
    owner = [g for g, grp in enumerate(groups) for _ in grp]
    width = sum(p.shape[1] for p in groups[0])
    offsets = []
    for grp in groups:
        at = 0
        for p in grp:
            offsets.append(at)
            at += p.shape[1]
        assert at == width
    np_, nk = len(parts), s // tk

    def body(*refs):
        a_refs, b_ref, o_ref, acc_ref = refs[:np_], refs[np_], refs[np_ + 1], refs[np_ + 2]
        g, kk = pl.program_id(0), pl.program_id(1)

        @pl.when(kk == 0)
        def _():
            acc_ref[...] = jnp.zeros_like(acc_ref)

        for grp in range(len(groups)):
            @pl.when(g == grp)
            def _(grp=grp):
                for p in range(np_):
                    if owner[p] == grp:
                        rows = slice(offsets[p], offsets[p] + parts[p].shape[1])
                        acc_ref[rows, :] += _dot(a_refs[p][...].astype(BF16), b_ref[...], trans_a=True)

        @pl.when(kk == nk - 1)
        def _():
            o_ref[...] = acc_ref[...].astype(o_ref.dtype)

    def a_spec(p):
        return pl.BlockSpec((tk, parts[p].shape[1]), lambda g, kk: (jnp.where(g == owner[p], kk, 0), 0))

    return pl.pallas_call(
        body, name=name, grid=(len(groups), nk),
        in_specs=[a_spec(p) for p in range(np_)] + [pl.BlockSpec((tk, d), lambda g, kk: (kk, 0))],
        out_specs=pl.BlockSpec((width, d), lambda g, kk: (g, 0)),
        out_shape=jax.ShapeDtypeStruct((len(groups) * width, d), BF16),
        scratch_shapes=[pltpu.VMEM((width, d), F32)],
        compiler_params=_params(("arbitrary", "arbitrary"), VMEM_BIG),
    )(*parts, b)


def _norm_fwd(x, g1, rider=None):
    s, d = x.shape
    tm = _tile(s, 512, 8)

    def body(x_ref, g_ref, h_ref):
        xv = x_ref[...]
        r = lax.rsqrt(jnp.mean(xv * xv, axis=-1, keepdims=True) + EPS)
        h_ref[...] = (xv * r * g_ref[...]).astype(BF16)

    row = pl.BlockSpec((tm, d), lambda i: (i, 0))
    (h,), carried = _call(
        body, rider, name="norm_fwd", grid=(s // tm,),
        in_specs=[row, pl.BlockSpec(g1.shape, lambda i: (0, 0))], out_specs=[row],
        out_shape=[jax.ShapeDtypeStruct((s, d), BF16)],
        compiler_params=_params(("arbitrary",), VMEM_BIG), args=(x, g1))
    return h, carried


def _proj_fwd(h, w_in_t, qg, kg, ones, rider=None):
    s, d = h.shape
    tm = _tile(s, 512, 8)
    c2, aw = 2 * CONV_CH, ATTN_W

    def body(h_ref, w_ref, qg_ref, kg_ref, ones_ref, a_ref, qk_ref, qkv_ref):
        proj = _dot(h_ref[...], w_ref[...], trans_b=True)
        a_ref[...] = proj[:, :c2]
        qk_ref[...] = proj[:, c2:c2 + 2 * aw]
        q, k = proj[:, c2:c2 + aw], proj[:, c2 + aw:c2 + 2 * aw]
        rq = lax.rsqrt(_head_sum(q * q, ones_ref) * (1.0 / HEAD_DIM) + EPS)
        rk = lax.rsqrt(_head_sum(k * k, ones_ref) * (1.0 / HEAD_DIM) + EPS)
        qkv_ref[:, :aw] = q * rq * qg_ref[...] * (HEAD_DIM ** -0.5)
        qkv_ref[:, aw:2 * aw] = k * rk * kg_ref[...]
        qkv_ref[:, 2 * aw:] = proj[:, c2 + 2 * aw:]

    row = lambda w: pl.BlockSpec((tm, w), lambda i: (i, 0))
    full = lambda a: pl.BlockSpec(a.shape, lambda i: (0, 0))
    outs, carried = _call(
        body, rider, name="proj_fwd", grid=(s // tm,),
        in_specs=[row(d), full(w_in_t), full(qg), full(kg), full(ones)],
        out_specs=[row(c2), row(2 * aw), row(3 * aw)],
        out_shape=[jax.ShapeDtypeStruct((s, c2), F32), jax.ShapeDtypeStruct((s, 2 * aw), F32),
                   jax.ShapeDtypeStruct((s, 3 * aw), F32)],
        compiler_params=_params(("arbitrary",), VMEM_BIG),
        args=(h, w_in_t, qg, kg, ones))
    return (*outs, carried)


def _mix_out(u, o, w_out, x, g2):
    s, d = x.shape
    tm = _tile(s, 512, 8)

    def body(u_ref, o_ref, w_ref, x_ref, g_ref, x1_ref, h2_ref):
        x1 = x_ref[...] + _dot(u_ref[...], w_ref[:CONV_CH, :]) + _dot(o_ref[...].astype(BF16), w_ref[CONV_CH:, :])
        x1_ref[...] = x1
        r = lax.rsqrt(jnp.mean(x1 * x1, axis=-1, keepdims=True) + EPS)
        h2_ref[...] = (x1 * r * g_ref[...]).astype(BF16)

    row = lambda w: pl.BlockSpec((tm, w), lambda i: (i, 0))
    full = lambda a: pl.BlockSpec(a.shape, lambda i: (0, 0))
    return pl.pallas_call(
        body, name="mix_out", grid=(s // tm,),
        in_specs=[row(CONV_CH), row(ATTN_W), full(w_out), row(d), full(g2)],
        out_specs=[row(d), row(d)],
        out_shape=[jax.ShapeDtypeStruct((s, d), F32), jax.ShapeDtypeStruct((s, d), BF16)],
        compiler_params=_params(("parallel",), VMEM_BIG),
    )(u, o, w_out, x, g2)


def _down_loss(act, w_down, x1, target):
    s, d = x1.shape
    f = act.shape[1]
    tm = _tile(s, 512, 8)

    def body(a_ref, w_ref, x1_ref, t_ref, loss_ref, dyf_ref, dyb_ref):
        @pl.when(pl.program_id(0) == 0)
        def _():
            loss_ref[...] = jnp.zeros_like(loss_ref)

        diff = x1_ref[...] + _dot(a_ref[...], w_ref[...]) - t_ref[...]
        sq = jnp.sum(jnp.sum(diff * diff, axis=1, keepdims=True), axis=0, keepdims=True)
        loss_ref[...] += jnp.broadcast_to(sq * (0.5 / d), loss_ref.shape)
        dy = diff * (1.0 / d)
        dyf_ref[...] = dy
        dyb_ref[...] = dy.astype(BF16)

    row = lambda w: pl.BlockSpec((tm, w), lambda i: (i, 0))
    return pl.pallas_call(
        body, name="down_loss", grid=(s // tm,),
        in_specs=[row(f), pl.BlockSpec((f, d), lambda i: (0, 0)), row(d), row(d)],
        out_specs=[pl.BlockSpec((8, 128), lambda i: (0, 0)), row(d), row(d)],
        out_shape=[jax.ShapeDtypeStruct((8, 128), F32), jax.ShapeDtypeStruct((s, d), F32),
                   jax.ShapeDtypeStruct((s, d), BF16)],
        compiler_params=_params(("arbitrary",), VMEM_BIG),
    )(act, w_down, x1, target)


def _norm_bwd_tail(dh, xv, g, resid, d):
    r = lax.rsqrt(jnp.mean(xv * xv, axis=-1, keepdims=True) + EPS)
    xh = xv * r
    gd = dh * g
    dx = r * (gd - xh * (jnp.sum(gd * xh, axis=-1, keepdims=True) * (1.0 / d)))
    return resid + dx, jnp.sum(dh * xh, axis=0, keepdims=True)


def _up_bwd(dg, dv, w_up_t, x1, dyf, g2):
    s, d = x1.shape
    f = dg.shape[1]
    tm = _tile(s, 512, 8)

    def body(dg_ref, dv_ref, w_ref, x1_ref, dy_ref, g_ref, dxf_ref, dxb_ref, gg_ref):
        @pl.when(pl.program_id(0) == 0)
        def _():
            gg_ref[...] = jnp.zeros_like(gg_ref)

        dh = _dot(dg_ref[...], w_ref[:f, :]) + _dot(dv_ref[...], w_ref[f:, :])
        dx, gg = _norm_bwd_tail(dh, x1_ref[...], g_ref[...], dy_ref[...], d)
        dxf_ref[...] = dx
        dxb_ref[...] = dx.astype(BF16)
        gg_ref[...] += gg

    row = lambda w: pl.BlockSpec((tm, w), lambda i: (i, 0))
    full = lambda a: pl.BlockSpec(a.shape, lambda i: (0, 0))
    return pl.pallas_call(
        body, name="up_bwd", grid=(s // tm,),
        in_specs=[row(f), row(f), full(w_up_t), row(d), row(d), full(g2)],
        out_specs=[row(d), row(d), pl.BlockSpec((1, d), lambda i: (0, 0))],
        out_shape=[jax.ShapeDtypeStruct((s, d), F32), jax.ShapeDtypeStruct((s, d), BF16),
                   jax.ShapeDtypeStruct((1, d), F32)],
        compiler_params=_params(("arbitrary",), VMEM_BIG),
    )(dg, dv, w_up_t, x1, dyf, g2)


def _proj_bwd(parts, w_in_t, x, dx1, g1, rider=None):
    s, d = x.shape
    widths = [p.shape[1] for p in parts]
    starts = [sum(widths[:i]) for i in range(len(parts))]
    np_ = len(parts)
    tm = _tile(s, 512, 8)

    def body(*refs):
        p_refs = refs[:np_]
        w_ref, x_ref, r_ref, g_ref, gx_ref, gg_ref = refs[np_:]

        @pl.when(pl.program_id(0) == 0)
        def _():
            gg_ref[...] = jnp.zeros_like(gg_ref)

        dh = _dot(p_refs[0][...], w_ref[:widths[0], :])
        for p in range(1, np_):
            dh = dh + _dot(p_refs[p][...], w_ref[starts[p]:starts[p] + widths[p], :])
        dx, gg = _norm_bwd_tail(dh, x_ref[...], g_ref[...], r_ref[...], d)
        gx_ref[...] = dx
        gg_ref[...] += gg

    row = lambda w: pl.BlockSpec((tm, w), lambda i: (i, 0))
    full = lambda a: pl.BlockSpec(a.shape, lambda i: (0, 0))
    outs, carried = _call(
        body, rider, name="proj_bwd", grid=(s // tm,),
        in_specs=[row(w) for w in widths] + [full(w_in_t), row(d), row(d), full(g1)],
        out_specs=[row(d), pl.BlockSpec((1, d), lambda i: (0, 0))],
        out_shape=[jax.ShapeDtypeStruct((s, d), F32), jax.ShapeDtypeStruct((1, d), F32)],
        compiler_params=_params(("arbitrary",), VMEM_BIG),
        args=(*parts, w_in_t, x, dx1, g1))
    return (*outs, carried)


CONV_CHUNK = 32
SUBLANES = 8


def _glu(av):
    return av[:, :CONV_CH] * _sigmoid(av[:, CONV_CH:])


def _chunks(n, size=CONV_CHUNK):
    return [(r0, min(size, n - r0)) for r0 in range(0, n, size)]


def _fill_shifted(sbuf, rows):
    for r in range(1, SUBLANES):
        for r0, n in _chunks(rows, 64):
            sbuf[r, r0:r0 + n, :] = sbuf[0, r0 + r:r0 + r + n, :]


def _tap(sbuf, offset, r0, rows):
    q, r = divmod(offset, SUBLANES)
    return sbuf[r, SUBLANES * q + r0:SUBLANES * q + r0 + rows, :]


def _layer_norm_stats(u1):
    mu = jnp.mean(u1, axis=-1, keepdims=True)
    cen = u1 - mu
    rstd = lax.rsqrt(jnp.mean(cen * cen, axis=-1, keepdims=True) + EPS)
    return cen * rstd, rstd


def _conv_fwd(a, cw, cb, cg, cbeta, rider=None):
    s = a.shape[0]
    tm = _tile(s, 256, CONV_CHUNK)
    hb = tm // CONV_HALO

    def body(a_ref, ap_ref, cw_ref, cb_ref, cg_ref, cbeta_ref, u1_ref, u_ref, ubuf):
        i = pl.program_id(0)
        ubuf[0, 0:CONV_HALO, :] = jnp.where(i > 0, _glu(ap_ref[...]), 0.0)
        for r0, n in _chunks(tm):
            ubuf[0, CONV_HALO + r0:CONV_HALO + r0 + n, :] = _glu(a_ref[r0:r0 + n, :])
        _fill_shifted(ubuf, tm + CONV_HALO - SUBLANES)
        for r0, n in _chunks(tm):
            acc = jnp.broadcast_to(cb_ref[...], (n, CONV_CH))
            for k in range(CONV_K):
                acc = acc + cw_ref[k:k + 1, :] * _tap(ubuf, 2 + k, r0, n)
            u1_ref[r0:r0 + n, :] = acc
            xh, _ = _layer_norm_stats(acc)
            z = xh * cg_ref[...] + cbeta_ref[...]
            u_ref[r0:r0 + n, :] = (z * _sigmoid(z)).astype(BF16)

    full = lambda t: pl.BlockSpec(t.shape, lambda i: (0, 0))
    outs, carried = _call(
        body, rider, name="conv_fwd", grid=(s // tm,),
        in_specs=[pl.BlockSpec((tm, 2 * CONV_CH), lambda i: (i, 0)),
                  pl.BlockSpec((CONV_HALO, 2 * CONV_CH), lambda i: (jnp.maximum(i * hb - 1, 0), 0)),
                  full(cw), full(cb), full(cg), full(cbeta)],
        out_specs=[pl.BlockSpec((tm, CONV_CH), lambda i: (i, 0))] * 2,
        out_shape=[jax.ShapeDtypeStruct((s, CONV_CH), F32), jax.ShapeDtypeStruct((s, CONV_CH), BF16)],
        scratch_shapes=[pltpu.VMEM((SUBLANES, CONV_HALO + tm, CONV_CH), F32)],
        compiler_params=_params(("arbitrary",), VMEM_BIG),
        args=(a, a, cw, cb, cg, cbeta))
    return (*outs, carried)


def _conv_bwd(a, u1, d_cat, cw, cg, cbeta, rider=None):
    s = a.shape[0]
    tm = _tile(s, 256, CONV_CHUNK)
    hb = tm // CONV_HALO
    last_halo = s // CONV_HALO - 1
    nt = s // tm
    te = tm + CONV_HALO

    def body(a_ref, ap_ref, u1_ref, u1n_ref, du_ref, dun_ref, cw_ref, cg_ref, cbeta_ref,
             da_ref, gw_ref, gb_ref, gg_ref, gbeta_ref, ubuf, dbuf):
        i = pl.program_id(0)

        @pl.when(i == 0)
        def _():
            gw_ref[...] = jnp.zeros_like(gw_ref)
            gb_ref[...] = jnp.zeros_like(gb_ref)
            gg_ref[...] = jnp.zeros_like(gg_ref)
            gbeta_ref[...] = jnp.zeros_like(gbeta_ref)

        def du1_of(u1, du):
            xh, rstd = _layer_norm_stats(u1)
            z = xh * cg_ref[...] + cbeta_ref[...]
            sz = _sigmoid(z)
            dz = du * (sz * (1.0 + z * (1.0 - sz)))
            dxh = dz * cg_ref[...]
            du1 = rstd * (dxh - jnp.mean(dxh, axis=-1, keepdims=True)
                          - xh * jnp.mean(dxh * xh, axis=-1, keepdims=True))
            return du1, dz, xh

        gg, gbeta, gb = [jnp.zeros((SUBLANES, CONV_CH), F32) for _ in range(3)]
        for r0, n in _chunks(tm):
            du1, dz, xh = du1_of(u1_ref[r0:r0 + n, :], du_ref[r0:r0 + n, :])
            gg, gbeta, gb = gg + _fold8(dz * xh), gbeta + _fold8(dz), gb + _fold8(du1)
            dbuf[0, r0:r0 + n, :] = du1
        gg_ref[...] += jnp.sum(gg, axis=0, keepdims=True)
        gbeta_ref[...] += jnp.sum(gbeta, axis=0, keepdims=True)
        gb_ref[...] += jnp.sum(gb, axis=0, keepdims=True)
        du1n, _, _ = du1_of(u1n_ref[...], jnp.where(i < nt - 1, dun_ref[...], 0.0))
        dbuf[0, tm:, :] = du1n
        _fill_shifted(dbuf, tm + CONV_HALO - SUBLANES)

        ubuf[0, 0:CONV_HALO, :] = jnp.where(i > 0, _glu(ap_ref[...]), 0.0)
        for r0, n in _chunks(tm):
            ubuf[0, CONV_HALO + r0:CONV_HALO + r0 + n, :] = _glu(a_ref[r0:r0 + n, :])
        _fill_shifted(ubuf, tm + CONV_HALO - SUBLANES)

        for k in range(CONV_K):
            part = jnp.zeros((SUBLANES, CONV_CH), F32)
            for r0, n in _chunks(tm):
                part = part + _fold8(dbuf[0, r0:r0 + n, :] * _tap(ubuf, 2 + k, r0, n))
            gw_ref[k:k + 1, :] += jnp.sum(part, axis=0, keepdims=True)

        for r0, n in _chunks(tm):
            acc = jnp.zeros((n, CONV_CH), F32)
            for k in range(CONV_K):
                acc = acc + cw_ref[k:k + 1, :] * _tap(dbuf, CONV_K - 1 - k, r0, n)
            avc = a_ref[r0:r0 + n, :CONV_CH]
            sgc = _sigmoid(a_ref[r0:r0 + n, CONV_CH:])
            da_ref[r0:r0 + n, :CONV_CH] = (acc * sgc).astype(BF16)
            da_ref[r0:r0 + n, CONV_CH:] = (acc * avc * sgc * (1.0 - sgc)).astype(BF16)

    full = lambda t: pl.BlockSpec(t.shape, lambda i: (0, 0))
    vec = pl.BlockSpec((1, CONV_CH), lambda i: (0, 0))
    nxt = lambda i: (jnp.minimum((i + 1) * hb, last_halo), 0)
    outs, carried = _call(
        body, rider, name="conv_bwd", grid=(nt,),
        in_specs=[pl.BlockSpec((tm, 2 * CONV_CH), lambda i: (i, 0)),
                  pl.BlockSpec((CONV_HALO, 2 * CONV_CH), lambda i: (jnp.maximum(i * hb - 1, 0), 0)),
                  pl.BlockSpec((tm, CONV_CH), lambda i: (i, 0)), pl.BlockSpec((CONV_HALO, CONV_CH), nxt),
                  pl.BlockSpec((tm, CONV_CH), lambda i: (i, 0)), pl.BlockSpec((CONV_HALO, CONV_CH), nxt),
                  full(cw), full(cg), full(cbeta)],
        out_specs=[pl.BlockSpec((tm, 2 * CONV_CH), lambda i: (i, 0)),
                   pl.BlockSpec((CONV_HALO, CONV_CH), lambda i: (0, 0)), vec, vec, vec],
        out_shape=[jax.ShapeDtypeStruct((s, 2 * CONV_CH), BF16), jax.ShapeDtypeStruct((CONV_HALO, CONV_CH), F32),
                   jax.ShapeDtypeStruct((1, CONV_CH), F32), jax.ShapeDtypeStruct((1, CONV_CH), F32),
                   jax.ShapeDtypeStruct((1, CONV_CH), F32)],
        scratch_shapes=[pltpu.VMEM((SUBLANES, CONV_HALO + tm, CONV_CH), F32), pltpu.VMEM((SUBLANES, te, CONV_CH), F32)],
        compiler_params=_params(("arbitrary",), VMEM_BIG),
        args=(a, a, u1, u1, d_cat, d_cat, cw, cg, cbeta))
    return (*outs, carried)


def _ff_tiles(s, f):
    return _tile(s, 256, 8), _tile(f, 1408, 128)


FF_CHUNK, FF_LANES = 64, 128
MXU_WIDTH = 256
FWD_STRIP_ROWS = 256


def _row_chunks(n):
    return [(r0, min(FF_CHUNK, n - r0)) for r0 in range(0, n, FF_CHUNK)]


def _fold8(v):
    acc = v[0:8]
    for r in range(8, v.shape[0], 8):
        acc = acc + v[r:r + 8]
    return acc


def _ff_conv(pre_buf, w_ref, b_ref, half, r0, rows, cols):
    acc = b_ref[half:half + 1, cols]
    for k in range(FF_K):
        off = FF_HALO - (FF_K - 1) + k + r0
        acc = acc + w_ref[k, half:half + 1, cols] * pre_buf[half, off:off + rows, cols]
    return acc


def _up_fwd_act(h2, w_up_t, fw, fb, f):
    s, d = h2.shape
    tm, tn = _tile(s, 512, 8), _tile(f, 1408, 128)
    nc, nt = f // tn, s // tm

    def body(h_ref, wg_ref, wv_ref, w_ref, b_ref, g_ref, v_ref, act_ref, xbuf):
        @pl.when(pl.program_id(1) == 0)
        def _():
            xbuf[...] = jnp.zeros_like(xbuf)

        strips = [(slice(c0, c0 + FF_LANES), r0, rows) for c0 in range(0, tn, FF_LANES)
                  for r0, rows in _chunks(tm, FWD_STRIP_ROWS)]
        pieces = [(out, wt, slice(c0, min(c0 + MXU_WIDTH, tn))) for out, wt in ((g_ref, wg_ref), (v_ref, wv_ref))
                  for c0 in range(0, tn, MXU_WIDTH)]
        per_piece = -(-len(strips) // len(pieces))
        hv = h_ref[...]
        for n, (out, wt, wcols) in enumerate(pieces):
            out[:, wcols] = _dot(hv, wt[wcols, :], trans_b=True)
            for cols, r0, rows in strips[n * per_piece:(n + 1) * per_piece]:
                gate = _ff_conv(xbuf, w_ref, b_ref, 0, r0, rows, cols)
                val = _ff_conv(xbuf, w_ref, b_ref, 1, r0, rows, cols)
                act_ref[r0:r0 + rows, cols] = (gate * _sigmoid(gate) * val).astype(BF16)
        for half, ref in enumerate((g_ref, v_ref)):
            xbuf[half, 0:FF_HALO, :] = xbuf[half, tm:tm + FF_HALO, :]
            xbuf[half, FF_HALO:, :] = ref[...]

    cur = lambda j, i: (jnp.minimum(i, nt - 1), j)
    return pl.pallas_call(
        body, name="up_fwd_act", grid=(nc, nt + 1),
        in_specs=[pl.BlockSpec((tm, d), lambda j, i: (jnp.minimum(i, nt - 1), 0)),
                  pl.BlockSpec((tn, d), lambda j, i: (j, 0)), pl.BlockSpec((tn, d), lambda j, i: (j + nc, 0)),
                  pl.BlockSpec((FF_K, 2, tn), lambda j, i: (0, 0, j)), pl.BlockSpec((2, tn), lambda j, i: (0, j))],
        out_specs=[pl.BlockSpec((tm, tn), cur), pl.BlockSpec((tm, tn), cur),
                   pl.BlockSpec((tm, tn), lambda j, i: (jnp.maximum(i - 1, 0), j))],
        out_shape=[jax.ShapeDtypeStruct((s, f), F32), jax.ShapeDtypeStruct((s, f), F32), jax.ShapeDtypeStruct((s, f), BF16)],
        scratch_shapes=[pltpu.VMEM((2, FF_HALO + tm, tn), F32)],
        compiler_params=_params(("arbitrary", "arbitrary"), VMEM_BIG),
    )(h2, w_up_t, w_up_t, fw, fb)


def _ff_bwd(up_g, up_v, d_act, fw, fb, rider=None):
    s, f = up_g.shape
    tm, tc = _ff_tiles(s, f)
    hb = tm // FF_HALO
    nt = s // tm
    last_halo = s // FF_HALO - 1
    te = tm + FF_HALO

    def body(g_ref, gp_ref, gn_ref, v_ref, vp_ref, vn_ref, da_ref, dan_ref, w_ref, b_ref,
             dg_ref, dv_ref, gw_ref, gb_ref, buf, shifted, dbuf, dshifted, dabuf):
        i = pl.program_id(1)

        @pl.when(i == 0)
        def _():
            gw_ref[...] = jnp.zeros_like(gw_ref)
            gb_ref[...] = jnp.zeros_like(gb_ref)

        for half, (m_ref, p_ref, n_ref) in enumerate(((g_ref, gp_ref, gn_ref), (v_ref, vp_ref, vn_ref))):
            buf[half, 0:FF_HALO, :] = jnp.where(i > 0, p_ref[...], 0.0)
            buf[half, FF_HALO:FF_HALO + tm, :] = m_ref[...]
            buf[half, FF_HALO + tm:, :] = n_ref[...]
            for k in range(FF_K - 1):
                lo = FF_HALO - (FF_K - 1) + k
                for r0, rows in _row_chunks(te):
                    shifted[half, k, r0:r0 + rows, :] = buf[half, lo + r0:lo + r0 + rows, :]
        dabuf[0:tm, :] = da_ref[...]
        dabuf[tm:, :] = jnp.where(i < nt - 1, dan_ref[...], 0.0)

        for c0 in range(0, tc, FF_LANES):
            cols = slice(c0, c0 + FF_LANES)
            gb = [jnp.zeros((8, FF_LANES), F32) for _ in range(2)]
            gw = [[jnp.zeros((8, FF_LANES), F32) for _ in range(FF_K)] for _ in range(2)]
            for r0, rows in _row_chunks(te):
                taps = [[shifted[half, 0, r0:r0 + rows, cols], shifted[half, 1, r0:r0 + rows, cols],
                         buf[half, FF_HALO + r0:FF_HALO + r0 + rows, cols]] for half in range(2)]
                gate, val = [b_ref[half:half + 1, cols] + sum(w_ref[k, half:half + 1, cols] * taps[half][k]
                                                              for k in range(FF_K)) for half in range(2)]
                da = dabuf[r0:r0 + rows, cols]
                sg = _sigmoid(gate)
                dup = [da * val * (sg * (1.0 + gate * (1.0 - sg))), da * (gate * sg)]
                for half in range(2):
                    dbuf[half, r0:r0 + rows, cols] = dup[half]
                    if r0 < tm:
                        gb[half] = gb[half] + _fold8(dup[half])
                        for k in range(FF_K):
                            gw[half][k] = gw[half][k] + _fold8(dup[half] * taps[half][k])
            for half, out_ref in enumerate((dg_ref, dv_ref)):
                gb_ref[half:half + 1, cols] += jnp.sum(gb[half], axis=0, keepdims=True)
                for k in range(FF_K):
                    gw_ref[k, half:half + 1, cols] += jnp.sum(gw[half][k], axis=0, keepdims=True)
                for k in range(1, FF_K):
                    for r0, rows in _row_chunks(tm):
                        dshifted[half, k - 1, r0:r0 + rows, cols] = dbuf[half, k + r0:k + r0 + rows, cols]
                for r0, rows in _row_chunks(tm):
                    acc = (w_ref[2, half:half + 1, cols] * dbuf[half, r0:r0 + rows, cols]
                           + w_ref[1, half:half + 1, cols] * dshifted[half, 0, r0:r0 + rows, cols]
                           + w_ref[0, half:half + 1, cols] * dshifted[half, 1, r0:r0 + rows, cols])
                    out_ref[r0:r0 + rows, cols] = acc.astype(BF16)

    main = pl.BlockSpec((tm, tc), lambda j, i: (i, j))
    prev = pl.BlockSpec((FF_HALO, tc), lambda j, i: (jnp.maximum(i * hb - 1, 0), j))
    nxt = pl.BlockSpec((FF_HALO, tc), lambda j, i: (jnp.minimum((i + 1) * hb, last_halo), j))
    (dg, dv, gw, gb), carried = _call(
        body, rider, name="ff_bwd", grid=(f // tc, nt),
        in_specs=[main, prev, nxt, main, prev, nxt, main, nxt,
                  pl.BlockSpec((FF_K, 2, tc), lambda j, i: (0, 0, j)), pl.BlockSpec((2, tc), lambda j, i: (0, j))],
        out_specs=[main, main, pl.BlockSpec((FF_K, 2, tc), lambda j, i: (0, 0, j)),
                   pl.BlockSpec((2, tc), lambda j, i: (0, j))],
        out_shape=[jax.ShapeDtypeStruct((s, f), BF16), jax.ShapeDtypeStruct((s, f), BF16),
                   jax.ShapeDtypeStruct((FF_K, 2, f), F32), jax.ShapeDtypeStruct((2, f), F32)],
        scratch_shapes=[pltpu.VMEM((2, FF_HALO + te, tc), F32), pltpu.VMEM((2, FF_K - 1, te, tc), F32),
                        pltpu.VMEM((2, te, tc), F32), pltpu.VMEM((2, FF_K - 1, tm, tc), F32), pltpu.VMEM((te, tc), F32)],
        compiler_params=_params(("arbitrary", "arbitrary"), VMEM_BIG),
        args=(up_g, up_g, up_g, up_v, up_v, up_v, d_act, d_act, fw, fb))
    return dg, dv, gw, gb, carried


ATT_TILE = Q_BLOCK * max(PATTERN_DILATIONS)


def _stream_rows(start, d, n=Q_BLOCK):
    return pl.ds(start, n) if d == 1 else pl.ds(start, n, stride=d)


def _band_geometry():
    qi = lax.broadcasted_iota(jnp.int32, (Q_BLOCK, 2 * Q_BLOCK), 0)
    ki = lax.broadcasted_iota(jnp.int32, (Q_BLOCK, 2 * Q_BLOCK), 1)
    delta = qi + Q_BLOCK - ki
    return (delta >= 0) & (delta <= Q_BLOCK), delta.astype(F32), ki


def _stream_blocks(d):
    out, nb = [], ATT_TILE // (Q_BLOCK * d)
    for r in range(d):
        for b in range(nb):
            start = b * Q_BLOCK * d + r
            out.append((start, start - Q_BLOCK * d if b > 0 else None, b == nb - 1))
    return out


def _attn_forward(qkv, slopes, rider=None):
    s = qkv.shape[0]
    nt = s // ATT_TILE
    nhp = ATTN_W // Q_BLOCK

    def body(sl_ref, q_ref, k_ref, kp_ref, v_ref, vp_ref, o_ref, l_ref):
        hp, i = pl.program_id(0), pl.program_id(1)
        head0 = lax.broadcasted_iota(jnp.int32, (Q_BLOCK, Q_BLOCK), 1) < HEAD_DIM
        head0_k = lax.broadcasted_iota(jnp.int32, (2 * Q_BLOCK, Q_BLOCK), 1) < HEAD_DIM
        valid, dist, ki = _band_geometry()
        first_key = jnp.where(i == 0, Q_BLOCK, 0)
        for d in PATTERN_DILATIONS:
            biases = [jnp.where(valid, dist * (-sl_ref[2 * hp + hh] * d), NEG) for hh in range(2)]
            for start, prev, _ in _stream_blocks(d):
                rows = _stream_rows(start, d)
                if prev is None:
                    prow = _stream_rows(ATT_TILE - Q_BLOCK * d + start, d)
                    kp, vp = kp_ref[prow, :].astype(BF16), vp_ref[prow, :].astype(BF16)
                else:
                    kp, vp = kc, vc
                kc, vc = k_ref[rows, :].astype(BF16), v_ref[rows, :].astype(BF16)
                qv = q_ref[rows, :].astype(BF16)
                k2 = jnp.concatenate([kp, kc], axis=0)
                v2 = jnp.concatenate([vp, vc], axis=0)
                res, mxs = [], []
                for hh in range(2):
                    mine = head0 if hh == 0 else jnp.logical_not(head0)
                    mine_k = head0_k if hh == 0 else jnp.logical_not(head0_k)
                    sc = _dot(jnp.where(mine, qv, jnp.zeros_like(qv)), k2, trans_b=True) + biases[hh]
                    if prev is None:
                        sc = jnp.where(ki < first_key, NEG, sc)
                    mx = jnp.max(sc, axis=1, keepdims=True)
                    p = jnp.exp(sc - mx).astype(BF16)
                    res.append(_dot(p, jnp.where(mine_k, v2, jnp.ones_like(v2))))
                    mxs.append(mx)
                num = jnp.where(head0, res[0], res[1])
                den = pltpu.roll(jnp.where(head0, res[1], res[0]), HEAD_DIM, 1)
                o_new = num / den
                l_new = jnp.where(head0, mxs[0], mxs[1]) + jnp.log(den)
                if d != PATTERN_DILATIONS[0]:
                    oa, la = o_ref[rows, :], l_ref[rows, :]
                    mm = jnp.maximum(la, l_new)
                    wa, wn = jnp.exp(la - mm), jnp.exp(l_new - mm)
                    o_new = (wa * oa + wn * o_new) / (wa + wn)
                    l_new = mm + jnp.log(wa + wn)
                o_ref[rows, :] = o_new
                l_ref[rows, :] = l_new

    def col(off):
        return pl.BlockSpec((ATT_TILE, Q_BLOCK), lambda hp, i: (i, off + hp))

    def col_prev(off):
        return pl.BlockSpec((ATT_TILE, Q_BLOCK), lambda hp, i: (jnp.maximum(i - 1, 0), off + hp))

    outs, carried = _call(
        body, rider, name="attn_fwd", grid=(nhp, nt),
        in_specs=[pl.BlockSpec(memory_space=pltpu.SMEM), col(0), col(nhp), col_prev(nhp), col(2 * nhp), col_prev(2 * nhp)],
        out_specs=[col(0), col(0)],
        out_shape=[jax.ShapeDtypeStruct((s, ATTN_W), F32)] * 2,
        compiler_params=_params(("arbitrary", "arbitrary"), VMEM_BIG),
        args=(slopes, qkv, qkv, qkv, qkv, qkv))
    return (*outs, carried)


def _attn_backward(qkv, qk, qg, kg, d_cat, o, lse, slopes, ones2, rider=None):
    s = qkv.shape[0]
    nt = s // ATT_TILE
    nhp = ATTN_W // Q_BLOCK
    tt = ATT_TILE

    def body(sl_ref, q_ref, k_ref, kp_ref, v_ref, vp_ref, do_ref, o_ref, l_ref, ones_ref, qraw_ref, kraw_ref,
             qg_ref, kg_ref, dq_ref, dk_ref, dv_ref, gq_ref, gk_ref, dkacc, dvacc, dd, dqacc, kcarry, vcarry):
        hp, step = pl.program_id(0), pl.program_id(1)
        tile = nt - 1 - step
        head0 = lax.broadcasted_iota(jnp.int32, (Q_BLOCK, Q_BLOCK), 1) < HEAD_DIM
        valid, dist, ki = _band_geometry()
        first_key = jnp.where(tile == 0, Q_BLOCK, 0)

        @pl.when((step == 0) & (hp == 0))
        def _():
            gq_ref[...] = jnp.zeros_like(gq_ref)
            gk_ref[...] = jnp.zeros_like(gk_ref)

        @pl.when(step == 0)
        def _():
            kcarry[...] = jnp.zeros_like(kcarry)
            vcarry[...] = jnp.zeros_like(vcarry)

        @pl.when(step > 0)
        def _():
            kcarry[...] = dkacc[:tt, :]
            vcarry[...] = dvacc[:tt, :]

        dd[...] = _head_sum(do_ref[...] * o_ref[...], ones_ref)

        def key_grad(acc, at, value, first):
            if first:
                acc[at, :] = value
            else:
                acc[at, :] += value

        ki2 = jnp.concatenate([ki, ki], axis=0)
        for d in PATTERN_DILATIONS:
            first = d == PATTERN_DILATIONS[0]
            bias2 = jnp.concatenate([jnp.where(valid, dist * (-sl_ref[2 * hp + hh] * d), NEG) for hh in range(2)], axis=0)
            for start, prev, last in _stream_blocks(d):
                rows = _stream_rows(start, d)
                if prev is None:
                    prow = _stream_rows(ATT_TILE - Q_BLOCK * d + start, d)
                    kp, vp = kp_ref[prow, :].astype(BF16), vp_ref[prow, :].astype(BF16)
                else:
                    kp, vp = kc, vc
                kc, vc = k_ref[rows, :].astype(BF16), v_ref[rows, :].astype(BF16)
                qv = q_ref[rows, :].astype(BF16)
                dov = do_ref[rows, :].astype(BF16)
                lv, ddv = l_ref[rows, :], dd[rows, :]
                lsw, dsw = pltpu.roll(lv, HEAD_DIM, 1), pltpu.roll(ddv, HEAD_DIM, 1)
                k2 = jnp.concatenate([kp, kc], axis=0)
                v2 = jnp.concatenate([vp, vc], axis=0)
                zq = jnp.zeros_like(qv)
                q2 = jnp.concatenate([jnp.where(head0, qv, zq), jnp.where(head0, zq, qv)], axis=0)
                do2 = jnp.concatenate([jnp.where(head0, dov, zq), jnp.where(head0, zq, dov)], axis=0)
                lh = jnp.concatenate([jnp.where(head0, lv, lsw), jnp.where(head0, lsw, lv)], axis=0)
                dh = jnp.concatenate([jnp.where(head0, ddv, dsw), jnp.where(head0, dsw, ddv)], axis=0)
                sc = _dot(q2, k2, trans_b=True) + bias2
                if prev is None:
                    sc = jnp.where(ki2 < first_key, NEG, sc)
                p = jnp.exp(sc - jnp.concatenate([lh, lh], axis=1))
                ds = p * (_dot(do2, v2, trans_b=True) - jnp.concatenate([dh, dh], axis=1))
                dq2 = _dot(ds.astype(BF16), k2)
                dk2 = _dot(ds.T.astype(BF16), q2)
                dv2 = _dot(p.T.astype(BF16), do2)
                dq = jnp.where(head0, dq2[:Q_BLOCK], dq2[Q_BLOCK:])
                if first:
                    dqacc[rows, :] = dq
                else:
                    dqacc[rows, :] += dq
                before = _stream_rows(tt + start - Q_BLOCK * d, d)
                if prev is None:
                    key_grad(dkacc, before, dk2[:Q_BLOCK], first)
                    key_grad(dvacc, before, dv2[:Q_BLOCK], first)
                else:
                    key_grad(dkacc, before, kpend + dk2[:Q_BLOCK], first)
                    key_grad(dvacc, before, vpend + dv2[:Q_BLOCK], first)
                kpend, vpend = dk2[Q_BLOCK:], dv2[Q_BLOCK:]
                if last:
                    key_grad(dkacc, _stream_rows(tt + start, d), kpend, first)
                    key_grad(dvacc, _stream_rows(tt + start, d), vpend, first)
        for r0, n in _chunks(tt, 256):
            dv_ref[r0:r0 + n, :] = (dvacc[tt + r0:tt + r0 + n, :] + vcarry[r0:r0 + n, :]).astype(BF16)

        def norm_bwd(dy, raw, g, scale):
            r = lax.rsqrt(_head_sum(raw * raw, ones_ref) * (1.0 / HEAD_DIM) + EPS)
            xh = raw * r
            gd = dy * (g * scale)
            dx = r * (gd - xh * (_head_sum(gd * xh, ones_ref) * (1.0 / HEAD_DIM)))
            return dx, _fold8(dy * xh) * scale

        gq, gk = jnp.zeros((SUBLANES, Q_BLOCK), F32), jnp.zeros((SUBLANES, Q_BLOCK), F32)
        for r0, n in _chunks(tt, 256):
            dxq, pq = norm_bwd(dqacc[r0:r0 + n, :], qraw_ref[r0:r0 + n, :], qg_ref[...], HEAD_DIM ** -0.5)
            dxk, pk = norm_bwd(dkacc[tt + r0:tt + r0 + n, :] + kcarry[r0:r0 + n, :], kraw_ref[r0:r0 + n, :], kg_ref[...], 1.0)
            dq_ref[r0:r0 + n, :] = dxq.astype(BF16)
            dk_ref[r0:r0 + n, :] = dxk.astype(BF16)
            gq, gk = gq + pq, gk + pk
        for acc, out in ((gq, gq_ref), (gk, gk_ref)):
            row = jnp.sum(acc, axis=0, keepdims=True)
            out[...] += row + pltpu.roll(row, HEAD_DIM, 1)

    def col(off):
        return pl.BlockSpec((tt, Q_BLOCK), lambda hp, st: (nt - 1 - st, off + hp))

    def col_prev(off):
        return pl.BlockSpec((tt, Q_BLOCK), lambda hp, st: (jnp.maximum(nt - 2 - st, 0), off + hp))

    gain = pl.BlockSpec((1, Q_BLOCK), lambda hp, st: (0, hp))
    total = pl.BlockSpec((1, Q_BLOCK), lambda hp, st: (0, 0))
    outs, carried = _call(
        body, rider, name="attn_bwd", grid=(nhp, nt),
        in_specs=[pl.BlockSpec(memory_space=pltpu.SMEM), col(0), col(nhp), col_prev(nhp), col(2 * nhp), col_prev(2 * nhp),
                  col(nhp), col(0), col(0), pl.BlockSpec((Q_BLOCK, Q_BLOCK), lambda hp, st: (0, 0)),
                  col(0), col(nhp), gain, gain],
        out_specs=[col(0)] * 3 + [total, total],
        out_shape=[jax.ShapeDtypeStruct((s, ATTN_W), BF16)] * 3 + [jax.ShapeDtypeStruct((1, Q_BLOCK), F32)] * 2,
        scratch_shapes=[pltpu.VMEM((2 * tt, Q_BLOCK), F32)] * 2 + [pltpu.VMEM((tt, Q_BLOCK), F32)] * 4,
        compiler_params=_params(("arbitrary", "arbitrary"), VMEM_BIG),
        args=(slopes, qkv, qkv, qkv, qkv, qkv, d_cat, o, lse, ones2, qk, qk, qg, kg))
    return (*outs, carried)


def _sum_parts(name, parts):
    _, n, w = parts.shape
    tn = _tile(n, 256, 8)

    def body(p_ref, o_ref):
        acc = p_ref[0].astype(F32)
        for j in range(1, N_DEV):
            acc = acc + p_ref[j].astype(F32)
        o_ref[...] = acc

    return pl.pallas_call(
        body, name=name, grid=(n // tn,),
        in_specs=[pl.BlockSpec((N_DEV, tn, w), lambda i: (0, i, 0))],
        out_specs=pl.BlockSpec((tn, w), lambda i: (i, 0)),
        out_shape=jax.ShapeDtypeStruct((n, w), F32),
        compiler_params=_params(("parallel",), VMEM_BIG),
    )(parts)


def _adamw(name, w, g, m, v):
    n, cols = w.shape
    tn = _tile(n, 256, 8)
    c1 = 1.0 - ADAM_B1 ** ADAM_STEP
    c2 = 1.0 - ADAM_B2 ** ADAM_STEP

    def body(w_ref, g_ref, m_ref, v_ref, d_ref, nm_ref, nv_ref):
        gv = g_ref[...]
        nm = ADAM_B1 * m_ref[...] + (1.0 - ADAM_B1) * gv
        nv = ADAM_B2 * v_ref[...] + (1.0 - ADAM_B2) * (gv * gv)
        nm_ref[...] = nm
        nv_ref[...] = nv
        d_ref[...] = -ADAM_LR * ((nm / c1) / (jnp.sqrt(nv / c2) + ADAM_EPS) + ADAM_WD * w_ref[...])

    blk = pl.BlockSpec((tn, cols), lambda i: (i, 0))
    return pl.pallas_call(
        body, name=name, grid=(n // tn,),
        in_specs=[blk] * 4, out_specs=[blk] * 3,
        out_shape=[jax.ShapeDtypeStruct((n, cols), F32)] * 3,
        compiler_params=_params(("parallel",), VMEM_BIG),
    )(w, g, m, v)


def _pack(vectors, width):
    flat = jnp.concatenate([t.reshape(-1) for t in vectors])
    rows = -(-flat.shape[0] // (8 * width)) * 8
    return jnp.pad(flat, (0, rows * width - flat.shape[0])).reshape(rows, width)


def _unpack(packed, shapes):
    flat = packed.reshape(-1)
    out, at = [], 0
    for shp in shapes:
        size = 1
        for dim in shp:
            size *= dim
        out.append(flat[at:at + size].reshape(shp))
        at += size
    return out


def kernel(x, norm1_g, w_in, conv_w, conv_b, cn_g, cn_b, q_norm_g, k_norm_g, w_out, norm2_g, w_up, ffconv_w, ffconv_b, w_down, loss_target, m_norm1_g, m_w_in, m_conv_w, m_conv_b, m_cn_g, m_cn_b, m_q_norm_g, m_k_norm_g, m_w_out, m_norm2_g, m_w_up, m_ffconv_w, m_ffconv_b, m_w_down, v_norm1_g, v_w_in, v_conv_w, v_conv_b, v_cn_g, v_cn_b, v_q_norm_g, v_k_norm_g, v_w_out, v_norm2_g, v_w_up, v_ffconv_w, v_ffconv_b, v_w_down):
    s, d = x.shape[1], x.shape[2]
    f = w_down.shape[0] * N_DEV
    n_in = w_in.shape[1] * N_DEV
    xs, target = x.reshape(s, d), loss_target.reshape(s, d)
    me = _linear((lax.axis_index("x"), lax.axis_index("y"), lax.axis_index("c")))

    row = lambda t: t.reshape(1, -1)
    g1, g2 = row(norm1_g), row(norm2_g)
    h, (g_in, g_filt) = _norm_fwd(xs, g1, _Gather([w_in.T.astype(BF16), _pack([conv_w, ffconv_w], 128)]))
    w_in_t = g_in.reshape(n_in, d)
    filt = g_filt.reshape(N_DEV, -1)
    n_cw = conv_w.size
    cw = filt[:, :n_cw].reshape(N_DEV, CONV_K, -1).transpose(1, 0, 2).reshape(CONV_K, CONV_CH)
    cw = jnp.pad(cw, ((0, CONV_HALO - CONV_K), (0, 0)))
    fw = filt[:, n_cw:n_cw + ffconv_w.size].reshape(N_DEV, FF_K, -1).transpose(1, 0, 2).reshape(FF_K, 2, f)
    fb = ffconv_b.reshape(2, f)
    qg, kg =row(jnp.tile(q_norm_g, N_HEADS)), row(jnp.tile(k_norm_g, N_HEADS))
    lanes = jnp.arange(ATTN_W) // HEAD_DIM
    ones = (lanes[:, None] == lanes[None, :]).astype(BF16)
    slopes = 2.0 ** (-8.0 * jnp.arange(1, N_HEADS + 1, dtype=F32) / N_HEADS)

    a, qk, qkv, (g_out,) = _proj_fwd(h, w_in_t, qg, kg, ones, _Gather([w_out.astype(BF16)]))
    u1, u, (g_up,) = _conv_fwd(a, cw, row(conv_b), row(cn_g), row(cn_b), _Gather([w_up.T.astype(BF16)]))
    o, lse, (g_down,) = _attn_forward(qkv, slopes, _Gather([w_down.astype(BF16)]))
    w_out_f = g_out.reshape(2 * CONV_CH, d)
    w_up_t = g_up.reshape(2 * f, d)
    w_down_f = g_down.reshape(f, d)
    x1, h2 = _mix_out(u, o, w_out_f, xs, g2)
    up_g, up_v, act = _up_fwd_act(h2, w_up_t, fw, fb, f)
    loss_acc, dyf, dyb = _down_loss(act, w_down_f, x1, target)

    tk, tk_wide = _tile(s, 1024, 8), _tile(s, 512, 8)
    gw_down = _mm_tn("grad_w_down", [[act]], dyb, tk)
    d_act = _mm("d_act", dyb, w_down_f, True, F32, _tile(s, 1024, 8), _tile(f, 1408, 128))
    blocks = lambda t: _Exchange([t.reshape(N_DEV, t.shape[0] // N_DEV, d)])
    dpre_g, dpre_v, gfw, gfb, (r_down,) = _ff_bwd(up_g, up_v, d_act, fw, fb, blocks(gw_down))
    gw_up = _mm_tn("grad_w_up", [[dpre_g], [dpre_v]], h2, tk_wide)
    dx1f, dx1b, gg2 = _up_bwd(dpre_g, dpre_v, w_up_t, x1, dyf, g2)
    gw_out = _mm_tn("grad_w_out", [[u, o]], dx1b, tk)
    d_cat = _mm("d_cat", dx1b, w_out_f, True, F32, _tile(s, 1024, 8), 2 * CONV_CH)
    d_a, gcw, gcb, gcg, gcbeta, (r_out,) = _conv_bwd(a, u1, d_cat, cw, row(cn_g), row(cn_b), blocks(gw_out))
    dq, dk, dv, gqg, gkg, (r_up,) = _attn_backward(qkv, qk, qg, kg, d_cat, o, lse, slopes, ones[:Q_BLOCK, :Q_BLOCK],
                                                   blocks(gw_up))
    d_proj = [d_a, dq, dk, dv]
    gw_in = _mm_tn("grad_w_in", [d_proj], h, tk)
    grad_x, gg1, (r_in,) = _proj_bwd(d_proj, w_in_t, xs, dx1f, g1, blocks(gw_in))
    g_w_in_t, g_w_out, g_w_up_t, g_w_down = [_sum_parts(f"sum_grad_{i}", r) for i, r in enumerate((r_in, r_out, r_up, r_down))]
    g_w_in, g_w_up = g_w_in_t.T, g_w_up_t.T

    small_shapes = [(d,), (d,), (CONV_CH,), (CONV_CH,), (CONV_CH,), (Q_BLOCK,), (Q_BLOCK,), (2 * f,),
                    (CONV_K, CONV_CH), (FF_K, 2 * f), (1,)]
    small = _pack([gg1, gg2, gcb, gcg, gcbeta, gqg, gkg, gfb, gcw[:CONV_K], gfw, loss_acc[0:1, 0:1]], 1024)
    (small_all,) = _communicate("gather_small_grads", _Gather([small]))
    sg1, sg2, scb, scg, scbeta, sqg, skg, sfb, scw, sfw, loss_sum = _unpack(
        _sum_parts("sum_small_grads", small_all), small_shapes)
    loss = loss_sum[0]
    cwl, fwl = conv_w.shape[1], ffconv_w.shape[1]
    g_small = [sg1, lax.dynamic_slice_in_dim(scw, me * cwl, cwl, 1), scb, scg, scbeta,
               sqg[:HEAD_DIM], skg[:HEAD_DIM], sg2, lax.dynamic_slice_in_dim(sfw, me * fwl, fwl, 1), sfb]

    w_small = [norm1_g, conv_w, conv_b, cn_g, cn_b, q_norm_g, k_norm_g, norm2_g, ffconv_w, ffconv_b]
    m_small = [m_norm1_g, m_conv_w, m_conv_b, m_cn_g, m_cn_b, m_q_norm_g, m_k_norm_g, m_norm2_g, m_ffconv_w, m_ffconv_b]
    v_small = [v_norm1_g, v_conv_w, v_conv_b, v_cn_g, v_cn_b, v_q_norm_g, v_k_norm_g, v_norm2_g, v_ffconv_w, v_ffconv_b]
    shapes = [t.shape for t in w_small]
    packed = _adamw("adamw_small", *[_pack(ts, 128) for ts in (w_small, g_small, m_small, v_small)])
    d_small, nm_small, nv_small = [_unpack(p, shapes) for p in packed]
    upd = {}
    for name, wt, gt, mt, vt in (("w_in", w_in, g_w_in, m_w_in, v_w_in), ("w_out", w_out, g_w_out, m_w_out, v_w_out),
                                 ("w_up", w_up, g_w_up, m_w_up, v_w_up), ("w_down", w_down, g_w_down, m_w_down, v_w_down)):
        upd[name] = (gt,) + tuple(_adamw("adamw_" + name, wt, gt, mt, vt))
    order = ["norm1_g", "w_in", "conv_w", "conv_b", "cn_g", "cn_b", "q_norm_g", "k_norm_g", "w_out", "norm2_g",
             "w_up", "ffconv_w", "ffconv_b", "w_down"]
    small_names = ["norm1_g", "conv_w", "conv_b", "cn_g", "cn_b", "q_norm_g", "k_norm_g", "norm2_g", "ffconv_w", "ffconv_b"]
    for i, name in enumerate(small_names):
        upd[name] = (g_small[i], d_small[i], nm_small[i], nv_small[i])
    outs = [loss, grad_x.reshape(x.shape)]
    for field in range(4):
        outs += [upd[name][field] for name in order]
    return tuple(outs)
```

```python
import jax
import jax.numpy as jnp
from jax import lax
from jax.experimental import pallas as pl
from jax.experimental.pallas import tpu as pltpu

F32, BF16 = jnp.float32, jnp.bfloat16
N_DEV = 8
N_HEADS, HEAD_DIM = 8, 64
CONV_CH = 512
ATTN_W = N_HEADS * HEAD_DIM
CONV_K, FF_K = 31, 3
CONV_HALO = 32
FF_HALO = 8
PATTERN_DILATIONS = (16, 4, 1)
Q_BLOCK = 128
EPS = 1e-6
NEG = -1e30
ADAM_LR, ADAM_B1, ADAM_B2, ADAM_EPS, ADAM_WD, ADAM_STEP = 0.001, 0.9, 0.999, 1e-08, 0.01, 10
VMEM_BIG = 56 * 1024 * 1024
ANY = pl.BlockSpec(memory_space=pl.ANY)


def _params(sem=None, vmem=None):
    return pltpu.CompilerParams(dimension_semantics=sem, vmem_limit_bytes=vmem)


def _dot(a, b, trans_a=False, trans_b=False):
    dims = (((0 if trans_a else 1,), (1 if trans_b else 0,)), ((), ()))
    return lax.dot_general(a, b, dims, preferred_element_type=F32)


def _sigmoid(z):
    return 1.0 / (1.0 + jnp.exp(-z))


def _tile(n, pref, mult):
    if n <= pref:
        return n
    t = (pref // mult) * mult
    while n % t:
        t -= mult
    return t


def _head_sum(v, ones_ref):
    hi = v.astype(BF16)
    lo = (v - hi.astype(F32)).astype(BF16)
    ones = ones_ref[...]
    return _dot(hi, ones) + _dot(lo, ones)


def _linear(p):
    return 4 * p[0] + 2 * p[1] + p[2]


class _Gather:
    def __init__(self, arrs):
        self.arrs = list(arrs)
        self.n = len(self.arrs)
        self.out_shape = [jax.ShapeDtypeStruct((N_DEV,) + a.shape, a.dtype) for a in self.arrs]

    def _setup(self, ins, outs, sems):
        send_sems, recv_sems, local_sems = sems
        x, y, c = lax.axis_index("x"), lax.axis_index("y"), lax.axis_index("c")
        me, sib = (x, y, c), (x, y, 1 - c)
        chips = [(1 - x, y), (x, 1 - y), (1 - x, 1 - y)]

        def copy(a, k, block, to, src=None):
            dst = outs[a].at[_linear(block)]
            return pltpu.make_async_remote_copy(
                src_ref=dst if src is None else src, dst_ref=dst,
                send_sem=send_sems.at[7 * a + k], recv_sem=recv_sems.at[7 * a + k],
                device_id=to, device_id_type=pl.DeviceIdType.MESH)

        mine = [pltpu.make_async_copy(ins[a], outs[a].at[_linear(me)], local_sems.at[a]) for a in range(self.n)]
        first = []
        for a in range(self.n):
            first.append(copy(a, 0, me, sib, src=ins[a]))
            first += [copy(a, 1 + j, me, (*chip, c), src=ins[a]) for j, chip in enumerate(chips)]
        return copy, mine, first, me, sib, chips, c

    def start(self, ins, outs, sems):
        _, mine, first, *_ = self._setup(ins, outs, sems)
        for cp in mine + first:
            cp.start()

    def finish(self, ins, outs, sems):
        copy, mine, first, me, sib, chips, c = self._setup(ins, outs, sems)
        passed = []
        for j, chip in enumerate(chips):
            for a in range(self.n):
                copy(a, 1 + j, (*chip, c), me).wait_recv()
                fwd = copy(a, 4 + j, (*chip, c), sib)
                fwd.start()
                passed.append(fwd)
        for a in range(self.n):
            copy(a, 0, sib, me).wait_recv()
            for j, chip in enumerate(chips):
                copy(a, 4 + j, (*chip, 1 - c), me).wait_recv()
        for cp in first + passed:
            cp.wait_send()
        for cp in mine:
            cp.wait()


class _Exchange:
    def __init__(self, arrs):
        self.arrs = list(arrs)
        self.n = len(self.arrs)
        self.out_shape = [jax.ShapeDtypeStruct(a.shape, a.dtype) for a in self.arrs]

    def _setup(self, ins, outs, sems):
        send_sems, recv_sems, local_sems = sems
        x, y, c = lax.axis_index("x"), lax.axis_index("y"), lax.axis_index("c")
        me = _linear((x, y, c))
        mine = [pltpu.make_async_copy(ins[a].at[me], outs[a].at[me], local_sems.at[a]) for a in range(self.n)]
        sends, recvs = [], []
        for p in range(1, N_DEV):
            peer = (1 - x if p & 4 else x, 1 - y if p & 2 else y, 1 - c if p & 1 else c)
            for a in range(self.n):
                sem = dict(send_sem=send_sems.at[7 * a + p - 1], recv_sem=recv_sems.at[7 * a + p - 1],
                           device_id=peer, device_id_type=pl.DeviceIdType.MESH)
                sends.append(pltpu.make_async_remote_copy(
                    src_ref=ins[a].at[_linear(peer)], dst_ref=outs[a].at[me], **sem))
                recvs.append(pltpu.make_async_remote_copy(
                    src_ref=ins[a].at[_linear(peer)], dst_ref=outs[a].at[_linear(peer)], **sem))
        return mine, sends, recvs

    def start(self, ins, outs, sems):
        mine, sends, _ = self._setup(ins, outs, sems)
        for cp in mine + sends:
            cp.start()

    def finish(self, ins, outs, sems):
        mine, sends, recvs = self._setup(ins, outs, sems)
        for cp in recvs:
            cp.wait_recv()
        for cp in sends:
            cp.wait_send()
        for cp in mine:
            cp.wait()


def _rider_scratch(rider):
    return [pltpu.SemaphoreType.DMA((7 * rider.n,)), pltpu.SemaphoreType.DMA((7 * rider.n,)),
            pltpu.SemaphoreType.DMA((rider.n,))]


def _communicate(name, rider):
    na = rider.n

    def body(*refs):
        ins, outs, sems = refs[:na], refs[na:2 * na], refs[2 * na:]
        rider.start(ins, outs, sems)
        rider.finish(ins, outs, sems)

    return pl.pallas_call(
        body, name=name, out_shape=rider.out_shape, in_specs=[ANY] * na, out_specs=[ANY] * na,
        scratch_shapes=_rider_scratch(rider),
    )(*rider.arrs)


def _call(body, rider=None, *, name, grid, in_specs, out_specs, out_shape, scratch_shapes=(), compiler_params, args):
    if rider is None:
        out = pl.pallas_call(body, name=name, grid=grid, in_specs=in_specs, out_specs=out_specs, out_shape=out_shape,
                             scratch_shapes=list(scratch_shapes), compiler_params=compiler_params)(*args)
        return out, None
    n_in, n_out, n_scr, na = len(in_specs), len(out_specs), len(scratch_shapes), rider.n

    def carried(*refs):
        ins, refs = refs[:n_in], refs[n_in:]
        r_ins, refs = refs[:na], refs[na:]
        outs, refs = refs[:n_out], refs[n_out:]
        r_outs, refs = refs[:na], refs[na:]
        scratch, sems = refs[:n_scr], refs[n_scr:]
        ids = [pl.program_id(ax) for ax in range(len(grid))]
        first, last = ids[0] == 0, ids[0] == grid[0] - 1
        for ax in range(1, len(grid)):
            first, last = first & (ids[ax] == 0), last & (ids[ax] == grid[ax] - 1)

        @pl.when(first)
        def _():
            rider.start(r_ins, r_outs, sems)

        body(*ins, *outs, *scratch)

        @pl.when(last)
        def _():
            rider.finish(r_ins, r_outs, sems)

    out = pl.pallas_call(
        carried, name=name, grid=grid, in_specs=list(in_specs) + [ANY] * na, out_specs=list(out_specs) + [ANY] * na,
        out_shape=list(out_shape) + rider.out_shape, scratch_shapes=list(scratch_shapes) + _rider_scratch(rider),
        compiler_params=compiler_params)(*args, *rider.arrs)
    return out[:n_out], out[n_out:]


def _mm(name, a, w, trans_b, out_dtype, tm, tn):
    m, k = a.shape
    n = w.shape[0] if trans_b else w.shape[1]

    def body(a_ref, w_ref, o_ref):
        o_ref[...] = _dot(a_ref[...], w_ref[...], trans_b=trans_b).astype(o_ref.dtype)

    w_spec = pl.BlockSpec((tn, k), lambda j, i: (j, 0)) if trans_b else pl.BlockSpec((k, tn), lambda j, i: (0, j))
    return pl.pallas_call(
        body, name=name, grid=(n // tn, m // tm),
        in_specs=[pl.BlockSpec((tm, k), lambda j, i: (i, 0)), w_spec],
        out_specs=pl.BlockSpec((tm, tn), lambda j, i: (i, j)),
        out_shape=jax.ShapeDtypeStruct((m, n), out_dtype),
        compiler_params=_params(("parallel", "parallel"), VMEM_BIG),
    )(a, w)


def _mm_tn(name, groups, b, tk):
    s, d = b.shape
    parts = [p for grp in groups for p in grp]
    owner = [g for g, grp in enumerate(groups) for _ in grp]
    width = sum(p.shape[1] for p in groups[0])
    offsets = []
    for grp in groups:
        at = 0
        for p in grp:
            offsets.append(at)
            at += p.shape[1]
        assert at == width
    np_, nk = len(parts), s // tk

    def body(*refs):
        a_refs, b_ref, o_ref, acc_ref = refs[:np_], refs[np_], refs[np_ + 1], refs[np_ + 2]
        g, kk = pl.program_id(0), pl.program_id(1)

        @pl.when(kk == 0)
        def _():
            acc_ref[...] = jnp.zeros_like(acc_ref)

        for grp in range(len(groups)):
            @pl.when(g == grp)
            def _(grp=grp):
                for p in range(np_):
                    if owner[p] == grp:
                        rows = slice(offsets[p], offsets[p] + parts[p].shape[1])
                        acc_ref[rows, :] += _dot(a_refs[p][...].astype(BF16), b_ref[...], trans_a=True)

        @pl.when(kk == nk - 1)
        def _():
            o_ref[...] = acc_ref[...].astype(o_ref.dtype)

    def a_spec(p):
        return pl.BlockSpec((tk, parts[p].shape[1]), lambda g, kk: (jnp.where(g == owner[p], kk, 0), 0))

    return pl.pallas_call(
        body, name=name, grid=(len(groups), nk),
        in_specs=[a_spec(p) for p in range(np_)] + [pl.BlockSpec((tk, d), lambda g, kk: (kk, 0))],
        out_specs=pl.BlockSpec((width, d), lambda g, kk: (g, 0)),
        out_shape=jax.ShapeDtypeStruct((len(groups) * width, d), BF16),
        scratch_shapes=[pltpu.VMEM((width, d), F32)],
        compiler_params=_params(("arbitrary", "arbitrary"), VMEM_BIG),
    )(*parts, b)


def _norm_fwd(x, g1, rider=None):
    s, d = x.shape
    tm = _tile(s, 512, 8)

    def body(x_ref, g_ref, h_ref):
        xv = x_ref[...]
        r = lax.rsqrt(jnp.mean(xv * xv, axis=-1, keepdims=True) + EPS)
        h_ref[...] = (xv * r * g_ref[...]).astype(BF16)

    row = pl.BlockSpec((tm, d), lambda i: (i, 0))
    (h,), carried = _call(
        body, rider, name="norm_fwd", grid=(s // tm,),
        in_specs=[row, pl.BlockSpec(g1.shape, lambda i: (0, 0))], out_specs=[row],
        out_shape=[jax.ShapeDtypeStruct((s, d), BF16)],
        compiler_params=_params(("arbitrary",), VMEM_BIG), args=(x, g1))
    return h, carried


def _proj_fwd(h, w_in_t, qg, kg, ones, rider=None):
    s, d = h.shape
    tm = _tile(s, 1024, 8)
    c2, aw = 2 * CONV_CH, ATTN_W

    def body(h_ref, w_ref, qg_ref, kg_ref, ones_ref, a_ref, qk_ref, qkv_ref):
        proj = _dot(h_ref[...], w_ref[...], trans_b=True)
        a_ref[...] = proj[:, :c2]
        qk_ref[...] = proj[:, c2:c2 + 2 * aw]
        q, k = proj[:, c2:c2 + aw], proj[:, c2 + aw:c2 + 2 * aw]
        rq = lax.rsqrt(_head_sum(q * q, ones_ref) * (1.0 / HEAD_DIM) + EPS)
        rk = lax.rsqrt(_head_sum(k * k, ones_ref) * (1.0 / HEAD_DIM) + EPS)
        qkv_ref[:, :aw] = q * rq * qg_ref[...] * (HEAD_DIM ** -0.5)
        qkv_ref[:, aw:2 * aw] = k * rk * kg_ref[...]
        qkv_ref[:, 2 * aw:] = proj[:, c2 + 2 * aw:]

    row = lambda w: pl.BlockSpec((tm, w), lambda i: (i, 0))
    full = lambda a: pl.BlockSpec(a.shape, lambda i: (0, 0))
    outs, carried = _call(
        body, rider, name="proj_fwd", grid=(s // tm,),
        in_specs=[row(d), full(w_in_t), full(qg), full(kg), full(ones)],
        out_specs=[row(c2), row(2 * aw), row(3 * aw)],
        out_shape=[jax.ShapeDtypeStruct((s, c2), F32), jax.ShapeDtypeStruct((s, 2 * aw), F32),
                   jax.ShapeDtypeStruct((s, 3 * aw), F32)],
        compiler_params=_params(("arbitrary",), VMEM_BIG),
        args=(h, w_in_t, qg, kg, ones))
    return (*outs, carried)


def _mix_out(u, o, w_out, x, g2):
    s, d = x.shape
    tm = _tile(s, 1024, 8)

    def body(u_ref, o_ref, w_ref, x_ref, g_ref, x1_ref, h2_ref):
        x1 = x_ref[...] + _dot(u_ref[...], w_ref[:CONV_CH, :]) + _dot(o_ref[...].astype(BF16), w_ref[CONV_CH:, :])
        x1_ref[...] = x1
        r = lax.rsqrt(jnp.mean(x1 * x1, axis=-1, keepdims=True) + EPS)
        h2_ref[...] = (x1 * r * g_ref[...]).astype(BF16)

    row = lambda w: pl.BlockSpec((tm, w), lambda i: (i, 0))
    full = lambda a: pl.BlockSpec(a.shape, lambda i: (0, 0))
    return pl.pallas_call(
        body, name="mix_out", grid=(s // tm,),
        in_specs=[row(CONV_CH), row(ATTN_W), full(w_out), row(d), full(g2)],
        out_specs=[row(d), row(d)],
        out_shape=[jax.ShapeDtypeStruct((s, d), F32), jax.ShapeDtypeStruct((s, d), BF16)],
        compiler_params=_params(("parallel",), VMEM_BIG),
    )(u, o, w_out, x, g2)


def _down_loss(act, w_down, x1, target):
    s, d = x1.shape
    f = act.shape[1]
    tm = _tile(s, 512, 8)

    def body(a_ref, w_ref, x1_ref, t_ref, loss_ref, dyf_ref, dyb_ref):
        @pl.when(pl.program_id(0) == 0)
        def _():
            loss_ref[...] = jnp.zeros_like(loss_ref)

        diff = x1_ref[...] + _dot(a_ref[...], w_ref[...]) - t_ref[...]
        sq = jnp.sum(jnp.sum(diff * diff, axis=1, keepdims=True), axis=0, keepdims=True)
        loss_ref[...] += jnp.broadcast_to(sq * (0.5 / d), loss_ref.shape)
        dy = diff * (1.0 / d)
        dyf_ref[...] = dy
        dyb_ref[...] = dy.astype(BF16)

    row = lambda w: pl.BlockSpec((tm, w), lambda i: (i, 0))
    return pl.pallas_call(
        body, name="down_loss", grid=(s // tm,),
        in_specs=[row(f), pl.BlockSpec((f, d), lambda i: (0, 0)), row(d), row(d)],
        out_specs=[pl.BlockSpec((8, 128), lambda i: (0, 0)), row(d), row(d)],
        out_shape=[jax.ShapeDtypeStruct((8, 128), F32), jax.ShapeDtypeStruct((s, d), F32),
                   jax.ShapeDtypeStruct((s, d), BF16)],
        compiler_params=_params(("arbitrary",), VMEM_BIG),
    )(act, w_down, x1, target)


def _norm_bwd_tail(dh, xv, g, resid, d):
    r = lax.rsqrt(jnp.mean(xv * xv, axis=-1, keepdims=True) + EPS)
    xh = xv * r
    gd = dh * g
    dx = r * (gd - xh * (jnp.sum(gd * xh, axis=-1, keepdims=True) * (1.0 / d)))
    return resid + dx, jnp.sum(dh * xh, axis=0, keepdims=True)


def _up_bwd(dg, dv, w_up_t, x1, dyf, g2):
    s, d = x1.shape
    f = dg.shape[1]
    tm = _tile(s, 512, 8)

    def body(dg_ref, dv_ref, w_ref, x1_ref, dy_ref, g_ref, dxf_ref, dxb_ref, gg_ref):
        @pl.when(pl.program_id(0) == 0)
        def _():
            gg_ref[...] = jnp.zeros_like(gg_ref)

        dh = _dot(dg_ref[...], w_ref[:f, :]) + _dot(dv_ref[...], w_ref[f:, :])
        dx, gg = _norm_bwd_tail(dh, x1_ref[...], g_ref[...], dy_ref[...], d)
        dxf_ref[...] = dx
        dxb_ref[...] = dx.astype(BF16)
        gg_ref[...] += gg

    row = lambda w: pl.BlockSpec((tm, w), lambda i: (i, 0))
    full = lambda a: pl.BlockSpec(a.shape, lambda i: (0, 0))
    return pl.pallas_call(
        body, name="up_bwd", grid=(s // tm,),
        in_specs=[row(f), row(f), full(w_up_t), row(d), row(d), full(g2)],
        out_specs=[row(d), row(d), pl.BlockSpec((1, d), lambda i: (0, 0))],
        out_shape=[jax.ShapeDtypeStruct((s, d), F32), jax.ShapeDtypeStruct((s, d), BF16),
                   jax.ShapeDtypeStruct((1, d), F32)],
        compiler_params=_params(("arbitrary",), VMEM_BIG),
    )(dg, dv, w_up_t, x1, dyf, g2)


def _proj_bwd(parts, w_in_t, x, dx1, g1, rider=None):
    s, d = x.shape
    widths = [p.shape[1] for p in parts]
    starts = [sum(widths[:i]) for i in range(len(parts))]
    np_ = len(parts)
    tm = _tile(s, 1024, 8)

    def body(*refs):
        p_refs = refs[:np_]
        w_ref, x_ref, r_ref, g_ref, gx_ref, gg_ref = refs[np_:]

        @pl.when(pl.program_id(0) == 0)
        def _():
            gg_ref[...] = jnp.zeros_like(gg_ref)

        dh = _dot(p_refs[0][...], w_ref[:widths[0], :])
        for p in range(1, np_):
            dh = dh + _dot(p_refs[p][...], w_ref[starts[p]:starts[p] + widths[p], :])
        dx, gg = _norm_bwd_tail(dh, x_ref[...], g_ref[...], r_ref[...], d)
        gx_ref[...] = dx
        gg_ref[...] += gg

    row = lambda w: pl.BlockSpec((tm, w), lambda i: (i, 0))
    full = lambda a: pl.BlockSpec(a.shape, lambda i: (0, 0))
    outs, carried = _call(
        body, rider, name="proj_bwd", grid=(s // tm,),
        in_specs=[row(w) for w in widths] + [full(w_in_t), row(d), row(d), full(g1)],
        out_specs=[row(d), pl.BlockSpec((1, d), lambda i: (0, 0))],
        out_shape=[jax.ShapeDtypeStruct((s, d), F32), jax.ShapeDtypeStruct((1, d), F32)],
        compiler_params=_params(("arbitrary",), VMEM_BIG),
        args=(*parts, w_in_t, x, dx1, g1))
    return (*outs, carried)


CONV_CHUNK = 32
SUBLANES = 8


def _glu(av):
    return av[:, :CONV_CH] * _sigmoid(av[:, CONV_CH:])


def _chunks(n, size=CONV_CHUNK):
    return [(r0, min(size, n - r0)) for r0 in range(0, n, size)]


def _fill_shifted(sbuf, rows):
    for r in range(1, SUBLANES):
        for r0, n in _chunks(rows, 64):
            sbuf[r, r0:r0 + n, :] = sbuf[0, r0 + r:r0 + r + n, :]


def _tap(sbuf, offset, r0, rows):
    q, r = divmod(offset, SUBLANES)
    return sbuf[r, SUBLANES * q + r0:SUBLANES * q + r0 + rows, :]


def _layer_norm_stats(u1):
    mu = jnp.mean(u1, axis=-1, keepdims=True)
    cen = u1 - mu
    rstd = lax.rsqrt(jnp.mean(cen * cen, axis=-1, keepdims=True) + EPS)
    return cen * rstd, rstd


def _conv_fwd(a, cw, cb, cg, cbeta, rider=None):
    s = a.shape[0]
    tm = _tile(s, 256, CONV_CHUNK)
    hb = tm // CONV_HALO

    def body(a_ref, ap_ref, cw_ref, cb_ref, cg_ref, cbeta_ref, u1_ref, u_ref, ubuf):
        i = pl.program_id(0)
        ubuf[0, 0:CONV_HALO, :] = jnp.where(i > 0, _glu(ap_ref[...]), 0.0)
        for r0, n in _chunks(tm):
            ubuf[0, CONV_HALO + r0:CONV_HALO + r0 + n, :] = _glu(a_ref[r0:r0 + n, :])
        _fill_shifted(ubuf, tm + CONV_HALO - SUBLANES)
        for r0, n in _chunks(tm):
            acc = jnp.broadcast_to(cb_ref[...], (n, CONV_CH))
            for k in range(CONV_K):
                acc = acc + cw_ref[k:k + 1, :] * _tap(ubuf, 2 + k, r0, n)
            u1_ref[r0:r0 + n, :] = acc
            xh, _ = _layer_norm_stats(acc)
            z = xh * cg_ref[...] + cbeta_ref[...]
            u_ref[r0:r0 + n, :] = (z * _sigmoid(z)).astype(BF16)

    full = lambda t: pl.BlockSpec(t.shape, lambda i: (0, 0))
    outs, carried = _call(
        body, rider, name="conv_fwd", grid=(s // tm,),
        in_specs=[pl.BlockSpec((tm, 2 * CONV_CH), lambda i: (i, 0)),
                  pl.BlockSpec((CONV_HALO, 2 * CONV_CH), lambda i: (jnp.maximum(i * hb - 1, 0), 0)),
                  full(cw), full(cb), full(cg), full(cbeta)],
        out_specs=[pl.BlockSpec((tm, CONV_CH), lambda i: (i, 0))] * 2,
        out_shape=[jax.ShapeDtypeStruct((s, CONV_CH), F32), jax.ShapeDtypeStruct((s, CONV_CH), BF16)],
        scratch_shapes=[pltpu.VMEM((SUBLANES, CONV_HALO + tm, CONV_CH), F32)],
        compiler_params=_params(("arbitrary",), VMEM_BIG),
        args=(a, a, cw, cb, cg, cbeta))
    return (*outs, carried)


def _conv_bwd(a, u1, d_cat, cw, cg, cbeta, rider=None):
    s = a.shape[0]
    tm = _tile(s, 256, CONV_CHUNK)
    hb = tm // CONV_HALO
    last_halo = s // CONV_HALO - 1
    nt = s // tm
    te = tm + CONV_HALO

    def body(a_ref, ap_ref, u1_ref, u1n_ref, du_ref, dun_ref, cw_ref, cg_ref, cbeta_ref,
             da_ref, gw_ref, gb_ref, gg_ref, gbeta_ref, ubuf, dbuf):
        i = pl.program_id(0)

        @pl.when(i == 0)
        def _():
            gw_ref[...] = jnp.zeros_like(gw_ref)
            gb_ref[...] = jnp.zeros_like(gb_ref)
            gg_ref[...] = jnp.zeros_like(gg_ref)
            gbeta_ref[...] = jnp.zeros_like(gbeta_ref)

        def du1_of(u1, du):
            xh, rstd = _layer_norm_stats(u1)
            z = xh * cg_ref[...] + cbeta_ref[...]
            sz = _sigmoid(z)
            dz = du * (sz * (1.0 + z * (1.0 - sz)))
            dxh = dz * cg_ref[...]
            du1 = rstd * (dxh - jnp.mean(dxh, axis=-1, keepdims=True)
                          - xh * jnp.mean(dxh * xh, axis=-1, keepdims=True))
            return du1, dz, xh

        gg, gbeta, gb = [jnp.zeros((SUBLANES, CONV_CH), F32) for _ in range(3)]
        for r0, n in _chunks(tm):
            du1, dz, xh = du1_of(u1_ref[r0:r0 + n, :], du_ref[r0:r0 + n, :])
            gg, gbeta, gb = gg + _fold8(dz * xh), gbeta + _fold8(dz), gb + _fold8(du1)
            dbuf[0, r0:r0 + n, :] = du1
        gg_ref[...] += jnp.sum(gg, axis=0, keepdims=True)
        gbeta_ref[...] += jnp.sum(gbeta, axis=0, keepdims=True)
        gb_ref[...] += jnp.sum(gb, axis=0, keepdims=True)
        du1n, _, _ = du1_of(u1n_ref[...], jnp.where(i < nt - 1, dun_ref[...], 0.0))
        dbuf[0, tm:, :] = du1n
        _fill_shifted(dbuf, tm + CONV_HALO - SUBLANES)

        ubuf[0, 0:CONV_HALO, :] = jnp.where(i > 0, _glu(ap_ref[...]), 0.0)
        for r0, n in _chunks(tm):
            ubuf[0, CONV_HALO + r0:CONV_HALO + r0 + n, :] = _glu(a_ref[r0:r0 + n, :])
        _fill_shifted(ubuf, tm + CONV_HALO - SUBLANES)

        for k in range(CONV_K):
            part = jnp.zeros((SUBLANES, CONV_CH), F32)
            for r0, n in _chunks(tm):
                part = part + _fold8(dbuf[0, r0:r0 + n, :] * _tap(ubuf, 2 + k, r0, n))
            gw_ref[k:k + 1, :] += jnp.sum(part, axis=0, keepdims=True)

        for r0, n in _chunks(tm):
            acc = jnp.zeros((n, CONV_CH), F32)
            for k in range(CONV_K):
                acc = acc + cw_ref[k:k + 1, :] * _tap(dbuf, CONV_K - 1 - k, r0, n)
            avc = a_ref[r0:r0 + n, :CONV_CH]
            sgc = _sigmoid(a_ref[r0:r0 + n, CONV_CH:])
            da_ref[r0:r0 + n, :CONV_CH] = (acc * sgc).astype(BF16)
            da_ref[r0:r0 + n, CONV_CH:] = (acc * avc * sgc * (1.0 - sgc)).astype(BF16)

    full = lambda t: pl.BlockSpec(t.shape, lambda i: (0, 0))
    vec = pl.BlockSpec((1, CONV_CH), lambda i: (0, 0))
    nxt = lambda i: (jnp.minimum((i + 1) * hb, last_halo), 0)
    outs, carried = _call(
        body, rider, name="conv_bwd", grid=(nt,),
        in_specs=[pl.BlockSpec((tm, 2 * CONV_CH), lambda i: (i, 0)),
                  pl.BlockSpec((CONV_HALO, 2 * CONV_CH), lambda i: (jnp.maximum(i * hb - 1, 0), 0)),
                  pl.BlockSpec((tm, CONV_CH), lambda i: (i, 0)), pl.BlockSpec((CONV_HALO, CONV_CH), nxt),
                  pl.BlockSpec((tm, CONV_CH), lambda i: (i, 0)), pl.BlockSpec((CONV_HALO, CONV_CH), nxt),
                  full(cw), full(cg), full(cbeta)],
        out_specs=[pl.BlockSpec((tm, 2 * CONV_CH), lambda i: (i, 0)),
                   pl.BlockSpec((CONV_HALO, CONV_CH), lambda i: (0, 0)), vec, vec, vec],
        out_shape=[jax.ShapeDtypeStruct((s, 2 * CONV_CH), BF16), jax.ShapeDtypeStruct((CONV_HALO, CONV_CH), F32),
                   jax.ShapeDtypeStruct((1, CONV_CH), F32), jax.ShapeDtypeStruct((1, CONV_CH), F32),
                   jax.ShapeDtypeStruct((1, CONV_CH), F32)],
        scratch_shapes=[pltpu.VMEM((SUBLANES, CONV_HALO + tm, CONV_CH), F32), pltpu.VMEM((SUBLANES, te, CONV_CH), F32)],
        compiler_params=_params(("arbitrary",), VMEM_BIG),
        args=(a, a, u1, u1, d_cat, d_cat, cw, cg, cbeta))
    return (*outs, carried)


def _ff_tiles(s, f):
    return _tile(s, 256, 8), _tile(f, 1408, 128)


FF_CHUNK, FF_LANES = 64, 128
MXU_WIDTH = 256
FWD_STRIP_ROWS = 256


def _row_chunks(n):
    return [(r0, min(FF_CHUNK, n - r0)) for r0 in range(0, n, FF_CHUNK)]


def _fold8(v):
    acc = v[0:8]
    for r in range(8, v.shape[0], 8):
        acc = acc + v[r:r + 8]
    return acc


def _ff_conv(pre_buf, w_ref, b_ref, half, r0, rows, cols):
    acc = b_ref[half:half + 1, cols]
    for k in range(FF_K):
        off = FF_HALO - (FF_K - 1) + k + r0
        acc = acc + w_ref[k, half:half + 1, cols] * pre_buf[half, off:off + rows, cols]
    return acc


def _up_fwd_act(h2, w_up_t, fw, fb, f):
    s, d = h2.shape
    tm, tn = _tile(s, 512, 8), _tile(f, 1408, 128)
    nc, nt = f // tn, s // tm

    def body(h_ref, wg_ref, wv_ref, w_ref, b_ref, g_ref, v_ref, act_ref, xbuf):
        @pl.when(pl.program_id(1) == 0)
        def _():
            xbuf[...] = jnp.zeros_like(xbuf)

        strips = [(slice(c0, c0 + FF_LANES), r0, rows) for c0 in range(0, tn, FF_LANES)
                  for r0, rows in _chunks(tm, FWD_STRIP_ROWS)]
        pieces = [(out, wt, slice(c0, min(c0 + MXU_WIDTH, tn))) for out, wt in ((g_ref, wg_ref), (v_ref, wv_ref))
                  for c0 in range(0, tn, MXU_WIDTH)]
        per_piece = -(-len(strips) // len(pieces))
        hv = h_ref[...]
        for n, (out, wt, wcols) in enumerate(pieces):
            out[:, wcols] = _dot(hv, wt[wcols, :], trans_b=True)
            for cols, r0, rows in strips[n * per_piece:(n + 1) * per_piece]:
                gate = _ff_conv(xbuf, w_ref, b_ref, 0, r0, rows, cols)
                val = _ff_conv(xbuf, w_ref, b_ref, 1, r0, rows, cols)
                act_ref[r0:r0 + rows, cols] = (gate * _sigmoid(gate) * val).astype(BF16)
        for half, ref in enumerate((g_ref, v_ref)):
            xbuf[half, 0:FF_HALO, :] = xbuf[half, tm:tm + FF_HALO, :]
            xbuf[half, FF_HALO:, :] = ref[...]

    cur = lambda j, i: (jnp.minimum(i, nt - 1), j)
    return pl.pallas_call(
        body, name="up_fwd_act", grid=(nc, nt + 1),
        in_specs=[pl.BlockSpec((tm, d), lambda j, i: (jnp.minimum(i, nt - 1), 0)),
                  pl.BlockSpec((tn, d), lambda j, i: (j, 0)), pl.BlockSpec((tn, d), lambda j, i: (j + nc, 0)),
                  pl.BlockSpec((FF_K, 2, tn), lambda j, i: (0, 0, j)), pl.BlockSpec((2, tn), lambda j, i: (0, j))],
        out_specs=[pl.BlockSpec((tm, tn), cur), pl.BlockSpec((tm, tn), cur),
                   pl.BlockSpec((tm, tn), lambda j, i: (jnp.maximum(i - 1, 0), j))],
        out_shape=[jax.ShapeDtypeStruct((s, f), F32), jax.ShapeDtypeStruct((s, f), F32), jax.ShapeDtypeStruct((s, f), BF16)],
        scratch_shapes=[pltpu.VMEM((2, FF_HALO + tm, tn), F32)],
        compiler_params=_params(("arbitrary", "arbitrary"), VMEM_BIG),
    )(h2, w_up_t, w_up_t, fw, fb)


def _ff_bwd(up_g, up_v, d_act, fw, fb, rider=None):
    s, f = up_g.shape
    tm, tc = _ff_tiles(s, f)
    hb = tm // FF_HALO
    nt = s // tm
    last_halo = s // FF_HALO - 1
    te = tm + FF_HALO

    def body(g_ref, gp_ref, gn_ref, v_ref, vp_ref, vn_ref, da_ref, dan_ref, w_ref, b_ref,
             dg_ref, dv_ref, gw_ref, gb_ref, buf, shifted, dbuf, dshifted, dabuf):
        i = pl.program_id(1)

        @pl.when(i == 0)
        def _():
            gw_ref[...] = jnp.zeros_like(gw_ref)
            gb_ref[...] = jnp.zeros_like(gb_ref)

        for half, (m_ref, p_ref, n_ref) in enumerate(((g_ref, gp_ref, gn_ref), (v_ref, vp_ref, vn_ref))):
            buf[half, 0:FF_HALO, :] = jnp.where(i > 0, p_ref[...], 0.0)
            buf[half, FF_HALO:FF_HALO + tm, :] = m_ref[...]
            buf[half, FF_HALO + tm:, :] = n_ref[...]
            for k in range(FF_K - 1):
                lo = FF_HALO - (FF_K - 1) + k
                for r0, rows in _row_chunks(te):
                    shifted[half, k, r0:r0 + rows, :] = buf[half, lo + r0:lo + r0 + rows, :]
        dabuf[0:tm, :] = da_ref[...]
        dabuf[tm:, :] = jnp.where(i < nt - 1, dan_ref[...], 0.0)

        for c0 in range(0, tc, FF_LANES):
            cols = slice(c0, c0 + FF_LANES)
            gb = [jnp.zeros((8, FF_LANES), F32) for _ in range(2)]
            gw = [[jnp.zeros((8, FF_LANES), F32) for _ in range(FF_K)] for _ in range(2)]
            for r0, rows in _row_chunks(te):
                taps = [[shifted[half, 0, r0:r0 + rows, cols], shifted[half, 1, r0:r0 + rows, cols],
                         buf[half, FF_HALO + r0:FF_HALO + r0 + rows, cols]] for half in range(2)]
                gate, val = [b_ref[half:half + 1, cols] + sum(w_ref[k, half:half + 1, cols] * taps[half][k]
                                                              for k in range(FF_K)) for half in range(2)]
                da = dabuf[r0:r0 + rows, cols]
                sg = _sigmoid(gate)
                dup = [da * val * (sg * (1.0 + gate * (1.0 - sg))), da * (gate * sg)]
                for half in range(2):
                    dbuf[half, r0:r0 + rows, cols] = dup[half]
                    if r0 < tm:
                        gb[half] = gb[half] + _fold8(dup[half])
                        for k in range(FF_K):
                            gw[half][k] = gw[half][k] + _fold8(dup[half] * taps[half][k])
            for half, out_ref in enumerate((dg_ref, dv_ref)):
                gb_ref[half:half + 1, cols] += jnp.sum(gb[half], axis=0, keepdims=True)
                for k in range(FF_K):
                    gw_ref[k, half:half + 1, cols] += jnp.sum(gw[half][k], axis=0, keepdims=True)
                for k in range(1, FF_K):
                    for r0, rows in _row_chunks(tm):
                        dshifted[half, k - 1, r0:r0 + rows, cols] = dbuf[half, k + r0:k + r0 + rows, cols]
                for r0, rows in _row_chunks(tm):
                    acc = (w_ref[2, half:half + 1, cols] * dbuf[half, r0:r0 + rows, cols]
                           + w_ref[1, half:half + 1, cols] * dshifted[half, 0, r0:r0 + rows, cols]
                           + w_ref[0, half:half + 1, cols] * dshifted[half, 1, r0:r0 + rows, cols])
                    out_ref[r0:r0 + rows, cols] = acc.astype(BF16)

    main = pl.BlockSpec((tm, tc), lambda j, i: (i, j))
    prev = pl.BlockSpec((FF_HALO, tc), lambda j, i: (jnp.maximum(i * hb - 1, 0), j))
    nxt = pl.BlockSpec((FF_HALO, tc), lambda j, i: (jnp.minimum((i + 1) * hb, last_halo), j))
    (dg, dv, gw, gb), carried = _call(
        body, rider, name="ff_bwd", grid=(f // tc, nt),
        in_specs=[main, prev, nxt, main, prev, nxt, main, nxt,
                  pl.BlockSpec((FF_K, 2, tc), lambda j, i: (0, 0, j)), pl.BlockSpec((2, tc), lambda j, i: (0, j))],
        out_specs=[main, main, pl.BlockSpec((FF_K, 2, tc), lambda j, i: (0, 0, j)),
                   pl.BlockSpec((2, tc), lambda j, i: (0, j))],
        out_shape=[jax.ShapeDtypeStruct((s, f), BF16), jax.ShapeDtypeStruct((s, f), BF16),
                   jax.ShapeDtypeStruct((FF_K, 2, f), F32), jax.ShapeDtypeStruct((2, f), F32)],
        scratch_shapes=[pltpu.VMEM((2, FF_HALO + te, tc), F32), pltpu.VMEM((2, FF_K - 1, te, tc), F32),
                        pltpu.VMEM((2, te, tc), F32), pltpu.VMEM((2, FF_K - 1, tm, tc), F32), pltpu.VMEM((te, tc), F32)],
        compiler_params=_params(("arbitrary", "arbitrary"), VMEM_BIG),
        args=(up_g, up_g, up_g, up_v, up_v, up_v, d_act, d_act, fw, fb))
    return dg, dv, gw, gb, carried


ATT_TILE = Q_BLOCK * max(PATTERN_DILATIONS)


def _stream_rows(start, d, n=Q_BLOCK):
    return pl.ds(start, n) if d == 1 else pl.ds(start, n, stride=d)


def _band_geometry():
    qi = lax.broadcasted_iota(jnp.int32, (Q_BLOCK, 2 * Q_BLOCK), 0)
    ki = lax.broadcasted_iota(jnp.int32, (Q_BLOCK, 2 * Q_BLOCK), 1)
    delta = qi + Q_BLOCK - ki
    return (delta >= 0) & (delta <= Q_BLOCK), delta.astype(F32), ki


def _stream_blocks(d):
    out, nb = [], ATT_TILE // (Q_BLOCK * d)
    for r in range(d):
        for b in range(nb):
            start = b * Q_BLOCK * d + r
            out.append((start, start - Q_BLOCK * d if b > 0 else None, b == nb - 1))
    return out


def _attn_forward(qkv, slopes, rider=None):
    s = qkv.shape[0]
    nt = s // ATT_TILE
    nhp = ATTN_W // Q_BLOCK

    def body(sl_ref, q_ref, k_ref, kp_ref, v_ref, vp_ref, o_ref, l_ref):
        hp, i = pl.program_id(0), pl.program_id(1)
        head0 = lax.broadcasted_iota(jnp.int32, (Q_BLOCK, Q_BLOCK), 1) < HEAD_DIM
        head0_k = lax.broadcasted_iota(jnp.int32, (2 * Q_BLOCK, Q_BLOCK), 1) < HEAD_DIM
        valid, dist, ki = _band_geometry()
        first_key = jnp.where(i == 0, Q_BLOCK, 0)
        for d in PATTERN_DILATIONS:
            biases = [jnp.where(valid, dist * (-sl_ref[2 * hp + hh] * d), NEG) for hh in range(2)]
            for start, prev, _ in _stream_blocks(d):
                rows = _stream_rows(start, d)
                if prev is None:
                    prow = _stream_rows(ATT_TILE - Q_BLOCK * d + start, d)
                    kp, vp = kp_ref[prow, :].astype(BF16), vp_ref[prow, :].astype(BF16)
                else:
                    kp, vp = kc, vc
                kc, vc = k_ref[rows, :].astype(BF16), v_ref[rows, :].astype(BF16)
                qv = q_ref[rows, :].astype(BF16)
                k2 = jnp.concatenate([kp, kc], axis=0)
                v2 = jnp.concatenate([vp, vc], axis=0)
                res, mxs = [], []
                for hh in range(2):
                    mine = head0 if hh == 0 else jnp.logical_not(head0)
                    mine_k = head0_k if hh == 0 else jnp.logical_not(head0_k)
                    sc = _dot(jnp.where(mine, qv, jnp.zeros_like(qv)), k2, trans_b=True) + biases[hh]
                    if prev is None:
                        sc = jnp.where(ki < first_key, NEG, sc)
                    mx = jnp.max(sc, axis=1, keepdims=True)
                    p = jnp.exp(sc - mx).astype(BF16)
                    res.append(_dot(p, jnp.where(mine_k, v2, jnp.ones_like(v2))))
                    mxs.append(mx)
                num = jnp.where(head0, res[0], res[1])
                den = pltpu.roll(jnp.where(head0, res[1], res[0]), HEAD_DIM, 1)
                o_new = num / den
                l_new = jnp.where(head0, mxs[0], mxs[1]) + jnp.log(den)
                if d != PATTERN_DILATIONS[0]:
                    oa, la = o_ref[rows, :], l_ref[rows, :]
                    mm = jnp.maximum(la, l_new)
                    wa, wn = jnp.exp(la - mm), jnp.exp(l_new - mm)
                    o_new = (wa * oa + wn * o_new) / (wa + wn)
                    l_new = mm + jnp.log(wa + wn)
                o_ref[rows, :] = o_new
                l_ref[rows, :] = l_new

    def col(off):
        return pl.BlockSpec((ATT_TILE, Q_BLOCK), lambda hp, i: (i, off + hp))

    def col_prev(off):
        return pl.BlockSpec((ATT_TILE, Q_BLOCK), lambda hp, i: (jnp.maximum(i - 1, 0), off + hp))

    outs, carried = _call(
        body, rider, name="attn_fwd", grid=(nhp, nt),
        in_specs=[pl.BlockSpec(memory_space=pltpu.SMEM), col(0), col(nhp), col_prev(nhp), col(2 * nhp), col_prev(2 * nhp)],
        out_specs=[col(0), col(0)],
        out_shape=[jax.ShapeDtypeStruct((s, ATTN_W), F32)] * 2,
        compiler_params=_params(("arbitrary", "arbitrary"), VMEM_BIG),
        args=(slopes, qkv, qkv, qkv, qkv, qkv))
    return (*outs, carried)


def _attn_backward(qkv, qk, qg, kg, d_cat, o, lse, slopes, ones2, rider=None):
    s = qkv.shape[0]
    nt = s // ATT_TILE
    nhp = ATTN_W // Q_BLOCK
    tt = ATT_TILE

    def body(sl_ref, q_ref, k_ref, kp_ref, v_ref, vp_ref, do_ref, o_ref, l_ref, ones_ref, qraw_ref, kraw_ref,
             qg_ref, kg_ref, dq_ref, dk_ref, dv_ref, gq_ref, gk_ref, dkacc, dvacc, dd, dqacc, kcarry, vcarry):
        hp, step = pl.program_id(0), pl.program_id(1)
        tile = nt - 1 - step
        head0 = lax.broadcasted_iota(jnp.int32, (Q_BLOCK, Q_BLOCK), 1) < HEAD_DIM
        valid, dist, ki = _band_geometry()
        first_key = jnp.where(tile == 0, Q_BLOCK, 0)

        @pl.when((step == 0) & (hp == 0))
        def _():
            gq_ref[...] = jnp.zeros_like(gq_ref)
            gk_ref[...] = jnp.zeros_like(gk_ref)

        @pl.when(step == 0)
        def _():
            kcarry[...] = jnp.zeros_like(kcarry)
            vcarry[...] = jnp.zeros_like(vcarry)

        @pl.when(step > 0)
        def _():
            kcarry[...] = dkacc[:tt, :]
            vcarry[...] = dvacc[:tt, :]

        dd[...] = _head_sum(do_ref[...] * o_ref[...], ones_ref)

        def key_grad(acc, at, value, first):
            if first:
                acc[at, :] = value
            else:
                acc[at, :] += value

        ki2 = jnp.concatenate([ki, ki], axis=0)
        for d in PATTERN_DILATIONS:
            first = d == PATTERN_DILATIONS[0]
            bias2 = jnp.concatenate([jnp.where(valid, dist * (-sl_ref[2 * hp + hh] * d), NEG) for hh in range(2)], axis=0)
            for start, prev, last in _stream_blocks(d):
                rows = _stream_rows(start, d)
                if prev is None:
                    prow = _stream_rows(ATT_TILE - Q_BLOCK * d + start, d)
                    kp, vp = kp_ref[prow, :].astype(BF16), vp_ref[prow, :].astype(BF16)
                else:
                    kp, vp = kc, vc
                kc, vc = k_ref[rows, :].astype(BF16), v_ref[rows, :].astype(BF16)
                qv = q_ref[rows, :].astype(BF16)
                dov = do_ref[rows, :].astype(BF16)
                lv, ddv = l_ref[rows, :], dd[rows, :]
                lsw, dsw = pltpu.roll(lv, HEAD_DIM, 1), pltpu.roll(ddv, HEAD_DIM, 1)
                k2 = jnp.concatenate([kp, kc], axis=0)
                v2 = jnp.concatenate([vp, vc], axis=0)
                zq = jnp.zeros_like(qv)
                q2 = jnp.concatenate([jnp.where(head0, qv, zq), jnp.where(head0, zq, qv)], axis=0)
                do2 = jnp.concatenate([jnp.where(head0, dov, zq), jnp.where(head0, zq, dov)], axis=0)
                lh = jnp.concatenate([jnp.where(head0, lv, lsw), jnp.where(head0, lsw, lv)], axis=0)
                dh = jnp.concatenate([jnp.where(head0, ddv, dsw), jnp.where(head0, dsw, ddv)], axis=0)
                sc = _dot(q2, k2, trans_b=True) + bias2
                if prev is None:
                    sc = jnp.where(ki2 < first_key, NEG, sc)
                p = jnp.exp(sc - jnp.concatenate([lh, lh], axis=1))
                ds = p * (_dot(do2, v2, trans_b=True) - jnp.concatenate([dh, dh], axis=1))
                dq2 = _dot(ds.astype(BF16), k2)
                dk2 = _dot(ds.T.astype(BF16), q2)
                dv2 = _dot(p.T.astype(BF16), do2)
                dq = jnp.where(head0, dq2[:Q_BLOCK], dq2[Q_BLOCK:])
                if first:
                    dqacc[rows, :] = dq
                else:
                    dqacc[rows, :] += dq
                before = _stream_rows(tt + start - Q_BLOCK * d, d)
                if prev is None:
                    key_grad(dkacc, before, dk2[:Q_BLOCK], first)
                    key_grad(dvacc, before, dv2[:Q_BLOCK], first)
                else:
                    key_grad(dkacc, before, kpend + dk2[:Q_BLOCK], first)
                    key_grad(dvacc, before, vpend + dv2[:Q_BLOCK], first)
                kpend, vpend = dk2[Q_BLOCK:], dv2[Q_BLOCK:]
                if last:
                    key_grad(dkacc, _stream_rows(tt + start, d), kpend, first)
                    key_grad(dvacc, _stream_rows(tt + start, d), vpend, first)
        for r0, n in _chunks(tt, 256):
            dv_ref[r0:r0 + n, :] = (dvacc[tt + r0:tt + r0 + n, :] + vcarry[r0:r0 + n, :]).astype(BF16)

        def norm_bwd(dy, raw, g, scale):
            r = lax.rsqrt(_head_sum(raw * raw, ones_ref) * (1.0 / HEAD_DIM) + EPS)
            xh = raw * r
            gd = dy * (g * scale)
            dx = r * (gd - xh * (_head_sum(gd * xh, ones_ref) * (1.0 / HEAD_DIM)))
            return dx, _fold8(dy * xh) * scale

        gq, gk = jnp.zeros((SUBLANES, Q_BLOCK), F32), jnp.zeros((SUBLANES, Q_BLOCK), F32)
        for r0, n in _chunks(tt, 256):
            dxq, pq = norm_bwd(dqacc[r0:r0 + n, :], qraw_ref[r0:r0 + n, :], qg_ref[...], HEAD_DIM ** -0.5)
            dxk, pk = norm_bwd(dkacc[tt + r0:tt + r0 + n, :] + kcarry[r0:r0 + n, :], kraw_ref[r0:r0 + n, :], kg_ref[...], 1.0)
            dq_ref[r0:r0 + n, :] = dxq.astype(BF16)
            dk_ref[r0:r0 + n, :] = dxk.astype(BF16)
            gq, gk = gq + pq, gk + pk
        for acc, out in ((gq, gq_ref), (gk, gk_ref)):
            row = jnp.sum(acc, axis=0, keepdims=True)
            out[...] += row + pltpu.roll(row, HEAD_DIM, 1)

    def col(off):
        return pl.BlockSpec((tt, Q_BLOCK), lambda hp, st: (nt - 1 - st, off + hp))

    def col_prev(off):
        return pl.BlockSpec((tt, Q_BLOCK), lambda hp, st: (jnp.maximum(nt - 2 - st, 0), off + hp))

    gain = pl.BlockSpec((1, Q_BLOCK), lambda hp, st: (0, hp))
    total = pl.BlockSpec((1, Q_BLOCK), lambda hp, st: (0, 0))
    outs, carried = _call(
        body, rider, name="attn_bwd", grid=(nhp, nt),
        in_specs=[pl.BlockSpec(memory_space=pltpu.SMEM), col(0), col(nhp), col_prev(nhp), col(2 * nhp), col_prev(2 * nhp),
                  col(nhp), col(0), col(0), pl.BlockSpec((Q_BLOCK, Q_BLOCK), lambda hp, st: (0, 0)),
                  col(0), col(nhp), gain, gain],
        out_specs=[col(0)] * 3 + [total, total],
        out_shape=[jax.ShapeDtypeStruct((s, ATTN_W), BF16)] * 3 + [jax.ShapeDtypeStruct((1, Q_BLOCK), F32)] * 2,
        scratch_shapes=[pltpu.VMEM((2 * tt, Q_BLOCK), F32)] * 2 + [pltpu.VMEM((tt, Q_BLOCK), F32)] * 4,
        compiler_params=_params(("arbitrary", "arbitrary"), VMEM_BIG),
        args=(slopes, qkv, qkv, qkv, qkv, qkv, d_cat, o, lse, ones2, qk, qk, qg, kg))
    return (*outs, carried)


def _sum_parts(name, parts):
    _, n, w = parts.shape
    tn = _tile(n, 256, 8)

    def body(p_ref, o_ref):
        acc = p_ref[0].astype(F32)
        for j in range(1, N_DEV):
            acc = acc + p_ref[j].astype(F32)
        o_ref[...] = acc

    return pl.pallas_call(
        body, name=name, grid=(n // tn,),
        in_specs=[pl.BlockSpec((N_DEV, tn, w), lambda i: (0, i, 0))],
        out_specs=pl.BlockSpec((tn, w), lambda i: (i, 0)),
        out_shape=jax.ShapeDtypeStruct((n, w), F32),
        compiler_params=_params(("parallel",), VMEM_BIG),
    )(parts)


def _adamw(name, w, g, m, v):
    n, cols = w.shape
    tn = _tile(n, 256, 8)
    c1 = 1.0 - ADAM_B1 ** ADAM_STEP
    c2 = 1.0 - ADAM_B2 ** ADAM_STEP

    def body(w_ref, g_ref, m_ref, v_ref, d_ref, nm_ref, nv_ref):
        gv = g_ref[...]
        nm = ADAM_B1 * m_ref[...] + (1.0 - ADAM_B1) * gv
        nv = ADAM_B2 * v_ref[...] + (1.0 - ADAM_B2) * (gv * gv)
        nm_ref[...] = nm
        nv_ref[...] = nv
        d_ref[...] = -ADAM_LR * ((nm / c1) / (jnp.sqrt(nv / c2) + ADAM_EPS) + ADAM_WD * w_ref[...])

    blk = pl.BlockSpec((tn, cols), lambda i: (i, 0))
    return pl.pallas_call(
        body, name=name, grid=(n // tn,),
        in_specs=[blk] * 4, out_specs=[blk] * 3,
        out_shape=[jax.ShapeDtypeStruct((n, cols), F32)] * 3,
        compiler_params=_params(("parallel",), VMEM_BIG),
    )(w, g, m, v)


def _pack(vectors, width):
    flat = jnp.concatenate([t.reshape(-1) for t in vectors])
    rows = -(-flat.shape[0] // (8 * width)) * 8
    return jnp.pad(flat, (0, rows * width - flat.shape[0])).reshape(rows, width)


def _unpack(packed, shapes):
    flat = packed.reshape(-1)
    out, at = [], 0
    for shp in shapes:
        size = 1
        for dim in shp:
            size *= dim
        out.append(flat[at:at + size].reshape(shp))
        at += size
    return out


def kernel(x, norm1_g, w_in, conv_w, conv_b, cn_g, cn_b, q_norm_g, k_norm_g, w_out, norm2_g, w_up, ffconv_w, ffconv_b, w_down, loss_target, m_norm1_g, m_w_in, m_conv_w, m_conv_b, m_cn_g, m_cn_b, m_q_norm_g, m_k_norm_g, m_w_out, m_norm2_g, m_w_up, m_ffconv_w, m_ffconv_b, m_w_down, v_norm1_g, v_w_in, v_conv_w, v_conv_b, v_cn_g, v_cn_b, v_q_norm_g, v_k_norm_g, v_w_out, v_norm2_g, v_w_up, v_ffconv_w, v_ffconv_b, v_w_down):
    s, d = x.shape[1], x.shape[2]
    f = w_down.shape[0] * N_DEV
    n_in = w_in.shape[1] * N_DEV
    xs, target = x.reshape(s, d), loss_target.reshape(s, d)
    me = _linear((lax.axis_index("x"), lax.axis_index("y"), lax.axis_index("c")))

    row = lambda t: t.reshape(1, -1)
    g1, g2 = row(norm1_g), row(norm2_g)
    h, (g_in, g_filt) = _norm_fwd(xs, g1, _Gather([w_in.T.astype(BF16), _pack([conv_w, ffconv_w], 128)]))
    w_in_t = g_in.reshape(n_in, d)
    filt = g_filt.reshape(N_DEV, -1)
    n_cw = conv_w.size
    cw = filt[:, :n_cw].reshape(N_DEV, CONV_K, -1).transpose(1, 0, 2).reshape(CONV_K, CONV_CH)
    cw = jnp.pad(cw, ((0, CONV_HALO - CONV_K), (0, 0)))
    fw = filt[:, n_cw:n_cw + ffconv_w.size].reshape(N_DEV, FF_K, -1).transpose(1, 0, 2).reshape(FF_K, 2, f)
    fb = ffconv_b.reshape(2, f)
    qg, kg =row(jnp.tile(q_norm_g, N_HEADS)), row(jnp.tile(k_norm_g, N_HEADS))
    lanes = jnp.arange(ATTN_W) // HEAD_DIM
    ones = (lanes[:, None] == lanes[None, :]).astype(BF16)
    slopes = 2.0 ** (-8.0 * jnp.arange(1, N_HEADS + 1, dtype=F32) / N_HEADS)

    a, qk, qkv, (g_out,) = _proj_fwd(h, w_in_t, qg, kg, ones, _Gather([w_out.astype(BF16)]))
    u1, u, (g_up,) = _conv_fwd(a, cw, row(conv_b), row(cn_g), row(cn_b), _Gather([w_up.T.astype(BF16)]))
    o, lse, (g_down,) = _attn_forward(qkv, slopes, _Gather([w_down.astype(BF16)]))
    w_out_f = g_out.reshape(2 * CONV_CH, d)
    w_up_t = g_up.reshape(2 * f, d)
    w_down_f = g_down.reshape(f, d)
    x1, h2 = _mix_out(u, o, w_out_f, xs, g2)
    up_g, up_v, act = _up_fwd_act(h2, w_up_t, fw, fb, f)
    loss_acc, dyf, dyb = _down_loss(act, w_down_f, x1, target)

    tk, tk_wide = _tile(s, 1024, 8), _tile(s, 512, 8)
    gw_down = _mm_tn("grad_w_down", [[act]], dyb, tk)
    d_act = _mm("d_act", dyb, w_down_f, True, F32, _tile(s, 1024, 8), _tile(f, 1408, 128))
    blocks = lambda t: _Exchange([t.reshape(N_DEV, t.shape[0] // N_DEV, d)])
    dpre_g, dpre_v, gfw, gfb, (r_down,) = _ff_bwd(up_g, up_v, d_act, fw, fb, blocks(gw_down))
    gw_up = _mm_tn("grad_w_up", [[dpre_g], [dpre_v]], h2, tk_wide)
    dx1f, dx1b, gg2 = _up_bwd(dpre_g, dpre_v, w_up_t, x1, dyf, g2)
    gw_out = _mm_tn("grad_w_out", [[u, o]], dx1b, tk)
    d_cat = _mm("d_cat", dx1b, w_out_f, True, F32, _tile(s, 1024, 8), 2 * CONV_CH)
    d_a, gcw, gcb, gcg, gcbeta, (r_out,) = _conv_bwd(a, u1, d_cat, cw, row(cn_g), row(cn_b), blocks(gw_out))
    dq, dk, dv, gqg, gkg, (r_up,) = _attn_backward(qkv, qk, qg, kg, d_cat, o, lse, slopes, ones[:Q_BLOCK, :Q_BLOCK],
                                                   blocks(gw_up))
    d_proj = [d_a, dq, dk, dv]
    gw_in = _mm_tn("grad_w_in", [d_proj], h, tk)
    grad_x, gg1, (r_in,) = _proj_bwd(d_proj, w_in_t, xs, dx1f, g1, blocks(gw_in))
    g_w_in_t, g_w_out, g_w_up_t, g_w_down = [_sum_parts(f"sum_grad_{i}", r) for i, r in enumerate((r_in, r_out, r_up, r_down))]
    g_w_in, g_w_up = g_w_in_t.T, g_w_up_t.T

    small_shapes = [(d,), (d,), (CONV_CH,), (CONV_CH,), (CONV_CH,), (Q_BLOCK,), (Q_BLOCK,), (2 * f,),
                    (CONV_K, CONV_CH), (FF_K, 2 * f), (1,)]
    small = _pack([gg1, gg2, gcb, gcg, gcbeta, gqg, gkg, gfb, gcw[:CONV_K], gfw, loss_acc[0:1, 0:1]], 1024)
    (small_all,) = _communicate("gather_small_grads", _Gather([small]))
    sg1, sg2, scb, scg, scbeta, sqg, skg, sfb, scw, sfw, loss_sum = _unpack(
        _sum_parts("sum_small_grads", small_all), small_shapes)
    loss = loss_sum[0]
    cwl, fwl = conv_w.shape[1], ffconv_w.shape[1]
    g_small = [sg1, lax.dynamic_slice_in_dim(scw, me * cwl, cwl, 1), scb, scg, scbeta,
               sqg[:HEAD_DIM], skg[:HEAD_DIM], sg2, lax.dynamic_slice_in_dim(sfw, me * fwl, fwl, 1), sfb]

    w_small = [norm1_g, conv_w, conv_b, cn_g, cn_b, q_norm_g, k_norm_g, norm2_g, ffconv_w, ffconv_b]
    m_small = [m_norm1_g, m_conv_w, m_conv_b, m_cn_g, m_cn_b, m_q_norm_g, m_k_norm_g, m_norm2_g, m_ffconv_w, m_ffconv_b]
    v_small = [v_norm1_g, v_conv_w, v_conv_b, v_cn_g, v_cn_b, v_q_norm_g, v_k_norm_g, v_norm2_g, v_ffconv_w, v_ffconv_b]
    shapes = [t.shape for t in w_small]
    packed = _adamw("adamw_small", *[_pack(ts, 128) for ts in (w_small, g_small, m_small, v_small)])
    d_small, nm_small, nv_small = [_unpack(p, shapes) for p in packed]
    upd = {}
    for name, wt, gt, mt, vt in (("w_in", w_in, g_w_in, m_w_in, v_w_in), ("w_out", w_out, g_w_out, m_w_out, v_w_out),
                                 ("w_up", w_up, g_w_up, m_w_up, v_w_up), ("w_down", w_down, g_w_down, m_w_down, v_w_down)):
        upd[name] = (gt,) + tuple(_adamw("adamw_" + name, wt, gt, mt, vt))
    order = ["norm1_g", "w_in", "conv_w", "conv_b", "cn_g", "cn_b", "q_norm_g", "k_norm_g", "w_out", "norm2_g",
             "w_up", "ffconv_w", "ffconv_b", "w_down"]
    small_names = ["norm1_g", "conv_w", "conv_b", "cn_g", "cn_b", "q_norm_g", "k_norm_g", "norm2_g", "ffconv_w", "ffconv_b"]
    for i, name in enumerate(small_names):
        upd[name] = (g_small[i], d_small[i], nm_small[i], nv_small[i])
    outs = [loss, grad_x.reshape(x.shape)]
    for field in range(4):
        outs += [upd[name][field] for name in order]
    return tuple(outs)
```

```python
import jax
import jax.numpy as jnp
from jax import lax
from jax.experimental import pallas as pl
from jax.experimental.pallas import tpu as pltpu

F32, BF16 = jnp.float32, jnp.bfloat16
N_DEV = 8
N_HEADS, HEAD_DIM = 8, 64
CONV_CH = 512
ATTN_W = N_HEADS * HEAD_DIM
CONV_K, FF_K = 31, 3
CONV_HALO = 32
FF_HALO = 8
PATTERN_DILATIONS = (16, 4, 1)
Q_BLOCK = 128
EPS = 1e-6
NEG = -1e30
ADAM_LR, ADAM_B1, ADAM_B2, ADAM_EPS, ADAM_WD, ADAM_STEP = 0.001, 0.9, 0.999, 1e-08, 0.01, 10
VMEM_BIG = 56 * 1024 * 1024
ANY = pl.BlockSpec(memory_space=pl.ANY)


def _params(sem=None, vmem=None):
    return pltpu.CompilerParams(dimension_semantics=sem, vmem_limit_bytes=vmem)


def _dot(a, b, trans_a=False, trans_b=False):
    dims = (((0 if trans_a else 1,), (1 if trans_b else 0,)), ((), ()))
    return lax.dot_general(a, b, dims, preferred_element_type=F32)


def _sigmoid(z):
    return 1.0 / (1.0 + jnp.exp(-z))


def _tile(n, pref, mult):
    if n <= pref:
        return n
    t = (pref // mult) * mult
    while n % t:
        t -= mult
    return t


def _head_sum(v, ones_ref):
    hi = v.astype(BF16)
    lo = (v - hi.astype(F32)).astype(BF16)
    ones = ones_ref[...]
    return _dot(hi, ones) + _dot(lo, ones)


def _linear(p):
    return 4 * p[0] + 2 * p[1] + p[2]


class _Gather:
    def __init__(self, arrs):
        self.arrs = list(arrs)
        self.n = len(self.arrs)
        self.out_shape = [jax.ShapeDtypeStruct((N_DEV,) + a.shape, a.dtype) for a in self.arrs]

    def _setup(self, ins, outs, sems):
        send_sems, recv_sems, local_sems = sems
        x, y, c = lax.axis_index("x"), lax.axis_index("y"), lax.axis_index("c")
        me, sib = (x, y, c), (x, y, 1 - c)
        chips = [(1 - x, y), (x, 1 - y), (1 - x, 1 - y)]

        def copy(a, k, block, to, src=None):
            dst = outs[a].at[_linear(block)]
            return pltpu.make_async_remote_copy(
                src_ref=dst if src is None else src, dst_ref=dst,
                send_sem=send_sems.at[7 * a + k], recv_sem=recv_sems.at[7 * a + k],
                device_id=to, device_id_type=pl.DeviceIdType.MESH)

        mine = [pltpu.make_async_copy(ins[a], outs[a].at[_linear(me)], local_sems.at[a]) for a in range(self.n)]
        first = []
        for a in range(self.n):
            first.append(copy(a, 0, me, sib, src=ins[a]))
            first += [copy(a, 1 + j, me, (*chip, c), src=ins[a]) for j, chip in enumerate(chips)]
        return copy, mine, first, me, sib, chips, c

    def start(self, ins, outs, sems):
        _, mine, first, *_ = self._setup(ins, outs, sems)
        for cp in mine + first:
            cp.start()

    def finish(self, ins, outs, sems):
        copy, mine, first, me, sib, chips, c = self._setup(ins, outs, sems)
        passed = []
        for j, chip in enumerate(chips):
            for a in range(self.n):
                copy(a, 1 + j, (*chip, c), me).wait_recv()
                fwd = copy(a, 4 + j, (*chip, c), sib)
                fwd.start()
                passed.append(fwd)
        for a in range(self.n):
            copy(a, 0, sib, me).wait_recv()
            for j, chip in enumerate(chips):
                copy(a, 4 + j, (*chip, 1 - c), me).wait_recv()
        for cp in first + passed:
            cp.wait_send()
        for cp in mine:
            cp.wait()


class _Exchange:
    def __init__(self, arrs):
        self.arrs = list(arrs)
        self.n = len(self.arrs)
        self.out_shape = [jax.ShapeDtypeStruct(a.shape, a.dtype) for a in self.arrs]

    def _setup(self, ins, outs, sems):
        send_sems, recv_sems, local_sems = sems
        x, y, c = lax.axis_index("x"), lax.axis_index("y"), lax.axis_index("c")
        me = _linear((x, y, c))
        mine = [pltpu.make_async_copy(ins[a].at[me], outs[a].at[me], local_sems.at[a]) for a in range(self.n)]
        sends, recvs = [], []
        for p in range(1, N_DEV):
            peer = (1 - x if p & 4 else x, 1 - y if p & 2 else y, 1 - c if p & 1 else c)
            for a in range(self.n):
                sem = dict(send_sem=send_sems.at[7 * a + p - 1], recv_sem=recv_sems.at[7 * a + p - 1],
                           device_id=peer, device_id_type=pl.DeviceIdType.MESH)
                sends.append(pltpu.make_async_remote_copy(
                    src_ref=ins[a].at[_linear(peer)], dst_ref=outs[a].at[me], **sem))
                recvs.append(pltpu.make_async_remote_copy(
                    src_ref=ins[a].at[_linear(peer)], dst_ref=outs[a].at[_linear(peer)], **sem))
        return mine, sends, recvs

    def start(self, ins, outs, sems):
        mine, sends, _ = self._setup(ins, outs, sems)
        for cp in mine + sends:
            cp.start()

    def finish(self, ins, outs, sems):
        mine, sends, recvs = self._setup(ins, outs, sems)
        for cp in recvs:
            cp.wait_recv()
        for cp in sends:
            cp.wait_send()
        for cp in mine:
            cp.wait()


def _rider_scratch(rider):
    return [pltpu.SemaphoreType.DMA((7 * rider.n,)), pltpu.SemaphoreType.DMA((7 * rider.n,)),
            pltpu.SemaphoreType.DMA((rider.n,))]


def _communicate(name, rider):
    na = rider.n

    def body(*refs):
        ins, outs, sems = refs[:na], refs[na:2 * na], refs[2 * na:]
        rider.start(ins, outs, sems)
        rider.finish(ins, outs, sems)

    return pl.pallas_call(
        body, name=name, out_shape=rider.out_shape, in_specs=[ANY] * na, out_specs=[ANY] * na,
        scratch_shapes=_rider_scratch(rider),
    )(*rider.arrs)


def _call(body, rider=None, *, name, grid, in_specs, out_specs, out_shape, scratch_shapes=(), compiler_params, args):
    if rider is None:
        out = pl.pallas_call(body, name=name, grid=grid, in_specs=in_specs, out_specs=out_specs, out_shape=out_shape,
                             scratch_shapes=list(scratch_shapes), compiler_params=compiler_params)(*args)
        return out, None
    n_in, n_out, n_scr, na = len(in_specs), len(out_specs), len(scratch_shapes), rider.n

    def carried(*refs):
        ins, refs = refs[:n_in], refs[n_in:]
        r_ins, refs = refs[:na], refs[na:]
        outs, refs = refs[:n_out], refs[n_out:]
        r_outs, refs = refs[:na], refs[na:]
        scratch, sems = refs[:n_scr], refs[n_scr:]
        ids = [pl.program_id(ax) for ax in range(len(grid))]
        first, last = ids[0] == 0, ids[0] == grid[0] - 1
        for ax in range(1, len(grid)):
            first, last = first & (ids[ax] == 0), last & (ids[ax] == grid[ax] - 1)

        @pl.when(first)
        def _():
            rider.start(r_ins, r_outs, sems)

        body(*ins, *outs, *scratch)

        @pl.when(last)
        def _():
            rider.finish(r_ins, r_outs, sems)

    out = pl.pallas_call(
        carried, name=name, grid=grid, in_specs=list(in_specs) + [ANY] * na, out_specs=list(out_specs) + [ANY] * na,
        out_shape=list(out_shape) + rider.out_shape, scratch_shapes=list(scratch_shapes) + _rider_scratch(rider),
        compiler_params=compiler_params)(*args, *rider.arrs)
    return out[:n_out], out[n_out:]


def _mm(name, a, w, trans_b, out_dtype, tm, tn):
    m, k = a.shape
    n = w.shape[0] if trans_b else w.shape[1]

    def body(a_ref, w_ref, o_ref):
        o_ref[...] = _dot(a_ref[...], w_ref[...], trans_b=trans_b).astype(o_ref.dtype)

    w_spec = pl.BlockSpec((tn, k), lambda j, i: (j, 0)) if trans_b else pl.BlockSpec((k, tn), lambda j, i: (0, j))
    return pl.pallas_call(
        body, name=name, grid=(n // tn, m // tm),
        in_specs=[pl.BlockSpec((tm, k), lambda j, i: (i, 0)), w_spec],
        out_specs=pl.BlockSpec((tm, tn), lambda j, i: (i, j)),
        out_shape=jax.ShapeDtypeStruct((m, n), out_dtype),
        compiler_params=_params(("parallel", "parallel"), VMEM_BIG),
    )(a, w)


def _mm_tn(name, groups, b, tk):
    s, d = b.shape
    parts = [p for grp in groups for p in grp]
    owner = [g for g, grp in enumerate(groups) for _ in grp]
    width = sum(p.shape[1] for p in groups[0])
    offsets = []
    for grp in groups:
        at = 0
        for p in grp:
            offsets.append(at)
            at += p.shape[1]
        assert at == width
    np_, nk = len(parts), s // tk

    def body(*refs):
        a_refs, b_ref, o_ref, acc_ref = refs[:np_], refs[np_], refs[np_ + 1], refs[np_ + 2]
        g, kk = pl.program_id(0), pl.program_id(1)

        @pl.when(kk == 0)
        def _():
            acc_ref[...] = jnp.zeros_like(acc_ref)

        for grp in range(len(groups)):
            @pl.when(g == grp)
            def _(grp=grp):
                for p in range(np_):
                    if owner[p] == grp:
                        rows = slice(offsets[p], offsets[p] + parts[p].shape[1])
                        acc_ref[rows, :] += _dot(a_refs[p][...].astype(BF16), b_ref[...], trans_a=True)

        @pl.when(kk == nk - 1)
        def _():
            o_ref[...] = acc_ref[...].astype(o_ref.dtype)

    def a_spec(p):
        return pl.BlockSpec((tk, parts[p].shape[1]), lambda g, kk: (jnp.where(g == owner[p], kk, 0), 0))

    return pl.pallas_call(
        body, name=name, grid=(len(groups), nk),
        in_specs=[a_spec(p) for p in range(np_)] + [pl.BlockSpec((tk, d), lambda g, kk: (kk, 0))],
        out_specs=pl.BlockSpec((width, d), lambda g, kk: (g, 0)),
        out_shape=jax.ShapeDtypeStruct((len(groups) * width, d), BF16),
        scratch_shapes=[pltpu.VMEM((width, d), F32)],
        compiler_params=_params(("arbitrary", "arbitrary"), VMEM_BIG),
    )(*parts, b)


def _norm_fwd(x, g1, rider=None):
    s, d = x.shape
    tm = _tile(s, 512, 8)

    def body(x_ref, g_ref, h_ref):
        xv = x_ref[...]
        r = lax.rsqrt(jnp.mean(xv * xv, axis=-1, keepdims=True) + EPS)
        h_ref[...] = (xv * r * g_ref[...]).astype(BF16)

    row = pl.BlockSpec((tm, d), lambda i: (i, 0))
    (h,), carried = _call(
        body, rider, name="norm_fwd", grid=(s // tm,),
        in_specs=[row, pl.BlockSpec(g1.shape, lambda i: (0, 0))], out_specs=[row],
        out_shape=[jax.ShapeDtypeStruct((s, d), BF16)],
        compiler_params=_params(("arbitrary",), VMEM_BIG), args=(x, g1))
    return h, carried


def _proj_fwd(h, w_in_t, qg, kg, ones, rider=None):
    s, d = h.shape
    tm = _tile(s, 1024, 8)
    c2, aw = 2 * CONV_CH, ATTN_W

    def body(h_ref, w_ref, qg_ref, kg_ref, ones_ref, a_ref, qk_ref, qkv_ref):
        proj = _dot(h_ref[...], w_ref[...], trans_b=True)
        a_ref[...] = proj[:, :c2]
        qk_ref[...] = proj[:, c2:c2 + 2 * aw]
        q, k = proj[:, c2:c2 + aw], proj[:, c2 + aw:c2 + 2 * aw]
        rq = lax.rsqrt(_head_sum(q * q, ones_ref) * (1.0 / HEAD_DIM) + EPS)
        rk = lax.rsqrt(_head_sum(k * k, ones_ref) * (1.0 / HEAD_DIM) + EPS)
        qkv_ref[:, :aw] = q * rq * qg_ref[...] * (HEAD_DIM ** -0.5)
        qkv_ref[:, aw:2 * aw] = k * rk * kg_ref[...]
        qkv_ref[:, 2 * aw:] = proj[:, c2 + 2 * aw:]

    row = lambda w: pl.BlockSpec((tm, w), lambda i: (i, 0))
    full = lambda a: pl.BlockSpec(a.shape, lambda i: (0, 0))
    outs, carried = _call(
        body, rider, name="proj_fwd", grid=(s // tm,),
        in_specs=[row(d), full(w_in_t), full(qg), full(kg), full(ones)],
        out_specs=[row(c2), row(2 * aw), row(3 * aw)],
        out_shape=[jax.ShapeDtypeStruct((s, c2), F32), jax.ShapeDtypeStruct((s, 2 * aw), F32),
                   jax.ShapeDtypeStruct((s, 3 * aw), F32)],
        compiler_params=_params(("arbitrary",), VMEM_BIG),
        args=(h, w_in_t, qg, kg, ones))
    return (*outs, carried)


def _mix_out(u, o, w_out, x, g2):
    s, d = x.shape
    tm = _tile(s, 1024, 8)

    def body(u_ref, o_ref, w_ref, x_ref, g_ref, x1_ref, h2_ref):
        x1 = x_ref[...] + _dot(u_ref[...], w_ref[:CONV_CH, :]) + _dot(o_ref[...].astype(BF16), w_ref[CONV_CH:, :])
        x1_ref[...] = x1
        r = lax.rsqrt(jnp.mean(x1 * x1, axis=-1, keepdims=True) + EPS)
        h2_ref[...] = (x1 * r * g_ref[...]).astype(BF16)

    row = lambda w: pl.BlockSpec((tm, w), lambda i: (i, 0))
    full = lambda a: pl.BlockSpec(a.shape, lambda i: (0, 0))
    return pl.pallas_call(
        body, name="mix_out", grid=(s // tm,),
        in_specs=[row(CONV_CH), row(ATTN_W), full(w_out), row(d), full(g2)],
        out_specs=[row(d), row(d)],
        out_shape=[jax.ShapeDtypeStruct((s, d), F32), jax.ShapeDtypeStruct((s, d), BF16)],
        compiler_params=_params(("parallel",), VMEM_BIG),
    )(u, o, w_out, x, g2)


def _down_loss(act, w_down, x1, target):
    s, d = x1.shape
    f = act.shape[1]
    tm = _tile(s, 512, 8)

    def body(a_ref, w_ref, x1_ref, t_ref, loss_ref, dyf_ref, dyb_ref):
        @pl.when(pl.program_id(0) == 0)
        def _():
            loss_ref[...] = jnp.zeros_like(loss_ref)

        diff = x1_ref[...] + _dot(a_ref[...], w_ref[...]) - t_ref[...]
        sq = jnp.sum(jnp.sum(diff * diff, axis=1, keepdims=True), axis=0, keepdims=True)
        loss_ref[...] += jnp.broadcast_to(sq * (0.5 / d), loss_ref.shape)
        dy = diff * (1.0 / d)
        dyf_ref[...] = dy
        dyb_ref[...] = dy.astype(BF16)

    row = lambda w: pl.BlockSpec((tm, w), lambda i: (i, 0))
    return pl.pallas_call(
        body, name="down_loss", grid=(s // tm,),
        in_specs=[row(f), pl.BlockSpec((f, d), lambda i: (0, 0)), row(d), row(d)],
        out_specs=[pl.BlockSpec((8, 128), lambda i: (0, 0)), row(d), row(d)],
        out_shape=[jax.ShapeDtypeStruct((8, 128), F32), jax.ShapeDtypeStruct((s, d), F32),
                   jax.ShapeDtypeStruct((s, d), BF16)],
        compiler_params=_params(("arbitrary",), VMEM_BIG),
    )(act, w_down, x1, target)


def _norm_bwd_tail(dh, xv, g, resid, d):
    r = lax.rsqrt(jnp.mean(xv * xv, axis=-1, keepdims=True) + EPS)
    xh = xv * r
    gd = dh * g
    dx = r * (gd - xh * (jnp.sum(gd * xh, axis=-1, keepdims=True) * (1.0 / d)))
    return resid + dx, jnp.sum(dh * xh, axis=0, keepdims=True)


def _up_bwd(dg, dv, w_up_t, x1, dyf, g2):
    s, d = x1.shape
    f = dg.shape[1]
    tm = _tile(s, 512, 8)

    def body(dg_ref, dv_ref, w_ref, x1_ref, dy_ref, g_ref, dxf_ref, dxb_ref, gg_ref):
        @pl.when(pl.program_id(0) == 0)
        def _():
            gg_ref[...] = jnp.zeros_like(gg_ref)

        dh = _dot(dg_ref[...], w_ref[:f, :]) + _dot(dv_ref[...], w_ref[f:, :])
        dx, gg = _norm_bwd_tail(dh, x1_ref[...], g_ref[...], dy_ref[...], d)
        dxf_ref[...] = dx
        dxb_ref[...] = dx.astype(BF16)
        gg_ref[...] += gg

    row = lambda w: pl.BlockSpec((tm, w), lambda i: (i, 0))
    full = lambda a: pl.BlockSpec(a.shape, lambda i: (0, 0))
    return pl.pallas_call(
        body, name="up_bwd", grid=(s // tm,),
        in_specs=[row(f), row(f), full(w_up_t), row(d), row(d), full(g2)],
        out_specs=[row(d), row(d), pl.BlockSpec((1, d), lambda i: (0, 0))],
        out_shape=[jax.ShapeDtypeStruct((s, d), F32), jax.ShapeDtypeStruct((s, d), BF16),
                   jax.ShapeDtypeStruct((1, d), F32)],
        compiler_params=_params(("arbitrary",), VMEM_BIG),
    )(dg, dv, w_up_t, x1, dyf, g2)


def _proj_bwd(parts, w_in_t, x, dx1, g1, rider=None):
    s, d = x.shape
    widths = [p.shape[1] for p in parts]
    starts = [sum(widths[:i]) for i in range(len(parts))]
    np_ = len(parts)
    tm = _tile(s, 512, 8)

    def body(*refs):
        p_refs = refs[:np_]
        w_ref, x_ref, r_ref, g_ref, gx_ref, gg_ref = refs[np_:]

        @pl.when(pl.program_id(0) == 0)
        def _():
            gg_ref[...] = jnp.zeros_like(gg_ref)

        dh = _dot(p_refs[0][...], w_ref[:widths[0], :])
        for p in range(1, np_):
            dh = dh + _dot(p_refs[p][...], w_ref[starts[p]:starts[p] + widths[p], :])
        dx, gg = _norm_bwd_tail(dh, x_ref[...], g_ref[...], r_ref[...], d)
        gx_ref[...] = dx
        gg_ref[...] += gg

    row = lambda w: pl.BlockSpec((tm, w), lambda i: (i, 0))
    full = lambda a: pl.BlockSpec(a.shape, lambda i: (0, 0))
    outs, carried = _call(
        body, rider, name="proj_bwd", grid=(s // tm,),
        in_specs=[row(w) for w in widths] + [full(w_in_t), row(d), row(d), full(g1)],
        out_specs=[row(d), pl.BlockSpec((1, d), lambda i: (0, 0))],
        out_shape=[jax.ShapeDtypeStruct((s, d), F32), jax.ShapeDtypeStruct((1, d), F32)],
        compiler_params=_params(("arbitrary",), VMEM_BIG),
        args=(*parts, w_in_t, x, dx1, g1))
    return (*outs, carried)


CONV_CHUNK = 32
SUBLANES = 8


def _glu(av):
    return av[:, :CONV_CH] * _sigmoid(av[:, CONV_CH:])


def _chunks(n, size=CONV_CHUNK):
    return [(r0, min(size, n - r0)) for r0 in range(0, n, size)]


def _fill_shifted(sbuf, rows):
    for r in range(1, SUBLANES):
        for r0, n in _chunks(rows, 64):
            sbuf[r, r0:r0 + n, :] = sbuf[0, r0 + r:r0 + r + n, :]


def _tap(sbuf, offset, r0, rows):
    q, r = divmod(offset, SUBLANES)
    return sbuf[r, SUBLANES * q + r0:SUBLANES * q + r0 + rows, :]


def _layer_norm_stats(u1):
    mu = jnp.mean(u1, axis=-1, keepdims=True)
    cen = u1 - mu
    rstd = lax.rsqrt(jnp.mean(cen * cen, axis=-1, keepdims=True) + EPS)
    return cen * rstd, rstd


def _conv_fwd(a, cw, cb, cg, cbeta, rider=None):
    s = a.shape[0]
    tm = _tile(s, 256, CONV_CHUNK)
    hb = tm // CONV_HALO

    def body(a_ref, ap_ref, cw_ref, cb_ref, cg_ref, cbeta_ref, u1_ref, u_ref, ubuf):
        i = pl.program_id(0)
        ubuf[0, 0:CONV_HALO, :] = jnp.where(i > 0, _glu(ap_ref[...]), 0.0)
        for r0, n in _chunks(tm):
            ubuf[0, CONV_HALO + r0:CONV_HALO + r0 + n, :] = _glu(a_ref[r0:r0 + n, :])
        _fill_shifted(ubuf, tm + CONV_HALO - SUBLANES)
        for r0, n in _chunks(tm):
            acc = jnp.broadcast_to(cb_ref[...], (n, CONV_CH))
            for k in range(CONV_K):
                acc = acc + cw_ref[k:k + 1, :] * _tap(ubuf, 2 + k, r0, n)
            u1_ref[r0:r0 + n, :] = acc
            xh, _ = _layer_norm_stats(acc)
            z = xh * cg_ref[...] + cbeta_ref[...]
            u_ref[r0:r0 + n, :] = (z * _sigmoid(z)).astype(BF16)

    full = lambda t: pl.BlockSpec(t.shape, lambda i: (0, 0))
    outs, carried = _call(
        body, rider, name="conv_fwd", grid=(s // tm,),
        in_specs=[pl.BlockSpec((tm, 2 * CONV_CH), lambda i: (i, 0)),
                  pl.BlockSpec((CONV_HALO, 2 * CONV_CH), lambda i: (jnp.maximum(i * hb - 1, 0), 0)),
                  full(cw), full(cb), full(cg), full(cbeta)],
        out_specs=[pl.BlockSpec((tm, CONV_CH), lambda i: (i, 0))] * 2,
        out_shape=[jax.ShapeDtypeStruct((s, CONV_CH), F32), jax.ShapeDtypeStruct((s, CONV_CH), BF16)],
        scratch_shapes=[pltpu.VMEM((SUBLANES, CONV_HALO + tm, CONV_CH), F32)],
        compiler_params=_params(("arbitrary",), VMEM_BIG),
        args=(a, a, cw, cb, cg, cbeta))
    return (*outs, carried)


def _conv_bwd(a, u1, d_cat, cw, cg, cbeta, rider=None):
    s = a.shape[0]
    tm = _tile(s, 256, CONV_CHUNK)
    hb = tm // CONV_HALO
    last_halo = s // CONV_HALO - 1
    nt = s // tm
    te = tm + CONV_HALO

    def body(a_ref, ap_ref, u1_ref, u1n_ref, du_ref, dun_ref, cw_ref, cg_ref, cbeta_ref,
             da_ref, gw_ref, gb_ref, gg_ref, gbeta_ref, ubuf, dbuf):
        i = pl.program_id(0)

        @pl.when(i == 0)
        def _():
            gw_ref[...] = jnp.zeros_like(gw_ref)
            gb_ref[...] = jnp.zeros_like(gb_ref)
            gg_ref[...] = jnp.zeros_like(gg_ref)
            gbeta_ref[...] = jnp.zeros_like(gbeta_ref)

        def du1_of(u1, du):
            xh, rstd = _layer_norm_stats(u1)
            z = xh * cg_ref[...] + cbeta_ref[...]
            sz = _sigmoid(z)
            dz = du * (sz * (1.0 + z * (1.0 - sz)))
            dxh = dz * cg_ref[...]
            du1 = rstd * (dxh - jnp.mean(dxh, axis=-1, keepdims=True)
                          - xh * jnp.mean(dxh * xh, axis=-1, keepdims=True))
            return du1, dz, xh

        gg, gbeta, gb = [jnp.zeros((SUBLANES, CONV_CH), F32) for _ in range(3)]
        for r0, n in _chunks(tm):
            du1, dz, xh = du1_of(u1_ref[r0:r0 + n, :], du_ref[r0:r0 + n, :])
            gg, gbeta, gb = gg + _fold8(dz * xh), gbeta + _fold8(dz), gb + _fold8(du1)
            dbuf[0, r0:r0 + n, :] = du1
        gg_ref[...] += jnp.sum(gg, axis=0, keepdims=True)
        gbeta_ref[...] += jnp.sum(gbeta, axis=0, keepdims=True)
        gb_ref[...] += jnp.sum(gb, axis=0, keepdims=True)
        du1n, _, _ = du1_of(u1n_ref[...], jnp.where(i < nt - 1, dun_ref[...], 0.0))
        dbuf[0, tm:, :] = du1n
        _fill_shifted(dbuf, tm + CONV_HALO - SUBLANES)

        ubuf[0, 0:CONV_HALO, :] = jnp.where(i > 0, _glu(ap_ref[...]), 0.0)
        for r0, n in _chunks(tm):
            ubuf[0, CONV_HALO + r0:CONV_HALO + r0 + n, :] = _glu(a_ref[r0:r0 + n, :])
        _fill_shifted(ubuf, tm + CONV_HALO - SUBLANES)

        for k in range(CONV_K):
            part = jnp.zeros((SUBLANES, CONV_CH), F32)
            for r0, n in _chunks(tm):
                part = part + _fold8(dbuf[0, r0:r0 + n, :] * _tap(ubuf, 2 + k, r0, n))
            gw_ref[k:k + 1, :] += jnp.sum(part, axis=0, keepdims=True)

        for r0, n in _chunks(tm):
            acc = jnp.zeros((n, CONV_CH), F32)
            for k in range(CONV_K):
                acc = acc + cw_ref[k:k + 1, :] * _tap(dbuf, CONV_K - 1 - k, r0, n)
            avc = a_ref[r0:r0 + n, :CONV_CH]
            sgc = _sigmoid(a_ref[r0:r0 + n, CONV_CH:])
            da_ref[r0:r0 + n, :CONV_CH] = (acc * sgc).astype(BF16)
            da_ref[r0:r0 + n, CONV_CH:] = (acc * avc * sgc * (1.0 - sgc)).astype(BF16)

    full = lambda t: pl.BlockSpec(t.shape, lambda i: (0, 0))
    vec = pl.BlockSpec((1, CONV_CH), lambda i: (0, 0))
    nxt = lambda i: (jnp.minimum((i + 1) * hb, last_halo), 0)
    outs, carried = _call(
        body, rider, name="conv_bwd", grid=(nt,),
        in_specs=[pl.BlockSpec((tm, 2 * CONV_CH), lambda i: (i, 0)),
                  pl.BlockSpec((CONV_HALO, 2 * CONV_CH), lambda i: (jnp.maximum(i * hb - 1, 0), 0)),
                  pl.BlockSpec((tm, CONV_CH), lambda i: (i, 0)), pl.BlockSpec((CONV_HALO, CONV_CH), nxt),
                  pl.BlockSpec((tm, CONV_CH), lambda i: (i, 0)), pl.BlockSpec((CONV_HALO, CONV_CH), nxt),
                  full(cw), full(cg), full(cbeta)],
        out_specs=[pl.BlockSpec((tm, 2 * CONV_CH), lambda i: (i, 0)),
                   pl.BlockSpec((CONV_HALO, CONV_CH), lambda i: (0, 0)), vec, vec, vec],
        out_shape=[jax.ShapeDtypeStruct((s, 2 * CONV_CH), BF16), jax.ShapeDtypeStruct((CONV_HALO, CONV_CH), F32),
                   jax.ShapeDtypeStruct((1, CONV_CH), F32), jax.ShapeDtypeStruct((1, CONV_CH), F32),
                   jax.ShapeDtypeStruct((1, CONV_CH), F32)],
        scratch_shapes=[pltpu.VMEM((SUBLANES, CONV_HALO + tm, CONV_CH), F32), pltpu.VMEM((SUBLANES, te, CONV_CH), F32)],
        compiler_params=_params(("arbitrary",), VMEM_BIG),
        args=(a, a, u1, u1, d_cat, d_cat, cw, cg, cbeta))
    return (*outs, carried)


def _ff_tiles(s, f):
    return _tile(s, 256, 8), _tile(f, 1408, 128)


FF_CHUNK, FF_LANES = 64, 128
MXU_WIDTH = 256
FWD_STRIP_ROWS = 256


def _row_chunks(n):
    return [(r0, min(FF_CHUNK, n - r0)) for r0 in range(0, n, FF_CHUNK)]


def _fold8(v):
    acc = v[0:8]
    for r in range(8, v.shape[0], 8):
        acc = acc + v[r:r + 8]
    return acc


def _ff_conv(pre_buf, w_ref, b_ref, half, r0, rows, cols):
    acc = b_ref[half:half + 1, cols]
    for k in range(FF_K):
        off = FF_HALO - (FF_K - 1) + k + r0
        acc = acc + w_ref[k, half:half + 1, cols] * pre_buf[half, off:off + rows, cols]
    return acc


def _up_fwd_act(h2, w_up_t, fw, fb, f):
    s, d = h2.shape
    tm, tn = _tile(s, 512, 8), _tile(f, 1408, 128)
    nc, nt = f // tn, s // tm

    def body(h_ref, wg_ref, wv_ref, w_ref, b_ref, g_ref, v_ref, act_ref, xbuf):
        @pl.when(pl.program_id(1) == 0)
        def _():
            xbuf[...] = jnp.zeros_like(xbuf)

        strips = [(slice(c0, c0 + FF_LANES), r0, rows) for c0 in range(0, tn, FF_LANES)
                  for r0, rows in _chunks(tm, FWD_STRIP_ROWS)]
        pieces = [(out, wt, slice(c0, min(c0 + MXU_WIDTH, tn))) for out, wt in ((g_ref, wg_ref), (v_ref, wv_ref))
                  for c0 in range(0, tn, MXU_WIDTH)]
        per_piece = -(-len(strips) // len(pieces))
        hv = h_ref[...]
        for n, (out, wt, wcols) in enumerate(pieces):
            out[:, wcols] = _dot(hv, wt[wcols, :], trans_b=True)
            for cols, r0, rows in strips[n * per_piece:(n + 1) * per_piece]:
                gate = _ff_conv(xbuf, w_ref, b_ref, 0, r0, rows, cols)
                val = _ff_conv(xbuf, w_ref, b_ref, 1, r0, rows, cols)
                act_ref[r0:r0 + rows, cols] = (gate * _sigmoid(gate) * val).astype(BF16)
        for half, ref in enumerate((g_ref, v_ref)):
            xbuf[half, 0:FF_HALO, :] = xbuf[half, tm:tm + FF_HALO, :]
            xbuf[half, FF_HALO:, :] = ref[...]

    cur = lambda j, i: (jnp.minimum(i, nt - 1), j)
    return pl.pallas_call(
        body, name="up_fwd_act", grid=(nc, nt + 1),
        in_specs=[pl.BlockSpec((tm, d), lambda j, i: (jnp.minimum(i, nt - 1), 0)),
                  pl.BlockSpec((tn, d), lambda j, i: (j, 0)), pl.BlockSpec((tn, d), lambda j, i: (j + nc, 0)),
                  pl.BlockSpec((FF_K, 2, tn), lambda j, i: (0, 0, j)), pl.BlockSpec((2, tn), lambda j, i: (0, j))],
        out_specs=[pl.BlockSpec((tm, tn), cur), pl.BlockSpec((tm, tn), cur),
                   pl.BlockSpec((tm, tn), lambda j, i: (jnp.maximum(i - 1, 0), j))],
        out_shape=[jax.ShapeDtypeStruct((s, f), F32), jax.ShapeDtypeStruct((s, f), F32), jax.ShapeDtypeStruct((s, f), BF16)],
        scratch_shapes=[pltpu.VMEM((2, FF_HALO + tm, tn), F32)],
        compiler_params=_params(("arbitrary", "arbitrary"), VMEM_BIG),
    )(h2, w_up_t, w_up_t, fw, fb)


def _ff_bwd(up_g, up_v, d_act, fw, fb, rider=None):
    s, f = up_g.shape
    tm, tc = _ff_tiles(s, f)
    hb = tm // FF_HALO
    nt = s // tm
    last_halo = s // FF_HALO - 1
    te = tm + FF_HALO

    def body(g_ref, gp_ref, gn_ref, v_ref, vp_ref, vn_ref, da_ref, dan_ref, w_ref, b_ref,
             dg_ref, dv_ref, gw_ref, gb_ref, buf, shifted, dbuf, dshifted, dabuf):
        i = pl.program_id(1)

        @pl.when(i == 0)
        def _():
            gw_ref[...] = jnp.zeros_like(gw_ref)
            gb_ref[...] = jnp.zeros_like(gb_ref)

        for half, (m_ref, p_ref, n_ref) in enumerate(((g_ref, gp_ref, gn_ref), (v_ref, vp_ref, vn_ref))):
            buf[half, 0:FF_HALO, :] = jnp.where(i > 0, p_ref[...], 0.0)
            buf[half, FF_HALO:FF_HALO + tm, :] = m_ref[...]
            buf[half, FF_HALO + tm:, :] = n_ref[...]
            for k in range(FF_K - 1):
                lo = FF_HALO - (FF_K - 1) + k
                for r0, rows in _row_chunks(te):
                    shifted[half, k, r0:r0 + rows, :] = buf[half, lo + r0:lo + r0 + rows, :]
        dabuf[0:tm, :] = da_ref[...]
        dabuf[tm:, :] = jnp.where(i < nt - 1, dan_ref[...], 0.0)

        for c0 in range(0, tc, FF_LANES):
            cols = slice(c0, c0 + FF_LANES)
            gb = [jnp.zeros((8, FF_LANES), F32) for _ in range(2)]
            gw = [[jnp.zeros((8, FF_LANES), F32) for _ in range(FF_K)] for _ in range(2)]
            for r0, rows in _row_chunks(te):
                taps = [[shifted[half, 0, r0:r0 + rows, cols], shifted[half, 1, r0:r0 + rows, cols],
                         buf[half, FF_HALO + r0:FF_HALO + r0 + rows, cols]] for half in range(2)]
                gate, val = [b_ref[half:half + 1, cols] + sum(w_ref[k, half:half + 1, cols] * taps[half][k]
                                                              for k in range(FF_K)) for half in range(2)]
                da = dabuf[r0:r0 + rows, cols]
                sg = _sigmoid(gate)
                dup = [da * val * (sg * (1.0 + gate * (1.0 - sg))), da * (gate * sg)]
                for half in range(2):
                    dbuf[half, r0:r0 + rows, cols] = dup[half]
                    if r0 < tm:
                        gb[half] = gb[half] + _fold8(dup[half])
                        for k in range(FF_K):
                            gw[half][k] = gw[half][k] + _fold8(dup[half] * taps[half][k])
            for half, out_ref in enumerate((dg_ref, dv_ref)):
                gb_ref[half:half + 1, cols] += jnp.sum(gb[half], axis=0, keepdims=True)
                for k in range(FF_K):
                    gw_ref[k, half:half + 1, cols] += jnp.sum(gw[half][k], axis=0, keepdims=True)
                for k in range(1, FF_K):
                    for r0, rows in _row_chunks(tm):
                        dshifted[half, k - 1, r0:r0 + rows, cols] = dbuf[half, k + r0:k + r0 + rows, cols]
                for r0, rows in _row_chunks(tm):
                    acc = (w_ref[2, half:half + 1, cols] * dbuf[half, r0:r0 + rows, cols]
                           + w_ref[1, half:half + 1, cols] * dshifted[half, 0, r0:r0 + rows, cols]
                           + w_ref[0, half:half + 1, cols] * dshifted[half, 1, r0:r0 + rows, cols])
                    out_ref[r0:r0 + rows, cols] = acc.astype(BF16)

    main = pl.BlockSpec((tm, tc), lambda j, i: (i, j))
    prev = pl.BlockSpec((FF_HALO, tc), lambda j, i: (jnp.maximum(i * hb - 1, 0), j))
    nxt = pl.BlockSpec((FF_HALO, tc), lambda j, i: (jnp.minimum((i + 1) * hb, last_halo), j))
    (dg, dv, gw, gb), carried = _call(
        body, rider, name="ff_bwd", grid=(f // tc, nt),
        in_specs=[main, prev, nxt, main, prev, nxt, main, nxt,
                  pl.BlockSpec((FF_K, 2, tc), lambda j, i: (0, 0, j)), pl.BlockSpec((2, tc), lambda j, i: (0, j))],
        out_specs=[main, main, pl.BlockSpec((FF_K, 2, tc), lambda j, i: (0, 0, j)),
                   pl.BlockSpec((2, tc), lambda j, i: (0, j))],
        out_shape=[jax.ShapeDtypeStruct((s, f), BF16), jax.ShapeDtypeStruct((s, f), BF16),
                   jax.ShapeDtypeStruct((FF_K, 2, f), F32), jax.ShapeDtypeStruct((2, f), F32)],
        scratch_shapes=[pltpu.VMEM((2, FF_HALO + te, tc), F32), pltpu.VMEM((2, FF_K - 1, te, tc), F32),
                        pltpu.VMEM((2, te, tc), F32), pltpu.VMEM((2, FF_K - 1, tm, tc), F32), pltpu.VMEM((te, tc), F32)],
        compiler_params=_params(("arbitrary", "arbitrary"), VMEM_BIG),
        args=(up_g, up_g, up_g, up_v, up_v, up_v, d_act, d_act, fw, fb))
    return dg, dv, gw, gb, carried


ATT_TILE = Q_BLOCK * max(PATTERN_DILATIONS)


def _stream_rows(start, d, n=Q_BLOCK):
    return pl.ds(start, n) if d == 1 else pl.ds(start, n, stride=d)


def _band_geometry():
    qi = lax.broadcasted_iota(jnp.int32, (Q_BLOCK, 2 * Q_BLOCK), 0)
    ki = lax.broadcasted_iota(jnp.int32, (Q_BLOCK, 2 * Q_BLOCK), 1)
    delta = qi + Q_BLOCK - ki
    return (delta >= 0) & (delta <= Q_BLOCK), delta.astype(F32), ki


def _stream_blocks(d):
    out, nb = [], ATT_TILE // (Q_BLOCK * d)
    for r in range(d):
        for b in range(nb):
            start = b * Q_BLOCK * d + r
            out.append((start, start - Q_BLOCK * d if b > 0 else None, b == nb - 1))
    return out


def _attn_forward(qkv, slopes, rider=None):
    s = qkv.shape[0]
    nt = s // ATT_TILE
    nhp = ATTN_W // Q_BLOCK

    def body(sl_ref, q_ref, k_ref, kp_ref, v_ref, vp_ref, o_ref, l_ref):
        hp, i = pl.program_id(0), pl.program_id(1)
        head0 = lax.broadcasted_iota(jnp.int32, (Q_BLOCK, Q_BLOCK), 1) < HEAD_DIM
        head0_k = lax.broadcasted_iota(jnp.int32, (2 * Q_BLOCK, Q_BLOCK), 1) < HEAD_DIM
        valid, dist, ki = _band_geometry()
        first_key = jnp.where(i == 0, Q_BLOCK, 0)
        for d in PATTERN_DILATIONS:
            biases = [jnp.where(valid, dist * (-sl_ref[2 * hp + hh] * d), NEG) for hh in range(2)]
            for start, prev, _ in _stream_blocks(d):
                rows = _stream_rows(start, d)
                if prev is None:
                    prow = _stream_rows(ATT_TILE - Q_BLOCK * d + start, d)
                    kp, vp = kp_ref[prow, :].astype(BF16), vp_ref[prow, :].astype(BF16)
                else:
                    kp, vp = kc, vc
                kc, vc = k_ref[rows, :].astype(BF16), v_ref[rows, :].astype(BF16)
                qv = q_ref[rows, :].astype(BF16)
                k2 = jnp.concatenate([kp, kc], axis=0)
                v2 = jnp.concatenate([vp, vc], axis=0)
                res, mxs = [], []
                for hh in range(2):
                    mine = head0 if hh == 0 else jnp.logical_not(head0)
                    mine_k = head0_k if hh == 0 else jnp.logical_not(head0_k)
                    sc = _dot(jnp.where(mine, qv, jnp.zeros_like(qv)), k2, trans_b=True) + biases[hh]
                    if prev is None:
                        sc = jnp.where(ki < first_key, NEG, sc)
                    mx = jnp.max(sc, axis=1, keepdims=True)
                    p = jnp.exp(sc - mx).astype(BF16)
                    res.append(_dot(p, jnp.where(mine_k, v2, jnp.ones_like(v2))))
                    mxs.append(mx)
                num = jnp.where(head0, res[0], res[1])
                den = pltpu.roll(jnp.where(head0, res[1], res[0]), HEAD_DIM, 1)
                o_new = num / den
                l_new = jnp.where(head0, mxs[0], mxs[1]) + jnp.log(den)
                if d != PATTERN_DILATIONS[0]:
                    oa, la = o_ref[rows, :], l_ref[rows, :]
                    mm = jnp.maximum(la, l_new)
                    wa, wn = jnp.exp(la - mm), jnp.exp(l_new - mm)
                    o_new = (wa * oa + wn * o_new) / (wa + wn)
                    l_new = mm + jnp.log(wa + wn)
                o_ref[rows, :] = o_new
                l_ref[rows, :] = l_new

    def col(off):
        return pl.BlockSpec((ATT_TILE, Q_BLOCK), lambda hp, i: (i, off + hp))

    def col_prev(off):
        return pl.BlockSpec((ATT_TILE, Q_BLOCK), lambda hp, i: (jnp.maximum(i - 1, 0), off + hp))

    outs, carried = _call(
        body, rider, name="attn_fwd", grid=(nhp, nt),
        in_specs=[pl.BlockSpec(memory_space=pltpu.SMEM), col(0), col(nhp), col_prev(nhp), col(2 * nhp), col_prev(2 * nhp)],
        out_specs=[col(0), col(0)],
        out_shape=[jax.ShapeDtypeStruct((s, ATTN_W), F32)] * 2,
        compiler_params=_params(("arbitrary", "arbitrary"), VMEM_BIG),
        args=(slopes, qkv, qkv, qkv, qkv, qkv))
    return (*outs, carried)


def _attn_backward(qkv, qk, qg, kg, d_cat, o, lse, slopes, ones2, rider=None):
    s = qkv.shape[0]
    nt = s // ATT_TILE
    nhp = ATTN_W // Q_BLOCK
    tt = ATT_TILE

    def body(sl_ref, q_ref, k_ref, kp_ref, v_ref, vp_ref, do_ref, o_ref, l_ref, ones_ref, qraw_ref, kraw_ref,
             qg_ref, kg_ref, dq_ref, dk_ref, dv_ref, gq_ref, gk_ref, dkacc, dvacc, dd, dqacc, kcarry, vcarry):
        hp, step = pl.program_id(0), pl.program_id(1)
        tile = nt - 1 - step
        head0 = lax.broadcasted_iota(jnp.int32, (Q_BLOCK, Q_BLOCK), 1) < HEAD_DIM
        valid, dist, ki = _band_geometry()
        first_key = jnp.where(tile == 0, Q_BLOCK, 0)

        @pl.when((step == 0) & (hp == 0))
        def _():
            gq_ref[...] = jnp.zeros_like(gq_ref)
            gk_ref[...] = jnp.zeros_like(gk_ref)

        @pl.when(step == 0)
        def _():
            kcarry[...] = jnp.zeros_like(kcarry)
            vcarry[...] = jnp.zeros_like(vcarry)

        @pl.when(step > 0)
        def _():
            kcarry[...] = dkacc[:tt, :]
            vcarry[...] = dvacc[:tt, :]

        dd[...] = _head_sum(do_ref[...] * o_ref[...], ones_ref)

        def key_grad(acc, at, value, first):
            if first:
                acc[at, :] = value
            else:
                acc[at, :] += value

        ki2 = jnp.concatenate([ki, ki], axis=0)
        for d in PATTERN_DILATIONS:
            first = d == PATTERN_DILATIONS[0]
            bias2 = jnp.concatenate([jnp.where(valid, dist * (-sl_ref[2 * hp + hh] * d), NEG) for hh in range(2)], axis=0)
            for start, prev, last in _stream_blocks(d):
                rows = _stream_rows(start, d)
                if prev is None:
                    prow = _stream_rows(ATT_TILE - Q_BLOCK * d + start, d)
                    kp, vp = kp_ref[prow, :].astype(BF16), vp_ref[prow, :].astype(BF16)
                else:
                    kp, vp = kc, vc
                kc, vc = k_ref[rows, :].astype(BF16), v_ref[rows, :].astype(BF16)
                qv = q_ref[rows, :].astype(BF16)
                dov = do_ref[rows, :].astype(BF16)
                lv, ddv = l_ref[rows, :], dd[rows, :]
                lsw, dsw = pltpu.roll(lv, HEAD_DIM, 1), pltpu.roll(ddv, HEAD_DIM, 1)
                k2 = jnp.concatenate([kp, kc], axis=0)
                v2 = jnp.concatenate([vp, vc], axis=0)
                zq = jnp.zeros_like(qv)
                q2 = jnp.concatenate([jnp.where(head0, qv, zq), jnp.where(head0, zq, qv)], axis=0)
                do2 = jnp.concatenate([jnp.where(head0, dov, zq), jnp.where(head0, zq, dov)], axis=0)
                lh = jnp.concatenate([jnp.where(head0, lv, lsw), jnp.where(head0, lsw, lv)], axis=0)
                dh = jnp.concatenate([jnp.where(head0, ddv, dsw), jnp.where(head0, dsw, ddv)], axis=0)
                sc = _dot(q2, k2, trans_b=True) + bias2
                if prev is None:
                    sc = jnp.where(ki2 < first_key, NEG, sc)
                p = jnp.exp(sc - jnp.concatenate([lh, lh], axis=1))
                ds = p * (_dot(do2, v2, trans_b=True) - jnp.concatenate([dh, dh], axis=1))
                dq2 = _dot(ds.astype(BF16), k2)
                dk2 = _dot(ds.T.astype(BF16), q2)
                dv2 = _dot(p.T.astype(BF16), do2)
                dq = jnp.where(head0, dq2[:Q_BLOCK], dq2[Q_BLOCK:])
                if first:
                    dqacc[rows, :] = dq
                else:
                    dqacc[rows, :] += dq
                before = _stream_rows(tt + start - Q_BLOCK * d, d)
                if prev is None:
                    key_grad(dkacc, before, dk2[:Q_BLOCK], first)
                    key_grad(dvacc, before, dv2[:Q_BLOCK], first)
                else:
                    key_grad(dkacc, before, kpend + dk2[:Q_BLOCK], first)
                    key_grad(dvacc, before, vpend + dv2[:Q_BLOCK], first)
                kpend, vpend = dk2[Q_BLOCK:], dv2[Q_BLOCK:]
                if last:
                    key_grad(dkacc, _stream_rows(tt + start, d), kpend, first)
                    key_grad(dvacc, _stream_rows(tt + start, d), vpend, first)
        for r0, n in _chunks(tt, 256):
            dv_ref[r0:r0 + n, :] = (dvacc[tt + r0:tt + r0 + n, :] + vcarry[r0:r0 + n, :]).astype(BF16)

        def norm_bwd(dy, raw, g, scale):
            r = lax.rsqrt(_head_sum(raw * raw, ones_ref) * (1.0 / HEAD_DIM) + EPS)
            xh = raw * r
            gd = dy * (g * scale)
            dx = r * (gd - xh * (_head_sum(gd * xh, ones_ref) * (1.0 / HEAD_DIM)))
            return dx, _fold8(dy * xh) * scale

        gq, gk = jnp.zeros((SUBLANES, Q_BLOCK), F32), jnp.zeros((SUBLANES, Q_BLOCK), F32)
        for r0, n in _chunks(tt, 256):
            dxq, pq = norm_bwd(dqacc[r0:r0 + n, :], qraw_ref[r0:r0 + n, :], qg_ref[...], HEAD_DIM ** -0.5)
            dxk, pk = norm_bwd(dkacc[tt + r0:tt + r0 + n, :] + kcarry[r0:r0 + n, :], kraw_ref[r0:r0 + n, :], kg_ref[...], 1.0)
            dq_ref[r0:r0 + n, :] = dxq.astype(BF16)
            dk_ref[r0:r0 + n, :] = dxk.astype(BF16)
            gq, gk = gq + pq, gk + pk
        for acc, out in ((gq, gq_ref), (gk, gk_ref)):
            row = jnp.sum(acc, axis=0, keepdims=True)
            out[...] += row + pltpu.roll(row, HEAD_DIM, 1)

    def col(off):
        return pl.BlockSpec((tt, Q_BLOCK), lambda hp, st: (nt - 1 - st, off + hp))

    def col_prev(off):
        return pl.BlockSpec((tt, Q_BLOCK), lambda hp, st: (jnp.maximum(nt - 2 - st, 0), off + hp))

    gain = pl.BlockSpec((1, Q_BLOCK), lambda hp, st: (0, hp))
    total = pl.BlockSpec((1, Q_BLOCK), lambda hp, st: (0, 0))
    outs, carried = _call(
        body, rider, name="attn_bwd", grid=(nhp, nt),
        in_specs=[pl.BlockSpec(memory_space=pltpu.SMEM), col(0), col(nhp), col_prev(nhp), col(2 * nhp), col_prev(2 * nhp),
                  col(nhp), col(0), col(0), pl.BlockSpec((Q_BLOCK, Q_BLOCK), lambda hp, st: (0, 0)),
                  col(0), col(nhp), gain, gain],
        out_specs=[col(0)] * 3 + [total, total],
        out_shape=[jax.ShapeDtypeStruct((s, ATTN_W), BF16)] * 3 + [jax.ShapeDtypeStruct((1, Q_BLOCK), F32)] * 2,
        scratch_shapes=[pltpu.VMEM((2 * tt, Q_BLOCK), F32)] * 2 + [pltpu.VMEM((tt, Q_BLOCK), F32)] * 4,
        compiler_params=_params(("arbitrary", "arbitrary"), VMEM_BIG),
        args=(slopes, qkv, qkv, qkv, qkv, qkv, d_cat, o, lse, ones2, qk, qk, qg, kg))
    return (*outs, carried)


def _sum_parts(name, parts):
    _, n, w = parts.shape
    tn = _tile(n, 256, 8)

    def body(p_ref, o_ref):
        acc = p_ref[0].astype(F32)
        for j in range(1, N_DEV):
            acc = acc + p_ref[j].astype(F32)
        o_ref[...] = acc

    return pl.pallas_call(
        body, name=name, grid=(n // tn,),
        in_specs=[pl.BlockSpec((N_DEV, tn, w), lambda i: (0, i, 0))],
        out_specs=pl.BlockSpec((tn, w), lambda i: (i, 0)),
        out_shape=jax.ShapeDtypeStruct((n, w), F32),
        compiler_params=_params(("parallel",), VMEM_BIG),
    )(parts)


def _adamw(name, w, g, m, v):
    n, cols = w.shape
    tn = _tile(n, 256, 8)
    c1 = 1.0 - ADAM_B1 ** ADAM_STEP
    c2 = 1.0 - ADAM_B2 ** ADAM_STEP

    def body(w_ref, g_ref, m_ref, v_ref, d_ref, nm_ref, nv_ref):
        gv = g_ref[...]
        nm = ADAM_B1 * m_ref[...] + (1.0 - ADAM_B1) * gv
        nv = ADAM_B2 * v_ref[...] + (1.0 - ADAM_B2) * (gv * gv)
        nm_ref[...] = nm
        nv_ref[...] = nv
        d_ref[...] = -ADAM_LR * ((nm / c1) / (jnp.sqrt(nv / c2) + ADAM_EPS) + ADAM_WD * w_ref[...])

    blk = pl.BlockSpec((tn, cols), lambda i: (i, 0))
    return pl.pallas_call(
        body, name=name, grid=(n // tn,),
        in_specs=[blk] * 4, out_specs=[blk] * 3,
        out_shape=[jax.ShapeDtypeStruct((n, cols), F32)] * 3,
        compiler_params=_params(("parallel",), VMEM_BIG),
    )(w, g, m, v)


def _pack(vectors, width):
    flat = jnp.concatenate([t.reshape(-1) for t in vectors])
    rows = -(-flat.shape[0] // (8 * width)) * 8
    return jnp.pad(flat, (0, rows * width - flat.shape[0])).reshape(rows, width)


def _unpack(packed, shapes):
    flat = packed.reshape(-1)
    out, at = [], 0
    for shp in shapes:
        size = 1
        for dim in shp:
            size *= dim
        out.append(flat[at:at + size].reshape(shp))
        at += size
    return out


def kernel(x, norm1_g, w_in, conv_w, conv_b, cn_g, cn_b, q_norm_g, k_norm_g, w_out, norm2_g, w_up, ffconv_w, ffconv_b, w_down, loss_target, m_norm1_g, m_w_in, m_conv_w, m_conv_b, m_cn_g, m_cn_b, m_q_norm_g, m_k_norm_g, m_w_out, m_norm2_g, m_w_up, m_ffconv_w, m_ffconv_b, m_w_down, v_norm1_g, v_w_in, v_conv_w, v_conv_b, v_cn_g, v_cn_b, v_q_norm_g, v_k_norm_g, v_w_out, v_norm2_g, v_w_up, v_ffconv_w, v_ffconv_b, v_w_down):
    s, d = x.shape[1], x.shape[2]
    f = w_down.shape[0] * N_DEV
    n_in = w_in.shape[1] * N_DEV
    xs, target = x.reshape(s, d), loss_target.reshape(s, d)
    me = _linear((lax.axis_index("x"), lax.axis_index("y"), lax.axis_index("c")))

    row = lambda t: t.reshape(1, -1)
    g1, g2 = row(norm1_g), row(norm2_g)
    h, (g_in, g_filt) = _norm_fwd(xs, g1, _Gather([w_in.T.astype(BF16), _pack([conv_w, ffconv_w], 128)]))
    w_in_t = g_in.reshape(n_in, d)
    filt = g_filt.reshape(N_DEV, -1)
    n_cw = conv_w.size
    cw = filt[:, :n_cw].reshape(N_DEV, CONV_K, -1).transpose(1, 0, 2).reshape(CONV_K, CONV_CH)
    cw = jnp.pad(cw, ((0, CONV_HALO - CONV_K), (0, 0)))
    fw = filt[:, n_cw:n_cw + ffconv_w.size].reshape(N_DEV, FF_K, -1).transpose(1, 0, 2).reshape(FF_K, 2, f)
    fb = ffconv_b.reshape(2, f)
    qg, kg =row(jnp.tile(q_norm_g, N_HEADS)), row(jnp.tile(k_norm_g, N_HEADS))
    lanes = jnp.arange(ATTN_W) // HEAD_DIM
    ones = (lanes[:, None] == lanes[None, :]).astype(BF16)
    slopes = 2.0 ** (-8.0 * jnp.arange(1, N_HEADS + 1, dtype=F32) / N_HEADS)

    a, qk, qkv, (g_out,) = _proj_fwd(h, w_in_t, qg, kg, ones, _Gather([w_out.astype(BF16)]))
    u1, u, (g_up,) = _conv_fwd(a, cw, row(conv_b), row(cn_g), row(cn_b), _Gather([w_up.T.astype(BF16)]))
    o, lse, (g_down,) = _attn_forward(qkv, slopes, _Gather([w_down.astype(BF16)]))
    w_out_f = g_out.reshape(2 * CONV_CH, d)
    w_up_t = g_up.reshape(2 * f, d)
    w_down_f = g_down.reshape(f, d)
    x1, h2 = _mix_out(u, o, w_out_f, xs, g2)
    up_g, up_v, act = _up_fwd_act(h2, w_up_t, fw, fb, f)
    loss_acc, dyf, dyb = _down_loss(act, w_down_f, x1, target)

    tk, tk_wide = _tile(s, 1024, 8), _tile(s, 512, 8)
    gw_down = _mm_tn("grad_w_down", [[act]], dyb, tk)
    d_act = _mm("d_act", dyb, w_down_f, True, F32, _tile(s, 1024, 8), _tile(f, 1408, 128))
    blocks = lambda t: _Exchange([t.reshape(N_DEV, t.shape[0] // N_DEV, d)])
    dpre_g, dpre_v, gfw, gfb, (r_down,) = _ff_bwd(up_g, up_v, d_act, fw, fb, blocks(gw_down))
    gw_up = _mm_tn("grad_w_up", [[dpre_g], [dpre_v]], h2, tk_wide)
    dx1f, dx1b, gg2 = _up_bwd(dpre_g, dpre_v, w_up_t, x1, dyf, g2)
    gw_out = _mm_tn("grad_w_out", [[u, o]], dx1b, tk)
    d_cat = _mm("d_cat", dx1b, w_out_f, True, F32, _tile(s, 1024, 8), 2 * CONV_CH)
    d_a, gcw, gcb, gcg, gcbeta, (r_out,) = _conv_bwd(a, u1, d_cat, cw, row(cn_g), row(cn_b), blocks(gw_out))
    dq, dk, dv, gqg, gkg, (r_up,) = _attn_backward(qkv, qk, qg, kg, d_cat, o, lse, slopes, ones[:Q_BLOCK, :Q_BLOCK],
                                                   blocks(gw_up))
    d_proj = [d_a, dq, dk, dv]
    gw_in = _mm_tn("grad_w_in", [d_proj], h, tk)
    grad_x, gg1, (r_in,) = _proj_bwd(d_proj, w_in_t, xs, dx1f, g1, blocks(gw_in))
    g_w_in_t, g_w_out, g_w_up_t, g_w_down = [_sum_parts(f"sum_grad_{i}", r) for i, r in enumerate((r_in, r_out, r_up, r_down))]
    g_w_in, g_w_up = g_w_in_t.T, g_w_up_t.T

    small_shapes = [(d,), (d,), (CONV_CH,), (CONV_CH,), (CONV_CH,), (Q_BLOCK,), (Q_BLOCK,), (2 * f,),
                    (CONV_K, CONV_CH), (FF_K, 2 * f), (1,)]
    small = _pack([gg1, gg2, gcb, gcg, gcbeta, gqg, gkg, gfb, gcw[:CONV_K], gfw, loss_acc[0:1, 0:1]], 1024)
    (small_all,) = _communicate("gather_small_grads", _Gather([small]))
    sg1, sg2, scb, scg, scbeta, sqg, skg, sfb, scw, sfw, loss_sum = _unpack(
        _sum_parts("sum_small_grads", small_all), small_shapes)
    loss = loss_sum[0]
    cwl, fwl = conv_w.shape[1], ffconv_w.shape[1]
    g_small = [sg1, lax.dynamic_slice_in_dim(scw, me * cwl, cwl, 1), scb, scg, scbeta,
               sqg[:HEAD_DIM], skg[:HEAD_DIM], sg2, lax.dynamic_slice_in_dim(sfw, me * fwl, fwl, 1), sfb]

    w_small = [norm1_g, conv_w, conv_b, cn_g, cn_b, q_norm_g, k_norm_g, norm2_g, ffconv_w, ffconv_b]
    m_small = [m_norm1_g, m_conv_w, m_conv_b, m_cn_g, m_cn_b, m_q_norm_g, m_k_norm_g, m_norm2_g, m_ffconv_w, m_ffconv_b]
    v_small = [v_norm1_g, v_conv_w, v_conv_b, v_cn_g, v_cn_b, v_q_norm_g, v_k_norm_g, v_norm2_g, v_ffconv_w, v_ffconv_b]
    shapes = [t.shape for t in w_small]
    packed = _adamw("adamw_small", *[_pack(ts, 128) for ts in (w_small, g_small, m_small, v_small)])
    d_small, nm_small, nv_small = [_unpack(p, shapes) for p in packed]
    upd = {}
    for name, wt, gt, mt, vt in (("w_in", w_in, g_w_in, m_w_in, v_w_in), ("w_out", w_out, g_w_out, m_w_out, v_w_out),
                                 ("w_up", w_up, g_w_up, m_w_up, v_w_up), ("w_down", w_down, g_w_down, m_w_down, v_w_down)):
        upd[name] = (gt,) + tuple(_adamw("adamw_" + name, wt, gt, mt, vt))
    order = ["norm1_g", "w_in", "conv_w", "conv_b", "cn_g", "cn_b", "q_norm_g", "k_norm_g", "w_out", "norm2_g",
             "w_up", "ffconv_w", "ffconv_b", "w_down"]
    small_names = ["norm1_g", "conv_w", "conv_b", "cn_g", "cn_b", "q_norm_g", "k_norm_g", "norm2_g", "ffconv_w", "ffconv_b"]
    for i, name in enumerate(small_names):
        upd[name] = (g_small[i], d_small[i], nm_small[i], nv_small[i])
    outs = [loss, grad_x.reshape(x.shape)]
    for field in range(4):
        outs += [upd[name][field] for name in order]
    return tuple(outs)
```
